```python
import jax, jax.numpy as jnp
from jax import lax
import numpy as np

D_MODEL = 1024
BATCH = 16
SEQ = 256
DEPTH = 2
DEC_BATCH = 8
DEC_SEQ = 2048
PAST_LEN = 512

GRID_W = 64
EPS = 1e-6
ROPE_BASE = 10000.0
N_BRANCH = 4
BRANCH_WIDTH = D_MODEL // 4
N_MOD = 9
D_FF = 2816
S5_WIDTH = BRANCH_WIDTH
S5_GROUP = 16
S5_GROUPS = S5_WIDTH // S5_GROUP
S5_STATE = 64
RET_HEADS = 4
RET_DIM = BRANCH_WIDTH // RET_HEADS
RET_WIDTH = RET_HEADS * RET_DIM
RET_CHUNK = 128
CONV_WIDTH = BRANCH_WIDTH
CONV_K = 3
MLA_HEADS = 4
MLA_Q_LORA = 192
MLA_KV_LORA = 128
MLA_NOPE = 64
MLA_ROPE = 32
MLA_V = BRANCH_WIDTH // MLA_HEADS
Q_BLOCK = 128
IN_SIZES = (S5_WIDTH, RET_WIDTH, RET_WIDTH, RET_WIDTH, RET_WIDTH, CONV_WIDTH, CONV_WIDTH, CONV_WIDTH, MLA_Q_LORA, MLA_KV_LORA, MLA_ROPE)
IN_COLS = sum(IN_SIZES)

kernel_name = 'hybrid_diffusion_prefix_trunk_step'


def rmsnorm(x, g):
    xf = x.astype(jnp.float32)
    y = xf * lax.rsqrt(jnp.mean(xf * xf, axis=-1, keepdims=True) + EPS)
    return (y * g.astype(jnp.float32)).astype(x.dtype)


def swiglu(h, w1, w3, w2):
    return jnp.dot(jax.nn.silu(jnp.dot(h, w1)) * jnp.dot(h, w3), w2)


def axial_angles(T, dim):
    rows = T // GRID_W
    row = jnp.repeat(jnp.arange(rows, dtype=jnp.float32), GRID_W)
    col = jnp.tile(jnp.arange(GRID_W, dtype=jnp.float32), rows)
    quarter = dim // 4
    inv = ROPE_BASE ** (-jnp.arange(quarter, dtype=jnp.float32) / quarter)
    ang = jnp.concatenate([row[:, None] * inv, col[:, None] * inv], axis=-1)
    return jnp.cos(ang), jnp.sin(ang)


def apply_rope(x, cos, sin):
    half = x.shape[-1] // 2
    c = cos[None, :, None, :]
    s = sin[None, :, None, :]
    x1 = x[..., :half].astype(jnp.float32)
    x2 = x[..., half:].astype(jnp.float32)
    return jnp.concatenate([x1 * c - x2 * s, x1 * s + x2 * c], axis=-1).astype(x.dtype)


def _complex_affine_combine(e1, e2):
    a1r, a1i, b1r, b1i = e1
    a2r, a2i, b2r, b2i = e2
    ar = a2r * a1r - a2i * a1i
    ai = a2r * a1i + a2i * a1r
    br = a2r * b1r - a2i * b1i + b2r
    bi = a2r * b1i + a2i * b1r + b2i
    return ar, ai, br, bi


def s5_discretize(lam_re, lam_im, log_dt, b_re, b_im):
    lam_re = lam_re.astype(jnp.float32)
    lam_im = lam_im.astype(jnp.float32)
    dt = jnp.exp(log_dt.astype(jnp.float32))[:, None]
    mag = jnp.exp(lam_re * dt)
    ab_re = mag * jnp.cos(lam_im * dt)
    ab_im = mag * jnp.sin(lam_im * dt)
    den = lam_re * lam_re + lam_im * lam_im
    f_re = ((ab_re - 1.0) * lam_re + ab_im * lam_im) / den
    f_im = (ab_im * lam_re - (ab_re - 1.0) * lam_im) / den
    b_re = b_re.astype(jnp.float32)
    b_im = b_im.astype(jnp.float32)
    bb_re = f_re[..., None] * b_re - f_im[..., None] * b_im
    bb_im = f_re[..., None] * b_im + f_im[..., None] * b_re
    return ab_re, ab_im, bb_re, bb_im


def s5_scan_dir(u, lam_re, lam_im, log_dt, b_re, b_im, s0_re, s0_im, reverse):
    ab_re, ab_im, bb_re, bb_im = s5_discretize(lam_re, lam_im, log_dt, b_re, b_im)
    bu_re = jnp.einsum('btgh,gph->btgp', u, bb_re)
    bu_im = jnp.einsum('btgh,gph->btgp', u, bb_im)
    t0 = -1 if reverse else 0
    bu_re = bu_re.at[:, t0].add(ab_re * s0_re - ab_im * s0_im)
    bu_im = bu_im.at[:, t0].add(ab_re * s0_im + ab_im * s0_re)
    a_re = jnp.broadcast_to(ab_re, bu_re.shape)
    a_im = jnp.broadcast_to(ab_im, bu_im.shape)
    _, _, s_re, s_im = lax.associative_scan(_complex_affine_combine, (a_re, a_im, bu_re, bu_im), reverse=reverse, axis=1)
    return s_re, s_im


def s5_mixer(u, s0, lam_re, lam_im, log_dt, b_re, b_im, c_re, c_im, d_skip, w_glu):
    B_, T = u.shape[:2]
    uf = u.astype(jnp.float32).reshape(B_, T, S5_GROUPS, S5_GROUP)
    s0 = s0.astype(jnp.float32)
    tot_re, tot_im, finals = 0.0, 0.0, []
    for d, rev in enumerate((False, True)):
        s_re, s_im = s5_scan_dir(uf, lam_re[d], lam_im[d], log_dt[d], b_re[d], b_im[d], s0[:, d, ..., 0], s0[:, d, ..., 1], rev)
        tot_re = tot_re + s_re
        tot_im = tot_im + s_im
        end = 0 if rev else -1
        finals.append(jnp.stack([s_re[:, end], s_im[:, end]], axis=-1))
    y = jnp.einsum('ghp,btgp->btgh', c_re.astype(jnp.float32), tot_re) - jnp.einsum('ghp,btgp->btgh', c_im.astype(jnp.float32), tot_im)
    y = y.reshape(B_, T, S5_WIDTH) + d_skip.astype(jnp.float32) * uf.reshape(B_, T, S5_WIDTH)
    z = jax.nn.gelu(y)
    out = z * jax.nn.sigmoid(jnp.dot(z, w_glu.astype(jnp.float32)))
    return out.astype(u.dtype), jnp.stack(finals, axis=1)


def retention_dir(q, k, v, log_g, s0, strict):
    B_, T, H, d = q.shape
    n = T // RET_CHUNK

    def chunks(a):
        return a.reshape(B_, n, RET_CHUNK, H, d).transpose(1, 0, 3, 2, 4)

    idx = jnp.arange(RET_CHUNK, dtype=jnp.float32)
    diff = idx[:, None] - idx[None, :]
    mask = (diff > 0) if strict else (diff >= 0)
    dmat = jnp.where(mask, jnp.exp(log_g[:, None, None] * jnp.maximum(diff, 0.0)), 0.0)
    q_dec = jnp.exp(log_g[:, None] * (idx + 1.0))[None, :, :, None]
    k_dec = jnp.exp(log_g[:, None] * (RET_CHUNK - 1.0 - idx))[None, :, :, None]
    c_dec = jnp.exp(log_g * RET_CHUNK)[None, :, None, None]

    def step(s, blk):
        qb, kb, vb = blk
        att = jnp.einsum('bhid,bhjd->bhij', qb, kb) * dmat
        o = jnp.einsum('bhij,bhjd->bhid', att, vb) + jnp.einsum('bhid,bhde->bhie', qb, s) * q_dec
        s = s * c_dec + jnp.einsum('bhjd,bhje->bhde', kb * k_dec, vb)
        return s, o

    s_fin, o = lax.scan(step, s0, (chunks(q), chunks(k), chunks(v)))
    return o.transpose(1, 0, 3, 2, 4).reshape(B_, T, H, d), s_fin


def retention_mixer(rq, rk, rv, rg, s0, decay_logit, gn, rope):
    B_, T = rq.shape[:2]
    shp = (B_, T, RET_HEADS, RET_DIM)
    q = rq.reshape(shp)
    k = rk.reshape(shp)
    if rope is not None:
        q = apply_rope(q, rope[0], rope[1])
        k = apply_rope(k, rope[0], rope[1])
    q = q.astype(jnp.float32)
    k = k.astype(jnp.float32) * (RET_DIM ** -0.5)
    v = rv.reshape(shp).astype(jnp.float32)
    s0 = s0.astype(jnp.float32)
    log_g = jax.nn.log_sigmoid(decay_logit.astype(jnp.float32))
    o_f, s_f = retention_dir(q, k, v, log_g[0], s0[:, 0], False)
    o_b, s_b = retention_dir(jnp.flip(q, 1), jnp.flip(k, 1), jnp.flip(v, 1), log_g[1], s0[:, 1], True)
    o = o_f + jnp.flip(o_b, 1)
    mu = jnp.mean(o, axis=-1, keepdims=True)
    var = jnp.mean(jnp.square(o - mu), axis=-1, keepdims=True)
    o = ((o - mu) * lax.rsqrt(var + EPS)).reshape(B_, T, RET_WIDTH) * gn.astype(jnp.float32)
    out = jax.nn.silu(rg.astype(jnp.float32)) * o
    return out.astype(rg.dtype), jnp.stack([s_f, s_b], axis=1)


def short_conv_mixer(cx, cb, cc, w, b):
    z = cc * cx
    T = z.shape[1]
    pad = CONV_K // 2
    zp = jnp.pad(z, ((0, 0), (pad, pad), (0, 0)))
    y = b
    for j in range(CONV_K):
        y = y + zp[:, j:j + T] * w[j]
    return cb * y


def mla_expand(cache, w_ukv):
    B_, S = cache.shape[:2]
    kv = jnp.dot(cache[..., :MLA_KV_LORA], w_ukv).reshape(B_, S, MLA_HEADS, MLA_NOPE + MLA_V)
    return kv[..., :MLA_NOPE], kv[..., MLA_NOPE:], cache[..., MLA_KV_LORA:]


def mla_attend(q_nope, q_rope, k_nope, k_rope, v):
    B_, T = q_nope.shape[:2]
    n = T // Q_BLOCK
    scale = (MLA_NOPE + MLA_ROPE) ** -0.5

    def block(args):
        qn, qr = args
        s = jnp.einsum('bqhd,bkhd->bhqk', qn, k_nope) + jnp.einsum('bqhd,bkd->bhqk', qr, k_rope)
        p = jax.nn.softmax(s.astype(jnp.float32) * scale, axis=-1).astype(v.dtype)
        return jnp.einsum('bhqk,bkhd->bqhd', p, v)

    qn = q_nope.reshape(B_, n, Q_BLOCK, MLA_HEADS, MLA_NOPE).swapaxes(0, 1)
    qr = q_rope.reshape(B_, n, Q_BLOCK, MLA_HEADS, MLA_ROPE).swapaxes(0, 1)
    o = lax.map(block, (qn, qr))
    return o.swapaxes(0, 1).reshape(B_, T, MLA_HEADS * MLA_V)


def mla_mixer(cq, ckv, kr, q_norm, w_uq, kv_norm, w_ukv, rope, ctx_cache):
    B_, T = cq.shape[:2]
    q = jnp.dot(rmsnorm(cq, q_norm), w_uq).reshape(B_, T, MLA_HEADS, MLA_NOPE + MLA_ROPE)
    q_nope, q_rope = q[..., :MLA_NOPE], q[..., MLA_NOPE:]
    cache = jnp.concatenate([rmsnorm(ckv, kv_norm), kr], axis=-1)
    k_nope, v, k_rope = mla_expand(cache, w_ukv)
    if ctx_cache is not None:
        cos, sin = rope
        q_rope = apply_rope(q_rope, cos, sin)
        k_rope = apply_rope(k_rope[:, :, None, :], cos, sin)[:, :, 0, :]
        kn_c, v_c, kr_c = mla_expand(ctx_cache, w_ukv)
        k_nope = jnp.concatenate([k_nope, kn_c], axis=1)
        v = jnp.concatenate([v, v_c], axis=1)
        k_rope = jnp.concatenate([k_rope, kr_c], axis=1)
    return mla_attend(q_nope, q_rope, k_nope, k_rope, v), cache


def parallel_mixers(h, p, ctx):
    B_, T, _ = h.shape
    splits = np.cumsum(IN_SIZES)[:-1].tolist()
    u, rq, rk, rv, rg, cx, cb, cc, cq, ckv, kr = jnp.split(jnp.dot(h, p['w_in']), splits, axis=-1)
    if ctx is None:
        s5_0 = jnp.zeros((B_, 2, S5_GROUPS, S5_STATE, 2), jnp.float32)
        ret_0 = jnp.zeros((B_, 2, RET_HEADS, RET_DIM, RET_DIM), jnp.float32)
        mla_ctx, rope_ret, rope_mla = None, None, None
    else:
        s5_0, ret_0, mla_ctx = ctx
        rope_ret = axial_angles(T, RET_DIM)
        rope_mla = axial_angles(T, MLA_ROPE)
    y_s5, s5_fin = s5_mixer(u, s5_0, p['s5_lam_re'], p['s5_lam_im'], p['s5_log_dt'], p['s5_b_re'], p['s5_b_im'], p['s5_c_re'], p['s5_c_im'], p['s5_d'], p['s5_w_glu'])
    y_ret, ret_fin = retention_mixer(rq, rk, rv, rg, ret_0, p['ret_decay'], p['ret_gn'], rope_ret)
    y_conv = short_conv_mixer(cx, cb, cc, p['conv_w'], p['conv_b'])
    y_mla, mla_cache = mla_mixer(cq, ckv, kr, p['mla_q_norm'], p['mla_w_uq'], p['mla_kv_norm'], p['mla_w_ukv'], rope_mla, mla_ctx)
    branches = jnp.stack([y_s5, y_ret, y_conv.astype(h.dtype), y_mla.astype(h.dtype)], axis=2)
    proj = jnp.einsum('btnw,nwd->btnd', branches, p['w_branch'])
    gates = jax.nn.sigmoid(jnp.dot(h, p['w_gate']) + p['b_gate']).reshape(B_, T, N_BRANCH, D_MODEL)
    merged = jnp.einsum('btnd,btnd->btd', gates, proj)
    return jnp.dot(merged, p['w_o']), (s5_fin, ret_fin, mla_cache)


def trunk_layer(x, cond, p, ctx):
    B_ = x.shape[0]
    mod = (jnp.dot(jax.nn.silu(cond), p['w_mod']) + p['b_mod']).reshape(B_, N_MOD, 1, D_MODEL)
    sh1, sc1, g1, sh2, sc2, g2, sh3, sc3, g3 = [mod[:, i] for i in range(N_MOD)]
    h = rmsnorm(x, p['norm_pre'][0]) * (1.0 + sc1) + sh1
    f = swiglu(h, p['ffn_w1'][0], p['ffn_w3'][0], p['ffn_w2'][0])
    x = x + 0.5 * g1 * rmsnorm(f, p['norm_post'][0])
    h = rmsnorm(x, p['norm_pre'][1]) * (1.0 + sc2) + sh2
    m, state = parallel_mixers(h, p, ctx)
    x = x + g2 * rmsnorm(m, p['norm_post'][1])
    h = rmsnorm(x, p['norm_pre'][2]) * (1.0 + sc3) + sh3
    f = swiglu(h, p['ffn_w1'][1], p['ffn_w3'][1], p['ffn_w2'][1])
    x = x + 0.5 * g3 * rmsnorm(f, p['norm_post'][2])
    return x, state


def setup_inputs(seed: int = 0) -> dict:
    key = jax.random.key(seed)
    ks = iter(jax.random.split(key, 48))
    f32 = jnp.float32

    def nrm(shape, scale):
        return scale * jax.random.normal(next(ks), shape, f32)

    def gain(shape):
        return 1.0 + nrm(shape, 0.02)

    lam_im0 = jnp.pi * jnp.arange(S5_STATE, dtype=f32)
    ret_logit0 = jnp.log(2.0 ** (5.0 + jnp.arange(RET_HEADS, dtype=f32)) - 1.0)
    return {
        'x_prompt': nrm((BATCH, SEQ, D_MODEL), 1.0),
        'x_sample': nrm((DEC_BATCH, DEC_SEQ, D_MODEL), 1.0),
        'state_s5': nrm((DEC_BATCH, DEPTH, 2, S5_GROUPS, S5_STATE, 2), 0.5),
        'state_ret': nrm((DEC_BATCH, DEPTH, 2, RET_HEADS, RET_DIM, RET_DIM), 0.1),
        'cache_mla': nrm((DEC_BATCH, DEPTH, PAST_LEN, MLA_KV_LORA + MLA_ROPE), 1.0),
        'c': nrm((DEC_BATCH, D_MODEL), 1.0),
        'c_ctx': nrm((D_MODEL,), 1.0),
        'w_mod': nrm((DEPTH, D_MODEL, N_MOD * D_MODEL), 0.5 * D_MODEL ** -0.5),
        'b_mod': nrm((DEPTH, N_MOD * D_MODEL), 0.01),
        'norm_pre': gain((DEPTH, 3, D_MODEL)),
        'norm_post': gain((DEPTH, 3, D_MODEL)),
        'ffn_w1': nrm((DEPTH, 2, D_MODEL, D_FF), D_MODEL ** -0.5),
        'ffn_w3': nrm((DEPTH, 2, D_MODEL, D_FF), D_MODEL ** -0.5),
        'ffn_w2': nrm((DEPTH, 2, D_FF, D_MODEL), D_FF ** -0.5),
        'w_in': nrm((DEPTH, D_MODEL, IN_COLS), D_MODEL ** -0.5),
        's5_lam_re': -0.5 + nrm((DEPTH, 2, S5_GROUPS, S5_STATE), 0.01),
        's5_lam_im': lam_im0 + nrm((DEPTH, 2, S5_GROUPS, S5_STATE), 0.01),
        's5_log_dt': jax.random.uniform(next(ks), (DEPTH, 2, S5_GROUPS), f32, float(np.log(1e-3)), float(np.log(1e-1))),
        's5_b_re': nrm((DEPTH, 2, S5_GROUPS, S5_STATE, S5_GROUP), (2.0 * S5_GROUP) ** -0.5),
        's5_b_im': nrm((DEPTH, 2, S5_GROUPS, S5_STATE, S5_GROUP), (2.0 * S5_GROUP) ** -0.5),
        's5_c_re': nrm((DEPTH, S5_GROUPS, S5_GROUP, S5_STATE), (2.0 * S5_STATE) ** -0.5),
        's5_c_im': nrm((DEPTH, S5_GROUPS, S5_GROUP, S5_STATE), (2.0 * S5_STATE) ** -0.5),
        's5_d': nrm((DEPTH, S5_WIDTH), 1.0),
        's5_w_glu': nrm((DEPTH, S5_WIDTH, S5_WIDTH), S5_WIDTH ** -0.5),
        'ret_decay': ret_logit0 + nrm((DEPTH, 2, RET_HEADS), 0.1),
        'ret_gn': gain((DEPTH, RET_WIDTH)),
        'conv_w': nrm((DEPTH, CONV_K, CONV_WIDTH), CONV_K ** -0.5),
        'conv_b': nrm((DEPTH, CONV_WIDTH), 0.01),
        'mla_q_norm': gain((DEPTH, MLA_Q_LORA)),
        'mla_w_uq': nrm((DEPTH, MLA_Q_LORA, MLA_HEADS * (MLA_NOPE + MLA_ROPE)), MLA_Q_LORA ** -0.5),
        'mla_kv_norm': gain((DEPTH, MLA_KV_LORA)),
        'mla_w_ukv': nrm((DEPTH, MLA_KV_LORA, MLA_HEADS * (MLA_NOPE + MLA_V)), MLA_KV_LORA ** -0.5),
        'w_branch': nrm((DEPTH, N_BRANCH, BRANCH_WIDTH, D_MODEL), BRANCH_WIDTH ** -0.5),
        'w_gate': nrm((DEPTH, D_MODEL, N_BRANCH * D_MODEL), D_MODEL ** -0.5),
        'b_gate': nrm((DEPTH, N_BRANCH * D_MODEL), 0.01),
        'w_o': nrm((DEPTH, D_MODEL, D_MODEL), D_MODEL ** -0.5),
    }


def reference(x_prompt, x_sample, state_s5, state_ret, cache_mla, c, c_ctx, w_mod, b_mod, norm_pre, norm_post, ffn_w1, ffn_w3, ffn_w2, w_in, s5_lam_re, s5_lam_im, s5_log_dt, s5_b_re, s5_b_im, s5_c_re, s5_c_im, s5_d, s5_w_glu, ret_decay, ret_gn, conv_w, conv_b, mla_q_norm, mla_w_uq, mla_kv_norm, mla_w_ukv, w_branch, w_gate, b_gate, w_o):
    y_prompt = x_prompt
    y_sample = x_sample
    cond_ctx = jnp.broadcast_to(c_ctx, (x_prompt.shape[0], D_MODEL))
    s5_list, ret_list, mla_list = [], [], []
    for l in range(DEPTH):
        p = dict(w_mod=w_mod[l], b_mod=b_mod[l], norm_pre=norm_pre[l], norm_post=norm_post[l],
                 ffn_w1=ffn_w1[l], ffn_w3=ffn_w3[l], ffn_w2=ffn_w2[l], w_in=w_in[l],
                 s5_lam_re=s5_lam_re[l], s5_lam_im=s5_lam_im[l], s5_log_dt=s5_log_dt[l],
                 s5_b_re=s5_b_re[l], s5_b_im=s5_b_im[l], s5_c_re=s5_c_re[l], s5_c_im=s5_c_im[l],
                 s5_d=s5_d[l], s5_w_glu=s5_w_glu[l], ret_decay=ret_decay[l], ret_gn=ret_gn[l],
                 conv_w=conv_w[l], conv_b=conv_b[l], mla_q_norm=mla_q_norm[l], mla_w_uq=mla_w_uq[l],
                 mla_kv_norm=mla_kv_norm[l], mla_w_ukv=mla_w_ukv[l], w_branch=w_branch[l],
                 w_gate=w_gate[l], b_gate=b_gate[l], w_o=w_o[l])
        y_prompt, (s5_s, ret_s, mla_c) = trunk_layer(y_prompt, cond_ctx, p, None)
        s5_list.append(s5_s)
        ret_list.append(ret_s)
        mla_list.append(mla_c)
        y_sample, _ = trunk_layer(y_sample, c, p, (state_s5[:, l], state_ret[:, l], cache_mla[:, l]))
    new_state_s5 = jnp.stack(s5_list, axis=1)
    new_state_ret = jnp.stack(ret_list, axis=1)
    new_cache_mla = jnp.stack(mla_list, axis=1)
    return (y_prompt, y_sample, new_state_s5, new_state_ret, new_cache_mla)
```

```python
import functools

import numpy as np
import jax
import jax.numpy as jnp
from jax import lax
from jax.experimental import pallas as pl
from jax.experimental.pallas import tpu as pltpu

F32 = jnp.float32
BF16 = jnp.bfloat16

D_MODEL = 1024
DEPTH = 2
GRID_W = 64
EPS = 1e-6
ROPE_BASE = 10000.0
N_BRANCH = 4
BRANCH_WIDTH = D_MODEL // 4
N_MOD = 9
D_FF = 2816
S5_GROUP = 16
S5_GROUPS = BRANCH_WIDTH // S5_GROUP
S5_STATE = 64
S5_LANES = S5_GROUPS * S5_STATE
RET_HEADS = 4
RET_DIM = BRANCH_WIDTH // RET_HEADS
CONV_K = 3
MLA_HEADS = 4
MLA_Q_LORA = 192
MLA_KV_LORA = 128
MLA_NOPE = 64
MLA_ROPE = 32
MLA_V = BRANCH_WIDTH // MLA_HEADS
MLA_SLAB = 128

SEG = 256
FF_CHUNK = 256
N_FF_CHUNKS = D_FF // FF_CHUNK
MOD_ROWS = 16
HALO = 8
V7X_VMEM_LIMIT = 56 * 1024 * 1024

S_RQ, S_RK, S_RV, S_RG, S_CX, S_CB, S_CC, S_CQ, S_KV, S_RQR, S_RKR, S_KRR = range(12)
NSEG_PLAIN = 9
NSEG_ROPE = 12


def _cparams(sem):
    return pltpu.CompilerParams(dimension_semantics=sem, vmem_limit_bytes=V7X_VMEM_LIMIT)


def _dot(a, b):
    return jnp.dot(a.astype(BF16), b.astype(BF16), preferred_element_type=F32)


def _dot_nt(a, b):
    return lax.dot_general(a.astype(BF16), b.astype(BF16), (((1,), (1,)), ((), ())),
                           preferred_element_type=F32)


def _dot_tn(a, b):
    return lax.dot_general(a.astype(BF16), b.astype(BF16), (((0,), (0,)), ((), ())),
                           preferred_element_type=F32)


def _sigmoid(x):
    return 1.0 / (1.0 + jnp.exp(-x))


def _rms(x, g, n=None):
    n = x.shape[-1] if n is None else n
    ms = jnp.sum(x * x, axis=-1, keepdims=True) * (1.0 / n)
    return x * lax.rsqrt(ms + EPS) * g


def _gelu_tanh(x):
    return 0.5 * x * (1.0 + jnp.tanh(np.sqrt(2.0 / np.pi) * (x + 0.044715 * (x * x * x))))


def _const_spec(shape):
    nd = len(shape)
    return pl.BlockSpec(shape, lambda *_: (0,) * nd)


def _mod_kernel(c_ref, w_ref, b_ref, o_ref):
    c = c_ref[...]
    o_ref[...] = _dot(c * _sigmoid(c), w_ref[...]) + b_ref[...]


def _mod_call(cond, w_mod, b_mod):
    ncol = N_MOD * D_MODEL
    tn = D_MODEL
    return pl.pallas_call(
        _mod_kernel,
        grid=(DEPTH, ncol // tn),
        in_specs=[
            pl.BlockSpec((MOD_ROWS, D_MODEL), lambda l, j: (0, 0)),
            pl.BlockSpec((None, D_MODEL, tn), lambda l, j: (l, 0, j)),
            pl.BlockSpec((None, 1, tn), lambda l, j: (l, 0, j)),
        ],
        out_specs=pl.BlockSpec((None, MOD_ROWS, tn), lambda l, j: (l, 0, j)),
        out_shape=jax.ShapeDtypeStruct((DEPTH, MOD_ROWS, ncol), F32),
        compiler_params=_cparams(("arbitrary", "arbitrary")),
        name="mod",
    )(cond, w_mod, b_mod.reshape(DEPTH, 1, ncol))


def _ffn_kernel(x_ref, mod_ref, npre_ref, npost_ref, w1_ref, w3_ref, w2_ref, o_ref,
                h_scr, acc_scr, *, k):
    x = x_ref[...]
    sh = mod_ref[3 * k:3 * k + 1, :]
    sc = mod_ref[3 * k + 1:3 * k + 2, :]
    gt = mod_ref[3 * k + 2:3 * k + 3, :]
    h_scr[...] = (_rms(x, npre_ref[...]) * (1.0 + sc) + sh).astype(BF16)
    acc_scr[...] = jnp.zeros_like(acc_scr)

    def body(j, carry):
        h = h_scr[...]
        a = jnp.dot(h, w1_ref[j], preferred_element_type=F32)
        b = jnp.dot(h, w3_ref[j], preferred_element_type=F32)
        gg = (a * _sigmoid(a) * b).astype(BF16)
        acc_scr[...] += jnp.dot(gg, w2_ref[j], preferred_element_type=F32)
        return carry

    lax.fori_loop(0, N_FF_CHUNKS, body, 0)
    o_ref[...] = x + (0.5 * gt) * _rms(acc_scr[...], npost_ref[...])


def _ffn_call(x, mod_l, npre, npost, w1c, w3c, w2c, *, k, tm, row_of_tile):
    n = x.shape[0]
    return pl.pallas_call(
        functools.partial(_ffn_kernel, k=k),
        grid=(n // tm,),
        in_specs=[
            pl.BlockSpec((tm, D_MODEL), lambda i: (i, 0)),
            pl.BlockSpec((None, N_MOD, D_MODEL), lambda i: (row_of_tile(i), 0, 0)),
            _const_spec((1, D_MODEL)),
            _const_spec((1, D_MODEL)),
            _const_spec((N_FF_CHUNKS, D_MODEL, FF_CHUNK)),
            _const_spec((N_FF_CHUNKS, D_MODEL, FF_CHUNK)),
            _const_spec((N_FF_CHUNKS, FF_CHUNK, D_MODEL)),
        ],
        out_specs=pl.BlockSpec((tm, D_MODEL), lambda i: (i, 0)),
        out_shape=jax.ShapeDtypeStruct((n, D_MODEL), F32),
        scratch_shapes=[pltpu.VMEM((tm, D_MODEL), BF16), pltpu.VMEM((tm, D_MODEL), F32)],
        compiler_params=_cparams(("arbitrary",)),
        name="ffn",
    )(x, mod_l, npre, npost, w1c, w3c, w2c)


def _in_kernel(x_ref, mod_ref, npre_ref, w_ref, u_ref, p_ref, h_scr, *, nseg):
    sh = mod_ref[3:4, :]
    sc = mod_ref[4:5, :]
    h_scr[...] = (_rms(x_ref[...], npre_ref[...]) * (1.0 + sc) + sh).astype(BF16)
    u_ref[...] = jnp.dot(h_scr[...], w_ref[:, 0:SEG], preferred_element_type=F32)
    for j in range(nseg):
        p_ref[:, j * SEG:(j + 1) * SEG] = jnp.dot(
            h_scr[...], w_ref[:, (j + 1) * SEG:(j + 2) * SEG], preferred_element_type=F32)


def _in_call(x, mod_l, npre, w_ext, *, nb, t, tm, nseg, row_of_batch):
    tt = t // tm
    return pl.pallas_call(
        functools.partial(_in_kernel, nseg=nseg),
        grid=(nb, tt),
        in_specs=[
            pl.BlockSpec((tm, D_MODEL), lambda b, i: (b * tt + i, 0)),
            pl.BlockSpec((None, N_MOD, D_MODEL), lambda b, i: (row_of_batch(b), 0, 0)),
            _const_spec((1, D_MODEL)),
            _const_spec((D_MODEL, (nseg + 1) * SEG)),
        ],
        out_specs=[
            pl.BlockSpec((tm, SEG), lambda b, i: (i, b)),
            pl.BlockSpec((tm, nseg * SEG), lambda b, i: (b * tt + i, 0)),
        ],
        out_shape=[
            jax.ShapeDtypeStruct((t, nb * SEG), F32),
            jax.ShapeDtypeStruct((nb * t, nseg * SEG), F32),
        ],
        scratch_shapes=[pltpu.VMEM((tm, D_MODEL), BF16)],
        compiler_params=_cparams(("arbitrary", "arbitrary")),
        name="in_proj",
    )(x, mod_l, npre, w_ext)


def _s5_kernel(*refs, nb, steps, want_fin):
    if want_fin:
        u_ref, bb_ref, a_ref, cc_ref, s0_ref, y_ref, fin_ref, s_scr, bu_scr = refs
    else:
        u_ref, bb_ref, a_ref, cc_ref, s0_ref, y_ref, s_scr, bu_scr = refs
        fin_ref = None
    d = pl.program_id(0)
    c = pl.program_id(1)

    @pl.when(c == 0)
    def _():
        s_scr[...] = s0_ref[...]

    bu_scr[...] = _dot(u_ref[...], bb_ref[...])
    a_re = jnp.broadcast_to(a_ref[0:1, :], (nb, S5_LANES))
    a_im = jnp.broadcast_to(a_ref[1:2, :], (nb, S5_LANES))

    def step(i, carry):
        s_re, s_im = carry
        t = jnp.where(d == 0, i, steps - 1 - i)
        r0 = pl.multiple_of(t * nb, nb)
        n_re = a_re * s_re - a_im * s_im + bu_scr[pl.ds(r0, nb), 0:S5_LANES]
        n_im = a_re * s_im + a_im * s_re + bu_scr[pl.ds(r0, nb), S5_LANES:2 * S5_LANES]
        bu_scr[pl.ds(r0, nb), 0:S5_LANES] = n_re
        bu_scr[pl.ds(r0, nb), S5_LANES:2 * S5_LANES] = n_im
        return n_re, n_im

    s_re, s_im = lax.fori_loop(0, steps, step, (s_scr[0], s_scr[1]), unroll=4)
    s_scr[0] = s_re
    s_scr[1] = s_im
    y_ref[...] = _dot(bu_scr[...], cc_ref[...])
    if want_fin:
        fin_ref[0] = s_re
        fin_ref[1] = s_im


def _s5_call(u_tm, bblk, a_tab, cblk, s0, *, nb, t, steps, want_fin):
    nch = t // steps
    rows = steps * nb

    def chunk(d, c):
        return jnp.where(d == 0, c, nch - 1 - c)

    out_specs = [pl.BlockSpec((None, rows, SEG), lambda d, c: (d, chunk(d, c), 0))]
    out_shape = [jax.ShapeDtypeStruct((2, t * nb, SEG), F32)]
    if want_fin:
        out_specs.append(pl.BlockSpec((None, 2, nb, S5_LANES), lambda d, c: (d, 0, 0, 0)))
        out_shape.append(jax.ShapeDtypeStruct((2, 2, nb, S5_LANES), F32))
    return pl.pallas_call(
        functools.partial(_s5_kernel, nb=nb, steps=steps, want_fin=want_fin),
        grid=(2, nch),
        in_specs=[
            pl.BlockSpec((rows, SEG), lambda d, c: (chunk(d, c), 0)),
            pl.BlockSpec((None, SEG, 2 * S5_LANES), lambda d, c: (d, 0, 0)),
            pl.BlockSpec((None, 2, S5_LANES), lambda d, c: (d, 0, 0)),
            _const_spec((2 * S5_LANES, SEG)),
            pl.BlockSpec((None, 2, nb, S5_LANES), lambda d, c: (d, 0, 0, 0)),
        ],
        out_specs=out_specs,
        out_shape=out_shape,
        scratch_shapes=[pltpu.VMEM((2, nb, S5_LANES), F32), pltpu.VMEM((rows, 2 * S5_LANES), F32)],
        compiler_params=_cparams(("arbitrary", "arbitrary")),
        name="s5_scan",
    )(u_tm, bblk, a_tab, cblk, s0)


def _ret_kernel(*refs, rope, has_s0, want_fin, nch):
    refs = list(refs)
    q_ref, k_ref, v_ref, g_ref = refs[:4]
    del refs[:4]
    if rope:
        qr_ref, kr_ref, cos_ref, sin_ref = refs[:4]
        del refs[:4]
    dm_ref, qdf_ref, kdf_ref, cdf_ref, qdb_ref, kdb_ref, cdb_ref, gn_ref = refs[:8]
    del refs[:8]
    if has_s0:
        s0_ref = refs.pop(0)
    o_ref = refs.pop(0)
    fin_ref = refs.pop(0) if want_fin else None
    sb_all, sf_scr, sb_scr = refs

    p = pl.program_id(1)
    c = pl.program_id(2)
    hd = RET_DIM

    def roped_k():
        k = k_ref[...]
        if rope:
            k = k * cos_ref[...] + kr_ref[...] * sin_ref[...]
        return k

    @pl.when(p == 0)
    def _():
        ck = nch - 1 - c

        @pl.when(c == 0)
        def _():
            if has_s0:
                sb_scr[...] = s0_ref[1]
            else:
                sb_scr[...] = jnp.zeros_like(sb_scr)

        k = roped_k()
        v = v_ref[...]
        for h in range(RET_HEADS):
            sl = slice(h * hd, (h + 1) * hd)
            s_old = sb_scr[h]
            sb_all[ck, h] = s_old
            sb_scr[h] = s_old * cdb_ref[h] + _dot_tn(k[:, sl] * kdb_ref[h], v[:, sl])
        if want_fin:
            @pl.when(c == nch - 1)
            def _():
                fin_ref[1] = sb_scr[...]

    @pl.when(p == 1)
    def _():
        @pl.when(c == 0)
        def _():
            if has_s0:
                sf_scr[...] = s0_ref[0]
            else:
                sf_scr[...] = jnp.zeros_like(sf_scr)

        q = q_ref[...]
        if rope:
            q = q * cos_ref[...] + qr_ref[...] * sin_ref[...]
        k = roped_k()
        v = v_ref[...]
        outs = []
        for h in range(RET_HEADS):
            sl = slice(h * hd, (h + 1) * hd)
            qh, kh, vh = q[:, sl], k[:, sl], v[:, sl]
            att = _dot_nt(qh, kh) * dm_ref[h]
            s_f = sf_scr[h]
            o = (_dot(att, vh) + _dot(qh, s_f) * qdf_ref[h] + _dot(qh, sb_all[c, h]) * qdb_ref[h])
            sf_scr[h] = s_f * cdf_ref[h] + _dot_tn(kh * kdf_ref[h], vh)
            mu = jnp.mean(o, axis=-1, keepdims=True)
            oc = o - mu
            var = jnp.mean(oc * oc, axis=-1, keepdims=True)
            outs.append(oc * lax.rsqrt(var + EPS))
        on = jnp.concatenate(outs, axis=-1) * gn_ref[...]
        g = g_ref[...]
        o_ref[...] = (g * _sigmoid(g)) * on
        if want_fin:
            @pl.when(c == nch - 1)
            def _():
                fin_ref[0] = sf_scr[...]


def _ret_call(p3, tabs, gn, s0, rope_tabs, *, nb, t, tc, want_fin):
    nch = t // tc
    rope = rope_tabs is not None
    has_s0 = s0 is not None

    def kv_chunk(p, c):
        return jnp.where(p == 0, nch - 1 - c, c)

    def q_chunk(p, c):
        return jnp.where(p == 0, 0, c)

    def seg_spec(seg, cmap):
        return pl.BlockSpec((None, tc, SEG), lambda b, p, c: (b, cmap(p, c), seg))

    args = [p3, p3, p3, p3]
    in_specs = [seg_spec(S_RQ, q_chunk), seg_spec(S_RK, kv_chunk), seg_spec(S_RV, kv_chunk),
                seg_spec(S_RG, q_chunk)]
    if rope:
        args += [p3, p3, rope_tabs[0], rope_tabs[1]]
        in_specs += [seg_spec(S_RQR, q_chunk), seg_spec(S_RKR, kv_chunk),
                     pl.BlockSpec((tc, SEG), lambda b, p, c: (kv_chunk(p, c), 0)),
                     pl.BlockSpec((tc, SEG), lambda b, p, c: (kv_chunk(p, c), 0))]
    args += list(tabs) + [gn]
    in_specs += [_const_spec(a.shape) for a in tabs] + [_const_spec((1, SEG))]
    if has_s0:
        args.append(s0)
        in_specs.append(pl.BlockSpec((None, 2, RET_HEADS, RET_DIM, RET_DIM),
                                     lambda b, p, c: (b, 0, 0, 0, 0)))
    out_specs = [pl.BlockSpec((None, tc, SEG), lambda b, p, c: (b, q_chunk(p, c), 0))]
    out_shape = [jax.ShapeDtypeStruct((nb, t, SEG), F32)]
    if want_fin:
        out_specs.append(pl.BlockSpec((None, 2, RET_HEADS, RET_DIM, RET_DIM),
                                      lambda b, p, c: (b, 0, 0, 0, 0)))
        out_shape.append(jax.ShapeDtypeStruct((nb, 2, RET_HEADS, RET_DIM, RET_DIM), F32))
    return pl.pallas_call(
        functools.partial(_ret_kernel, rope=rope, has_s0=has_s0, want_fin=want_fin, nch=nch),
        grid=(nb, 2, nch),
        in_specs=in_specs,
        out_specs=out_specs,
        out_shape=out_shape,
        scratch_shapes=[pltpu.VMEM((nch, RET_HEADS, RET_DIM, RET_DIM), F32),
                        pltpu.VMEM((RET_HEADS, RET_DIM, RET_DIM), F32),
                        pltpu.VMEM((RET_HEADS, RET_DIM, RET_DIM), F32)],
        compiler_params=_cparams(("arbitrary", "arbitrary", "arbitrary")),
        name="retention",
    )(*args)


def _mla_kernel(*refs, t, tq, nctx, want_cache):
    refs = list(refs)
    cq_ref, ckv_ref, krp_ref = refs[:3]
    del refs[:3]
    has_ctx = nctx > 0
    if has_ctx:
        krr_ref, cm_ref, sm_ref, clat_ref, ckrp_ref = refs[:5]
        del refs[:5]
    qn_ref, kvn_ref, wq_ref = refs[:3]
    del refs[:3]
    if has_ctx:
        wqr_ref = refs.pop(0)
    wk_ref, wv_ref = refs[:2]
    del refs[:2]
    o_ref = refs.pop(0)
    lat_ref = refs.pop(0) if want_cache else None
    k_scr, v_scr = refs
    qi = pl.program_id(1)

    @pl.when(qi == 0)
    def _():
        lat = _rms(ckv_ref[...], kvn_ref[...])
        if want_cache:
            lat_ref[...] = lat
        latb = lat.astype(BF16)
        kr = krp_ref[...]
        if has_ctx:
            kr = kr * cm_ref[...] + krr_ref[...] * sm_ref[...]
        kk = jnp.dot(latb, wk_ref[...], preferred_element_type=F32)
        for h in range(MLA_HEADS):
            k_scr[h, 0:t, :] = (kk[:, h * MLA_SLAB:(h + 1) * MLA_SLAB] + kr).astype(BF16)
        v_scr[0:t, :] = jnp.dot(latb, wv_ref[...], preferred_element_type=F32).astype(BF16)
        if has_ctx:
            latc = clat_ref[...].astype(BF16)
            kkc = jnp.dot(latc, wk_ref[...], preferred_element_type=F32)
            krc = ckrp_ref[...]
            for h in range(MLA_HEADS):
                k_scr[h, t:t + nctx, :] = (kkc[:, h * MLA_SLAB:(h + 1) * MLA_SLAB] + krc).astype(BF16)
            v_scr[t:t + nctx, :] = jnp.dot(latc, wv_ref[...], preferred_element_type=F32).astype(BF16)

    a = _rms(cq_ref[...], qn_ref[...], n=MLA_Q_LORA).astype(BF16)
    q = jnp.dot(a, wq_ref[...], preferred_element_type=F32)
    if has_ctx:
        qrot = jnp.dot(a, wqr_ref[...], preferred_element_type=F32)
        r0 = pl.multiple_of(qi * tq, tq)
        cm = cm_ref[pl.ds(r0, tq), :]
        sm = sm_ref[pl.ds(r0, tq), :]
    scale = (MLA_NOPE + MLA_ROPE) ** -0.5
    lane = lax.broadcasted_iota(jnp.int32, (tq, SEG), 1)
    acc = jnp.zeros((tq, SEG), F32)
    for h in range(MLA_HEADS):
        qh = q[:, h * MLA_SLAB:(h + 1) * MLA_SLAB]
        if has_ctx:
            qh = qh * cm + qrot[:, h * MLA_SLAB:(h + 1) * MLA_SLAB] * sm
        s = _dot_nt(qh * scale, k_scr[h])
        m = jnp.max(s, axis=-1, keepdims=True)
        e = jnp.exp(s - m)
        l = jnp.sum(e, axis=-1, keepdims=True)
        oh = jnp.dot(e.astype(BF16), v_scr[...], preferred_element_type=F32) * (1.0 / l)
        acc = jnp.where((lane >= h * MLA_V) & (lane < (h + 1) * MLA_V), oh, acc)
    o_ref[...] = acc


def _mla_call(p3, qn, kvn, wq, wqr, wk, wv, rope_tabs, ctx, *, nb, t, tq, want_cache):
    has_ctx = ctx is not None
    nctx = ctx[0].shape[1] if has_ctx else 0
    s_tot = t + nctx
    half = MLA_KV_LORA

    args = [p3, p3, p3]
    in_specs = [
        pl.BlockSpec((None, tq, SEG), lambda b, i: (b, i, S_CQ)),
        pl.BlockSpec((None, t, half), lambda b, i: (b, 0, 2 * S_KV)),
        pl.BlockSpec((None, t, half), lambda b, i: (b, 0, 2 * S_KV + 1)),
    ]
    if has_ctx:
        args += [p3, rope_tabs[0], rope_tabs[1], ctx[0], ctx[1]]
        in_specs += [
            pl.BlockSpec((None, t, half), lambda b, i: (b, 0, 2 * S_KRR)),
            _const_spec((t, MLA_SLAB)),
            _const_spec((t, MLA_SLAB)),
            pl.BlockSpec((None, nctx, MLA_KV_LORA), lambda b, i: (b, 0, 0)),
            pl.BlockSpec((None, nctx, MLA_SLAB), lambda b, i: (b, 0, 0)),
        ]
    args += [qn, kvn, wq]
    in_specs += [_const_spec(qn.shape), _const_spec(kvn.shape), _const_spec(wq.shape)]
    if has_ctx:
        args.append(wqr)
        in_specs.append(_const_spec(wqr.shape))
    args += [wk, wv]
    in_specs += [_const_spec(wk.shape), _const_spec(wv.shape)]
    out_specs = [pl.BlockSpec((None, tq, SEG), lambda b, i: (b, i, 0))]
    out_shape = [jax.ShapeDtypeStruct((nb, t, SEG), F32)]
    if want_cache:
        out_specs.append(pl.BlockSpec((None, t, MLA_KV_LORA), lambda b, i: (b, 0, 0)))
        out_shape.append(jax.ShapeDtypeStruct((nb, t, MLA_KV_LORA), F32))
    return pl.pallas_call(
        functools.partial(_mla_kernel, t=t, tq=tq, nctx=nctx, want_cache=want_cache),
        grid=(nb, t // tq),
        in_specs=in_specs,
        out_specs=out_specs,
        out_shape=out_shape,
        scratch_shapes=[pltpu.VMEM((MLA_HEADS, s_tot, MLA_SLAB), BF16),
                        pltpu.VMEM((s_tot, SEG), BF16)],
        compiler_params=_cparams(("arbitrary", "arbitrary")),
        name="mla",
    )(*args)


def _merge_kernel(x_ref, mod_ref, npre_ref, npost_ref, yf_ref, yb_ref, u_ref, ret_ref,
                  cx_ref, cb_ref, cc_ref, cxp_ref, ccp_ref, cxn_ref, ccn_ref, mla_ref,
                  s5d_ref, wglu_ref, cw_ref, cbias_ref, wbr_ref, wg_ref, bg_ref, wo_ref,
                  o_ref, h_scr, *, tm, nt):
    i = pl.program_id(1)
    x = x_ref[...]
    sh = mod_ref[3:4, :]
    sc = mod_ref[4:5, :]
    gt = mod_ref[5:6, :]
    h_scr[...] = (_rms(x, npre_ref[...]) * (1.0 + sc) + sh).astype(BF16)

    y = yf_ref[...] + yb_ref[...] + s5d_ref[...] * u_ref[...]
    z5 = _gelu_tanh(y)
    b_s5 = z5 * _sigmoid(_dot(z5, wglu_ref[...]))

    z = cc_ref[...] * cx_ref[...]
    zprev = ccp_ref[HALO - 1:HALO, :] * cxp_ref[HALO - 1:HALO, :] * jnp.where(i > 0, 1.0, 0.0)
    znext = ccn_ref[0:1, :] * cxn_ref[0:1, :] * jnp.where(i < nt - 1, 1.0, 0.0)
    row = lax.broadcasted_iota(jnp.int32, (tm, SEG), 0)
    z_dn = jnp.where(row == 0, zprev, pltpu.roll(z, 1, axis=0))
    z_up = jnp.where(row == tm - 1, znext, pltpu.roll(z, tm - 1, axis=0))
    b_conv = cb_ref[...] * (cbias_ref[...] + z_dn * cw_ref[0:1, :] + z * cw_ref[1:2, :]
                            + z_up * cw_ref[2:3, :])

    branches = (b_s5, ret_ref[...], b_conv, mla_ref[...])
    merged = jnp.zeros((tm, D_MODEL), F32)
    for n in range(N_BRANCH):
        gate = _sigmoid(jnp.dot(h_scr[...], wg_ref[:, n * D_MODEL:(n + 1) * D_MODEL],
                                preferred_element_type=F32) + bg_ref[:, n * D_MODEL:(n + 1) * D_MODEL])
        merged = merged + gate * _dot(branches[n], wbr_ref[n])
    m = _dot(merged, wo_ref[...])
    o_ref[...] = x + gt * _rms(m, npost_ref[...])


def _merge_call(x, mod_l, npre, npost, y_s5, u_tm, ret_o, p3, mla_o, s5d, wglu, cw, cbias, wbr, wg, bg, wo,
                *, nb, t, tm, row_of_batch):
    nt = t // tm
    hb = tm // HALO
    n_halo = t // HALO

    def tm_spec():
        return pl.BlockSpec((tm, SEG), lambda b, i: (i, b))

    def dir_spec(d):
        return pl.BlockSpec((None, tm, SEG), lambda b, i: (d, i, b))

    def seg_spec(seg):
        return pl.BlockSpec((None, tm, SEG), lambda b, i: (b, i, seg))

    def prev_spec(seg):
        return pl.BlockSpec((None, HALO, SEG), lambda b, i: (b, jnp.maximum(i * hb - 1, 0), seg))

    def next_spec(seg):
        return pl.BlockSpec((None, HALO, SEG), lambda b, i: (b, jnp.minimum((i + 1) * hb, n_halo - 1), seg))

    y_tm = y_s5.reshape(2, t, nb * SEG)
    return pl.pallas_call(
        functools.partial(_merge_kernel, tm=tm, nt=nt),
        grid=(nb, nt),
        in_specs=[
            pl.BlockSpec((tm, D_MODEL), lambda b, i: (b * nt + i, 0)),
            pl.BlockSpec((None, N_MOD, D_MODEL), lambda b, i: (row_of_batch(b), 0, 0)),
            _const_spec((1, D_MODEL)),
            _const_spec((1, D_MODEL)),
            dir_spec(0), dir_spec(1), tm_spec(),
            pl.BlockSpec((None, tm, SEG), lambda b, i: (b, i, 0)),
            seg_spec(S_CX), seg_spec(S_CB), seg_spec(S_CC),
            prev_spec(S_CX), prev_spec(S_CC), next_spec(S_CX), next_spec(S_CC),
            pl.BlockSpec((None, tm, SEG), lambda b, i: (b, i, 0)),
            _const_spec(s5d.shape), _const_spec(wglu.shape), _const_spec(cw.shape),
            _const_spec(cbias.shape), _const_spec(wbr.shape), _const_spec(wg.shape),
            _const_spec(bg.shape), _const_spec(wo.shape),
        ],
        out_specs=pl.BlockSpec((tm, D_MODEL), lambda b, i: (b * nt + i, 0)),
        out_shape=jax.ShapeDtypeStruct((nb * t, D_MODEL), F32),
        scratch_shapes=[pltpu.VMEM((tm, D_MODEL), BF16)],
        compiler_params=_cparams(("arbitrary", "arbitrary")),
        name="merge",
    )(x, mod_l, npre, npost, y_tm, y_tm, u_tm, ret_o, p3, p3, p3, p3, p3, p3, p3, mla_o,
      s5d, wglu, cw, cbias, wbr, wg, bg, wo)


def _rot_half(w, heads, dim):
    w4 = w.reshape(w.shape[0], heads, dim)
    half = dim // 2
    return jnp.concatenate([-w4[..., half:], w4[..., :half]], axis=-1).reshape(w.shape[0], heads * dim)


def _axial_angles(t, dim):
    rows = t // GRID_W
    row = jnp.repeat(jnp.arange(rows, dtype=F32), GRID_W)
    col = jnp.tile(jnp.arange(GRID_W, dtype=F32), rows)
    quarter = dim // 4
    inv = ROPE_BASE ** (-jnp.arange(quarter, dtype=F32) / quarter)
    ang = jnp.concatenate([row[:, None] * inv, col[:, None] * inv], axis=-1)
    return jnp.cos(ang), jnp.sin(ang)


def _rope_tables(t):
    cos, sin = _axial_angles(t, RET_DIM)
    ret_cos = jnp.tile(jnp.concatenate([cos, cos], -1), (1, RET_HEADS))
    ret_sin = jnp.tile(jnp.concatenate([sin, sin], -1), (1, RET_HEADS))
    cos, sin = _axial_angles(t, MLA_ROPE)
    pad = MLA_SLAB - MLA_NOPE - MLA_ROPE
    mla_cos = jnp.concatenate([jnp.ones((t, MLA_NOPE), F32), cos, cos, jnp.zeros((t, pad), F32)], -1)
    mla_sin = jnp.concatenate([jnp.zeros((t, MLA_NOPE), F32), sin, sin, jnp.zeros((t, pad), F32)], -1)
    return (ret_cos, ret_sin), (mla_cos, mla_sin)


def _in_weights(w_in, rope):
    sizes = (BRANCH_WIDTH,) * 8 + (MLA_Q_LORA, MLA_KV_LORA, MLA_ROPE)
    offs = np.cumsum((0,) + sizes)
    u, rq, rk, rv, rg, cx, cb, cc, cq, ckv, kr = [w_in[:, offs[i]:offs[i + 1]] for i in range(11)]
    rk = rk * (RET_DIM ** -0.5)
    d = w_in.shape[0]

    def place_kr(w):
        return jnp.concatenate([jnp.zeros((d, MLA_NOPE), F32), w,
                                jnp.zeros((d, MLA_SLAB - MLA_NOPE - MLA_ROPE), F32)], -1)

    cols = [u, rq, rk, rv, rg, cx, cb, cc,
            jnp.concatenate([cq, jnp.zeros((d, SEG - MLA_Q_LORA), F32)], -1),
            jnp.concatenate([ckv, place_kr(kr)], -1)]
    if rope:
        cols += [_rot_half(rq, RET_HEADS, RET_DIM), _rot_half(rk, RET_HEADS, RET_DIM),
                 jnp.concatenate([place_kr(_rot_half(kr, 1, MLA_ROPE)), jnp.zeros((d, SEG - MLA_SLAB), F32)], -1)]
    return jnp.concatenate(cols, axis=-1).astype(BF16)


def _mla_weights(w_uq, w_ukv):
    dq = MLA_NOPE + MLA_ROPE
    w4 = w_uq.reshape(MLA_Q_LORA, MLA_HEADS, dq)
    nope, ropew = w4[..., :MLA_NOPE], w4[..., MLA_NOPE:]
    half = MLA_ROPE // 2
    rot = jnp.concatenate([-ropew[..., half:], ropew[..., :half]], -1)
    zpad = jnp.zeros((MLA_Q_LORA, MLA_HEADS, MLA_SLAB - dq), F32)
    wq = jnp.concatenate([nope, ropew, zpad], -1).reshape(MLA_Q_LORA, MLA_HEADS * MLA_SLAB)
    wqr = jnp.concatenate([jnp.zeros_like(nope), rot, zpad], -1).reshape(MLA_Q_LORA, MLA_HEADS * MLA_SLAB)
    rpad = jnp.zeros((SEG - MLA_Q_LORA, MLA_HEADS * MLA_SLAB), F32)
    wq = jnp.concatenate([wq, rpad], 0).astype(BF16)
    wqr = jnp.concatenate([wqr, rpad], 0).astype(BF16)
    kv4 = w_ukv.reshape(MLA_KV_LORA, MLA_HEADS, MLA_NOPE + MLA_V)
    wk = jnp.concatenate([kv4[..., :MLA_NOPE], jnp.zeros((MLA_KV_LORA, MLA_HEADS, MLA_SLAB - MLA_NOPE), F32)],
                         -1).reshape(MLA_KV_LORA, MLA_HEADS * MLA_SLAB).astype(BF16)
    wv = kv4[..., MLA_NOPE:].reshape(MLA_KV_LORA, MLA_HEADS * MLA_V).astype(BF16)
    return wq, wqr, wk, wv


def _s5_tables(lam_re, lam_im, log_dt, b_re, b_im, c_re, c_im):
    dt = jnp.exp(log_dt)[..., None]
    mag = jnp.exp(lam_re * dt)
    ab_re = mag * jnp.cos(lam_im * dt)
    ab_im = mag * jnp.sin(lam_im * dt)
    den = lam_re * lam_re + lam_im * lam_im
    f_re = ((ab_re - 1.0) * lam_re + ab_im * lam_im) / den
    f_im = (ab_im * lam_re - (ab_re - 1.0) * lam_im) / den
    bb_re = f_re[..., None] * b_re - f_im[..., None] * b_im
    bb_im = f_re[..., None] * b_im + f_im[..., None] * b_re
    eye = jnp.eye(S5_GROUPS, dtype=F32)
    blk_in = lambda w: jnp.einsum('dgph,gk->dghkp', w, eye).reshape(2, BRANCH_WIDTH, S5_LANES)
    bblk = jnp.concatenate([blk_in(bb_re), blk_in(bb_im)], -1).astype(BF16)
    blk_out = lambda w: jnp.einsum('ghp,gk->gpkh', w, eye).reshape(S5_LANES, BRANCH_WIDTH)
    cblk = jnp.concatenate([blk_out(c_re), -blk_out(c_im)], 0).astype(BF16)
    a_tab = jnp.stack([ab_re.reshape(2, S5_LANES), ab_im.reshape(2, S5_LANES)], axis=1)
    return bblk, a_tab, cblk


def _ret_tables(decay_logit, c):
    lg = jax.nn.log_sigmoid(decay_logit.astype(F32))
    lf = lg[0][:, None, None]
    lb = lg[1][:, None, None]
    idx = jnp.arange(c, dtype=F32)
    diff = idx[:, None] - idx[None, :]
    dm = jnp.where(diff >= 0, jnp.exp(lf * jnp.maximum(diff, 0.0)), jnp.exp(lb * jnp.maximum(-diff, 0.0)))
    col = lambda e: jnp.broadcast_to(e, (RET_HEADS, c, RET_DIM))
    sq = lambda e: jnp.broadcast_to(e, (RET_HEADS, RET_DIM, RET_DIM))
    i3 = idx[None, :, None]
    qdf = col(jnp.exp(lf * (i3 + 1.0)))
    kdf = col(jnp.exp(lf * (c - 1.0 - i3)))
    cdf = sq(jnp.exp(lf * c))
    qdb = col(jnp.exp(lb * (c - i3)))
    kdb = col(jnp.exp(lb * i3))
    cdb = sq(jnp.exp(lb * c))
    return dm, qdf, kdf, cdf, qdb, kdb, cdb


def _layer(x, mod_l, w, *, nb, t, tm, row_of_batch, ctx, rope_tabs):
    is_ctx_pass = ctx is None
    tiles_per_batch = t // tm
    row_of_tile = lambda i: row_of_batch(i // tiles_per_batch)
    x = _ffn_call(x, mod_l, w['npre'][0], w['npost'][0], *w['ffn'][0], k=0, tm=tm, row_of_tile=row_of_tile)

    nseg = NSEG_PLAIN if is_ctx_pass else NSEG_ROPE
    u_tm, p = _in_call(x, mod_l, w['npre'][1], w['w_in_plain'] if is_ctx_pass else w['w_in_rope'],
                       nb=nb, t=t, tm=tm, nseg=nseg, row_of_batch=row_of_batch)
    p3 = p.reshape(nb, t, nseg * SEG)

    steps = 512 // nb
    if is_ctx_pass:
        s5_0 = jnp.zeros((2, 2, nb, S5_LANES), F32)
        ret_0, mla_ctx, ret_rope, mla_rope = None, None, None, None
    else:
        s5_0, ret_0, mla_ctx = ctx
        ret_rope, mla_rope = rope_tabs
    s5_out = _s5_call(u_tm.reshape(t * nb, SEG), w['s5_bblk'], w['s5_a'], w['s5_cblk'], s5_0,
                      nb=nb, t=t, steps=steps, want_fin=is_ctx_pass)
    ret_out = _ret_call(p3, w['ret_tabs'], w['ret_gn'], ret_0, ret_rope,
                        nb=nb, t=t, tc=256, want_fin=is_ctx_pass)
    mla_out = _mla_call(p3, w['mla_qn'], w['mla_kvn'], w['mla_wq'], w['mla_wqr'], w['mla_wk'], w['mla_wv'],
                        mla_rope, mla_ctx, nb=nb, t=t, tq=256, want_cache=is_ctx_pass)

    x = _merge_call(x, mod_l, w['npre'][1], w['npost'][1], s5_out[0], u_tm, ret_out[0], p3, mla_out[0],
                    w['s5_d'], w['s5_wglu'], w['conv_w'], w['conv_b'], w['w_branch'], w['w_gate'],
                    w['b_gate'], w['w_o'], nb=nb, t=t, tm=tm, row_of_batch=row_of_batch)
    x = _ffn_call(x, mod_l, w['npre'][2], w['npost'][2], *w['ffn'][1], k=2, tm=tm, row_of_tile=row_of_tile)

    state = None
    if is_ctx_pass:
        fin = s5_out[1].reshape(2, 2, nb, S5_GROUPS, S5_STATE).transpose(2, 0, 3, 4, 1)
        kr = p3[:, :, S_KV * SEG + MLA_KV_LORA + MLA_NOPE:S_KV * SEG + MLA_KV_LORA + MLA_NOPE + MLA_ROPE]
        cache = jnp.concatenate([mla_out[1], kr], axis=-1)
        state = (fin, ret_out[1], cache)
    return x, state


def kernel(x_prompt, x_sample, state_s5, state_ret, cache_mla, c, c_ctx, w_mod, b_mod, norm_pre, norm_post, ffn_w1, ffn_w3, ffn_w2, w_in, s5_lam_re, s5_lam_im, s5_log_dt, s5_b_re, s5_b_im, s5_c_re, s5_c_im, s5_d, s5_w_glu, ret_decay, ret_gn, conv_w, conv_b, mla_q_norm, mla_w_uq, mla_kv_norm, mla_w_ukv, w_branch, w_gate, b_gate, w_o):
    bp, tp, _ = x_prompt.shape
    bs, ts, _ = x_sample.shape
    assert 1 + bs <= MOD_ROWS
    past = cache_mla.shape[2]

    cond = jnp.concatenate([c_ctx[None, :], c, jnp.zeros((MOD_ROWS - 1 - bs, D_MODEL), F32)], axis=0)
    mod = _mod_call(cond, w_mod, b_mod).reshape(DEPTH, MOD_ROWS, N_MOD, D_MODEL)

    rope_tabs = _rope_tables(ts)
    xp = x_prompt.reshape(bp * tp, D_MODEL)
    xs = x_sample.reshape(bs * ts, D_MODEL)
    s5_list, ret_list, mla_list = [], [], []
    for l in range(DEPTH):
        ffn = []
        for j in range(2):
            w1c = ffn_w1[l, j].reshape(D_MODEL, N_FF_CHUNKS, FF_CHUNK).transpose(1, 0, 2).astype(BF16)
            w3c = ffn_w3[l, j].reshape(D_MODEL, N_FF_CHUNKS, FF_CHUNK).transpose(1, 0, 2).astype(BF16)
            w2c = ffn_w2[l, j].reshape(N_FF_CHUNKS, FF_CHUNK, D_MODEL).astype(BF16)
            ffn.append((w1c, w3c, w2c))
        wq, wqr, wk, wv = _mla_weights(mla_w_uq[l], mla_w_ukv[l])
        bblk, a_tab, cblk = _s5_tables(s5_lam_re[l], s5_lam_im[l], s5_log_dt[l], s5_b_re[l], s5_b_im[l],
                                       s5_c_re[l], s5_c_im[l])
        w = dict(
            npre=[norm_pre[l, i][None, :] for i in range(3)],
            npost=[norm_post[l, i][None, :] for i in range(3)],
            ffn=ffn,
            w_in_plain=_in_weights(w_in[l], rope=False),
            w_in_rope=_in_weights(w_in[l], rope=True),
            s5_bblk=bblk, s5_a=a_tab, s5_cblk=cblk,
            s5_d=s5_d[l][None, :], s5_wglu=s5_w_glu[l].astype(BF16),
            ret_tabs=_ret_tables(ret_decay[l], 256), ret_gn=ret_gn[l][None, :],
            conv_w=conv_w[l], conv_b=conv_b[l][None, :],
            mla_qn=jnp.concatenate([mla_q_norm[l], jnp.zeros((SEG - MLA_Q_LORA,), F32)])[None, :],
            mla_kvn=mla_kv_norm[l][None, :],
            mla_wq=wq, mla_wqr=wqr, mla_wk=wk, mla_wv=wv,
            w_branch=w_branch[l].astype(BF16), w_gate=w_gate[l].astype(BF16),
            b_gate=b_gate[l][None, :], w_o=w_o[l].astype(BF16),
        )
        xp, (s5_s, ret_s, mla_c) = _layer(xp, mod[l], w, nb=bp, t=tp, tm=256,
                                          row_of_batch=lambda b: 0, ctx=None, rope_tabs=None)
        s5_list.append(s5_s)
        ret_list.append(ret_s)
        mla_list.append(mla_c)

        s5_0 = state_s5[:, l].transpose(1, 4, 0, 2, 3).reshape(2, 2, bs, S5_LANES)
        cache_l = cache_mla[:, l]
        ctx_lat = cache_l[..., :MLA_KV_LORA]
        ctx_krp = jnp.concatenate([jnp.zeros((bs, past, MLA_NOPE), F32), cache_l[..., MLA_KV_LORA:],
                                   jnp.zeros((bs, past, MLA_SLAB - MLA_NOPE - MLA_ROPE), F32)], -1)
        xs, _ = _layer(xs, mod[l], w, nb=bs, t=ts, tm=256, row_of_batch=lambda b: 1 + b,
                       ctx=(s5_0, state_ret[:, l], (ctx_lat, ctx_krp)), rope_tabs=rope_tabs)
    return (xp.reshape(bp, tp, D_MODEL), xs.reshape(bs, ts, D_MODEL),
            jnp.stack(s5_list, axis=1), jnp.stack(ret_list, axis=1), jnp.stack(mla_list, axis=1))
```

```python
import functools

import numpy as np
import jax
import jax.numpy as jnp
from jax import lax
from jax.experimental import pallas as pl
from jax.experimental.pallas import tpu as pltpu

F32 = jnp.float32
BF16 = jnp.bfloat16

D_MODEL = 1024
DEPTH = 2
GRID_W = 64
EPS = 1e-6
ROPE_BASE = 10000.0
N_BRANCH = 4
BRANCH_WIDTH = D_MODEL // 4
N_MOD = 9
D_FF = 2816
S5_GROUP = 16
S5_GROUPS = BRANCH_WIDTH // S5_GROUP
S5_STATE = 64
S5_LANES = S5_GROUPS * S5_STATE
RET_HEADS = 4
RET_DIM = BRANCH_WIDTH // RET_HEADS
CONV_K = 3
MLA_HEADS = 4
MLA_Q_LORA = 192
MLA_KV_LORA = 128
MLA_NOPE = 64
MLA_ROPE = 32
MLA_V = BRANCH_WIDTH // MLA_HEADS
MLA_SLAB = 128

SEG = 256
FF_CHUNK = 256
N_FF_CHUNKS = D_FF // FF_CHUNK
MOD_ROWS = 16
HALO = 8
V7X_VMEM_LIMIT = 56 * 1024 * 1024

S_U, S_RQ, S_RK, S_RV, S_RG, S_CX, S_CB, S_CC, S_CQ, S_KV, S_RQR, S_RKR, S_KRR = range(13)
NSEG_PLAIN = 10
NSEG_ROPE = 13


def _cparams(sem):
    return pltpu.CompilerParams(dimension_semantics=sem, vmem_limit_bytes=V7X_VMEM_LIMIT)


def _dot(a, b):
    return jnp.dot(a.astype(BF16), b.astype(BF16), preferred_element_type=F32)


def _dot_nt(a, b):
    return lax.dot_general(a.astype(BF16), b.astype(BF16), (((1,), (1,)), ((), ())),
                           preferred_element_type=F32)


def _dot_tn(a, b):
    return lax.dot_general(a.astype(BF16), b.astype(BF16), (((0,), (0,)), ((), ())),
                           preferred_element_type=F32)


def _sigmoid(x):
    return 1.0 / (1.0 + jnp.exp(-x))


def _rms(x, g, n=None):
    n = x.shape[-1] if n is None else n
    ms = jnp.sum(x * x, axis=-1, keepdims=True) * (1.0 / n)
    return x * lax.rsqrt(ms + EPS) * g


def _gelu_tanh(x):
    return 0.5 * x * (1.0 + jnp.tanh(np.sqrt(2.0 / np.pi) * (x + 0.044715 * (x * x * x))))


def _const_spec(shape):
    nd = len(shape)
    return pl.BlockSpec(shape, lambda *_: (0,) * nd)


def _resident_spec(shape):
    nd = len(shape)
    return pl.BlockSpec(shape, lambda *_: (0,) * nd, pipeline_mode=pl.Buffered(1))


def _mod_kernel(c_ref, w_ref, b_ref, o_ref):
    c = c_ref[...]
    o_ref[...] = _dot(c * _sigmoid(c), w_ref[...]) + b_ref[...]


def _mod_call(cond, w_mod, b_mod):
    ncol = N_MOD * D_MODEL
    tn = D_MODEL
    return pl.pallas_call(
        _mod_kernel,
        grid=(DEPTH, ncol // tn),
        in_specs=[
            pl.BlockSpec((MOD_ROWS, D_MODEL), lambda l, j: (0, 0)),
            pl.BlockSpec((None, D_MODEL, tn), lambda l, j: (l, 0, j)),
            pl.BlockSpec((None, 1, tn), lambda l, j: (l, 0, j)),
        ],
        out_specs=pl.BlockSpec((None, MOD_ROWS, tn), lambda l, j: (l, 0, j)),
        out_shape=jax.ShapeDtypeStruct((DEPTH, MOD_ROWS, ncol), F32),
        compiler_params=_cparams(("arbitrary", "arbitrary")),
        name="mod",
    )(cond, w_mod, b_mod.reshape(DEPTH, 1, ncol))


def _ffn_kernel(x_ref, mod_ref, npre_ref, npost_ref, w1_ref, w3_ref, w2_ref, o_ref,
                h_scr, g_scr, *, k):
    x = x_ref[...]
    sh = mod_ref[3 * k:3 * k + 1, :]
    sc = mod_ref[3 * k + 1:3 * k + 2, :]
    gt = mod_ref[3 * k + 2:3 * k + 3, :]
    h_scr[...] = (_rms(x, npre_ref[...]) * (1.0 + sc) + sh).astype(BF16)
    for j in range(N_FF_CHUNKS):
        cols = slice(j * FF_CHUNK, (j + 1) * FF_CHUNK)
        a = jnp.dot(h_scr[...], w1_ref[:, cols], preferred_element_type=F32)
        b = jnp.dot(h_scr[...], w3_ref[:, cols], preferred_element_type=F32)
        g_scr[:, cols] = (a * _sigmoid(a) * b).astype(BF16)
    f = jnp.dot(g_scr[...], w2_ref[...], preferred_element_type=F32)
    o_ref[...] = x + (0.5 * gt) * _rms(f, npost_ref[...])


def _ffn_call(x, mod_l, npre, npost, w1, w3, w2, *, k, tm, row_of_tile):
    n = x.shape[0]
    return pl.pallas_call(
        functools.partial(_ffn_kernel, k=k),
        grid=(n // tm,),
        in_specs=[
            pl.BlockSpec((tm, D_MODEL), lambda i: (i, 0)),
            pl.BlockSpec((None, N_MOD, D_MODEL), lambda i: (row_of_tile(i), 0, 0)),
            _const_spec((1, D_MODEL)),
            _const_spec((1, D_MODEL)),
            _resident_spec((D_MODEL, D_FF)),
            _resident_spec((D_MODEL, D_FF)),
            _resident_spec((D_FF, D_MODEL)),
        ],
        out_specs=pl.BlockSpec((tm, D_MODEL), lambda i: (i, 0)),
        out_shape=jax.ShapeDtypeStruct((n, D_MODEL), F32),
        scratch_shapes=[pltpu.VMEM((tm, D_MODEL), BF16), pltpu.VMEM((tm, D_FF), BF16)],
        compiler_params=_cparams(("arbitrary",)),
        name="ffn",
    )(x, mod_l, npre, npost, w1, w3, w2)


def _in_kernel(x_ref, mod_ref, npre_ref, w_ref, p_ref, h_scr, *, nseg):
    sh = mod_ref[3:4, :]
    sc = mod_ref[4:5, :]
    h_scr[...] = (_rms(x_ref[...], npre_ref[...]) * (1.0 + sc) + sh).astype(BF16)
    for j in range(nseg):
        cols = slice(j * SEG, (j + 1) * SEG)
        p_ref[:, cols] = jnp.dot(h_scr[...], w_ref[:, cols], preferred_element_type=F32)


def _in_call(x, mod_l, npre, w_ext, *, nb, t, tm, nseg, row_of_batch):
    tt = t // tm
    return pl.pallas_call(
        functools.partial(_in_kernel, nseg=nseg),
        grid=(nb, tt),
        in_specs=[
            pl.BlockSpec((tm, D_MODEL), lambda b, i: (b * tt + i, 0)),
            pl.BlockSpec((None, N_MOD, D_MODEL), lambda b, i: (row_of_batch(b), 0, 0)),
            _const_spec((1, D_MODEL)),
            _resident_spec((D_MODEL, nseg * SEG)),
        ],
        out_specs=pl.BlockSpec((tm, nseg * SEG), lambda b, i: (b * tt + i, 0)),
        out_shape=jax.ShapeDtypeStruct((nb * t, nseg * SEG), F32),
        scratch_shapes=[pltpu.VMEM((tm, D_MODEL), BF16)],
        compiler_params=_cparams(("arbitrary", "arbitrary")),
        name="in_proj",
    )(x, mod_l, npre, w_ext)


def _s5_kernel(*refs, nb, steps, want_fin):
    if want_fin:
        u_ref, perm_ref, permt_ref, bb_ref, a_ref, cc_ref, s0_ref, y_ref, fin_ref, s_scr, bu_scr = refs
    else:
        u_ref, perm_ref, permt_ref, bb_ref, a_ref, cc_ref, s0_ref, y_ref, s_scr, bu_scr = refs
        fin_ref = None
    d = pl.program_id(0)
    c = pl.program_id(1)
    rows = nb * steps

    @pl.when(c == 0)
    def _():
        s_scr[...] = s0_ref[...]

    u_bt = u_ref[...].reshape(rows, SEG).astype(BF16)
    u_tb = jnp.dot(perm_ref[...], u_bt, preferred_element_type=F32).astype(BF16)
    bu_scr[...] = jnp.dot(u_tb, bb_ref[...], preferred_element_type=F32)
    a_re = jnp.broadcast_to(a_ref[0:1, :], (nb, S5_LANES))
    a_im = jnp.broadcast_to(a_ref[1:2, :], (nb, S5_LANES))

    def step(i, carry):
        s_re, s_im = carry
        t = jnp.where(d == 0, i, steps - 1 - i)
        r0 = pl.multiple_of(t * nb, nb)
        n_re = a_re * s_re - a_im * s_im + bu_scr[pl.ds(r0, nb), 0:S5_LANES]
        n_im = a_re * s_im + a_im * s_re + bu_scr[pl.ds(r0, nb), S5_LANES:2 * S5_LANES]
        bu_scr[pl.ds(r0, nb), 0:S5_LANES] = n_re
        bu_scr[pl.ds(r0, nb), S5_LANES:2 * S5_LANES] = n_im
        return n_re, n_im

    s_re, s_im = lax.fori_loop(0, steps, step, (s_scr[0], s_scr[1]), unroll=4)
    s_scr[0] = s_re
    s_scr[1] = s_im
    y_tb = _dot(bu_scr[...], cc_ref[...])
    hi = y_tb.astype(BF16)
    lo = (y_tb - hi.astype(F32)).astype(BF16)
    y_bt = (jnp.dot(permt_ref[...], hi, preferred_element_type=F32)
            + jnp.dot(permt_ref[...], lo, preferred_element_type=F32))
    y_ref[...] = y_bt.reshape(nb, steps, SEG)
    if want_fin:
        fin_ref[0] = s_re
        fin_ref[1] = s_im


def _s5_call(p3, bblk, a_tab, cblk, s0, *, nb, t, steps, want_fin):
    nch = t // steps
    rows = steps * nb
    r = np.arange(rows)
    perm_np = np.zeros((rows, rows), np.float32)
    perm_np[r, (r % nb) * steps + r // nb] = 1.0
    perm = jnp.asarray(perm_np, BF16)
    permt = jnp.asarray(perm_np.T, BF16)

    def chunk(d, c):
        return jnp.where(d == 0, c, nch - 1 - c)

    out_specs = [pl.BlockSpec((None, nb, steps, SEG), lambda d, c: (d, 0, chunk(d, c), 0))]
    out_shape = [jax.ShapeDtypeStruct((2, nb, t, SEG), F32)]
    if want_fin:
        out_specs.append(pl.BlockSpec((None, 2, nb, S5_LANES), lambda d, c: (d, 0, 0, 0)))
        out_shape.append(jax.ShapeDtypeStruct((2, 2, nb, S5_LANES), F32))
    return pl.pallas_call(
        functools.partial(_s5_kernel, nb=nb, steps=steps, want_fin=want_fin),
        grid=(2, nch),
        in_specs=[
            pl.BlockSpec((nb, steps, SEG), lambda d, c: (0, chunk(d, c), S_U)),
            _const_spec((rows, rows)),
            _const_spec((rows, rows)),
            pl.BlockSpec((None, SEG, 2 * S5_LANES), lambda d, c: (d, 0, 0)),
            pl.BlockSpec((None, 2, S5_LANES), lambda d, c: (d, 0, 0)),
            _const_spec((2 * S5_LANES, SEG)),
            pl.BlockSpec((None, 2, nb, S5_LANES), lambda d, c: (d, 0, 0, 0)),
        ],
        out_specs=out_specs,
        out_shape=out_shape,
        scratch_shapes=[pltpu.VMEM((2, nb, S5_LANES), F32), pltpu.VMEM((rows, 2 * S5_LANES), F32)],
        compiler_params=_cparams(("arbitrary", "arbitrary")),
        name="s5_scan",
    )(p3, perm, permt, bblk, a_tab, cblk, s0)


def _ret_kernel(*refs, rope, has_s0, want_fin, nch):
    refs = list(refs)
    q_ref, k_ref, v_ref, g_ref = refs[:4]
    del refs[:4]
    if rope:
        qr_ref, kr_ref, cos_ref, sin_ref = refs[:4]
        del refs[:4]
    dm_ref, qdf_ref, kdf_ref, cdf_ref, qdb_ref, kdb_ref, cdb_ref, gn_ref = refs[:8]
    del refs[:8]
    if has_s0:
        s0_ref = refs.pop(0)
    o_ref = refs.pop(0)
    fin_ref = refs.pop(0) if want_fin else None
    sb_all, sf_scr, sb_scr = refs

    p = pl.program_id(1)
    c = pl.program_id(2)
    hd = RET_DIM

    def roped_k():
        k = k_ref[...]
        if rope:
            k = k * cos_ref[...] + kr_ref[...] * sin_ref[...]
        return k

    @pl.when(p == 0)
    def _():
        ck = nch - 1 - c

        @pl.when(c == 0)
        def _():
            if has_s0:
                sb_scr[...] = s0_ref[1]
            else:
                sb_scr[...] = jnp.zeros_like(sb_scr)

        k = roped_k()
        v = v_ref[...]
        for h in range(RET_HEADS):
            sl = slice(h * hd, (h + 1) * hd)
            s_old = sb_scr[h]
            sb_all[ck, h] = s_old
            sb_scr[h] = s_old * cdb_ref[h] + _dot_tn(k[:, sl] * kdb_ref[h], v[:, sl])
        if want_fin:
            @pl.when(c == nch - 1)
            def _():
                fin_ref[1] = sb_scr[...]

    @pl.when(p == 1)
    def _():
        @pl.when(c == 0)
        def _():
            if has_s0:
                sf_scr[...] = s0_ref[0]
            else:
                sf_scr[...] = jnp.zeros_like(sf_scr)

        q = q_ref[...]
        if rope:
            q = q * cos_ref[...] + qr_ref[...] * sin_ref[...]
        k = roped_k()
        v = v_ref[...]
        outs = []
        for h in range(RET_HEADS):
            sl = slice(h * hd, (h + 1) * hd)
            qh, kh, vh = q[:, sl], k[:, sl], v[:, sl]
            att = _dot_nt(qh, kh) * dm_ref[h]
            s_f = sf_scr[h]
            o = (_dot(att, vh) + _dot(qh, s_f) * qdf_ref[h] + _dot(qh, sb_all[c, h]) * qdb_ref[h])
            sf_scr[h] = s_f * cdf_ref[h] + _dot_tn(kh * kdf_ref[h], vh)
            mu = jnp.mean(o, axis=-1, keepdims=True)
            oc = o - mu
            var = jnp.mean(oc * oc, axis=-1, keepdims=True)
            outs.append(oc * lax.rsqrt(var + EPS))
        on = jnp.concatenate(outs, axis=-1) * gn_ref[...]
        g = g_ref[...]
        o_ref[...] = (g * _sigmoid(g)) * on
        if want_fin:
            @pl.when(c == nch - 1)
            def _():
                fin_ref[0] = sf_scr[...]


def _ret_call(p3, tabs, gn, s0, rope_tabs, *, nb, t, tc, want_fin):
    nch = t // tc
    rope = rope_tabs is not None
    has_s0 = s0 is not None

    def kv_chunk(p, c):
        return jnp.where(p == 0, nch - 1 - c, c)

    def q_chunk(p, c):
        return jnp.where(p == 0, 0, c)

    def seg_spec(seg, cmap):
        return pl.BlockSpec((None, tc, SEG), lambda b, p, c: (b, cmap(p, c), seg))

    args = [p3, p3, p3, p3]
    in_specs = [seg_spec(S_RQ, q_chunk), seg_spec(S_RK, kv_chunk), seg_spec(S_RV, kv_chunk),
                seg_spec(S_RG, q_chunk)]
    if rope:
        args += [p3, p3, rope_tabs[0], rope_tabs[1]]
        in_specs += [seg_spec(S_RQR, q_chunk), seg_spec(S_RKR, kv_chunk),
                     pl.BlockSpec((tc, SEG), lambda b, p, c: (kv_chunk(p, c), 0)),
                     pl.BlockSpec((tc, SEG), lambda b, p, c: (kv_chunk(p, c), 0))]
    args += list(tabs) + [gn]
    in_specs += [_const_spec(a.shape) for a in tabs] + [_const_spec((1, SEG))]
    if has_s0:
        args.append(s0)
        in_specs.append(pl.BlockSpec((None, 2, RET_HEADS, RET_DIM, RET_DIM),
                                     lambda b, p, c: (b, 0, 0, 0, 0)))
    out_specs = [pl.BlockSpec((None, tc, SEG), lambda b, p, c: (b, q_chunk(p, c), 0))]
    out_shape = [jax.ShapeDtypeStruct((nb, t, SEG), F32)]
    if want_fin:
        out_specs.append(pl.BlockSpec((None, 2, RET_HEADS, RET_DIM, RET_DIM),
                                      lambda b, p, c: (b, 0, 0, 0, 0)))
        out_shape.append(jax.ShapeDtypeStruct((nb, 2, RET_HEADS, RET_DIM, RET_DIM), F32))
    return pl.pallas_call(
        functools.partial(_ret_kernel, rope=rope, has_s0=has_s0, want_fin=want_fin, nch=nch),
        grid=(nb, 2, nch),
        in_specs=in_specs,
        out_specs=out_specs,
        out_shape=out_shape,
        scratch_shapes=[pltpu.VMEM((nch, RET_HEADS, RET_DIM, RET_DIM), F32),
                        pltpu.VMEM((RET_HEADS, RET_DIM, RET_DIM), F32),
                        pltpu.VMEM((RET_HEADS, RET_DIM, RET_DIM), F32)],
        compiler_params=_cparams(("arbitrary", "arbitrary", "arbitrary")),
        name="retention",
    )(*args)


def _mla_kernel(*refs, t, tq, nctx, want_cache):
    refs = list(refs)
    cq_ref, ckv_ref, krp_ref = refs[:3]
    del refs[:3]
    has_ctx = nctx > 0
    if has_ctx:
        krr_ref, cm_ref, sm_ref, clat_ref, ckrp_ref = refs[:5]
        del refs[:5]
    qn_ref, kvn_ref, wq_ref = refs[:3]
    del refs[:3]
    if has_ctx:
        wqr_ref = refs.pop(0)
    wk_ref, wv_ref = refs[:2]
    del refs[:2]
    o_ref = refs.pop(0)
    lat_ref = refs.pop(0) if want_cache else None
    k_scr, v_scr = refs
    qi = pl.program_id(1)

    @pl.when(qi == 0)
    def _():
        lat = _rms(ckv_ref[...], kvn_ref[...])
        if want_cache:
            lat_ref[...] = lat
        latb = lat.astype(BF16)
        kr = krp_ref[...]
        if has_ctx:
            kr = kr * cm_ref[...] + krr_ref[...] * sm_ref[...]
        kk = jnp.dot(latb, wk_ref[...], preferred_element_type=F32)
        for h in range(MLA_HEADS):
            k_scr[h, 0:t, :] = (kk[:, h * MLA_SLAB:(h + 1) * MLA_SLAB] + kr).astype(BF16)
        v_scr[0:t, :] = jnp.dot(latb, wv_ref[...], preferred_element_type=F32).astype(BF16)
        if has_ctx:
            latc = clat_ref[...].astype(BF16)
            kkc = jnp.dot(latc, wk_ref[...], preferred_element_type=F32)
            krc = ckrp_ref[...]
            for h in range(MLA_HEADS):
                k_scr[h, t:t + nctx, :] = (kkc[:, h * MLA_SLAB:(h + 1) * MLA_SLAB] + krc).astype(BF16)
            v_scr[t:t + nctx, :] = jnp.dot(latc, wv_ref[...], preferred_element_type=F32).astype(BF16)

    a = _rms(cq_ref[...], qn_ref[...], n=MLA_Q_LORA).astype(BF16)
    q = jnp.dot(a, wq_ref[...], preferred_element_type=F32)
    if has_ctx:
        qrot = jnp.dot(a, wqr_ref[...], preferred_element_type=F32)
        r0 = pl.multiple_of(qi * tq, tq)
        cm = cm_ref[pl.ds(r0, tq), :]
        sm = sm_ref[pl.ds(r0, tq), :]
    scale = (MLA_NOPE + MLA_ROPE) ** -0.5
    lane = lax.broadcasted_iota(jnp.int32, (tq, SEG), 1)
    acc = jnp.zeros((tq, SEG), F32)
    for h in range(MLA_HEADS):
        qh = q[:, h * MLA_SLAB:(h + 1) * MLA_SLAB]
        if has_ctx:
            qh = qh * cm + qrot[:, h * MLA_SLAB:(h + 1) * MLA_SLAB] * sm
        s = _dot_nt(qh * scale, k_scr[h])
        m = jnp.max(s, axis=-1, keepdims=True)
        e = jnp.exp(s - m)
        l = jnp.sum(e, axis=-1, keepdims=True)
        oh = jnp.dot(e.astype(BF16), v_scr[...], preferred_element_type=F32) * (1.0 / l)
        acc = jnp.where((lane >= h * MLA_V) & (lane < (h + 1) * MLA_V), oh, acc)
    o_ref[...] = acc


def _mla_call(p3, qn, kvn, wq, wqr, wk, wv, rope_tabs, ctx, *, nb, t, tq, want_cache):
    has_ctx = ctx is not None
    nctx = ctx[0].shape[1] if has_ctx else 0
    s_tot = t + nctx
    half = MLA_KV_LORA

    args = [p3, p3, p3]
    in_specs = [
        pl.BlockSpec((None, tq, SEG), lambda b, i: (b, i, S_CQ)),
        pl.BlockSpec((None, t, half), lambda b, i: (b, 0, 2 * S_KV)),
        pl.BlockSpec((None, t, half), lambda b, i: (b, 0, 2 * S_KV + 1)),
    ]
    if has_ctx:
        args += [p3, rope_tabs[0], rope_tabs[1], ctx[0], ctx[1]]
        in_specs += [
            pl.BlockSpec((None, t, half), lambda b, i: (b, 0, 2 * S_KRR)),
            _const_spec((t, MLA_SLAB)),
            _const_spec((t, MLA_SLAB)),
            pl.BlockSpec((None, nctx, MLA_KV_LORA), lambda b, i: (b, 0, 0)),
            pl.BlockSpec((None, nctx, MLA_SLAB), lambda b, i: (b, 0, 0)),
        ]
    args += [qn, kvn, wq]
    in_specs += [_const_spec(qn.shape), _const_spec(kvn.shape), _const_spec(wq.shape)]
    if has_ctx:
        args.append(wqr)
        in_specs.append(_const_spec(wqr.shape))
    args += [wk, wv]
    in_specs += [_const_spec(wk.shape), _const_spec(wv.shape)]
    out_specs = [pl.BlockSpec((None, tq, SEG), lambda b, i: (b, i, 0))]
    out_shape = [jax.ShapeDtypeStruct((nb, t, SEG), F32)]
    if want_cache:
        out_specs.append(pl.BlockSpec((None, t, MLA_KV_LORA), lambda b, i: (b, 0, 0)))
        out_shape.append(jax.ShapeDtypeStruct((nb, t, MLA_KV_LORA), F32))
    return pl.pallas_call(
        functools.partial(_mla_kernel, t=t, tq=tq, nctx=nctx, want_cache=want_cache),
        grid=(nb, t // tq),
        in_specs=in_specs,
        out_specs=out_specs,
        out_shape=out_shape,
        scratch_shapes=[pltpu.VMEM((MLA_HEADS, s_tot, MLA_SLAB), BF16),
                        pltpu.VMEM((s_tot, SEG), BF16)],
        compiler_params=_cparams(("arbitrary", "arbitrary")),
        name="mla",
    )(*args)


def _merge_kernel(x_ref, mod_ref, npre_ref, npost_ref, yf_ref, yb_ref, u_ref, ret_ref,
                  cx_ref, cb_ref, cc_ref, cxp_ref, ccp_ref, cxn_ref, ccn_ref, mla_ref,
                  s5d_ref, wglu_ref, cw_ref, cbias_ref, wbr_ref, wg_ref, bg_ref, wo_ref,
                  o_ref, h_scr, *, tm, nt):
    i = pl.program_id(1)
    x = x_ref[...]
    sh = mod_ref[3:4, :]
    sc = mod_ref[4:5, :]
    gt = mod_ref[5:6, :]
    h_scr[...] = (_rms(x, npre_ref[...]) * (1.0 + sc) + sh).astype(BF16)

    y = yf_ref[...] + yb_ref[...] + s5d_ref[...] * u_ref[...]
    z5 = _gelu_tanh(y)
    b_s5 = z5 * _sigmoid(_dot(z5, wglu_ref[...]))

    z = cc_ref[...] * cx_ref[...]
    zprev = ccp_ref[HALO - 1:HALO, :] * cxp_ref[HALO - 1:HALO, :] * jnp.where(i > 0, 1.0, 0.0)
    znext = ccn_ref[0:1, :] * cxn_ref[0:1, :] * jnp.where(i < nt - 1, 1.0, 0.0)
    row = lax.broadcasted_iota(jnp.int32, (tm, SEG), 0)
    z_dn = jnp.where(row == 0, zprev, pltpu.roll(z, 1, axis=0))
    z_up = jnp.where(row == tm - 1, znext, pltpu.roll(z, tm - 1, axis=0))
    b_conv = cb_ref[...] * (cbias_ref[...] + z_dn * cw_ref[0:1, :] + z * cw_ref[1:2, :]
                            + z_up * cw_ref[2:3, :])

    branches = (b_s5, ret_ref[...], b_conv, mla_ref[...])
    merged = jnp.zeros((tm, D_MODEL), F32)
    for n in range(N_BRANCH):
        gate = _sigmoid(jnp.dot(h_scr[...], wg_ref[:, n * D_MODEL:(n + 1) * D_MODEL],
                                preferred_element_type=F32) + bg_ref[:, n * D_MODEL:(n + 1) * D_MODEL])
        merged = merged + gate * _dot(branches[n], wbr_ref[n])
    m = _dot(merged, wo_ref[...])
    o_ref[...] = x + gt * _rms(m, npost_ref[...])


def _merge_call(x, mod_l, npre, npost, y_s5, ret_o, p3, mla_o, s5d, wglu, cw, cbias, wbr, wg, bg, wo,
                *, nb, t, tm, row_of_batch):
    nt = t // tm
    hb = tm // HALO
    n_halo = t // HALO

    def dir_spec(d):
        return pl.BlockSpec((None, None, tm, SEG), lambda b, i: (d, b, i, 0))

    def seg_spec(seg):
        return pl.BlockSpec((None, tm, SEG), lambda b, i: (b, i, seg))

    def prev_spec(seg):
        return pl.BlockSpec((None, HALO, SEG), lambda b, i: (b, jnp.maximum(i * hb - 1, 0), seg))

    def next_spec(seg):
        return pl.BlockSpec((None, HALO, SEG), lambda b, i: (b, jnp.minimum((i + 1) * hb, n_halo - 1), seg))

    return pl.pallas_call(
        functools.partial(_merge_kernel, tm=tm, nt=nt),
        grid=(nb, nt),
        in_specs=[
            pl.BlockSpec((tm, D_MODEL), lambda b, i: (b * nt + i, 0)),
            pl.BlockSpec((None, N_MOD, D_MODEL), lambda b, i: (row_of_batch(b), 0, 0)),
            _const_spec((1, D_MODEL)),
            _const_spec((1, D_MODEL)),
            dir_spec(0), dir_spec(1), seg_spec(S_U),
            pl.BlockSpec((None, tm, SEG), lambda b, i: (b, i, 0)),
            seg_spec(S_CX), seg_spec(S_CB), seg_spec(S_CC),
            prev_spec(S_CX), prev_spec(S_CC), next_spec(S_CX), next_spec(S_CC),
            pl.BlockSpec((None, tm, SEG), lambda b, i: (b, i, 0)),
            _const_spec(s5d.shape), _const_spec(wglu.shape), _const_spec(cw.shape),
            _const_spec(cbias.shape), _resident_spec(wbr.shape), _resident_spec(wg.shape),
            _const_spec(bg.shape), _resident_spec(wo.shape),
        ],
        out_specs=pl.BlockSpec((tm, D_MODEL), lambda b, i: (b * nt + i, 0)),
        out_shape=jax.ShapeDtypeStruct((nb * t, D_MODEL), F32),
        scratch_shapes=[pltpu.VMEM((tm, D_MODEL), BF16)],
        compiler_params=_cparams(("arbitrary", "arbitrary")),
        name="merge",
    )(x, mod_l, npre, npost, y_s5, y_s5, p3, ret_o, p3, p3, p3, p3, p3, p3, p3, mla_o,
      s5d, wglu, cw, cbias, wbr, wg, bg, wo)


def _rot_half(w, heads, dim):
    w4 = w.reshape(w.shape[0], heads, dim)
    half = dim // 2
    return jnp.concatenate([-w4[..., half:], w4[..., :half]], axis=-1).reshape(w.shape[0], heads * dim)


def _axial_angles(t, dim):
    rows = t // GRID_W
    row = jnp.repeat(jnp.arange(rows, dtype=F32), GRID_W)
    col = jnp.tile(jnp.arange(GRID_W, dtype=F32), rows)
    quarter = dim // 4
    inv = ROPE_BASE ** (-jnp.arange(quarter, dtype=F32) / quarter)
    ang = jnp.concatenate([row[:, None] * inv, col[:, None] * inv], axis=-1)
    return jnp.cos(ang), jnp.sin(ang)


def _rope_tables(t):
    cos, sin = _axial_angles(t, RET_DIM)
    ret_cos = jnp.tile(jnp.concatenate([cos, cos], -1), (1, RET_HEADS))
    ret_sin = jnp.tile(jnp.concatenate([sin, sin], -1), (1, RET_HEADS))
    cos, sin = _axial_angles(t, MLA_ROPE)
    pad = MLA_SLAB - MLA_NOPE - MLA_ROPE
    mla_cos = jnp.concatenate([jnp.ones((t, MLA_NOPE), F32), cos, cos, jnp.zeros((t, pad), F32)], -1)
    mla_sin = jnp.concatenate([jnp.zeros((t, MLA_NOPE), F32), sin, sin, jnp.zeros((t, pad), F32)], -1)
    return (ret_cos, ret_sin), (mla_cos, mla_sin)


def _in_weights(w_in, rope):
    sizes = (BRANCH_WIDTH,) * 8 + (MLA_Q_LORA, MLA_KV_LORA, MLA_ROPE)
    offs = np.cumsum((0,) + sizes)
    u, rq, rk, rv, rg, cx, cb, cc, cq, ckv, kr = [w_in[:, offs[i]:offs[i + 1]] for i in range(11)]
    rk = rk * (RET_DIM ** -0.5)
    d = w_in.shape[0]

    def place_kr(w):
        return jnp.concatenate([jnp.zeros((d, MLA_NOPE), F32), w,
                                jnp.zeros((d, MLA_SLAB - MLA_NOPE - MLA_ROPE), F32)], -1)

    cols = [u, rq, rk, rv, rg, cx, cb, cc,
            jnp.concatenate([cq, jnp.zeros((d, SEG - MLA_Q_LORA), F32)], -1),
            jnp.concatenate([ckv, place_kr(kr)], -1)]
    if rope:
        cols += [_rot_half(rq, RET_HEADS, RET_DIM), _rot_half(rk, RET_HEADS, RET_DIM),
                 jnp.concatenate([place_kr(_rot_half(kr, 1, MLA_ROPE)), jnp.zeros((d, SEG - MLA_SLAB), F32)], -1)]
    return jnp.concatenate(cols, axis=-1).astype(BF16)


def _mla_weights(w_uq, w_ukv):
    dq = MLA_NOPE + MLA_ROPE
    w4 = w_uq.reshape(MLA_Q_LORA, MLA_HEADS, dq)
    nope, ropew = w4[..., :MLA_NOPE], w4[..., MLA_NOPE:]
    half = MLA_ROPE // 2
    rot = jnp.concatenate([-ropew[..., half:], ropew[..., :half]], -1)
    zpad = jnp.zeros((MLA_Q_LORA, MLA_HEADS, MLA_SLAB - dq), F32)
    wq = jnp.concatenate([nope, ropew, zpad], -1).reshape(MLA_Q_LORA, MLA_HEADS * MLA_SLAB)
    wqr = jnp.concatenate([jnp.zeros_like(nope), rot, zpad], -1).reshape(MLA_Q_LORA, MLA_HEADS * MLA_SLAB)
    rpad = jnp.zeros((SEG - MLA_Q_LORA, MLA_HEADS * MLA_SLAB), F32)
    wq = jnp.concatenate([wq, rpad], 0).astype(BF16)
    wqr = jnp.concatenate([wqr, rpad], 0).astype(BF16)
    kv4 = w_ukv.reshape(MLA_KV_LORA, MLA_HEADS, MLA_NOPE + MLA_V)
    wk = jnp.concatenate([kv4[..., :MLA_NOPE], jnp.zeros((MLA_KV_LORA, MLA_HEADS, MLA_SLAB - MLA_NOPE), F32)],
                         -1).reshape(MLA_KV_LORA, MLA_HEADS * MLA_SLAB).astype(BF16)
    wv = kv4[..., MLA_NOPE:].reshape(MLA_KV_LORA, MLA_HEADS * MLA_V).astype(BF16)
    return wq, wqr, wk, wv


def _s5_tables(lam_re, lam_im, log_dt, b_re, b_im, c_re, c_im):
    dt = jnp.exp(log_dt)[..., None]
    mag = jnp.exp(lam_re * dt)
    ab_re = mag * jnp.cos(lam_im * dt)
    ab_im = mag * jnp.sin(lam_im * dt)
    den = lam_re * lam_re + lam_im * lam_im
    f_re = ((ab_re - 1.0) * lam_re + ab_im * lam_im) / den
    f_im = (ab_im * lam_re - (ab_re - 1.0) * lam_im) / den
    bb_re = f_re[..., None] * b_re - f_im[..., None] * b_im
    bb_im = f_re[..., None] * b_im + f_im[..., None] * b_re
    eye = jnp.eye(S5_GROUPS, dtype=F32)
    blk_in = lambda w: jnp.einsum('dgph,gk->dghkp', w, eye).reshape(2, BRANCH_WIDTH, S5_LANES)
    bblk = jnp.concatenate([blk_in(bb_re), blk_in(bb_im)], -1).astype(BF16)
    blk_out = lambda w: jnp.einsum('ghp,gk->gpkh', w, eye).reshape(S5_LANES, BRANCH_WIDTH)
    cblk = jnp.concatenate([blk_out(c_re), -blk_out(c_im)], 0).astype(BF16)
    a_tab = jnp.stack([ab_re.reshape(2, S5_LANES), ab_im.reshape(2, S5_LANES)], axis=1)
    return bblk, a_tab, cblk


def _ret_tables(decay_logit, c):
    lg = jax.nn.log_sigmoid(decay_logit.astype(F32))
    lf = lg[0][:, None, None]
    lb = lg[1][:, None, None]
    idx = jnp.arange(c, dtype=F32)
    diff = idx[:, None] - idx[None, :]
    dm = jnp.where(diff >= 0, jnp.exp(lf * jnp.maximum(diff, 0.0)), jnp.exp(lb * jnp.maximum(-diff, 0.0)))
    col = lambda e: jnp.broadcast_to(e, (RET_HEADS, c, RET_DIM))
    sq = lambda e: jnp.broadcast_to(e, (RET_HEADS, RET_DIM, RET_DIM))
    i3 = idx[None, :, None]
    qdf = col(jnp.exp(lf * (i3 + 1.0)))
    kdf = col(jnp.exp(lf * (c - 1.0 - i3)))
    cdf = sq(jnp.exp(lf * c))
    qdb = col(jnp.exp(lb * (c - i3)))
    kdb = col(jnp.exp(lb * i3))
    cdb = sq(jnp.exp(lb * c))
    return dm, qdf, kdf, cdf, qdb, kdb, cdb


def _layer(x, mod_l, w, *, nb, t, tm, tm_ffn, row_of_batch, ctx, rope_tabs):
    is_ctx_pass = ctx is None
    row_of_tile = lambda i: row_of_batch((i * tm_ffn) // t)
    x = _ffn_call(x, mod_l, w['npre'][0], w['npost'][0], *w['ffn'][0], k=0, tm=tm_ffn, row_of_tile=row_of_tile)

    nseg = NSEG_PLAIN if is_ctx_pass else NSEG_ROPE
    p = _in_call(x, mod_l, w['npre'][1], w['w_in_plain'] if is_ctx_pass else w['w_in_rope'],
                 nb=nb, t=t, tm=tm, nseg=nseg, row_of_batch=row_of_batch)
    p3 = p.reshape(nb, t, nseg * SEG)

    steps = 512 // nb
    if is_ctx_pass:
        s5_0 = jnp.zeros((2, 2, nb, S5_LANES), F32)
        ret_0, mla_ctx, ret_rope, mla_rope = None, None, None, None
    else:
        s5_0, ret_0, mla_ctx = ctx
        ret_rope, mla_rope = rope_tabs
    s5_out = _s5_call(p3, w['s5_bblk'], w['s5_a'], w['s5_cblk'], s5_0,
                      nb=nb, t=t, steps=steps, want_fin=is_ctx_pass)
    ret_out = _ret_call(p3, w['ret_tabs'], w['ret_gn'], ret_0, ret_rope,
                        nb=nb, t=t, tc=256, want_fin=is_ctx_pass)
    mla_out = _mla_call(p3, w['mla_qn'], w['mla_kvn'], w['mla_wq'], w['mla_wqr'], w['mla_wk'], w['mla_wv'],
                        mla_rope, mla_ctx, nb=nb, t=t, tq=256, want_cache=is_ctx_pass)

    x = _merge_call(x, mod_l, w['npre'][1], w['npost'][1], s5_out[0], ret_out[0], p3, mla_out[0],
                    w['s5_d'], w['s5_wglu'], w['conv_w'], w['conv_b'], w['w_branch'], w['w_gate'],
                    w['b_gate'], w['w_o'], nb=nb, t=t, tm=tm, row_of_batch=row_of_batch)
    x = _ffn_call(x, mod_l, w['npre'][2], w['npost'][2], *w['ffn'][1], k=2, tm=tm_ffn, row_of_tile=row_of_tile)

    state = None
    if is_ctx_pass:
        fin = s5_out[1].reshape(2, 2, nb, S5_GROUPS, S5_STATE).transpose(2, 0, 3, 4, 1)
        kr = p3[:, :, S_KV * SEG + MLA_KV_LORA + MLA_NOPE:S_KV * SEG + MLA_KV_LORA + MLA_NOPE + MLA_ROPE]
        cache = jnp.concatenate([mla_out[1], kr], axis=-1)
        state = (fin, ret_out[1], cache)
    return x, state


def kernel(x_prompt, x_sample, state_s5, state_ret, cache_mla, c, c_ctx, w_mod, b_mod, norm_pre, norm_post, ffn_w1, ffn_w3, ffn_w2, w_in, s5_lam_re, s5_lam_im, s5_log_dt, s5_b_re, s5_b_im, s5_c_re, s5_c_im, s5_d, s5_w_glu, ret_decay, ret_gn, conv_w, conv_b, mla_q_norm, mla_w_uq, mla_kv_norm, mla_w_ukv, w_branch, w_gate, b_gate, w_o):
    bp, tp, _ = x_prompt.shape
    bs, ts, _ = x_sample.shape
    assert 1 + bs <= MOD_ROWS
    past = cache_mla.shape[2]

    cond = jnp.concatenate([c_ctx[None, :], c, jnp.zeros((MOD_ROWS - 1 - bs, D_MODEL), F32)], axis=0)
    mod = _mod_call(cond, w_mod, b_mod).reshape(DEPTH, MOD_ROWS, N_MOD, D_MODEL)

    rope_tabs = _rope_tables(ts)
    xp = x_prompt.reshape(bp * tp, D_MODEL)
    xs = x_sample.reshape(bs * ts, D_MODEL)
    s5_list, ret_list, mla_list = [], [], []
    for l in range(DEPTH):
        ffn = [(ffn_w1[l, j].astype(BF16), ffn_w3[l, j].astype(BF16), ffn_w2[l, j].astype(BF16))
               for j in range(2)]
        wq, wqr, wk, wv = _mla_weights(mla_w_uq[l], mla_w_ukv[l])
        bblk, a_tab, cblk = _s5_tables(s5_lam_re[l], s5_lam_im[l], s5_log_dt[l], s5_b_re[l], s5_b_im[l],
                                       s5_c_re[l], s5_c_im[l])
        w = dict(
            npre=[norm_pre[l, i][None, :] for i in range(3)],
            npost=[norm_post[l, i][None, :] for i in range(3)],
            ffn=ffn,
            w_in_plain=_in_weights(w_in[l], rope=False),
            w_in_rope=_in_weights(w_in[l], rope=True),
            s5_bblk=bblk, s5_a=a_tab, s5_cblk=cblk,
            s5_d=s5_d[l][None, :], s5_wglu=s5_w_glu[l].astype(BF16),
            ret_tabs=_ret_tables(ret_decay[l], 256), ret_gn=ret_gn[l][None, :],
            conv_w=conv_w[l], conv_b=conv_b[l][None, :],
            mla_qn=jnp.concatenate([mla_q_norm[l], jnp.zeros((SEG - MLA_Q_LORA,), F32)])[None, :],
            mla_kvn=mla_kv_norm[l][None, :],
            mla_wq=wq, mla_wqr=wqr, mla_wk=wk, mla_wv=wv,
            w_branch=w_branch[l].astype(BF16), w_gate=w_gate[l].astype(BF16),
            b_gate=b_gate[l][None, :], w_o=w_o[l].astype(BF16),
        )
        xp, (s5_s, ret_s, mla_c) = _layer(xp, mod[l], w, nb=bp, t=tp, tm=256, tm_ffn=512,
                                          row_of_batch=lambda b: 0, ctx=None, rope_tabs=None)
        s5_list.append(s5_s)
        ret_list.append(ret_s)
        mla_list.append(mla_c)

        s5_0 = state_s5[:, l].transpose(1, 4, 0, 2, 3).reshape(2, 2, bs, S5_LANES)
        cache_l = cache_mla[:, l]
        ctx_lat = cache_l[..., :MLA_KV_LORA]
        ctx_krp = jnp.concatenate([jnp.zeros((bs, past, MLA_NOPE), F32), cache_l[..., MLA_KV_LORA:],
                                   jnp.zeros((bs, past, MLA_SLAB - MLA_NOPE - MLA_ROPE), F32)], -1)
        xs, _ = _layer(xs, mod[l], w, nb=bs, t=ts, tm=512, tm_ffn=512, row_of_batch=lambda b: 1 + b,
                       ctx=(s5_0, state_ret[:, l], (ctx_lat, ctx_krp)), rope_tabs=rope_tabs)
    return (xp.reshape(bp, tp, D_MODEL), xs.reshape(bs, ts, D_MODEL),
            jnp.stack(s5_list, axis=1), jnp.stack(ret_list, axis=1), jnp.stack(mla_list, axis=1))
```

```python
import functools

import numpy as np
import jax
import jax.numpy as jnp
from jax import lax
from jax.experimental import pallas as pl
from jax.experimental.pallas import tpu as pltpu

F32 = jnp.float32
BF16 = jnp.bfloat16

D_MODEL = 1024
DEPTH = 2
GRID_W = 64
EPS = 1e-6
ROPE_BASE = 10000.0
N_BRANCH = 4
BRANCH_WIDTH = D_MODEL // 4
N_MOD = 9
D_FF = 2816
S5_GROUP = 16
S5_GROUPS = BRANCH_WIDTH // S5_GROUP
S5_STATE = 64
S5_LANES = S5_GROUPS * S5_STATE
RET_HEADS = 4
RET_DIM = BRANCH_WIDTH // RET_HEADS
CONV_K = 3
MLA_HEADS = 4
MLA_Q_LORA = 192
MLA_KV_LORA = 128
MLA_NOPE = 64
MLA_ROPE = 32
MLA_V = BRANCH_WIDTH // MLA_HEADS
MLA_SLAB = 128

SUBLANES = 8
S5_SUBCHUNKS = 4
SEG = 256
FF_CHUNK = 256
N_FF_CHUNKS = D_FF // FF_CHUNK
MOD_ROWS = 16
HALO = 8
V7X_VMEM_LIMIT = 56 * 1024 * 1024

S_U, S_RQ, S_RK, S_RV, S_RG, S_CX, S_CB, S_CC, S_CQ, S_KV, S_RQR, S_RKR, S_KRR = range(13)
NSEG_PLAIN = 10
NSEG_ROPE = 13


def _cparams(sem):
    return pltpu.CompilerParams(dimension_semantics=sem, vmem_limit_bytes=V7X_VMEM_LIMIT)


def _dot(a, b):
    return jnp.dot(a.astype(BF16), b.astype(BF16), preferred_element_type=F32)


def _dot_nt(a, b):
    return lax.dot_general(a.astype(BF16), b.astype(BF16), (((1,), (1,)), ((), ())),
                           preferred_element_type=F32)


def _dot_tn(a, b):
    return lax.dot_general(a.astype(BF16), b.astype(BF16), (((0,), (0,)), ((), ())),
                           preferred_element_type=F32)


def _sigmoid(x):
    return 1.0 / (1.0 + jnp.exp(-x))


def _rms(x, g, n=None):
    n = x.shape[-1] if n is None else n
    ms = jnp.sum(x * x, axis=-1, keepdims=True) * (1.0 / n)
    return x * lax.rsqrt(ms + EPS) * g


def _gelu_tanh(x):
    return 0.5 * x * (1.0 + jnp.tanh(np.sqrt(2.0 / np.pi) * (x + 0.044715 * (x * x * x))))


def _const_spec(shape):
    nd = len(shape)
    return pl.BlockSpec(shape, lambda *_: (0,) * nd)


def _resident_spec(shape, lead=()):
    nd = len(shape)
    return pl.BlockSpec((None,) * len(lead) + tuple(shape), lambda *_: tuple(lead) + (0,) * nd,
                        pipeline_mode=pl.Buffered(1))


def _mod_kernel(c_ref, w_ref, b_ref, o_ref):
    c = c_ref[...]
    o_ref[...] = _dot(c * _sigmoid(c), w_ref[...]) + b_ref[...]


def _mod_call(cond, w_mod, b_mod):
    ncol = N_MOD * D_MODEL
    tn = D_MODEL
    return pl.pallas_call(
        _mod_kernel,
        grid=(DEPTH, ncol // tn),
        in_specs=[
            pl.BlockSpec((MOD_ROWS, D_MODEL), lambda l, j: (0, 0)),
            pl.BlockSpec((None, D_MODEL, tn), lambda l, j: (l, 0, j)),
            pl.BlockSpec((None, 1, tn), lambda l, j: (l, 0, j)),
        ],
        out_specs=pl.BlockSpec((None, MOD_ROWS, tn), lambda l, j: (l, 0, j)),
        out_shape=jax.ShapeDtypeStruct((DEPTH, MOD_ROWS, ncol), F32),
        compiler_params=_cparams(("arbitrary", "arbitrary")),
        name="mod",
    )(cond, w_mod, b_mod.reshape(DEPTH, 1, ncol))


def _ffn_kernel(x_ref, mod_ref, npre_ref, npost_ref, w1_ref, w3_ref, w2_ref, o_ref,
                h_scr, g_scr, *, k):
    x = x_ref[...]
    sh = mod_ref[3 * k:3 * k + 1, :]
    sc = mod_ref[3 * k + 1:3 * k + 2, :]
    gt = mod_ref[3 * k + 2:3 * k + 3, :]
    h_scr[...] = (_rms(x, npre_ref[...]) * (1.0 + sc) + sh).astype(BF16)
    for j in range(N_FF_CHUNKS):
        cols = slice(j * FF_CHUNK, (j + 1) * FF_CHUNK)
        a = jnp.dot(h_scr[...], w1_ref[:, cols], preferred_element_type=F32)
        b = jnp.dot(h_scr[...], w3_ref[:, cols], preferred_element_type=F32)
        g_scr[:, cols] = (a * _sigmoid(a) * b).astype(BF16)
    f = jnp.dot(g_scr[...], w2_ref[...], preferred_element_type=F32)
    o_ref[...] = x + (0.5 * gt) * _rms(f, npost_ref[...])


def _ffn_call(x, mod_l, npre, npost, w1, w3, w2, *, lj, k, tm, row_of_tile):
    n = x.shape[0]
    return pl.pallas_call(
        functools.partial(_ffn_kernel, k=k),
        grid=(n // tm,),
        in_specs=[
            pl.BlockSpec((tm, D_MODEL), lambda i: (i, 0)),
            pl.BlockSpec((None, N_MOD, D_MODEL), lambda i: (row_of_tile(i), 0, 0)),
            _const_spec((1, D_MODEL)),
            _const_spec((1, D_MODEL)),
            _resident_spec((D_MODEL, D_FF), lj),
            _resident_spec((D_MODEL, D_FF), lj),
            _resident_spec((D_FF, D_MODEL), lj),
        ],
        out_specs=pl.BlockSpec((tm, D_MODEL), lambda i: (i, 0)),
        out_shape=jax.ShapeDtypeStruct((n, D_MODEL), F32),
        scratch_shapes=[pltpu.VMEM((tm, D_MODEL), BF16), pltpu.VMEM((tm, D_FF), BF16)],
        compiler_params=_cparams(("arbitrary",)),
        name="ffn",
    )(x, mod_l, npre, npost, w1, w3, w2)


def _in_kernel(x_ref, mod_ref, npre_ref, w_ref, p_ref, h_scr, *, nseg):
    sh = mod_ref[3:4, :]
    sc = mod_ref[4:5, :]
    h_scr[...] = (_rms(x_ref[...], npre_ref[...]) * (1.0 + sc) + sh).astype(BF16)
    for j in range(nseg):
        cols = slice(j * SEG, (j + 1) * SEG)
        p_ref[:, cols] = jnp.dot(h_scr[...], w_ref[:, cols], preferred_element_type=F32)


def _in_call(x, mod_l, npre, w_ext, *, nb, t, tm, nseg, row_of_batch):
    tt = t // tm
    return pl.pallas_call(
        functools.partial(_in_kernel, nseg=nseg),
        grid=(nb, tt),
        in_specs=[
            pl.BlockSpec((tm, D_MODEL), lambda b, i: (b * tt + i, 0)),
            pl.BlockSpec((None, N_MOD, D_MODEL), lambda b, i: (row_of_batch(b), 0, 0)),
            _const_spec((1, D_MODEL)),
            _resident_spec((D_MODEL, nseg * SEG)),
        ],
        out_specs=pl.BlockSpec((tm, nseg * SEG), lambda b, i: (b * tt + i, 0)),
        out_shape=jax.ShapeDtypeStruct((nb * t, nseg * SEG), F32),
        scratch_shapes=[pltpu.VMEM((tm, D_MODEL), BF16)],
        compiler_params=_cparams(("arbitrary", "arbitrary")),
        name="in_proj",
    )(x, mod_l, npre, w_ext)


def _s5_kernel(*refs, nb, steps, want_fin):
    if want_fin:
        u_ref, perm_ref, permt_ref, bb_ref, a_ref, cc_ref, s0_ref, y_ref, fin_ref, s_scr, bu_scr = refs
    else:
        u_ref, perm_ref, permt_ref, bb_ref, a_ref, cc_ref, s0_ref, y_ref, s_scr, bu_scr = refs
        fin_ref = None
    d = pl.program_id(0)
    c = pl.program_id(1)
    rows = nb * steps
    ngrp = nb // SUBLANES

    @pl.when(c == 0)
    def _():
        s_scr[...] = s0_ref[...]

    a_re = jnp.broadcast_to(a_ref[0:1, :], (SUBLANES, S5_LANES))
    a_im = jnp.broadcast_to(a_ref[1:2, :], (SUBLANES, S5_LANES))
    s_re = [s_scr[0, g * SUBLANES:(g + 1) * SUBLANES, :] for g in range(ngrp)]
    s_im = [s_scr[1, g * SUBLANES:(g + 1) * SUBLANES, :] for g in range(ngrp)]
    re_cols = slice(0, S5_LANES)
    im_cols = slice(S5_LANES, 2 * S5_LANES)
    for k in range(S5_SUBCHUNKS):
        sub = jnp.where(d == 0, k, S5_SUBCHUNKS - 1 - k)
        off = pl.multiple_of(sub * steps, steps)
        u_bt = u_ref[:, pl.ds(off, steps), :].reshape(rows, SEG).astype(BF16)
        u_tb = jnp.dot(perm_ref[...], u_bt, preferred_element_type=F32).astype(BF16)
        bu_scr[k] = jnp.dot(u_tb, bb_ref[...], preferred_element_type=F32)
        for i in range(steps):
            for g in range(ngrp):
                rr = slice(i * nb + g * SUBLANES, i * nb + (g + 1) * SUBLANES)
                n_re = a_re * s_re[g] - a_im * s_im[g] + bu_scr[k, rr, re_cols]
                n_im = a_re * s_im[g] + a_im * s_re[g] + bu_scr[k, rr, im_cols]
                bu_scr[k, rr, re_cols] = n_re
                bu_scr[k, rr, im_cols] = n_im
                s_re[g], s_im[g] = n_re, n_im
        y_tb = _dot(bu_scr[k], cc_ref[...])
        hi = y_tb.astype(BF16)
        lo = (y_tb - hi.astype(F32)).astype(BF16)
        y_bt = (jnp.dot(permt_ref[...], hi, preferred_element_type=F32)
                + jnp.dot(permt_ref[...], lo, preferred_element_type=F32))
        y_ref[:, pl.ds(off, steps), :] = y_bt.reshape(nb, steps, SEG)
    for g in range(ngrp):
        s_scr[0, g * SUBLANES:(g + 1) * SUBLANES, :] = s_re[g]
        s_scr[1, g * SUBLANES:(g + 1) * SUBLANES, :] = s_im[g]
    if want_fin:
        fin_ref[...] = s_scr[...]


def _s5_call(p3, bblk, a_tab, cblk, s0, *, nb, t, steps, want_fin):
    span = steps * S5_SUBCHUNKS
    nch = t // span
    rows = steps * nb
    r = np.arange(rows)
    perm_np = np.zeros((2, rows, rows), np.float32)
    perm_np[0, r, (r % nb) * steps + r // nb] = 1.0
    perm_np[1, r, (r % nb) * steps + (steps - 1 - r // nb)] = 1.0
    perm = jnp.asarray(perm_np, BF16)
    permt = jnp.asarray(perm_np.transpose(0, 2, 1), BF16)

    def chunk(d, c):
        return jnp.where(d == 0, c, nch - 1 - c)

    out_specs = [pl.BlockSpec((None, nb, span, SEG), lambda d, c: (d, 0, chunk(d, c), 0))]
    out_shape = [jax.ShapeDtypeStruct((2, nb, t, SEG), F32)]
    if want_fin:
        out_specs.append(pl.BlockSpec((None, 2, nb, S5_LANES), lambda d, c: (d, 0, 0, 0)))
        out_shape.append(jax.ShapeDtypeStruct((2, 2, nb, S5_LANES), F32))
    return pl.pallas_call(
        functools.partial(_s5_kernel, nb=nb, steps=steps, want_fin=want_fin),
        grid=(2, nch),
        in_specs=[
            pl.BlockSpec((nb, span, SEG), lambda d, c: (0, chunk(d, c), S_U)),
            pl.BlockSpec((None, rows, rows), lambda d, c: (d, 0, 0)),
            pl.BlockSpec((None, rows, rows), lambda d, c: (d, 0, 0)),
            pl.BlockSpec((None, SEG, 2 * S5_LANES), lambda d, c: (d, 0, 0)),
            pl.BlockSpec((None, 2, S5_LANES), lambda d, c: (d, 0, 0)),
            _const_spec((2 * S5_LANES, SEG)),
            pl.BlockSpec((None, 2, nb, S5_LANES), lambda d, c: (d, 0, 0, 0)),
        ],
        out_specs=out_specs,
        out_shape=out_shape,
        scratch_shapes=[pltpu.VMEM((2, nb, S5_LANES), F32),
                        pltpu.VMEM((S5_SUBCHUNKS, rows, 2 * S5_LANES), F32)],
        compiler_params=_cparams(("arbitrary", "arbitrary")),
        name="s5_scan",
    )(p3, perm, permt, bblk, a_tab, cblk, s0)


def _ret_kernel(*refs, rope, has_s0, want_fin, nch):
    refs = list(refs)
    q_ref, k_ref, v_ref, g_ref = refs[:4]
    del refs[:4]
    if rope:
        qr_ref, kr_ref, cos_ref, sin_ref = refs[:4]
        del refs[:4]
    dm_ref, qdf_ref, kdf_ref, cdf_ref, qdb_ref, kdb_ref, cdb_ref, gn_ref = refs[:8]
    del refs[:8]
    if has_s0:
        s0_ref = refs.pop(0)
    o_ref = refs.pop(0)
    fin_ref = refs.pop(0) if want_fin else None
    sb_all, sf_scr, sb_scr = refs

    p = pl.program_id(1)
    c = pl.program_id(2)
    hd = RET_DIM

    def roped_k():
        k = k_ref[...]
        if rope:
            k = k * cos_ref[...] + kr_ref[...] * sin_ref[...]
        return k

    @pl.when(p == 0)
    def _():
        ck = nch - 1 - c

        @pl.when(c == 0)
        def _():
            if has_s0:
                sb_scr[...] = s0_ref[1]
            else:
                sb_scr[...] = jnp.zeros_like(sb_scr)

        k = roped_k()
        v = v_ref[...]
        for h in range(RET_HEADS):
            sl = slice(h * hd, (h + 1) * hd)
            s_old = sb_scr[h]
            sb_all[ck, h] = s_old
            sb_scr[h] = s_old * cdb_ref[h] + _dot_tn(k[:, sl] * kdb_ref[h], v[:, sl])
        if want_fin:
            @pl.when(c == nch - 1)
            def _():
                fin_ref[1] = sb_scr[...]

    @pl.when(p == 1)
    def _():
        @pl.when(c == 0)
        def _():
            if has_s0:
                sf_scr[...] = s0_ref[0]
            else:
                sf_scr[...] = jnp.zeros_like(sf_scr)

        q = q_ref[...]
        if rope:
            q = q * cos_ref[...] + qr_ref[...] * sin_ref[...]
        k = roped_k()
        v = v_ref[...]
        outs = []
        for h in range(RET_HEADS):
            sl = slice(h * hd, (h + 1) * hd)
            qh, kh, vh = q[:, sl], k[:, sl], v[:, sl]
            att = _dot_nt(qh, kh) * dm_ref[h]
            s_f = sf_scr[h]
            o = (_dot(att, vh) + _dot(qh, s_f) * qdf_ref[h] + _dot(qh, sb_all[c, h]) * qdb_ref[h])
            sf_scr[h] = s_f * cdf_ref[h] + _dot_tn(kh * kdf_ref[h], vh)
            mu = jnp.mean(o, axis=-1, keepdims=True)
            oc = o - mu
            var = jnp.mean(oc * oc, axis=-1, keepdims=True)
            outs.append(oc * lax.rsqrt(var + EPS))
        on = jnp.concatenate(outs, axis=-1) * gn_ref[...]
        g = g_ref[...]
        o_ref[...] = (g * _sigmoid(g)) * on
        if want_fin:
            @pl.when(c == nch - 1)
            def _():
                fin_ref[0] = sf_scr[...]


def _ret_call(p3, tabs, gn, s0, rope_tabs, *, nb, t, tc, want_fin):
    nch = t // tc
    rope = rope_tabs is not None
    has_s0 = s0 is not None

    def kv_chunk(p, c):
        return jnp.where(p == 0, nch - 1 - c, c)

    def q_chunk(p, c):
        return jnp.where(p == 0, 0, c)

    def seg_spec(seg, cmap):
        return pl.BlockSpec((None, tc, SEG), lambda b, p, c: (b, cmap(p, c), seg))

    args = [p3, p3, p3, p3]
    in_specs = [seg_spec(S_RQ, q_chunk), seg_spec(S_RK, kv_chunk), seg_spec(S_RV, kv_chunk),
                seg_spec(S_RG, q_chunk)]
    if rope:
        args += [p3, p3, rope_tabs[0], rope_tabs[1]]
        in_specs += [seg_spec(S_RQR, q_chunk), seg_spec(S_RKR, kv_chunk),
                     pl.BlockSpec((tc, SEG), lambda b, p, c: (kv_chunk(p, c), 0)),
                     pl.BlockSpec((tc, SEG), lambda b, p, c: (kv_chunk(p, c), 0))]
    args += list(tabs) + [gn]
    in_specs += [_const_spec(a.shape) for a in tabs] + [_const_spec((1, SEG))]
    if has_s0:
        args.append(s0)
        in_specs.append(pl.BlockSpec((None, 2, RET_HEADS, RET_DIM, RET_DIM),
                                     lambda b, p, c: (b, 0, 0, 0, 0)))
    out_specs = [pl.BlockSpec((None, tc, SEG), lambda b, p, c: (b, q_chunk(p, c), 0))]
    out_shape = [jax.ShapeDtypeStruct((nb, t, SEG), F32)]
    if want_fin:
        out_specs.append(pl.BlockSpec((None, 2, RET_HEADS, RET_DIM, RET_DIM),
                                      lambda b, p, c: (b, 0, 0, 0, 0)))
        out_shape.append(jax.ShapeDtypeStruct((nb, 2, RET_HEADS, RET_DIM, RET_DIM), F32))
    return pl.pallas_call(
        functools.partial(_ret_kernel, rope=rope, has_s0=has_s0, want_fin=want_fin, nch=nch),
        grid=(nb, 2, nch),
        in_specs=in_specs,
        out_specs=out_specs,
        out_shape=out_shape,
        scratch_shapes=[pltpu.VMEM((nch, RET_HEADS, RET_DIM, RET_DIM), F32),
                        pltpu.VMEM((RET_HEADS, RET_DIM, RET_DIM), F32),
                        pltpu.VMEM((RET_HEADS, RET_DIM, RET_DIM), F32)],
        compiler_params=_cparams(("arbitrary", "arbitrary", "arbitrary")),
        name="retention",
    )(*args)


def _mla_kernel(*refs, t, tq, nctx, want_cache):
    refs = list(refs)
    cq_ref, ckv_ref, krp_ref = refs[:3]
    del refs[:3]
    has_ctx = nctx > 0
    if has_ctx:
        krr_ref, cm_ref, sm_ref, clat_ref, ckrp_ref = refs[:5]
        del refs[:5]
    qn_ref, kvn_ref, wq_ref = refs[:3]
    del refs[:3]
    if has_ctx:
        wqr_ref = refs.pop(0)
    wk_ref, wv_ref = refs[:2]
    del refs[:2]
    o_ref = refs.pop(0)
    lat_ref = refs.pop(0) if want_cache else None
    k_scr, v_scr = refs
    qi = pl.program_id(1)

    @pl.when(qi == 0)
    def _():
        lat = _rms(ckv_ref[...], kvn_ref[...])
        if want_cache:
            lat_ref[...] = lat
        latb = lat.astype(BF16)
        kr = krp_ref[...]
        if has_ctx:
            kr = kr * cm_ref[...] + krr_ref[...] * sm_ref[...]
        kk = jnp.dot(latb, wk_ref[...], preferred_element_type=F32)
        for h in range(MLA_HEADS):
            k_scr[h, 0:t, :] = (kk[:, h * MLA_SLAB:(h + 1) * MLA_SLAB] + kr).astype(BF16)
        v_scr[0:t, :] = jnp.dot(latb, wv_ref[...], preferred_element_type=F32).astype(BF16)
        if has_ctx:
            latc = clat_ref[...].astype(BF16)
            kkc = jnp.dot(latc, wk_ref[...], preferred_element_type=F32)
            krc = ckrp_ref[...]
            for h in range(MLA_HEADS):
                k_scr[h, t:t + nctx, :] = (kkc[:, h * MLA_SLAB:(h + 1) * MLA_SLAB] + krc).astype(BF16)
            v_scr[t:t + nctx, :] = jnp.dot(latc, wv_ref[...], preferred_element_type=F32).astype(BF16)

    a = _rms(cq_ref[...], qn_ref[...], n=MLA_Q_LORA).astype(BF16)
    q = jnp.dot(a, wq_ref[...], preferred_element_type=F32)
    if has_ctx:
        qrot = jnp.dot(a, wqr_ref[...], preferred_element_type=F32)
        r0 = pl.multiple_of(qi * tq, tq)
        cm = cm_ref[pl.ds(r0, tq), :]
        sm = sm_ref[pl.ds(r0, tq), :]
    scale = (MLA_NOPE + MLA_ROPE) ** -0.5
    lane = lax.broadcasted_iota(jnp.int32, (tq, SEG), 1)
    acc = jnp.zeros((tq, SEG), F32)
    for h in range(MLA_HEADS):
        qh = q[:, h * MLA_SLAB:(h + 1) * MLA_SLAB]
        if has_ctx:
            qh = qh * cm + qrot[:, h * MLA_SLAB:(h + 1) * MLA_SLAB] * sm
        s = _dot_nt(qh * scale, k_scr[h])
        m = jnp.max(s, axis=-1, keepdims=True)
        e = jnp.exp(s - m)
        l = jnp.sum(e, axis=-1, keepdims=True)
        oh = jnp.dot(e.astype(BF16), v_scr[...], preferred_element_type=F32) * (1.0 / l)
        acc = jnp.where((lane >= h * MLA_V) & (lane < (h + 1) * MLA_V), oh, acc)
    o_ref[...] = acc


def _mla_call(p3, qn, kvn, wq, wqr, wk, wv, rope_tabs, ctx, *, nb, t, tq, want_cache):
    has_ctx = ctx is not None
    nctx = ctx[0].shape[1] if has_ctx else 0
    s_tot = t + nctx
    half = MLA_KV_LORA

    args = [p3, p3, p3]
    in_specs = [
        pl.BlockSpec((None, tq, SEG), lambda b, i: (b, i, S_CQ)),
        pl.BlockSpec((None, t, half), lambda b, i: (b, 0, 2 * S_KV)),
        pl.BlockSpec((None, t, half), lambda b, i: (b, 0, 2 * S_KV + 1)),
    ]
    if has_ctx:
        args += [p3, rope_tabs[0], rope_tabs[1], ctx[0], ctx[1]]
        in_specs += [
            pl.BlockSpec((None, t, half), lambda b, i: (b, 0, 2 * S_KRR)),
            _const_spec((t, MLA_SLAB)),
            _const_spec((t, MLA_SLAB)),
            pl.BlockSpec((None, nctx, MLA_KV_LORA), lambda b, i: (b, 0, 0)),
            pl.BlockSpec((None, nctx, MLA_SLAB), lambda b, i: (b, 0, 0)),
        ]
    args += [qn, kvn, wq]
    in_specs += [_const_spec(qn.shape), _const_spec(kvn.shape), _const_spec(wq.shape)]
    if has_ctx:
        args.append(wqr)
        in_specs.append(_const_spec(wqr.shape))
    args += [wk, wv]
    in_specs += [_const_spec(wk.shape), _const_spec(wv.shape)]
    out_specs = [pl.BlockSpec((None, tq, SEG), lambda b, i: (b, i, 0))]
    out_shape = [jax.ShapeDtypeStruct((nb, t, SEG), F32)]
    if want_cache:
        out_specs.append(pl.BlockSpec((None, t, MLA_KV_LORA), lambda b, i: (b, 0, 0)))
        out_shape.append(jax.ShapeDtypeStruct((nb, t, MLA_KV_LORA), F32))
    return pl.pallas_call(
        functools.partial(_mla_kernel, t=t, tq=tq, nctx=nctx, want_cache=want_cache),
        grid=(nb, t // tq),
        in_specs=in_specs,
        out_specs=out_specs,
        out_shape=out_shape,
        scratch_shapes=[pltpu.VMEM((MLA_HEADS, s_tot, MLA_SLAB), BF16),
                        pltpu.VMEM((s_tot, SEG), BF16)],
        compiler_params=_cparams(("arbitrary", "arbitrary")),
        name="mla",
    )(*args)


def _merge_kernel(x_ref, mod_ref, npre_ref, npost_ref, yf_ref, yb_ref, u_ref, ret_ref,
                  cx_ref, cb_ref, cc_ref, cxp_ref, ccp_ref, cxn_ref, ccn_ref, mla_ref,
                  s5d_ref, wglu_ref, cw_ref, cbias_ref, wbr_ref, wg_ref, bg_ref, wo_ref,
                  o_ref, h_scr, *, tm, nt):
    i = pl.program_id(1)
    x = x_ref[...]
    sh = mod_ref[3:4, :]
    sc = mod_ref[4:5, :]
    gt = mod_ref[5:6, :]
    h_scr[...] = (_rms(x, npre_ref[...]) * (1.0 + sc) + sh).astype(BF16)

    y = yf_ref[...] + yb_ref[...] + s5d_ref[...] * u_ref[...]
    z5 = _gelu_tanh(y)
    b_s5 = z5 * _sigmoid(_dot(z5, wglu_ref[...]))

    z = cc_ref[...] * cx_ref[...]
    zprev = ccp_ref[HALO - 1:HALO, :] * cxp_ref[HALO - 1:HALO, :] * jnp.where(i > 0, 1.0, 0.0)
    znext = ccn_ref[0:1, :] * cxn_ref[0:1, :] * jnp.where(i < nt - 1, 1.0, 0.0)
    row = lax.broadcasted_iota(jnp.int32, (tm, SEG), 0)
    z_dn = jnp.where(row == 0, zprev, pltpu.roll(z, 1, axis=0))
    z_up = jnp.where(row == tm - 1, znext, pltpu.roll(z, tm - 1, axis=0))
    b_conv = cb_ref[...] * (cbias_ref[...] + z_dn * cw_ref[0:1, :] + z * cw_ref[1:2, :]
                            + z_up * cw_ref[2:3, :])

    branches = (b_s5, ret_ref[...], b_conv, mla_ref[...])
    merged = jnp.zeros((tm, D_MODEL), F32)
    for n in range(N_BRANCH):
        gate = _sigmoid(jnp.dot(h_scr[...], wg_ref[:, n * D_MODEL:(n + 1) * D_MODEL],
                                preferred_element_type=F32) + bg_ref[:, n * D_MODEL:(n + 1) * D_MODEL])
        merged = merged + gate * _dot(branches[n], wbr_ref[n])
    m = _dot(merged, wo_ref[...])
    o_ref[...] = x + gt * _rms(m, npost_ref[...])


def _merge_call(x, mod_l, npre, npost, y_s5, ret_o, p3, mla_o, s5d, wglu, cw, cbias, wbr, wg, bg, wo,
                *, layer, nb, t, tm, row_of_batch):
    nt = t // tm
    hb = tm // HALO
    n_halo = t // HALO

    def dir_spec(d):
        return pl.BlockSpec((None, None, tm, SEG), lambda b, i: (d, b, i, 0))

    def seg_spec(seg):
        return pl.BlockSpec((None, tm, SEG), lambda b, i: (b, i, seg))

    def prev_spec(seg):
        return pl.BlockSpec((None, HALO, SEG), lambda b, i: (b, jnp.maximum(i * hb - 1, 0), seg))

    def next_spec(seg):
        return pl.BlockSpec((None, HALO, SEG), lambda b, i: (b, jnp.minimum((i + 1) * hb, n_halo - 1), seg))

    return pl.pallas_call(
        functools.partial(_merge_kernel, tm=tm, nt=nt),
        grid=(nb, nt),
        in_specs=[
            pl.BlockSpec((tm, D_MODEL), lambda b, i: (b * nt + i, 0)),
            pl.BlockSpec((None, N_MOD, D_MODEL), lambda b, i: (row_of_batch(b), 0, 0)),
            _const_spec((1, D_MODEL)),
            _const_spec((1, D_MODEL)),
            dir_spec(0), dir_spec(1), seg_spec(S_U),
            pl.BlockSpec((None, tm, SEG), lambda b, i: (b, i, 0)),
            seg_spec(S_CX), seg_spec(S_CB), seg_spec(S_CC),
            prev_spec(S_CX), prev_spec(S_CC), next_spec(S_CX), next_spec(S_CC),
            pl.BlockSpec((None, tm, SEG), lambda b, i: (b, i, 0)),
            _const_spec(s5d.shape), _const_spec(wglu.shape), _const_spec(cw.shape),
            _const_spec(cbias.shape), _resident_spec(wbr.shape[1:], (layer,)),
            _resident_spec(wg.shape[1:], (layer,)), _const_spec(bg.shape),
            _resident_spec(wo.shape[1:], (layer,)),
        ],
        out_specs=pl.BlockSpec((tm, D_MODEL), lambda b, i: (b * nt + i, 0)),
        out_shape=jax.ShapeDtypeStruct((nb * t, D_MODEL), F32),
        scratch_shapes=[pltpu.VMEM((tm, D_MODEL), BF16)],
        compiler_params=_cparams(("arbitrary", "arbitrary")),
        name="merge",
    )(x, mod_l, npre, npost, y_s5, y_s5, p3, ret_o, p3, p3, p3, p3, p3, p3, p3, mla_o,
      s5d, wglu, cw, cbias, wbr, wg, bg, wo)


def _rot_half(w, heads, dim):
    w4 = w.reshape(w.shape[0], heads, dim)
    half = dim // 2
    return jnp.concatenate([-w4[..., half:], w4[..., :half]], axis=-1).reshape(w.shape[0], heads * dim)


def _axial_angles(t, dim):
    rows = t // GRID_W
    row = np.repeat(np.arange(rows, dtype=np.float64), GRID_W)
    col = np.tile(np.arange(GRID_W, dtype=np.float64), rows)
    quarter = dim // 4
    inv = ROPE_BASE ** (-np.arange(quarter, dtype=np.float64) / quarter)
    ang = np.concatenate([row[:, None] * inv, col[:, None] * inv], axis=-1)
    return np.cos(ang).astype(np.float32), np.sin(ang).astype(np.float32)


def _rope_tables(t):
    cos, sin = _axial_angles(t, RET_DIM)
    ret_cos = np.tile(np.concatenate([cos, cos], -1), (1, RET_HEADS))
    ret_sin = np.tile(np.concatenate([sin, sin], -1), (1, RET_HEADS))
    cos, sin = _axial_angles(t, MLA_ROPE)
    pad = MLA_SLAB - MLA_NOPE - MLA_ROPE
    mla_cos = np.concatenate([np.ones((t, MLA_NOPE), np.float32), cos, cos, np.zeros((t, pad), np.float32)], -1)
    mla_sin = np.concatenate([np.zeros((t, MLA_NOPE), np.float32), sin, sin, np.zeros((t, pad), np.float32)], -1)
    return ((jnp.asarray(ret_cos), jnp.asarray(ret_sin)), (jnp.asarray(mla_cos), jnp.asarray(mla_sin)))


def _in_weights(w_in, rope):
    sizes = (BRANCH_WIDTH,) * 8 + (MLA_Q_LORA, MLA_KV_LORA, MLA_ROPE)
    offs = np.cumsum((0,) + sizes)
    u, rq, rk, rv, rg, cx, cb, cc, cq, ckv, kr = [w_in[:, offs[i]:offs[i + 1]] for i in range(11)]
    rk = rk * (RET_DIM ** -0.5)
    d = w_in.shape[0]

    def place_kr(w):
        return jnp.concatenate([jnp.zeros((d, MLA_NOPE), F32), w,
                                jnp.zeros((d, MLA_SLAB - MLA_NOPE - MLA_ROPE), F32)], -1)

    cols = [u, rq, rk, rv, rg, cx, cb, cc,
            jnp.concatenate([cq, jnp.zeros((d, SEG - MLA_Q_LORA), F32)], -1),
            jnp.concatenate([ckv, place_kr(kr)], -1)]
    if rope:
        cols += [_rot_half(rq, RET_HEADS, RET_DIM), _rot_half(rk, RET_HEADS, RET_DIM),
                 jnp.concatenate([place_kr(_rot_half(kr, 1, MLA_ROPE)), jnp.zeros((d, SEG - MLA_SLAB), F32)], -1)]
    return jnp.concatenate(cols, axis=-1).astype(BF16)


def _mla_weights(w_uq, w_ukv):
    dq = MLA_NOPE + MLA_ROPE
    w4 = w_uq.reshape(MLA_Q_LORA, MLA_HEADS, dq)
    nope, ropew = w4[..., :MLA_NOPE], w4[..., MLA_NOPE:]
    half = MLA_ROPE // 2
    rot = jnp.concatenate([-ropew[..., half:], ropew[..., :half]], -1)
    zpad = jnp.zeros((MLA_Q_LORA, MLA_HEADS, MLA_SLAB - dq), F32)
    wq = jnp.concatenate([nope, ropew, zpad], -1).reshape(MLA_Q_LORA, MLA_HEADS * MLA_SLAB)
    wqr = jnp.concatenate([jnp.zeros_like(nope), rot, zpad], -1).reshape(MLA_Q_LORA, MLA_HEADS * MLA_SLAB)
    rpad = jnp.zeros((SEG - MLA_Q_LORA, MLA_HEADS * MLA_SLAB), F32)
    wq = jnp.concatenate([wq, rpad], 0).astype(BF16)
    wqr = jnp.concatenate([wqr, rpad], 0).astype(BF16)
    kv4 = w_ukv.reshape(MLA_KV_LORA, MLA_HEADS, MLA_NOPE + MLA_V)
    wk = jnp.concatenate([kv4[..., :MLA_NOPE], jnp.zeros((MLA_KV_LORA, MLA_HEADS, MLA_SLAB - MLA_NOPE), F32)],
                         -1).reshape(MLA_KV_LORA, MLA_HEADS * MLA_SLAB).astype(BF16)
    wv = kv4[..., MLA_NOPE:].reshape(MLA_KV_LORA, MLA_HEADS * MLA_V).astype(BF16)
    return wq, wqr, wk, wv


def _s5_tables(lam_re, lam_im, log_dt, b_re, b_im, c_re, c_im):
    dt = jnp.exp(log_dt)[..., None]
    mag = jnp.exp(lam_re * dt)
    ab_re = mag * jnp.cos(lam_im * dt)
    ab_im = mag * jnp.sin(lam_im * dt)
    den = lam_re * lam_re + lam_im * lam_im
    f_re = ((ab_re - 1.0) * lam_re + ab_im * lam_im) / den
    f_im = (ab_im * lam_re - (ab_re - 1.0) * lam_im) / den
    bb_re = f_re[..., None] * b_re - f_im[..., None] * b_im
    bb_im = f_re[..., None] * b_im + f_im[..., None] * b_re
    eye = jnp.eye(S5_GROUPS, dtype=F32)
    blk_in = lambda w: jnp.einsum('dgph,gk->dghkp', w, eye).reshape(2, BRANCH_WIDTH, S5_LANES)
    bblk = jnp.concatenate([blk_in(bb_re), blk_in(bb_im)], -1).astype(BF16)
    blk_out = lambda w: jnp.einsum('ghp,gk->gpkh', w, eye).reshape(S5_LANES, BRANCH_WIDTH)
    cblk = jnp.concatenate([blk_out(c_re), -blk_out(c_im)], 0).astype(BF16)
    a_tab = jnp.stack([ab_re.reshape(2, S5_LANES), ab_im.reshape(2, S5_LANES)], axis=1)
    return bblk, a_tab, cblk


def _ret_tables(decay_logit, c):
    lg = jax.nn.log_sigmoid(decay_logit.astype(F32))
    lf = lg[0][:, None, None]
    lb = lg[1][:, None, None]
    idx = jnp.arange(c, dtype=F32)
    diff = idx[:, None] - idx[None, :]
    dm = jnp.where(diff >= 0, jnp.exp(lf * jnp.maximum(diff, 0.0)), jnp.exp(lb * jnp.maximum(-diff, 0.0)))
    col = lambda e: jnp.broadcast_to(e, (RET_HEADS, c, RET_DIM))
    sq = lambda e: jnp.broadcast_to(e, (RET_HEADS, RET_DIM, RET_DIM))
    i3 = idx[None, :, None]
    qdf = col(jnp.exp(lf * (i3 + 1.0)))
    kdf = col(jnp.exp(lf * (c - 1.0 - i3)))
    cdf = sq(jnp.exp(lf * c))
    qdb = col(jnp.exp(lb * (c - i3)))
    kdb = col(jnp.exp(lb * i3))
    cdb = sq(jnp.exp(lb * c))
    return dm, qdf, kdf, cdf, qdb, kdb, cdb


def _layer(x, mod_l, w, *, layer, nb, t, tm, tm_ffn, row_of_batch, ctx, rope_tabs):
    is_ctx_pass = ctx is None
    row_of_tile = lambda i: row_of_batch((i * tm_ffn) // t)
    x = _ffn_call(x, mod_l, w['npre'][0], w['npost'][0], *w['ffn'], lj=(layer, 0), k=0, tm=tm_ffn,
                  row_of_tile=row_of_tile)

    nseg = NSEG_PLAIN if is_ctx_pass else NSEG_ROPE
    p = _in_call(x, mod_l, w['npre'][1], w['w_in'], nb=nb, t=t, tm=tm, nseg=nseg, row_of_batch=row_of_batch)
    p3 = p.reshape(nb, t, nseg * SEG)

    steps = 512 // nb
    if is_ctx_pass:
        s5_0 = jnp.zeros((2, 2, nb, S5_LANES), F32)
        ret_0, mla_ctx, ret_rope, mla_rope = None, None, None, None
    else:
        s5_0, ret_0, mla_ctx = ctx
        ret_rope, mla_rope = rope_tabs
    s5_out = _s5_call(p3, w['s5_bblk'], w['s5_a'], w['s5_cblk'], s5_0,
                      nb=nb, t=t, steps=steps, want_fin=is_ctx_pass)
    ret_out = _ret_call(p3, w['ret_tabs'], w['ret_gn'], ret_0, ret_rope,
                        nb=nb, t=t, tc=256, want_fin=is_ctx_pass)
    mla_out = _mla_call(p3, w['mla_qn'], w['mla_kvn'], w['mla_wq'], w['mla_wqr'], w['mla_wk'], w['mla_wv'],
                        mla_rope, mla_ctx, nb=nb, t=t, tq=256, want_cache=is_ctx_pass)

    x = _merge_call(x, mod_l, w['npre'][1], w['npost'][1], s5_out[0], ret_out[0], p3, mla_out[0],
                    w['s5_d'], w['s5_wglu'], w['conv_w'], w['conv_b'], w['w_branch'], w['w_gate'],
                    w['b_gate'], w['w_o'], layer=layer, nb=nb, t=t, tm=tm, row_of_batch=row_of_batch)
    x = _ffn_call(x, mod_l, w['npre'][2], w['npost'][2], *w['ffn'], lj=(layer, 1), k=2, tm=tm_ffn,
                  row_of_tile=row_of_tile)

    state = None
    if is_ctx_pass:
        fin = s5_out[1].reshape(2, 2, nb, S5_GROUPS, S5_STATE).transpose(2, 0, 3, 4, 1)
        kr = p3[:, :, S_KV * SEG + MLA_KV_LORA + MLA_NOPE:S_KV * SEG + MLA_KV_LORA + MLA_NOPE + MLA_ROPE]
        cache = jnp.concatenate([mla_out[1], kr], axis=-1)
        state = (fin, ret_out[1], cache)
    return x, state


def kernel(x_prompt, x_sample, state_s5, state_ret, cache_mla, c, c_ctx, w_mod, b_mod, norm_pre, norm_post, ffn_w1, ffn_w3, ffn_w2, w_in, s5_lam_re, s5_lam_im, s5_log_dt, s5_b_re, s5_b_im, s5_c_re, s5_c_im, s5_d, s5_w_glu, ret_decay, ret_gn, conv_w, conv_b, mla_q_norm, mla_w_uq, mla_kv_norm, mla_w_ukv, w_branch, w_gate, b_gate, w_o):
    bp, tp, _ = x_prompt.shape
    bs, ts, _ = x_sample.shape
    assert 1 + bs <= MOD_ROWS
    past = cache_mla.shape[2]

    cond = jnp.concatenate([c_ctx[None, :], c, jnp.zeros((MOD_ROWS - 1 - bs, D_MODEL), F32)], axis=0)
    mod = _mod_call(cond, w_mod, b_mod).reshape(DEPTH, MOD_ROWS, N_MOD, D_MODEL)

    rope_tabs = _rope_tables(ts)
    xp = x_prompt.reshape(bp * tp, D_MODEL)
    xs = x_sample.reshape(bs * ts, D_MODEL)
    s5_list, ret_list, mla_list = [], [], []
    ffn = (ffn_w1.astype(BF16), ffn_w3.astype(BF16), ffn_w2.astype(BF16))
    w_branch_b, w_gate_b, w_o_b = w_branch.astype(BF16), w_gate.astype(BF16), w_o.astype(BF16)
    for l in range(DEPTH):
        wq, wqr, wk, wv = _mla_weights(mla_w_uq[l], mla_w_ukv[l])
        bblk, a_tab, cblk = _s5_tables(s5_lam_re[l], s5_lam_im[l], s5_log_dt[l], s5_b_re[l], s5_b_im[l],
                                       s5_c_re[l], s5_c_im[l])
        w = dict(
            npre=[norm_pre[l, i][None, :] for i in range(3)],
            npost=[norm_post[l, i][None, :] for i in range(3)],
            ffn=ffn,
            w_in=_in_weights(w_in[l], rope=True),
            s5_bblk=bblk, s5_a=a_tab, s5_cblk=cblk,
            s5_d=s5_d[l][None, :], s5_wglu=s5_w_glu[l].astype(BF16),
            ret_tabs=_ret_tables(ret_decay[l], 256), ret_gn=ret_gn[l][None, :],
            conv_w=conv_w[l], conv_b=conv_b[l][None, :],
            mla_qn=jnp.concatenate([mla_q_norm[l], jnp.zeros((SEG - MLA_Q_LORA,), F32)])[None, :],
            mla_kvn=mla_kv_norm[l][None, :],
            mla_wq=wq, mla_wqr=wqr, mla_wk=wk, mla_wv=wv,
            w_branch=w_branch_b, w_gate=w_gate_b, b_gate=b_gate[l][None, :], w_o=w_o_b,
        )
        xp, (s5_s, ret_s, mla_c) = _layer(xp, mod[l], w, layer=l, nb=bp, t=tp, tm=256, tm_ffn=512,
                                          row_of_batch=lambda b: 0, ctx=None, rope_tabs=None)
        s5_list.append(s5_s)
        ret_list.append(ret_s)
        mla_list.append(mla_c)

        s5_0 = state_s5[:, l].transpose(1, 4, 0, 2, 3).reshape(2, 2, bs, S5_LANES)
        cache_l = cache_mla[:, l]
        ctx_lat = cache_l[..., :MLA_KV_LORA]
        ctx_krp = jnp.concatenate([jnp.zeros((bs, past, MLA_NOPE), F32), cache_l[..., MLA_KV_LORA:],
                                   jnp.zeros((bs, past, MLA_SLAB - MLA_NOPE - MLA_ROPE), F32)], -1)
        xs, _ = _layer(xs, mod[l], w, layer=l, nb=bs, t=ts, tm=512, tm_ffn=512, row_of_batch=lambda b: 1 + b,
                       ctx=(s5_0, state_ret[:, l], (ctx_lat, ctx_krp)), rope_tabs=rope_tabs)
    return (xp.reshape(bp, tp, D_MODEL), xs.reshape(bs, ts, D_MODEL),
            jnp.stack(s5_list, axis=1), jnp.stack(ret_list, axis=1), jnp.stack(mla_list, axis=1))
```

```python
import functools

import numpy as np
import jax
import jax.numpy as jnp
from jax import lax
from jax.experimental import pallas as pl
from jax.experimental.pallas import tpu as pltpu

F32 = jnp.float32
BF16 = jnp.bfloat16

D_MODEL = 1024
DEPTH = 2
GRID_W = 64
EPS = 1e-6
ROPE_BASE = 10000.0
N_BRANCH = 4
BRANCH_WIDTH = D_MODEL // 4
N_MOD = 9
D_FF = 2816
S5_GROUP = 16
S5_GROUPS = BRANCH_WIDTH // S5_GROUP
S5_STATE = 64
S5_LANES = S5_GROUPS * S5_STATE
RET_HEADS = 4
RET_DIM = BRANCH_WIDTH // RET_HEADS
CONV_K = 3
MLA_HEADS = 4
MLA_Q_LORA = 192
MLA_KV_LORA = 128
MLA_NOPE = 64
MLA_ROPE = 32
MLA_V = BRANCH_WIDTH // MLA_HEADS
MLA_SLAB = 128

SUBLANES = 8
S5_SUBCHUNKS = 4
SEG = 256
FF_CHUNK = 256
N_FF_CHUNKS = D_FF // FF_CHUNK
MOD_ROWS = 16
HALO = 8
V7X_VMEM_LIMIT = 56 * 1024 * 1024

S_U, S_RQ, S_RK, S_RV, S_RG, S_CX, S_CB, S_CC, S_CQ, S_KV, S_RQR, S_RKR, S_KRR = range(13)
NSEG_PLAIN = 10
NSEG_ROPE = 13


def _cparams(sem):
    return pltpu.CompilerParams(dimension_semantics=sem, vmem_limit_bytes=V7X_VMEM_LIMIT)


def _dot(a, b):
    return jnp.dot(a.astype(BF16), b.astype(BF16), preferred_element_type=F32)


def _dot_nt(a, b):
    return lax.dot_general(a.astype(BF16), b.astype(BF16), (((1,), (1,)), ((), ())),
                           preferred_element_type=F32)


def _dot_tn(a, b):
    return lax.dot_general(a.astype(BF16), b.astype(BF16), (((0,), (0,)), ((), ())),
                           preferred_element_type=F32)


def _sigmoid(x):
    return 1.0 / (1.0 + jnp.exp(-x))


def _rms(x, g, n=None):
    n = x.shape[-1] if n is None else n
    ms = jnp.sum(x * x, axis=-1, keepdims=True) * (1.0 / n)
    return x * lax.rsqrt(ms + EPS) * g


def _gelu_tanh(x):
    return 0.5 * x * (1.0 + jnp.tanh(np.sqrt(2.0 / np.pi) * (x + 0.044715 * (x * x * x))))


def _const_spec(shape):
    nd = len(shape)
    return pl.BlockSpec(shape, lambda *_: (0,) * nd)


def _resident_spec(shape, lead=()):
    nd = len(shape)
    return pl.BlockSpec((None,) * len(lead) + tuple(shape), lambda *_: tuple(lead) + (0,) * nd,
                        pipeline_mode=pl.Buffered(1))


def _mod_kernel(c_ref, w_ref, b_ref, o_ref):
    c = c_ref[...]
    o_ref[...] = _dot(c * _sigmoid(c), w_ref[...]) + b_ref[...]


def _mod_call(cond, w_mod, b_mod):
    ncol = N_MOD * D_MODEL
    tn = D_MODEL
    return pl.pallas_call(
        _mod_kernel,
        grid=(DEPTH, ncol // tn),
        in_specs=[
            pl.BlockSpec((MOD_ROWS, D_MODEL), lambda l, j: (0, 0)),
            pl.BlockSpec((None, D_MODEL, tn), lambda l, j: (l, 0, j)),
            pl.BlockSpec((None, 1, tn), lambda l, j: (l, 0, j)),
        ],
        out_specs=pl.BlockSpec((None, MOD_ROWS, tn), lambda l, j: (l, 0, j)),
        out_shape=jax.ShapeDtypeStruct((DEPTH, MOD_ROWS, ncol), F32),
        compiler_params=_cparams(("arbitrary", "arbitrary")),
        name="mod",
    )(cond, w_mod, b_mod.reshape(DEPTH, 1, ncol))


def _ffn_kernel(x_ref, mod_ref, npre_ref, npost_ref, w1_ref, w3_ref, w2_ref, o_ref,
                h_scr, g_scr, *, k):
    x = x_ref[...]
    sh = mod_ref[3 * k:3 * k + 1, :]
    sc = mod_ref[3 * k + 1:3 * k + 2, :]
    gt = mod_ref[3 * k + 2:3 * k + 3, :]
    h_scr[...] = (_rms(x, npre_ref[...]) * (1.0 + sc) + sh).astype(BF16)
    for j in range(N_FF_CHUNKS):
        cols = slice(j * FF_CHUNK, (j + 1) * FF_CHUNK)
        a = jnp.dot(h_scr[...], w1_ref[:, cols], preferred_element_type=F32)
        b = jnp.dot(h_scr[...], w3_ref[:, cols], preferred_element_type=F32)
        g_scr[:, cols] = (a * _sigmoid(a) * b).astype(BF16)
    f = jnp.dot(g_scr[...], w2_ref[...], preferred_element_type=F32)
    o_ref[...] = x + (0.5 * gt) * _rms(f, npost_ref[...])


def _ffn_call(x, mod_l, npre, npost, w1, w3, w2, *, lj, k, tm, row_of_tile):
    n = x.shape[0]
    return pl.pallas_call(
        functools.partial(_ffn_kernel, k=k),
        grid=(n // tm,),
        in_specs=[
            pl.BlockSpec((tm, D_MODEL), lambda i: (i, 0)),
            pl.BlockSpec((None, N_MOD, D_MODEL), lambda i: (row_of_tile(i), 0, 0)),
            _const_spec((1, D_MODEL)),
            _const_spec((1, D_MODEL)),
            _resident_spec((D_MODEL, D_FF), lj),
            _resident_spec((D_MODEL, D_FF), lj),
            _resident_spec((D_FF, D_MODEL), lj),
        ],
        out_specs=pl.BlockSpec((tm, D_MODEL), lambda i: (i, 0)),
        out_shape=jax.ShapeDtypeStruct((n, D_MODEL), F32),
        scratch_shapes=[pltpu.VMEM((tm, D_MODEL), BF16), pltpu.VMEM((tm, D_FF), BF16)],
        compiler_params=_cparams(("arbitrary",)),
        name="ffn",
    )(x, mod_l, npre, npost, w1, w3, w2)


def _in_kernel(x_ref, mod_ref, npre_ref, w_ref, p_ref, h_scr, *, nseg):
    sh = mod_ref[3:4, :]
    sc = mod_ref[4:5, :]
    h_scr[...] = (_rms(x_ref[...], npre_ref[...]) * (1.0 + sc) + sh).astype(BF16)
    for j in range(nseg):
        cols = slice(j * SEG, (j + 1) * SEG)
        p_ref[:, cols] = jnp.dot(h_scr[...], w_ref[:, cols], preferred_element_type=F32)


def _in_call(x, mod_l, npre, w_ext, *, nb, t, tm, nseg, row_of_batch):
    tt = t // tm
    return pl.pallas_call(
        functools.partial(_in_kernel, nseg=nseg),
        grid=(nb, tt),
        in_specs=[
            pl.BlockSpec((tm, D_MODEL), lambda b, i: (b * tt + i, 0)),
            pl.BlockSpec((None, N_MOD, D_MODEL), lambda b, i: (row_of_batch(b), 0, 0)),
            _const_spec((1, D_MODEL)),
            _resident_spec((D_MODEL, nseg * SEG)),
        ],
        out_specs=pl.BlockSpec((tm, nseg * SEG), lambda b, i: (b * tt + i, 0)),
        out_shape=jax.ShapeDtypeStruct((nb * t, nseg * SEG), F32),
        scratch_shapes=[pltpu.VMEM((tm, D_MODEL), BF16)],
        compiler_params=_cparams(("arbitrary", "arbitrary")),
        name="in_proj",
    )(x, mod_l, npre, w_ext)


def _s5_kernel(*refs, nb, steps, want_fin):
    if want_fin:
        u_ref, perm_ref, permt_ref, bb_ref, a_ref, cc_ref, s0_ref, y_ref, fin_ref, s_scr, bu_scr = refs
    else:
        u_ref, perm_ref, permt_ref, bb_ref, a_ref, cc_ref, s0_ref, y_ref, s_scr, bu_scr = refs
        fin_ref = None
    d = pl.program_id(0)
    c = pl.program_id(1)
    rows = nb * steps
    ngrp = nb // SUBLANES

    @pl.when(c == 0)
    def _():
        s_scr[...] = s0_ref[...]

    a_re = jnp.broadcast_to(a_ref[0:1, :], (SUBLANES, S5_LANES))
    a_im = jnp.broadcast_to(a_ref[1:2, :], (SUBLANES, S5_LANES))
    s_re = [s_scr[0, g * SUBLANES:(g + 1) * SUBLANES, :] for g in range(ngrp)]
    s_im = [s_scr[1, g * SUBLANES:(g + 1) * SUBLANES, :] for g in range(ngrp)]
    re_cols = slice(0, S5_LANES)
    im_cols = slice(S5_LANES, 2 * S5_LANES)
    for k in range(S5_SUBCHUNKS):
        sub = jnp.where(d == 0, k, S5_SUBCHUNKS - 1 - k)
        off = pl.multiple_of(sub * steps, steps)
        u_bt = u_ref[:, pl.ds(off, steps), :].reshape(rows, SEG).astype(BF16)
        u_tb = jnp.dot(perm_ref[...], u_bt, preferred_element_type=F32).astype(BF16)
        bu_scr[k] = jnp.dot(u_tb, bb_ref[...], preferred_element_type=F32)
        for i in range(steps):
            for g in range(ngrp):
                rr = slice(i * nb + g * SUBLANES, i * nb + (g + 1) * SUBLANES)
                n_re = a_re * s_re[g] - a_im * s_im[g] + bu_scr[k, rr, re_cols]
                n_im = a_re * s_im[g] + a_im * s_re[g] + bu_scr[k, rr, im_cols]
                bu_scr[k, rr, re_cols] = n_re
                bu_scr[k, rr, im_cols] = n_im
                s_re[g], s_im[g] = n_re, n_im
        y_tb = _dot(bu_scr[k], cc_ref[...])
        hi = y_tb.astype(BF16)
        lo = (y_tb - hi.astype(F32)).astype(BF16)
        y_bt = (jnp.dot(permt_ref[...], hi, preferred_element_type=F32)
                + jnp.dot(permt_ref[...], lo, preferred_element_type=F32))
        y_ref[:, pl.ds(off, steps), :] = y_bt.reshape(nb, steps, SEG)
    for g in range(ngrp):
        s_scr[0, g * SUBLANES:(g + 1) * SUBLANES, :] = s_re[g]
        s_scr[1, g * SUBLANES:(g + 1) * SUBLANES, :] = s_im[g]
    if want_fin:
        fin_ref[...] = s_scr[...]


def _s5_call(p3, bblk, a_tab, cblk, s0, *, nb, t, steps, want_fin):
    span = steps * S5_SUBCHUNKS
    nch = t // span
    rows = steps * nb
    r = np.arange(rows)
    perm_np = np.zeros((2, rows, rows), np.float32)
    perm_np[0, r, (r % nb) * steps + r // nb] = 1.0
    perm_np[1, r, (r % nb) * steps + (steps - 1 - r // nb)] = 1.0
    perm = jnp.asarray(perm_np, BF16)
    permt = jnp.asarray(perm_np.transpose(0, 2, 1), BF16)

    def chunk(d, c):
        return jnp.where(d == 0, c, nch - 1 - c)

    out_specs = [pl.BlockSpec((None, nb, span, SEG), lambda d, c: (d, 0, chunk(d, c), 0))]
    out_shape = [jax.ShapeDtypeStruct((2, nb, t, SEG), F32)]
    if want_fin:
        out_specs.append(pl.BlockSpec((None, 2, nb, S5_LANES), lambda d, c: (d, 0, 0, 0)))
        out_shape.append(jax.ShapeDtypeStruct((2, 2, nb, S5_LANES), F32))
    return pl.pallas_call(
        functools.partial(_s5_kernel, nb=nb, steps=steps, want_fin=want_fin),
        grid=(2, nch),
        in_specs=[
            pl.BlockSpec((nb, span, SEG), lambda d, c: (0, chunk(d, c), S_U)),
            pl.BlockSpec((None, rows, rows), lambda d, c: (d, 0, 0)),
            pl.BlockSpec((None, rows, rows), lambda d, c: (d, 0, 0)),
            pl.BlockSpec((None, SEG, 2 * S5_LANES), lambda d, c: (d, 0, 0)),
            pl.BlockSpec((None, 2, S5_LANES), lambda d, c: (d, 0, 0)),
            _const_spec((2 * S5_LANES, SEG)),
            pl.BlockSpec((None, 2, nb, S5_LANES), lambda d, c: (d, 0, 0, 0)),
        ],
        out_specs=out_specs,
        out_shape=out_shape,
        scratch_shapes=[pltpu.VMEM((2, nb, S5_LANES), F32),
                        pltpu.VMEM((S5_SUBCHUNKS, rows, 2 * S5_LANES), F32)],
        compiler_params=_cparams(("arbitrary", "arbitrary")),
        name="s5_scan",
    )(p3, perm, permt, bblk, a_tab, cblk, s0)


def _ret_kernel(*refs, rope, has_s0, want_fin, nch):
    refs = list(refs)
    q_ref, k_ref, v_ref, g_ref = refs[:4]
    del refs[:4]
    if rope:
        qr_ref, kr_ref, cos_ref, sin_ref = refs[:4]
        del refs[:4]
    (dm_ref, hm_ref, qdf_ref, qdb_ref, kdft_ref, kdbt_ref, cdf_ref, cdb_ref, bd_ref, gm_ref,
     gn_ref) = refs[:11]
    del refs[:11]
    if has_s0:
        s0_ref = refs.pop(0)
    o_ref = refs.pop(0)
    fin_ref = refs.pop(0) if want_fin else None
    sb_all, sf_scr, sb_scr = refs

    p = pl.program_id(1)
    c = pl.program_id(2)
    hd = RET_DIM

    def init_state(dst, d):
        dst[...] = jnp.zeros_like(dst)
        if has_s0:
            for h in range(RET_HEADS):
                dst[h * hd:(h + 1) * hd, h * hd:(h + 1) * hd] = s0_ref[d, h]

    def write_final(src, d):
        for h in range(RET_HEADS):
            fin_ref[d, h] = src[h * hd:(h + 1) * hd, h * hd:(h + 1) * hd]

    def roped_kt():
        k = k_ref[...]
        if rope:
            k = k * cos_ref[...] + kr_ref[...] * sin_ref[...]
        return k.T

    @pl.when(p == 0)
    def _():
        ck = nch - 1 - c

        @pl.when(c == 0)
        def _():
            init_state(sb_scr, 1)

        kt = roped_kt()
        s_old = sb_scr[...]
        sb_all[ck] = s_old
        sb_scr[...] = s_old * cdb_ref[...] + _dot(kt * kdbt_ref[...], v_ref[...]) * bd_ref[...]
        if want_fin:
            @pl.when(c == nch - 1)
            def _():
                write_final(sb_scr, 1)

    @pl.when(p == 1)
    def _():
        @pl.when(c == 0)
        def _():
            init_state(sf_scr, 0)

        q = q_ref[...]
        if rope:
            q = q * cos_ref[...] + qr_ref[...] * sin_ref[...]
        kt = roped_kt()
        ktb = kt.astype(BF16)
        vb = v_ref[...].astype(BF16)
        s_f = sf_scr[...]
        o = _dot(q, s_f) * qdf_ref[...] + _dot(q, sb_all[c]) * qdb_ref[...]
        for h in range(RET_HEADS):
            hm = hm_ref[h:h + 1, :]
            att = _dot(q * hm, ktb) * dm_ref[h]
            o = o + _dot(att, vb) * hm
        sf_scr[...] = s_f * cdf_ref[...] + _dot(kt * kdft_ref[...], vb) * bd_ref[...]
        hi = o.astype(BF16)
        lo = (o - hi.astype(F32)).astype(BF16)
        mu = (jnp.dot(hi, gm_ref[...], preferred_element_type=F32)
              + jnp.dot(lo, gm_ref[...], preferred_element_type=F32))
        oc = o - mu
        var = _dot(oc * oc, gm_ref[...])
        on = oc * lax.rsqrt(var + EPS) * gn_ref[...]
        g = g_ref[...]
        o_ref[...] = (g * _sigmoid(g)) * on
        if want_fin:
            @pl.when(c == nch - 1)
            def _():
                write_final(sf_scr, 0)


def _ret_call(p3, tabs, gn, s0, rope_tabs, *, nb, t, tc, want_fin):
    nch = t // tc
    rope = rope_tabs is not None
    has_s0 = s0 is not None

    def kv_chunk(p, c):
        return jnp.where(p == 0, nch - 1 - c, c)

    def q_chunk(p, c):
        return jnp.where(p == 0, 0, c)

    def seg_spec(seg, cmap):
        return pl.BlockSpec((None, tc, SEG), lambda b, p, c: (b, cmap(p, c), seg))

    args = [p3, p3, p3, p3]
    in_specs = [seg_spec(S_RQ, q_chunk), seg_spec(S_RK, kv_chunk), seg_spec(S_RV, kv_chunk),
                seg_spec(S_RG, q_chunk)]
    if rope:
        args += [p3, p3, rope_tabs[0], rope_tabs[1]]
        in_specs += [seg_spec(S_RQR, q_chunk), seg_spec(S_RKR, kv_chunk),
                     pl.BlockSpec((tc, SEG), lambda b, p, c: (kv_chunk(p, c), 0)),
                     pl.BlockSpec((tc, SEG), lambda b, p, c: (kv_chunk(p, c), 0))]
    args += list(tabs) + [gn]
    in_specs += [_const_spec(a.shape) for a in tabs] + [_const_spec((1, SEG))]
    if has_s0:
        args.append(s0)
        in_specs.append(pl.BlockSpec((None, 2, RET_HEADS, RET_DIM, RET_DIM),
                                     lambda b, p, c: (b, 0, 0, 0, 0)))
    out_specs = [pl.BlockSpec((None, tc, SEG), lambda b, p, c: (b, q_chunk(p, c), 0))]
    out_shape = [jax.ShapeDtypeStruct((nb, t, SEG), F32)]
    if want_fin:
        out_specs.append(pl.BlockSpec((None, 2, RET_HEADS, RET_DIM, RET_DIM),
                                      lambda b, p, c: (b, 0, 0, 0, 0)))
        out_shape.append(jax.ShapeDtypeStruct((nb, 2, RET_HEADS, RET_DIM, RET_DIM), F32))
    return pl.pallas_call(
        functools.partial(_ret_kernel, rope=rope, has_s0=has_s0, want_fin=want_fin, nch=nch),
        grid=(nb, 2, nch),
        in_specs=in_specs,
        out_specs=out_specs,
        out_shape=out_shape,
        scratch_shapes=[pltpu.VMEM((nch, SEG, SEG), F32),
                        pltpu.VMEM((SEG, SEG), F32),
                        pltpu.VMEM((SEG, SEG), F32)],
        compiler_params=_cparams(("arbitrary", "arbitrary", "arbitrary")),
        name="retention",
    )(*args)


def _mla_kernel(*refs, t, tq, nctx, want_cache):
    refs = list(refs)
    cq_ref, ckv_ref, krp_ref = refs[:3]
    del refs[:3]
    has_ctx = nctx > 0
    if has_ctx:
        krr_ref, cm_ref, sm_ref, clat_ref, ckrp_ref = refs[:5]
        del refs[:5]
    qn_ref, kvn_ref, wq_ref = refs[:3]
    del refs[:3]
    if has_ctx:
        wqr_ref = refs.pop(0)
    wk_ref, wv_ref = refs[:2]
    del refs[:2]
    o_ref = refs.pop(0)
    lat_ref = refs.pop(0) if want_cache else None
    k_scr, v_scr = refs
    qi = pl.program_id(1)

    @pl.when(qi == 0)
    def _():
        lat = _rms(ckv_ref[...], kvn_ref[...])
        if want_cache:
            lat_ref[...] = lat
        latb = lat.astype(BF16)
        kr = krp_ref[...]
        if has_ctx:
            kr = kr * cm_ref[...] + krr_ref[...] * sm_ref[...]
        kk = jnp.dot(latb, wk_ref[...], preferred_element_type=F32)
        for h in range(MLA_HEADS):
            k_scr[h, 0:t, :] = (kk[:, h * MLA_SLAB:(h + 1) * MLA_SLAB] + kr).astype(BF16)
        v_scr[0:t, :] = jnp.dot(latb, wv_ref[...], preferred_element_type=F32).astype(BF16)
        if has_ctx:
            latc = clat_ref[...].astype(BF16)
            kkc = jnp.dot(latc, wk_ref[...], preferred_element_type=F32)
            krc = ckrp_ref[...]
            for h in range(MLA_HEADS):
                k_scr[h, t:t + nctx, :] = (kkc[:, h * MLA_SLAB:(h + 1) * MLA_SLAB] + krc).astype(BF16)
            v_scr[t:t + nctx, :] = jnp.dot(latc, wv_ref[...], preferred_element_type=F32).astype(BF16)

    a = _rms(cq_ref[...], qn_ref[...], n=MLA_Q_LORA).astype(BF16)
    q = jnp.dot(a, wq_ref[...], preferred_element_type=F32)
    if has_ctx:
        qrot = jnp.dot(a, wqr_ref[...], preferred_element_type=F32)
        r0 = pl.multiple_of(qi * tq, tq)
        cm = cm_ref[pl.ds(r0, tq), :]
        sm = sm_ref[pl.ds(r0, tq), :]
    scale = (MLA_NOPE + MLA_ROPE) ** -0.5
    lane = lax.broadcasted_iota(jnp.int32, (tq, SEG), 1)
    acc = jnp.zeros((tq, SEG), F32)
    for h in range(MLA_HEADS):
        qh = q[:, h * MLA_SLAB:(h + 1) * MLA_SLAB]
        if has_ctx:
            qh = qh * cm + qrot[:, h * MLA_SLAB:(h + 1) * MLA_SLAB] * sm
        s = _dot_nt(qh * scale, k_scr[h])
        m = jnp.max(s, axis=-1, keepdims=True)
        e = jnp.exp(s - m)
        l = jnp.sum(e, axis=-1, keepdims=True)
        oh = jnp.dot(e.astype(BF16), v_scr[...], preferred_element_type=F32) * (1.0 / l)
        acc = jnp.where((lane >= h * MLA_V) & (lane < (h + 1) * MLA_V), oh, acc)
    o_ref[...] = acc


def _mla_call(p3, qn, kvn, wq, wqr, wk, wv, rope_tabs, ctx, *, nb, t, tq, want_cache):
    has_ctx = ctx is not None
    nctx = ctx[0].shape[1] if has_ctx else 0
    s_tot = t + nctx
    half = MLA_KV_LORA

    args = [p3, p3, p3]
    in_specs = [
        pl.BlockSpec((None, tq, SEG), lambda b, i: (b, i, S_CQ)),
        pl.BlockSpec((None, t, half), lambda b, i: (b, 0, 2 * S_KV)),
        pl.BlockSpec((None, t, half), lambda b, i: (b, 0, 2 * S_KV + 1)),
    ]
    if has_ctx:
        args += [p3, rope_tabs[0], rope_tabs[1], ctx[0], ctx[1]]
        in_specs += [
            pl.BlockSpec((None, t, half), lambda b, i: (b, 0, 2 * S_KRR)),
            _const_spec((t, MLA_SLAB)),
            _const_spec((t, MLA_SLAB)),
            pl.BlockSpec((None, nctx, MLA_KV_LORA), lambda b, i: (b, 0, 0)),
            pl.BlockSpec((None, nctx, MLA_SLAB), lambda b, i: (b, 0, 0)),
        ]
    args += [qn, kvn, wq]
    in_specs += [_const_spec(qn.shape), _const_spec(kvn.shape), _const_spec(wq.shape)]
    if has_ctx:
        args.append(wqr)
        in_specs.append(_const_spec(wqr.shape))
    args += [wk, wv]
    in_specs += [_const_spec(wk.shape), _const_spec(wv.shape)]
    out_specs = [pl.BlockSpec((None, tq, SEG), lambda b, i: (b, i, 0))]
    out_shape = [jax.ShapeDtypeStruct((nb, t, SEG), F32)]
    if want_cache:
        out_specs.append(pl.BlockSpec((None, t, MLA_KV_LORA), lambda b, i: (b, 0, 0)))
        out_shape.append(jax.ShapeDtypeStruct((nb, t, MLA_KV_LORA), F32))
    return pl.pallas_call(
        functools.partial(_mla_kernel, t=t, tq=tq, nctx=nctx, want_cache=want_cache),
        grid=(nb, t // tq),
        in_specs=in_specs,
        out_specs=out_specs,
        out_shape=out_shape,
        scratch_shapes=[pltpu.VMEM((MLA_HEADS, s_tot, MLA_SLAB), BF16),
                        pltpu.VMEM((s_tot, SEG), BF16)],
        compiler_params=_cparams(("arbitrary", "arbitrary")),
        name="mla",
    )(*args)


def _merge_kernel(x_ref, mod_ref, npre_ref, npost_ref, yf_ref, yb_ref, u_ref, ret_ref,
                  cx_ref, cb_ref, cc_ref, cxp_ref, ccp_ref, cxn_ref, ccn_ref, mla_ref,
                  s5d_ref, wglu_ref, cw_ref, cbias_ref, wbr_ref, wg_ref, bg_ref, wo_ref,
                  o_ref, h_scr, *, tm, nt):
    i = pl.program_id(1)
    x = x_ref[...]
    sh = mod_ref[3:4, :]
    sc = mod_ref[4:5, :]
    gt = mod_ref[5:6, :]
    h_scr[...] = (_rms(x, npre_ref[...]) * (1.0 + sc) + sh).astype(BF16)

    y = yf_ref[...] + yb_ref[...] + s5d_ref[...] * u_ref[...]
    z5 = _gelu_tanh(y)
    b_s5 = z5 * _sigmoid(_dot(z5, wglu_ref[...]))

    z = cc_ref[...] * cx_ref[...]
    zprev = ccp_ref[HALO - 1:HALO, :] * cxp_ref[HALO - 1:HALO, :] * jnp.where(i > 0, 1.0, 0.0)
    znext = ccn_ref[0:1, :] * cxn_ref[0:1, :] * jnp.where(i < nt - 1, 1.0, 0.0)
    row = lax.broadcasted_iota(jnp.int32, (tm, SEG), 0)
    z_dn = jnp.where(row == 0, zprev, pltpu.roll(z, 1, axis=0))
    z_up = jnp.where(row == tm - 1, znext, pltpu.roll(z, tm - 1, axis=0))
    b_conv = cb_ref[...] * (cbias_ref[...] + z_dn * cw_ref[0:1, :] + z * cw_ref[1:2, :]
                            + z_up * cw_ref[2:3, :])

    branches = (b_s5, ret_ref[...], b_conv, mla_ref[...])
    merged = jnp.zeros((tm, D_MODEL), F32)
    for n in range(N_BRANCH):
        gate = _sigmoid(jnp.dot(h_scr[...], wg_ref[:, n * D_MODEL:(n + 1) * D_MODEL],
                                preferred_element_type=F32) + bg_ref[:, n * D_MODEL:(n + 1) * D_MODEL])
        merged = merged + gate * _dot(branches[n], wbr_ref[n])
    m = _dot(merged, wo_ref[...])
    o_ref[...] = x + gt * _rms(m, npost_ref[...])


def _merge_call(x, mod_l, npre, npost, y_s5, ret_o, p3, mla_o, s5d, wglu, cw, cbias, wbr, wg, bg, wo,
                *, layer, nb, t, tm, row_of_batch):
    nt = t // tm
    hb = tm // HALO
    n_halo = t // HALO

    def dir_spec(d):
        return pl.BlockSpec((None, None, tm, SEG), lambda b, i: (d, b, i, 0))

    def seg_spec(seg):
        return pl.BlockSpec((None, tm, SEG), lambda b, i: (b, i, seg))

    def prev_spec(seg):
        return pl.BlockSpec((None, HALO, SEG), lambda b, i: (b, jnp.maximum(i * hb - 1, 0), seg))

    def next_spec(seg):
        return pl.BlockSpec((None, HALO, SEG), lambda b, i: (b, jnp.minimum((i + 1) * hb, n_halo - 1), seg))

    return pl.pallas_call(
        functools.partial(_merge_kernel, tm=tm, nt=nt),
        grid=(nb, nt),
        in_specs=[
            pl.BlockSpec((tm, D_MODEL), lambda b, i: (b * nt + i, 0)),
            pl.BlockSpec((None, N_MOD, D_MODEL), lambda b, i: (row_of_batch(b), 0, 0)),
            _const_spec((1, D_MODEL)),
            _const_spec((1, D_MODEL)),
            dir_spec(0), dir_spec(1), seg_spec(S_U),
            pl.BlockSpec((None, tm, SEG), lambda b, i: (b, i, 0)),
            seg_spec(S_CX), seg_spec(S_CB), seg_spec(S_CC),
            prev_spec(S_CX), prev_spec(S_CC), next_spec(S_CX), next_spec(S_CC),
            pl.BlockSpec((None, tm, SEG), lambda b, i: (b, i, 0)),
            _const_spec(s5d.shape), _const_spec(wglu.shape), _const_spec(cw.shape),
            _const_spec(cbias.shape), _resident_spec(wbr.shape[1:], (layer,)),
            _resident_spec(wg.shape[1:], (layer,)), _const_spec(bg.shape),
            _resident_spec(wo.shape[1:], (layer,)),
        ],
        out_specs=pl.BlockSpec((tm, D_MODEL), lambda b, i: (b * nt + i, 0)),
        out_shape=jax.ShapeDtypeStruct((nb * t, D_MODEL), F32),
        scratch_shapes=[pltpu.VMEM((tm, D_MODEL), BF16)],
        compiler_params=_cparams(("arbitrary", "arbitrary")),
        name="merge",
    )(x, mod_l, npre, npost, y_s5, y_s5, p3, ret_o, p3, p3, p3, p3, p3, p3, p3, mla_o,
      s5d, wglu, cw, cbias, wbr, wg, bg, wo)


def _rot_half(w, heads, dim):
    w4 = w.reshape(w.shape[0], heads, dim)
    half = dim // 2
    return jnp.concatenate([-w4[..., half:], w4[..., :half]], axis=-1).reshape(w.shape[0], heads * dim)


def _axial_angles(t, dim):
    rows = t // GRID_W
    row = np.repeat(np.arange(rows, dtype=np.float64), GRID_W)
    col = np.tile(np.arange(GRID_W, dtype=np.float64), rows)
    quarter = dim // 4
    inv = ROPE_BASE ** (-np.arange(quarter, dtype=np.float64) / quarter)
    ang = np.concatenate([row[:, None] * inv, col[:, None] * inv], axis=-1)
    return np.cos(ang).astype(np.float32), np.sin(ang).astype(np.float32)


def _rope_tables(t):
    cos, sin = _axial_angles(t, RET_DIM)
    ret_cos = np.tile(np.concatenate([cos, cos], -1), (1, RET_HEADS))
    ret_sin = np.tile(np.concatenate([sin, sin], -1), (1, RET_HEADS))
    cos, sin = _axial_angles(t, MLA_ROPE)
    pad = MLA_SLAB - MLA_NOPE - MLA_ROPE
    mla_cos = np.concatenate([np.ones((t, MLA_NOPE), np.float32), cos, cos, np.zeros((t, pad), np.float32)], -1)
    mla_sin = np.concatenate([np.zeros((t, MLA_NOPE), np.float32), sin, sin, np.zeros((t, pad), np.float32)], -1)
    return ((jnp.asarray(ret_cos), jnp.asarray(ret_sin)), (jnp.asarray(mla_cos), jnp.asarray(mla_sin)))


def _in_weights(w_in, rope):
    sizes = (BRANCH_WIDTH,) * 8 + (MLA_Q_LORA, MLA_KV_LORA, MLA_ROPE)
    offs = np.cumsum((0,) + sizes)
    u, rq, rk, rv, rg, cx, cb, cc, cq, ckv, kr = [w_in[:, offs[i]:offs[i + 1]] for i in range(11)]
    rk = rk * (RET_DIM ** -0.5)
    d = w_in.shape[0]

    def place_kr(w):
        return jnp.concatenate([jnp.zeros((d, MLA_NOPE), F32), w,
                                jnp.zeros((d, MLA_SLAB - MLA_NOPE - MLA_ROPE), F32)], -1)

    cols = [u, rq, rk, rv, rg, cx, cb, cc,
            jnp.concatenate([cq, jnp.zeros((d, SEG - MLA_Q_LORA), F32)], -1),
            jnp.concatenate([ckv, place_kr(kr)], -1)]
    if rope:
        cols += [_rot_half(rq, RET_HEADS, RET_DIM), _rot_half(rk, RET_HEADS, RET_DIM),
                 jnp.concatenate([place_kr(_rot_half(kr, 1, MLA_ROPE)), jnp.zeros((d, SEG - MLA_SLAB), F32)], -1)]
    return jnp.concatenate(cols, axis=-1).astype(BF16)


def _mla_weights(w_uq, w_ukv):
    dq = MLA_NOPE + MLA_ROPE
    w4 = w_uq.reshape(MLA_Q_LORA, MLA_HEADS, dq)
    nope, ropew = w4[..., :MLA_NOPE], w4[..., MLA_NOPE:]
    half = MLA_ROPE // 2
    rot = jnp.concatenate([-ropew[..., half:], ropew[..., :half]], -1)
    zpad = jnp.zeros((MLA_Q_LORA, MLA_HEADS, MLA_SLAB - dq), F32)
    wq = jnp.concatenate([nope, ropew, zpad], -1).reshape(MLA_Q_LORA, MLA_HEADS * MLA_SLAB)
    wqr = jnp.concatenate([jnp.zeros_like(nope), rot, zpad], -1).reshape(MLA_Q_LORA, MLA_HEADS * MLA_SLAB)
    rpad = jnp.zeros((SEG - MLA_Q_LORA, MLA_HEADS * MLA_SLAB), F32)
    wq = jnp.concatenate([wq, rpad], 0).astype(BF16)
    wqr = jnp.concatenate([wqr, rpad], 0).astype(BF16)
    kv4 = w_ukv.reshape(MLA_KV_LORA, MLA_HEADS, MLA_NOPE + MLA_V)
    wk = jnp.concatenate([kv4[..., :MLA_NOPE], jnp.zeros((MLA_KV_LORA, MLA_HEADS, MLA_SLAB - MLA_NOPE), F32)],
                         -1).reshape(MLA_KV_LORA, MLA_HEADS * MLA_SLAB).astype(BF16)
    wv = kv4[..., MLA_NOPE:].reshape(MLA_KV_LORA, MLA_HEADS * MLA_V).astype(BF16)
    return wq, wqr, wk, wv


def _s5_tables(lam_re, lam_im, log_dt, b_re, b_im, c_re, c_im):
    dt = jnp.exp(log_dt)[..., None]
    mag = jnp.exp(lam_re * dt)
    ab_re = mag * jnp.cos(lam_im * dt)
    ab_im = mag * jnp.sin(lam_im * dt)
    den = lam_re * lam_re + lam_im * lam_im
    f_re = ((ab_re - 1.0) * lam_re + ab_im * lam_im) / den
    f_im = (ab_im * lam_re - (ab_re - 1.0) * lam_im) / den
    bb_re = f_re[..., None] * b_re - f_im[..., None] * b_im
    bb_im = f_re[..., None] * b_im + f_im[..., None] * b_re
    eye = jnp.eye(S5_GROUPS, dtype=F32)
    blk_in = lambda w: jnp.einsum('dgph,gk->dghkp', w, eye).reshape(2, BRANCH_WIDTH, S5_LANES)
    bblk = jnp.concatenate([blk_in(bb_re), blk_in(bb_im)], -1).astype(BF16)
    blk_out = lambda w: jnp.einsum('ghp,gk->gpkh', w, eye).reshape(S5_LANES, BRANCH_WIDTH)
    cblk = jnp.concatenate([blk_out(c_re), -blk_out(c_im)], 0).astype(BF16)
    a_tab = jnp.stack([ab_re.reshape(2, S5_LANES), ab_im.reshape(2, S5_LANES)], axis=1)
    return bblk, a_tab, cblk


def _ret_tables(decay_logit, c):
    lg = jax.nn.log_sigmoid(decay_logit.astype(F32))
    lf = lg[0][:, None, None]
    lb = lg[1][:, None, None]
    idx = jnp.arange(c, dtype=F32)
    diff = idx[:, None] - idx[None, :]
    dm = jnp.where(diff >= 0, jnp.exp(lf * jnp.maximum(diff, 0.0)), jnp.exp(lb * jnp.maximum(-diff, 0.0)))
    lfl = jnp.repeat(lg[0], RET_DIM)
    lbl = jnp.repeat(lg[1], RET_DIM)
    qdf = jnp.exp(lfl[None, :] * (idx[:, None] + 1.0))
    qdb = jnp.exp(lbl[None, :] * (c - idx[:, None]))
    kdft = jnp.exp(lfl[:, None] * (c - 1.0 - idx[None, :]))
    kdbt = jnp.exp(lbl[:, None] * idx[None, :])
    cdf = jnp.broadcast_to(jnp.exp(lfl * c)[:, None], (SEG, SEG))
    cdb = jnp.broadcast_to(jnp.exp(lbl * c)[:, None], (SEG, SEG))
    head = np.arange(SEG) // RET_DIM
    hm = jnp.asarray((head[None, :] == np.arange(RET_HEADS)[:, None]).astype(np.float32))
    bd_np = (head[:, None] == head[None, :]).astype(np.float32)
    bd = jnp.asarray(bd_np)
    gm = jnp.asarray(bd_np / RET_DIM, BF16)
    return dm, hm, qdf, qdb, kdft, kdbt, cdf, cdb, bd, gm


def _layer(x, mod_l, w, *, layer, nb, t, tm, tm_ffn, row_of_batch, ctx, rope_tabs):
    is_ctx_pass = ctx is None
    row_of_tile = lambda i: row_of_batch((i * tm_ffn) // t)
    x = _ffn_call(x, mod_l, w['npre'][0], w['npost'][0], *w['ffn'], lj=(layer, 0), k=0, tm=tm_ffn,
                  row_of_tile=row_of_tile)

    nseg = NSEG_PLAIN if is_ctx_pass else NSEG_ROPE
    p = _in_call(x, mod_l, w['npre'][1], w['w_in'], nb=nb, t=t, tm=tm, nseg=nseg, row_of_batch=row_of_batch)
    p3 = p.reshape(nb, t, nseg * SEG)

    steps = 512 // nb
    if is_ctx_pass:
        s5_0 = jnp.zeros((2, 2, nb, S5_LANES), F32)
        ret_0, mla_ctx, ret_rope, mla_rope = None, None, None, None
    else:
        s5_0, ret_0, mla_ctx = ctx
        ret_rope, mla_rope = rope_tabs
    s5_out = _s5_call(p3, w['s5_bblk'], w['s5_a'], w['s5_cblk'], s5_0,
                      nb=nb, t=t, steps=steps, want_fin=is_ctx_pass)
    ret_out = _ret_call(p3, w['ret_tabs'], w['ret_gn'], ret_0, ret_rope,
                        nb=nb, t=t, tc=256, want_fin=is_ctx_pass)
    mla_out = _mla_call(p3, w['mla_qn'], w['mla_kvn'], w['mla_wq'], w['mla_wqr'], w['mla_wk'], w['mla_wv'],
                        mla_rope, mla_ctx, nb=nb, t=t, tq=min(t, 512), want_cache=is_ctx_pass)

    x = _merge_call(x, mod_l, w['npre'][1], w['npost'][1], s5_out[0], ret_out[0], p3, mla_out[0],
                    w['s5_d'], w['s5_wglu'], w['conv_w'], w['conv_b'], w['w_branch'], w['w_gate'],
                    w['b_gate'], w['w_o'], layer=layer, nb=nb, t=t, tm=tm, row_of_batch=row_of_batch)
    x = _ffn_call(x, mod_l, w['npre'][2], w['npost'][2], *w['ffn'], lj=(layer, 1), k=2, tm=tm_ffn,
                  row_of_tile=row_of_tile)

    state = None
    if is_ctx_pass:
        fin = s5_out[1].reshape(2, 2, nb, S5_GROUPS, S5_STATE).transpose(2, 0, 3, 4, 1)
        kr = p3[:, :, S_KV * SEG + MLA_KV_LORA + MLA_NOPE:S_KV * SEG + MLA_KV_LORA + MLA_NOPE + MLA_ROPE]
        cache = jnp.concatenate([mla_out[1], kr], axis=-1)
        state = (fin, ret_out[1], cache)
    return x, state


def kernel(x_prompt, x_sample, state_s5, state_ret, cache_mla, c, c_ctx, w_mod, b_mod, norm_pre, norm_post, ffn_w1, ffn_w3, ffn_w2, w_in, s5_lam_re, s5_lam_im, s5_log_dt, s5_b_re, s5_b_im, s5_c_re, s5_c_im, s5_d, s5_w_glu, ret_decay, ret_gn, conv_w, conv_b, mla_q_norm, mla_w_uq, mla_kv_norm, mla_w_ukv, w_branch, w_gate, b_gate, w_o):
    bp, tp, _ = x_prompt.shape
    bs, ts, _ = x_sample.shape
    assert 1 + bs <= MOD_ROWS
    past = cache_mla.shape[2]

    cond = jnp.concatenate([c_ctx[None, :], c, jnp.zeros((MOD_ROWS - 1 - bs, D_MODEL), F32)], axis=0)
    mod = _mod_call(cond, w_mod, b_mod).reshape(DEPTH, MOD_ROWS, N_MOD, D_MODEL)

    rope_tabs = _rope_tables(ts)
    xp = x_prompt.reshape(bp * tp, D_MODEL)
    xs = x_sample.reshape(bs * ts, D_MODEL)
    s5_list, ret_list, mla_list = [], [], []
    ffn = (ffn_w1.astype(BF16), ffn_w3.astype(BF16), ffn_w2.astype(BF16))
    w_branch_b, w_gate_b, w_o_b = w_branch.astype(BF16), w_gate.astype(BF16), w_o.astype(BF16)
    for l in range(DEPTH):
        wq, wqr, wk, wv = _mla_weights(mla_w_uq[l], mla_w_ukv[l])
        bblk, a_tab, cblk = _s5_tables(s5_lam_re[l], s5_lam_im[l], s5_log_dt[l], s5_b_re[l], s5_b_im[l],
                                       s5_c_re[l], s5_c_im[l])
        w = dict(
            npre=[norm_pre[l, i][None, :] for i in range(3)],
            npost=[norm_post[l, i][None, :] for i in range(3)],
            ffn=ffn,
            w_in=_in_weights(w_in[l], rope=True),
            s5_bblk=bblk, s5_a=a_tab, s5_cblk=cblk,
            s5_d=s5_d[l][None, :], s5_wglu=s5_w_glu[l].astype(BF16),
            ret_tabs=_ret_tables(ret_decay[l], 256), ret_gn=ret_gn[l][None, :],
            conv_w=conv_w[l], conv_b=conv_b[l][None, :],
            mla_qn=jnp.concatenate([mla_q_norm[l], jnp.zeros((SEG - MLA_Q_LORA,), F32)])[None, :],
            mla_kvn=mla_kv_norm[l][None, :],
            mla_wq=wq, mla_wqr=wqr, mla_wk=wk, mla_wv=wv,
            w_branch=w_branch_b, w_gate=w_gate_b, b_gate=b_gate[l][None, :], w_o=w_o_b,
        )
        xp, (s5_s, ret_s, mla_c) = _layer(xp, mod[l], w, layer=l, nb=bp, t=tp, tm=256, tm_ffn=512,
                                          row_of_batch=lambda b: 0, ctx=None, rope_tabs=None)
        s5_list.append(s5_s)
        ret_list.append(ret_s)
        mla_list.append(mla_c)

        s5_0 = state_s5[:, l].transpose(1, 4, 0, 2, 3).reshape(2, 2, bs, S5_LANES)
        cache_l = cache_mla[:, l]
        ctx_lat = cache_l[..., :MLA_KV_LORA]
        ctx_krp = jnp.concatenate([jnp.zeros((bs, past, MLA_NOPE), F32), cache_l[..., MLA_KV_LORA:],
                                   jnp.zeros((bs, past, MLA_SLAB - MLA_NOPE - MLA_ROPE), F32)], -1)
        xs, _ = _layer(xs, mod[l], w, layer=l, nb=bs, t=ts, tm=512, tm_ffn=512, row_of_batch=lambda b: 1 + b,
                       ctx=(s5_0, state_ret[:, l], (ctx_lat, ctx_krp)), rope_tabs=rope_tabs)
    return (xp.reshape(bp, tp, D_MODEL), xs.reshape(bs, ts, D_MODEL),
            jnp.stack(s5_list, axis=1), jnp.stack(ret_list, axis=1), jnp.stack(mla_list, axis=1))
```

```python
import functools

import numpy as np
import jax
import jax.numpy as jnp
from jax import lax
from jax.experimental import pallas as pl
from jax.experimental.pallas import tpu as pltpu

F32 = jnp.float32
BF16 = jnp.bfloat16

D_MODEL = 1024
DEPTH = 2
GRID_W = 64
EPS = 1e-6
ROPE_BASE = 10000.0
N_BRANCH = 4
BRANCH_WIDTH = D_MODEL // 4
N_MOD = 9
D_FF = 2816
S5_GROUP = 16
S5_GROUPS = BRANCH_WIDTH // S5_GROUP
S5_STATE = 64
S5_LANES = S5_GROUPS * S5_STATE
RET_HEADS = 4
RET_DIM = BRANCH_WIDTH // RET_HEADS
CONV_K = 3
MLA_HEADS = 4
MLA_Q_LORA = 192
MLA_KV_LORA = 128
MLA_NOPE = 64
MLA_ROPE = 32
MLA_V = BRANCH_WIDTH // MLA_HEADS
MLA_SLAB = 128

SUBLANES = 8
S5_SUBCHUNKS = 4
SEG = 256
FF_CHUNK = 256
N_FF_CHUNKS = D_FF // FF_CHUNK
MOD_ROWS = 16
HALO = 16
V7X_VMEM_LIMIT = 56 * 1024 * 1024

S_U, S_RQ, S_RK, S_RV, S_RG, S_CX, S_CB, S_CC, S_CQ, S_KV, S_RQR, S_RKR, S_KRR = range(13)
NSEG_PLAIN = 10
NSEG_ROPE = 13


def _cparams(sem):
    return pltpu.CompilerParams(dimension_semantics=sem, vmem_limit_bytes=V7X_VMEM_LIMIT)


def _dot(a, b):
    return jnp.dot(a.astype(BF16), b.astype(BF16), preferred_element_type=F32)


def _dot_nt(a, b):
    return lax.dot_general(a.astype(BF16), b.astype(BF16), (((1,), (1,)), ((), ())),
                           preferred_element_type=F32)


def _dot_tn(a, b):
    return lax.dot_general(a.astype(BF16), b.astype(BF16), (((0,), (0,)), ((), ())),
                           preferred_element_type=F32)


def _sigmoid(x):
    return 1.0 / (1.0 + jnp.exp(-x))


def _rms(x, g, n=None):
    n = x.shape[-1] if n is None else n
    ms = jnp.sum(x * x, axis=-1, keepdims=True) * (1.0 / n)
    return x * lax.rsqrt(ms + EPS) * g


def _gelu_tanh(x):
    return 0.5 * x * (1.0 + jnp.tanh(np.sqrt(2.0 / np.pi) * (x + 0.044715 * (x * x * x))))


def _const_spec(shape):
    nd = len(shape)
    return pl.BlockSpec(shape, lambda *_: (0,) * nd)


def _resident_spec(shape, lead=()):
    nd = len(shape)
    return pl.BlockSpec((None,) * len(lead) + tuple(shape), lambda *_: tuple(lead) + (0,) * nd,
                        pipeline_mode=pl.Buffered(1))


def _mod_kernel(c_ref, w_ref, b_ref, o_ref):
    c = c_ref[...]
    o_ref[...] = _dot(c * _sigmoid(c), w_ref[...]) + b_ref[...]


def _mod_call(cond, w_mod, b_mod):
    ncol = N_MOD * D_MODEL
    tn = D_MODEL
    return pl.pallas_call(
        _mod_kernel,
        grid=(DEPTH, ncol // tn),
        in_specs=[
            pl.BlockSpec((MOD_ROWS, D_MODEL), lambda l, j: (0, 0)),
            pl.BlockSpec((None, D_MODEL, tn), lambda l, j: (l, 0, j)),
            pl.BlockSpec((None, 1, tn), lambda l, j: (l, 0, j)),
        ],
        out_specs=pl.BlockSpec((None, MOD_ROWS, tn), lambda l, j: (l, 0, j)),
        out_shape=jax.ShapeDtypeStruct((DEPTH, MOD_ROWS, ncol), F32),
        compiler_params=_cparams(("arbitrary", "arbitrary")),
        name="mod",
    )(cond, w_mod, b_mod.reshape(DEPTH, 1, ncol))


def _ffn_kernel(x_ref, mod_ref, npre_ref, npost_ref, w1_ref, w3_ref, w2_ref, o_ref,
                h_scr, g_scr, *, k):
    x = x_ref[...]
    sh = mod_ref[3 * k:3 * k + 1, :]
    sc = mod_ref[3 * k + 1:3 * k + 2, :]
    gt = mod_ref[3 * k + 2:3 * k + 3, :]
    h_scr[...] = (_rms(x, npre_ref[...]) * (1.0 + sc) + sh).astype(BF16)
    for j in range(N_FF_CHUNKS):
        cols = slice(j * FF_CHUNK, (j + 1) * FF_CHUNK)
        a = jnp.dot(h_scr[...], w1_ref[:, cols], preferred_element_type=F32)
        b = jnp.dot(h_scr[...], w3_ref[:, cols], preferred_element_type=F32)
        g_scr[:, cols] = (a * _sigmoid(a) * b).astype(BF16)
    f = jnp.dot(g_scr[...], w2_ref[...], preferred_element_type=F32)
    o_ref[...] = x + (0.5 * gt) * _rms(f, npost_ref[...])


def _ffn_call(x, mod_l, npre, npost, w1, w3, w2, *, lj, k, tm, row_of_tile):
    n = x.shape[0]
    return pl.pallas_call(
        functools.partial(_ffn_kernel, k=k),
        grid=(n // tm,),
        in_specs=[
            pl.BlockSpec((tm, D_MODEL), lambda i: (i, 0)),
            pl.BlockSpec((None, N_MOD, D_MODEL), lambda i: (row_of_tile(i), 0, 0)),
            _const_spec((1, D_MODEL)),
            _const_spec((1, D_MODEL)),
            _resident_spec((D_MODEL, D_FF), lj),
            _resident_spec((D_MODEL, D_FF), lj),
            _resident_spec((D_FF, D_MODEL), lj),
        ],
        out_specs=pl.BlockSpec((tm, D_MODEL), lambda i: (i, 0)),
        out_shape=jax.ShapeDtypeStruct((n, D_MODEL), F32),
        scratch_shapes=[pltpu.VMEM((tm, D_MODEL), BF16), pltpu.VMEM((tm, D_FF), BF16)],
        compiler_params=_cparams(("arbitrary",)),
        name="ffn",
    )(x, mod_l, npre, npost, w1, w3, w2)


def _in_kernel(x_ref, mod_ref, npre_ref, w_ref, p_ref, h_scr, *, nseg):
    sh = mod_ref[3:4, :]
    sc = mod_ref[4:5, :]
    h_scr[...] = (_rms(x_ref[...], npre_ref[...]) * (1.0 + sc) + sh).astype(BF16)
    for j in range(nseg):
        cols = slice(j * SEG, (j + 1) * SEG)
        p_ref[:, cols] = jnp.dot(h_scr[...], w_ref[:, cols], preferred_element_type=F32).astype(BF16)


def _in_call(x, mod_l, npre, w_ext, *, nb, t, tm, nseg, row_of_batch):
    tt = t // tm
    return pl.pallas_call(
        functools.partial(_in_kernel, nseg=nseg),
        grid=(nb, tt),
        in_specs=[
            pl.BlockSpec((tm, D_MODEL), lambda b, i: (b * tt + i, 0)),
            pl.BlockSpec((None, N_MOD, D_MODEL), lambda b, i: (row_of_batch(b), 0, 0)),
            _const_spec((1, D_MODEL)),
            _resident_spec((D_MODEL, nseg * SEG)),
        ],
        out_specs=pl.BlockSpec((tm, nseg * SEG), lambda b, i: (b * tt + i, 0)),
        out_shape=jax.ShapeDtypeStruct((nb * t, nseg * SEG), BF16),
        scratch_shapes=[pltpu.VMEM((tm, D_MODEL), BF16)],
        compiler_params=_cparams(("arbitrary", "arbitrary")),
        name="in_proj",
    )(x, mod_l, npre, w_ext)


def _s5_kernel(*refs, nb, steps, want_fin):
    if want_fin:
        u_ref, perm_ref, permt_ref, bb_ref, a_ref, cc_ref, s0_ref, y_ref, fin_ref, s_scr, bu_scr = refs
    else:
        u_ref, perm_ref, permt_ref, bb_ref, a_ref, cc_ref, s0_ref, y_ref, s_scr, bu_scr = refs
        fin_ref = None
    d = pl.program_id(0)
    c = pl.program_id(1)
    rows = nb * steps
    ngrp = nb // SUBLANES

    @pl.when(c == 0)
    def _():
        s_scr[...] = s0_ref[...]

    a_re = jnp.broadcast_to(a_ref[0:1, :], (SUBLANES, S5_LANES))
    a_im = jnp.broadcast_to(a_ref[1:2, :], (SUBLANES, S5_LANES))
    s_re = [s_scr[0, g * SUBLANES:(g + 1) * SUBLANES, :] for g in range(ngrp)]
    s_im = [s_scr[1, g * SUBLANES:(g + 1) * SUBLANES, :] for g in range(ngrp)]
    re_cols = slice(0, S5_LANES)
    im_cols = slice(S5_LANES, 2 * S5_LANES)
    for k in range(S5_SUBCHUNKS):
        sub = jnp.where(d == 0, k, S5_SUBCHUNKS - 1 - k)
        off = pl.multiple_of(sub * steps, steps)
        u_bt = u_ref[:, pl.ds(off, steps), :].reshape(rows, SEG).astype(BF16)
        u_tb = jnp.dot(perm_ref[...], u_bt, preferred_element_type=F32).astype(BF16)
        bu_scr[k] = jnp.dot(u_tb, bb_ref[...], preferred_element_type=F32)
        for i in range(steps):
            for g in range(ngrp):
                rr = slice(i * nb + g * SUBLANES, i * nb + (g + 1) * SUBLANES)
                n_re = a_re * s_re[g] - a_im * s_im[g] + bu_scr[k, rr, re_cols]
                n_im = a_re * s_im[g] + a_im * s_re[g] + bu_scr[k, rr, im_cols]
                bu_scr[k, rr, re_cols] = n_re
                bu_scr[k, rr, im_cols] = n_im
                s_re[g], s_im[g] = n_re, n_im
        y_tb = _dot(bu_scr[k], cc_ref[...])
        hi = y_tb.astype(BF16)
        lo = (y_tb - hi.astype(F32)).astype(BF16)
        y_bt = (jnp.dot(permt_ref[...], hi, preferred_element_type=F32)
                + jnp.dot(permt_ref[...], lo, preferred_element_type=F32))
        y_ref[:, pl.ds(off, steps), :] = y_bt.reshape(nb, steps, SEG)
    for g in range(ngrp):
        s_scr[0, g * SUBLANES:(g + 1) * SUBLANES, :] = s_re[g]
        s_scr[1, g * SUBLANES:(g + 1) * SUBLANES, :] = s_im[g]
    if want_fin:
        fin_ref[...] = s_scr[...]


def _s5_call(p3, bblk, a_tab, cblk, s0, *, nb, t, steps, want_fin):
    span = steps * S5_SUBCHUNKS
    nch = t // span
    rows = steps * nb
    r = np.arange(rows)
    perm_np = np.zeros((2, rows, rows), np.float32)
    perm_np[0, r, (r % nb) * steps + r // nb] = 1.0
    perm_np[1, r, (r % nb) * steps + (steps - 1 - r // nb)] = 1.0
    perm = jnp.asarray(perm_np, BF16)
    permt = jnp.asarray(perm_np.transpose(0, 2, 1), BF16)

    def chunk(d, c):
        return jnp.where(d == 0, c, nch - 1 - c)

    out_specs = [pl.BlockSpec((None, nb, span, SEG), lambda d, c: (d, 0, chunk(d, c), 0))]
    out_shape = [jax.ShapeDtypeStruct((2, nb, t, SEG), F32)]
    if want_fin:
        out_specs.append(pl.BlockSpec((None, 2, nb, S5_LANES), lambda d, c: (d, 0, 0, 0)))
        out_shape.append(jax.ShapeDtypeStruct((2, 2, nb, S5_LANES), F32))
    return pl.pallas_call(
        functools.partial(_s5_kernel, nb=nb, steps=steps, want_fin=want_fin),
        grid=(2, nch),
        in_specs=[
            pl.BlockSpec((nb, span, SEG), lambda d, c: (0, chunk(d, c), S_U)),
            pl.BlockSpec((None, rows, rows), lambda d, c: (d, 0, 0)),
            pl.BlockSpec((None, rows, rows), lambda d, c: (d, 0, 0)),
            pl.BlockSpec((None, SEG, 2 * S5_LANES), lambda d, c: (d, 0, 0)),
            pl.BlockSpec((None, 2, S5_LANES), lambda d, c: (d, 0, 0)),
            _const_spec((2 * S5_LANES, SEG)),
            pl.BlockSpec((None, 2, nb, S5_LANES), lambda d, c: (d, 0, 0, 0)),
        ],
        out_specs=out_specs,
        out_shape=out_shape,
        scratch_shapes=[pltpu.VMEM((2, nb, S5_LANES), F32),
                        pltpu.VMEM((S5_SUBCHUNKS, rows, 2 * S5_LANES), F32)],
        compiler_params=_cparams(("arbitrary", "arbitrary")),
        name="s5_scan",
    )(p3, perm, permt, bblk, a_tab, cblk, s0)


def _ret_kernel(*refs, rope, has_s0, want_fin, nch):
    refs = list(refs)
    q_ref, k_ref, v_ref, g_ref = refs[:4]
    del refs[:4]
    if rope:
        qr_ref, kr_ref, cos_ref, sin_ref = refs[:4]
        del refs[:4]
    (dm_ref, hm_ref, qdf_ref, qdb_ref, kdft_ref, kdbt_ref, cdf_ref, cdb_ref, bd_ref, gm_ref,
     gn_ref) = refs[:11]
    del refs[:11]
    if has_s0:
        s0_ref = refs.pop(0)
    o_ref = refs.pop(0)
    fin_ref = refs.pop(0) if want_fin else None
    sb_all, sf_scr, sb_scr = refs

    p = pl.program_id(1)
    c = pl.program_id(2)
    hd = RET_DIM

    def init_state(dst, d):
        dst[...] = jnp.zeros_like(dst)
        if has_s0:
            for h in range(RET_HEADS):
                dst[h * hd:(h + 1) * hd, h * hd:(h + 1) * hd] = s0_ref[d, h]

    def write_final(src, d):
        for h in range(RET_HEADS):
            fin_ref[d, h] = src[h * hd:(h + 1) * hd, h * hd:(h + 1) * hd]

    def rope_tabs():
        cos = cos_ref[...]
        sin = sin_ref[...]
        return jnp.concatenate([cos, cos], axis=-1), jnp.concatenate([sin, sin], axis=-1)

    def roped_kt():
        k = k_ref[...].astype(F32)
        if rope:
            cos, sin = rope_tabs()
            k = k * cos + kr_ref[...].astype(F32) * sin
        return k.T

    @pl.when(p == 0)
    def _():
        ck = nch - 1 - c

        @pl.when(c == 0)
        def _():
            init_state(sb_scr, 1)

        kt = roped_kt()
        s_old = sb_scr[...]
        sb_all[ck] = s_old
        sb_scr[...] = s_old * cdb_ref[...] + _dot(kt * kdbt_ref[...], v_ref[...]) * bd_ref[...]
        if want_fin:
            @pl.when(c == nch - 1)
            def _():
                write_final(sb_scr, 1)

    @pl.when(p == 1)
    def _():
        @pl.when(c == 0)
        def _():
            init_state(sf_scr, 0)

        q = q_ref[...].astype(F32)
        if rope:
            cos, sin = rope_tabs()
            q = q * cos + qr_ref[...].astype(F32) * sin
        kt = roped_kt()
        ktb = kt.astype(BF16)
        vb = v_ref[...]
        s_f = sf_scr[...]
        o = _dot(q, s_f) * qdf_ref[...] + _dot(q, sb_all[c]) * qdb_ref[...]
        for h in range(RET_HEADS):
            hm = hm_ref[h:h + 1, :]
            att = _dot(q * hm, ktb) * dm_ref[h]
            o = o + _dot(att, vb) * hm
        sf_scr[...] = s_f * cdf_ref[...] + _dot(kt * kdft_ref[...], vb) * bd_ref[...]
        hi = o.astype(BF16)
        lo = (o - hi.astype(F32)).astype(BF16)
        mu = (jnp.dot(hi, gm_ref[...], preferred_element_type=F32)
              + jnp.dot(lo, gm_ref[...], preferred_element_type=F32))
        oc = o - mu
        var = _dot(oc * oc, gm_ref[...])
        on = oc * lax.rsqrt(var + EPS) * gn_ref[...]
        g = g_ref[...].astype(F32)
        o_ref[...] = (g * _sigmoid(g)) * on
        if want_fin:
            @pl.when(c == nch - 1)
            def _():
                write_final(sf_scr, 0)


def _ret_call(p3, tabs, gn, s0, rope_tabs, *, nb, t, tc, want_fin):
    nch = t // tc
    rope = rope_tabs is not None
    has_s0 = s0 is not None

    def kv_chunk(p, c):
        return jnp.where(p == 0, nch - 1 - c, c)

    def q_chunk(p, c):
        return jnp.where(p == 0, 0, c)

    def seg_spec(seg, cmap):
        return pl.BlockSpec((None, tc, SEG), lambda b, p, c: (b, cmap(p, c), seg))

    args = [p3, p3, p3, p3]
    in_specs = [seg_spec(S_RQ, q_chunk), seg_spec(S_RK, kv_chunk), seg_spec(S_RV, kv_chunk),
                seg_spec(S_RG, q_chunk)]
    if rope:
        args += [p3, p3, rope_tabs[0], rope_tabs[1]]
        in_specs += [seg_spec(S_RQR, q_chunk), seg_spec(S_RKR, kv_chunk),
                     pl.BlockSpec((tc, 2 * RET_DIM), lambda b, p, c: (kv_chunk(p, c), 0)),
                     pl.BlockSpec((tc, 2 * RET_DIM), lambda b, p, c: (kv_chunk(p, c), 0))]
    args += list(tabs) + [gn]
    in_specs += [_const_spec(a.shape) for a in tabs] + [_const_spec((1, SEG))]
    if has_s0:
        args.append(s0)
        in_specs.append(pl.BlockSpec((None, 2, RET_HEADS, RET_DIM, RET_DIM),
                                     lambda b, p, c: (b, 0, 0, 0, 0)))
    out_specs = [pl.BlockSpec((None, tc, SEG), lambda b, p, c: (b, q_chunk(p, c), 0))]
    out_shape = [jax.ShapeDtypeStruct((nb, t, SEG), F32)]
    if want_fin:
        out_specs.append(pl.BlockSpec((None, 2, RET_HEADS, RET_DIM, RET_DIM),
                                      lambda b, p, c: (b, 0, 0, 0, 0)))
        out_shape.append(jax.ShapeDtypeStruct((nb, 2, RET_HEADS, RET_DIM, RET_DIM), F32))
    return pl.pallas_call(
        functools.partial(_ret_kernel, rope=rope, has_s0=has_s0, want_fin=want_fin, nch=nch),
        grid=(nb, 2, nch),
        in_specs=in_specs,
        out_specs=out_specs,
        out_shape=out_shape,
        scratch_shapes=[pltpu.VMEM((nch, SEG, SEG), F32),
                        pltpu.VMEM((SEG, SEG), F32),
                        pltpu.VMEM((SEG, SEG), F32)],
        compiler_params=_cparams(("arbitrary", "arbitrary", "arbitrary")),
        name="retention",
    )(*args)


def _mla_kernel(*refs, t, tq, nctx, want_cache):
    refs = list(refs)
    cq_ref, ckv_ref, krp_ref = refs[:3]
    del refs[:3]
    has_ctx = nctx > 0
    if has_ctx:
        krr_ref, cm_ref, sm_ref, clat_ref, ckrp_ref = refs[:5]
        del refs[:5]
    qn_ref, kvn_ref, wq_ref = refs[:3]
    del refs[:3]
    if has_ctx:
        wqr_ref = refs.pop(0)
    wk_ref, wv_ref = refs[:2]
    del refs[:2]
    o_ref = refs.pop(0)
    lat_ref = refs.pop(0) if want_cache else None
    k_scr, v_scr = refs
    qi = pl.program_id(1)

    @pl.when(qi == 0)
    def _():
        lat = _rms(ckv_ref[...].astype(F32), kvn_ref[...])
        if want_cache:
            lat_ref[...] = lat
        latb = lat.astype(BF16)
        kr = krp_ref[...].astype(F32)
        if has_ctx:
            kr = kr * cm_ref[...] + krr_ref[...].astype(F32) * sm_ref[...]
        kk = jnp.dot(latb, wk_ref[...], preferred_element_type=F32)
        for h in range(MLA_HEADS):
            k_scr[h, 0:t, :] = (kk[:, h * MLA_SLAB:(h + 1) * MLA_SLAB] + kr).astype(BF16)
        v_scr[0:t, :] = jnp.dot(latb, wv_ref[...], preferred_element_type=F32).astype(BF16)
        if has_ctx:
            latc = clat_ref[...].astype(BF16)
            kkc = jnp.dot(latc, wk_ref[...], preferred_element_type=F32)
            krc = ckrp_ref[...]
            for h in range(MLA_HEADS):
                k_scr[h, t:t + nctx, :] = (kkc[:, h * MLA_SLAB:(h + 1) * MLA_SLAB] + krc).astype(BF16)
            v_scr[t:t + nctx, :] = jnp.dot(latc, wv_ref[...], preferred_element_type=F32).astype(BF16)

    a = _rms(cq_ref[...].astype(F32), qn_ref[...], n=MLA_Q_LORA).astype(BF16)
    q = jnp.dot(a, wq_ref[...], preferred_element_type=F32)
    if has_ctx:
        qrot = jnp.dot(a, wqr_ref[...], preferred_element_type=F32)
        r0 = pl.multiple_of(qi * tq, tq)
        cm = cm_ref[pl.ds(r0, tq), :]
        sm = sm_ref[pl.ds(r0, tq), :]
    scale = (MLA_NOPE + MLA_ROPE) ** -0.5
    lane = lax.broadcasted_iota(jnp.int32, (tq, SEG), 1)
    acc = jnp.zeros((tq, SEG), F32)
    for h in range(MLA_HEADS):
        qh = q[:, h * MLA_SLAB:(h + 1) * MLA_SLAB]
        if has_ctx:
            qh = qh * cm + qrot[:, h * MLA_SLAB:(h + 1) * MLA_SLAB] * sm
        s = _dot_nt(qh * scale, k_scr[h])
        m = jnp.max(s, axis=-1, keepdims=True)
        e = jnp.exp(s - m)
        l = jnp.sum(e, axis=-1, keepdims=True)
        oh = jnp.dot(e.astype(BF16), v_scr[...], preferred_element_type=F32) * (1.0 / l)
        acc = jnp.where((lane >= h * MLA_V) & (lane < (h + 1) * MLA_V), oh, acc)
    o_ref[...] = acc


def _mla_call(p3, qn, kvn, wq, wqr, wk, wv, rope_tabs, ctx, *, nb, t, tq, want_cache):
    has_ctx = ctx is not None
    nctx = ctx[0].shape[1] if has_ctx else 0
    s_tot = t + nctx
    half = MLA_KV_LORA

    args = [p3, p3, p3]
    in_specs = [
        pl.BlockSpec((None, tq, SEG), lambda b, i: (b, i, S_CQ)),
        pl.BlockSpec((None, t, half), lambda b, i: (b, 0, 2 * S_KV)),
        pl.BlockSpec((None, t, half), lambda b, i: (b, 0, 2 * S_KV + 1)),
    ]
    if has_ctx:
        args += [p3, rope_tabs[0], rope_tabs[1], ctx[0], ctx[1]]
        in_specs += [
            pl.BlockSpec((None, t, half), lambda b, i: (b, 0, 2 * S_KRR)),
            _const_spec((t, MLA_SLAB)),
            _const_spec((t, MLA_SLAB)),
            pl.BlockSpec((None, nctx, MLA_KV_LORA), lambda b, i: (b, 0, 0)),
            pl.BlockSpec((None, nctx, MLA_SLAB), lambda b, i: (b, 0, 0)),
        ]
    args += [qn, kvn, wq]
    in_specs += [_const_spec(qn.shape), _const_spec(kvn.shape), _const_spec(wq.shape)]
    if has_ctx:
        args.append(wqr)
        in_specs.append(_const_spec(wqr.shape))
    args += [wk, wv]
    in_specs += [_const_spec(wk.shape), _const_spec(wv.shape)]
    out_specs = [pl.BlockSpec((None, tq, SEG), lambda b, i: (b, i, 0))]
    out_shape = [jax.ShapeDtypeStruct((nb, t, SEG), F32)]
    if want_cache:
        out_specs.append(pl.BlockSpec((None, t, MLA_KV_LORA), lambda b, i: (b, 0, 0)))
        out_shape.append(jax.ShapeDtypeStruct((nb, t, MLA_KV_LORA), F32))
    return pl.pallas_call(
        functools.partial(_mla_kernel, t=t, tq=tq, nctx=nctx, want_cache=want_cache),
        grid=(nb, t // tq),
        in_specs=in_specs,
        out_specs=out_specs,
        out_shape=out_shape,
        scratch_shapes=[pltpu.VMEM((MLA_HEADS, s_tot, MLA_SLAB), BF16),
                        pltpu.VMEM((s_tot, SEG), BF16)],
        compiler_params=_cparams(("arbitrary", "arbitrary")),
        name="mla",
    )(*args)


def _merge_kernel(x_ref, mod_ref, npre_ref, npost_ref, yf_ref, yb_ref, u_ref, ret_ref,
                  cx_ref, cb_ref, cc_ref, cxp_ref, ccp_ref, cxn_ref, ccn_ref, mla_ref,
                  s5d_ref, wglu_ref, cw_ref, cbias_ref, wbr_ref, wg_ref, bg_ref, wo_ref,
                  o_ref, h_scr, *, tm, nt):
    i = pl.program_id(1)
    x = x_ref[...]
    sh = mod_ref[3:4, :]
    sc = mod_ref[4:5, :]
    gt = mod_ref[5:6, :]
    h_scr[...] = (_rms(x, npre_ref[...]) * (1.0 + sc) + sh).astype(BF16)

    y = yf_ref[...] + yb_ref[...] + s5d_ref[...] * u_ref[...].astype(F32)
    z5 = _gelu_tanh(y)
    b_s5 = z5 * _sigmoid(_dot(z5, wglu_ref[...]))

    z = cc_ref[...].astype(F32) * cx_ref[...].astype(F32)
    zprev = ((ccp_ref[...].astype(F32) * cxp_ref[...].astype(F32))[HALO - 1:HALO, :]
             * jnp.where(i > 0, 1.0, 0.0))
    znext = ((ccn_ref[...].astype(F32) * cxn_ref[...].astype(F32))[0:1, :]
             * jnp.where(i < nt - 1, 1.0, 0.0))
    row = lax.broadcasted_iota(jnp.int32, (tm, SEG), 0)
    z_dn = jnp.where(row == 0, zprev, pltpu.roll(z, 1, axis=0))
    z_up = jnp.where(row == tm - 1, znext, pltpu.roll(z, tm - 1, axis=0))
    b_conv = cb_ref[...].astype(F32) * (cbias_ref[...] + z_dn * cw_ref[0:1, :] + z * cw_ref[1:2, :]
                            + z_up * cw_ref[2:3, :])

    branches = (b_s5, ret_ref[...], b_conv, mla_ref[...])
    merged = jnp.zeros((tm, D_MODEL), F32)
    for n in range(N_BRANCH):
        gate = _sigmoid(jnp.dot(h_scr[...], wg_ref[:, n * D_MODEL:(n + 1) * D_MODEL],
                                preferred_element_type=F32) + bg_ref[:, n * D_MODEL:(n + 1) * D_MODEL])
        merged = merged + gate * _dot(branches[n], wbr_ref[n])
    m = _dot(merged, wo_ref[...])
    o_ref[...] = x + gt * _rms(m, npost_ref[...])


def _merge_call(x, mod_l, npre, npost, y_s5, ret_o, p3, mla_o, s5d, wglu, cw, cbias, wbr, wg, bg, wo,
                *, layer, nb, t, tm, row_of_batch):
    nt = t // tm
    hb = tm // HALO
    n_halo = t // HALO

    def dir_spec(d):
        return pl.BlockSpec((None, None, tm, SEG), lambda b, i: (d, b, i, 0))

    def seg_spec(seg):
        return pl.BlockSpec((None, tm, SEG), lambda b, i: (b, i, seg))

    def prev_spec(seg):
        return pl.BlockSpec((None, HALO, SEG), lambda b, i: (b, jnp.maximum(i * hb - 1, 0), seg))

    def next_spec(seg):
        return pl.BlockSpec((None, HALO, SEG), lambda b, i: (b, jnp.minimum((i + 1) * hb, n_halo - 1), seg))

    return pl.pallas_call(
        functools.partial(_merge_kernel, tm=tm, nt=nt),
        grid=(nb, nt),
        in_specs=[
            pl.BlockSpec((tm, D_MODEL), lambda b, i: (b * nt + i, 0)),
            pl.BlockSpec((None, N_MOD, D_MODEL), lambda b, i: (row_of_batch(b), 0, 0)),
            _const_spec((1, D_MODEL)),
            _const_spec((1, D_MODEL)),
            dir_spec(0), dir_spec(1), seg_spec(S_U),
            pl.BlockSpec((None, tm, SEG), lambda b, i: (b, i, 0)),
            seg_spec(S_CX), seg_spec(S_CB), seg_spec(S_CC),
            prev_spec(S_CX), prev_spec(S_CC), next_spec(S_CX), next_spec(S_CC),
            pl.BlockSpec((None, tm, SEG), lambda b, i: (b, i, 0)),
            _const_spec(s5d.shape), _const_spec(wglu.shape), _const_spec(cw.shape),
            _const_spec(cbias.shape), _resident_spec(wbr.shape[1:], (layer,)),
            _resident_spec(wg.shape[1:], (layer,)), _const_spec(bg.shape),
            _resident_spec(wo.shape[1:], (layer,)),
        ],
        out_specs=pl.BlockSpec((tm, D_MODEL), lambda b, i: (b * nt + i, 0)),
        out_shape=jax.ShapeDtypeStruct((nb * t, D_MODEL), F32),
        scratch_shapes=[pltpu.VMEM((tm, D_MODEL), BF16)],
        compiler_params=_cparams(("arbitrary", "arbitrary")),
        name="merge",
    )(x, mod_l, npre, npost, y_s5, y_s5, p3, ret_o, p3, p3, p3, p3, p3, p3, p3, mla_o,
      s5d, wglu, cw, cbias, wbr, wg, bg, wo)


def _rot_half(w, heads, dim):
    w4 = w.reshape(w.shape[0], heads, dim)
    half = dim // 2
    return jnp.concatenate([-w4[..., half:], w4[..., :half]], axis=-1).reshape(w.shape[0], heads * dim)


def _axial_angles(t, dim):
    rows = t // GRID_W
    row = np.repeat(np.arange(rows, dtype=np.float64), GRID_W)
    col = np.tile(np.arange(GRID_W, dtype=np.float64), rows)
    quarter = dim // 4
    inv = ROPE_BASE ** (-np.arange(quarter, dtype=np.float64) / quarter)
    ang = np.concatenate([row[:, None] * inv, col[:, None] * inv], axis=-1)
    return np.cos(ang).astype(np.float32), np.sin(ang).astype(np.float32)


def _rope_tables(t):
    cos, sin = _axial_angles(t, RET_DIM)
    ret_cos = np.tile(np.concatenate([cos, cos], -1), (1, 2))
    ret_sin = np.tile(np.concatenate([sin, sin], -1), (1, 2))
    cos, sin = _axial_angles(t, MLA_ROPE)
    pad = MLA_SLAB - MLA_NOPE - MLA_ROPE
    mla_cos = np.concatenate([np.ones((t, MLA_NOPE), np.float32), cos, cos, np.zeros((t, pad), np.float32)], -1)
    mla_sin = np.concatenate([np.zeros((t, MLA_NOPE), np.float32), sin, sin, np.zeros((t, pad), np.float32)], -1)
    return ((jnp.asarray(ret_cos), jnp.asarray(ret_sin)), (jnp.asarray(mla_cos), jnp.asarray(mla_sin)))


def _in_weights(w_in, rope):
    sizes = (BRANCH_WIDTH,) * 8 + (MLA_Q_LORA, MLA_KV_LORA, MLA_ROPE)
    offs = np.cumsum((0,) + sizes)
    u, rq, rk, rv, rg, cx, cb, cc, cq, ckv, kr = [w_in[:, offs[i]:offs[i + 1]] for i in range(11)]
    rk = rk * (RET_DIM ** -0.5)
    d = w_in.shape[0]

    def place_kr(w):
        return jnp.concatenate([jnp.zeros((d, MLA_NOPE), F32), w,
                                jnp.zeros((d, MLA_SLAB - MLA_NOPE - MLA_ROPE), F32)], -1)

    cols = [u, rq, rk, rv, rg, cx, cb, cc,
            jnp.concatenate([cq, jnp.zeros((d, SEG - MLA_Q_LORA), F32)], -1),
            jnp.concatenate([ckv, place_kr(kr)], -1)]
    if rope:
        cols += [_rot_half(rq, RET_HEADS, RET_DIM), _rot_half(rk, RET_HEADS, RET_DIM),
                 jnp.concatenate([place_kr(_rot_half(kr, 1, MLA_ROPE)), jnp.zeros((d, SEG - MLA_SLAB), F32)], -1)]
    return jnp.concatenate(cols, axis=-1).astype(BF16)


def _mla_weights(w_uq, w_ukv):
    dq = MLA_NOPE + MLA_ROPE
    w4 = w_uq.reshape(MLA_Q_LORA, MLA_HEADS, dq)
    nope, ropew = w4[..., :MLA_NOPE], w4[..., MLA_NOPE:]
    half = MLA_ROPE // 2
    rot = jnp.concatenate([-ropew[..., half:], ropew[..., :half]], -1)
    zpad = jnp.zeros((MLA_Q_LORA, MLA_HEADS, MLA_SLAB - dq), F32)
    wq = jnp.concatenate([nope, ropew, zpad], -1).reshape(MLA_Q_LORA, MLA_HEADS * MLA_SLAB)
    wqr = jnp.concatenate([jnp.zeros_like(nope), rot, zpad], -1).reshape(MLA_Q_LORA, MLA_HEADS * MLA_SLAB)
    rpad = jnp.zeros((SEG - MLA_Q_LORA, MLA_HEADS * MLA_SLAB), F32)
    wq = jnp.concatenate([wq, rpad], 0).astype(BF16)
    wqr = jnp.concatenate([wqr, rpad], 0).astype(BF16)
    kv4 = w_ukv.reshape(MLA_KV_LORA, MLA_HEADS, MLA_NOPE + MLA_V)
    wk = jnp.concatenate([kv4[..., :MLA_NOPE], jnp.zeros((MLA_KV_LORA, MLA_HEADS, MLA_SLAB - MLA_NOPE), F32)],
                         -1).reshape(MLA_KV_LORA, MLA_HEADS * MLA_SLAB).astype(BF16)
    wv = kv4[..., MLA_NOPE:].reshape(MLA_KV_LORA, MLA_HEADS * MLA_V).astype(BF16)
    return wq, wqr, wk, wv


def _s5_tables(lam_re, lam_im, log_dt, b_re, b_im, c_re, c_im):
    dt = jnp.exp(log_dt)[..., None]
    mag = jnp.exp(lam_re * dt)
    ab_re = mag * jnp.cos(lam_im * dt)
    ab_im = mag * jnp.sin(lam_im * dt)
    den = lam_re * lam_re + lam_im * lam_im
    f_re = ((ab_re - 1.0) * lam_re + ab_im * lam_im) / den
    f_im = (ab_im * lam_re - (ab_re - 1.0) * lam_im) / den
    bb_re = f_re[..., None] * b_re - f_im[..., None] * b_im
    bb_im = f_re[..., None] * b_im + f_im[..., None] * b_re
    eye = jnp.eye(S5_GROUPS, dtype=F32)
    blk_in = lambda w: jnp.einsum('dgph,gk->dghkp', w, eye).reshape(2, BRANCH_WIDTH, S5_LANES)
    bblk = jnp.concatenate([blk_in(bb_re), blk_in(bb_im)], -1).astype(BF16)
    blk_out = lambda w: jnp.einsum('ghp,gk->gpkh', w, eye).reshape(S5_LANES, BRANCH_WIDTH)
    cblk = jnp.concatenate([blk_out(c_re), -blk_out(c_im)], 0).astype(BF16)
    a_tab = jnp.stack([ab_re.reshape(2, S5_LANES), ab_im.reshape(2, S5_LANES)], axis=1)
    return bblk, a_tab, cblk


def _ret_tables(decay_logit, c):
    lg = jax.nn.log_sigmoid(decay_logit.astype(F32))
    lf = lg[0][:, None, None]
    lb = lg[1][:, None, None]
    idx = jnp.arange(c, dtype=F32)
    diff = idx[:, None] - idx[None, :]
    dm = jnp.where(diff >= 0, jnp.exp(lf * jnp.maximum(diff, 0.0)), jnp.exp(lb * jnp.maximum(-diff, 0.0)))
    lfl = jnp.repeat(lg[0], RET_DIM)
    lbl = jnp.repeat(lg[1], RET_DIM)
    qdf = jnp.exp(lfl[None, :] * (idx[:, None] + 1.0))
    qdb = jnp.exp(lbl[None, :] * (c - idx[:, None]))
    kdft = jnp.exp(lfl[:, None] * (c - 1.0 - idx[None, :]))
    kdbt = jnp.exp(lbl[:, None] * idx[None, :])
    cdf = jnp.broadcast_to(jnp.exp(lfl * c)[:, None], (SEG, SEG))
    cdb = jnp.broadcast_to(jnp.exp(lbl * c)[:, None], (SEG, SEG))
    head = np.arange(SEG) // RET_DIM
    hm = jnp.asarray((head[None, :] == np.arange(RET_HEADS)[:, None]).astype(np.float32))
    bd_np = (head[:, None] == head[None, :]).astype(np.float32)
    bd = jnp.asarray(bd_np)
    gm = jnp.asarray(bd_np / RET_DIM, BF16)
    return dm, hm, qdf, qdb, kdft, kdbt, cdf, cdb, bd, gm


def _layer(x, mod_l, w, *, layer, nb, t, tm, tm_ffn, row_of_batch, ctx, rope_tabs):
    is_ctx_pass = ctx is None
    row_of_tile = lambda i: row_of_batch((i * tm_ffn) // t)
    x = _ffn_call(x, mod_l, w['npre'][0], w['npost'][0], *w['ffn'], lj=(layer, 0), k=0, tm=tm_ffn,
                  row_of_tile=row_of_tile)

    nseg = NSEG_PLAIN if is_ctx_pass else NSEG_ROPE
    p = _in_call(x, mod_l, w['npre'][1], w['w_in'], nb=nb, t=t, tm=tm, nseg=nseg, row_of_batch=row_of_batch)
    p3 = p.reshape(nb, t, nseg * SEG)

    steps = 512 // nb
    if is_ctx_pass:
        s5_0 = jnp.zeros((2, 2, nb, S5_LANES), F32)
        ret_0, mla_ctx, ret_rope, mla_rope = None, None, None, None
    else:
        s5_0, ret_0, mla_ctx = ctx
        ret_rope, mla_rope = rope_tabs
    s5_out = _s5_call(p3, w['s5_bblk'], w['s5_a'], w['s5_cblk'], s5_0,
                      nb=nb, t=t, steps=steps, want_fin=is_ctx_pass)
    ret_out = _ret_call(p3, w['ret_tabs'], w['ret_gn'], ret_0, ret_rope,
                        nb=nb, t=t, tc=256, want_fin=is_ctx_pass)
    mla_out = _mla_call(p3, w['mla_qn'], w['mla_kvn'], w['mla_wq'], w['mla_wqr'], w['mla_wk'], w['mla_wv'],
                        mla_rope, mla_ctx, nb=nb, t=t, tq=min(t, 512), want_cache=is_ctx_pass)

    x = _merge_call(x, mod_l, w['npre'][1], w['npost'][1], s5_out[0], ret_out[0], p3, mla_out[0],
                    w['s5_d'], w['s5_wglu'], w['conv_w'], w['conv_b'], w['w_branch'], w['w_gate'],
                    w['b_gate'], w['w_o'], layer=layer, nb=nb, t=t, tm=tm, row_of_batch=row_of_batch)
    x = _ffn_call(x, mod_l, w['npre'][2], w['npost'][2], *w['ffn'], lj=(layer, 1), k=2, tm=tm_ffn,
                  row_of_tile=row_of_tile)

    state = None
    if is_ctx_pass:
        fin = s5_out[1].reshape(2, 2, nb, S5_GROUPS, S5_STATE).transpose(2, 0, 3, 4, 1)
        kr = p3[:, :, S_KV * SEG + MLA_KV_LORA + MLA_NOPE:S_KV * SEG + MLA_KV_LORA + MLA_NOPE + MLA_ROPE]
        kr = kr.astype(F32)
        cache = jnp.concatenate([mla_out[1], kr], axis=-1)
        state = (fin, ret_out[1], cache)
    return x, state


def kernel(x_prompt, x_sample, state_s5, state_ret, cache_mla, c, c_ctx, w_mod, b_mod, norm_pre, norm_post, ffn_w1, ffn_w3, ffn_w2, w_in, s5_lam_re, s5_lam_im, s5_log_dt, s5_b_re, s5_b_im, s5_c_re, s5_c_im, s5_d, s5_w_glu, ret_decay, ret_gn, conv_w, conv_b, mla_q_norm, mla_w_uq, mla_kv_norm, mla_w_ukv, w_branch, w_gate, b_gate, w_o):
    bp, tp, _ = x_prompt.shape
    bs, ts, _ = x_sample.shape
    assert 1 + bs <= MOD_ROWS
    past = cache_mla.shape[2]

    cond = jnp.concatenate([c_ctx[None, :], c, jnp.zeros((MOD_ROWS - 1 - bs, D_MODEL), F32)], axis=0)
    mod = _mod_call(cond, w_mod, b_mod).reshape(DEPTH, MOD_ROWS, N_MOD, D_MODEL)

    rope_tabs = _rope_tables(ts)
    xp = x_prompt.reshape(bp * tp, D_MODEL)
    xs = x_sample.reshape(bs * ts, D_MODEL)
    s5_list, ret_list, mla_list = [], [], []
    ffn = (ffn_w1.astype(BF16), ffn_w3.astype(BF16), ffn_w2.astype(BF16))
    w_branch_b, w_gate_b, w_o_b = w_branch.astype(BF16), w_gate.astype(BF16), w_o.astype(BF16)
    for l in range(DEPTH):
        wq, wqr, wk, wv = _mla_weights(mla_w_uq[l], mla_w_ukv[l])
        bblk, a_tab, cblk = _s5_tables(s5_lam_re[l], s5_lam_im[l], s5_log_dt[l], s5_b_re[l], s5_b_im[l],
                                       s5_c_re[l], s5_c_im[l])
        w = dict(
            npre=[norm_pre[l, i][None, :] for i in range(3)],
            npost=[norm_post[l, i][None, :] for i in range(3)],
            ffn=ffn,
            w_in=_in_weights(w_in[l], rope=True),
            s5_bblk=bblk, s5_a=a_tab, s5_cblk=cblk,
            s5_d=s5_d[l][None, :], s5_wglu=s5_w_glu[l].astype(BF16),
            ret_tabs=_ret_tables(ret_decay[l], 256), ret_gn=ret_gn[l][None, :],
            conv_w=conv_w[l], conv_b=conv_b[l][None, :],
            mla_qn=jnp.concatenate([mla_q_norm[l], jnp.zeros((SEG - MLA_Q_LORA,), F32)])[None, :],
            mla_kvn=mla_kv_norm[l][None, :],
            mla_wq=wq, mla_wqr=wqr, mla_wk=wk, mla_wv=wv,
            w_branch=w_branch_b, w_gate=w_gate_b, b_gate=b_gate[l][None, :], w_o=w_o_b,
        )
        xp, (s5_s, ret_s, mla_c) = _layer(xp, mod[l], w, layer=l, nb=bp, t=tp, tm=256, tm_ffn=512,
                                          row_of_batch=lambda b: 0, ctx=None, rope_tabs=None)
        s5_list.append(s5_s)
        ret_list.append(ret_s)
        mla_list.append(mla_c)

        s5_0 = state_s5[:, l].transpose(1, 4, 0, 2, 3).reshape(2, 2, bs, S5_LANES)
        cache_l = cache_mla[:, l]
        ctx_lat = cache_l[..., :MLA_KV_LORA]
        ctx_krp = jnp.concatenate([jnp.zeros((bs, past, MLA_NOPE), F32), cache_l[..., MLA_KV_LORA:],
                                   jnp.zeros((bs, past, MLA_SLAB - MLA_NOPE - MLA_ROPE), F32)], -1)
        xs, _ = _layer(xs, mod[l], w, layer=l, nb=bs, t=ts, tm=512, tm_ffn=1024, row_of_batch=lambda b: 1 + b,
                       ctx=(s5_0, state_ret[:, l], (ctx_lat, ctx_krp)), rope_tabs=rope_tabs)
    return (xp.reshape(bp, tp, D_MODEL), xs.reshape(bs, ts, D_MODEL),
            jnp.stack(s5_list, axis=1), jnp.stack(ret_list, axis=1), jnp.stack(mla_list, axis=1))
```

```python
import functools

import numpy as np
import jax
import jax.numpy as jnp
from jax import lax
from jax.experimental import pallas as pl
from jax.experimental.pallas import tpu as pltpu

F32 = jnp.float32
BF16 = jnp.bfloat16

D_MODEL = 1024
DEPTH = 2
GRID_W = 64
EPS = 1e-6
ROPE_BASE = 10000.0
N_BRANCH = 4
BRANCH_WIDTH = D_MODEL // 4
N_MOD = 9
N_NORM = 3
D_FF = 2816
S5_GROUP = 16
S5_GROUPS = BRANCH_WIDTH // S5_GROUP
S5_STATE = 64
S5_LANES = S5_GROUPS * S5_STATE
RET_HEADS = 4
RET_DIM = BRANCH_WIDTH // RET_HEADS
CONV_K = 3
MLA_HEADS = 4
MLA_Q_LORA = 192
MLA_KV_LORA = 128
MLA_NOPE = 64
MLA_ROPE = 32
MLA_V = BRANCH_WIDTH // MLA_HEADS
MLA_SLAB = 128

SUBLANES = 8
S5_SUBCHUNKS = 4
S5_ROWS = 512
SEG = 256
FF_CHUNK = 256
N_FF_CHUNKS = D_FF // FF_CHUNK
MOD_ROWS = 16
HALO = 16
RET_CHUNK = 256
V7X_VMEM_LIMIT = 56 * 1024 * 1024

S_U, S_RQ, S_RK, S_RV, S_RG, S_CX, S_CB, S_CC, S_CQ, S_KV, S_RQR, S_RKR, S_KRR = range(13)
NSEG_PLAIN = 10
NSEG_ROPE = 13


def _cparams(sem):
    return pltpu.CompilerParams(dimension_semantics=sem, vmem_limit_bytes=V7X_VMEM_LIMIT)


def _dot(a, b):
    return jnp.dot(a.astype(BF16), b.astype(BF16), preferred_element_type=F32)


def _dot_nt(a, b):
    return lax.dot_general(a.astype(BF16), b.astype(BF16), (((1,), (1,)), ((), ())),
                           preferred_element_type=F32)


def _sigmoid(x):
    return 1.0 / (1.0 + jnp.exp(-x))


def _rms(x, g, n=None):
    n = x.shape[-1] if n is None else n
    ms = jnp.sum(x * x, axis=-1, keepdims=True) * (1.0 / n)
    return x * lax.rsqrt(ms + EPS) * g


def _gelu_tanh(x):
    return 0.5 * x * (1.0 + jnp.tanh(np.sqrt(2.0 / np.pi) * (x + 0.044715 * (x * x * x))))


def _const_spec(shape):
    nd = len(shape)
    return pl.BlockSpec(shape, lambda *_: (0,) * nd)


def _pick_spec(arr, *lead):
    shape = arr.shape[len(lead):]
    return pl.BlockSpec((None,) * len(lead) + tuple(shape), lambda *_: tuple(lead) + (0,) * len(shape))


def _resident_spec(shape, lead=()):
    nd = len(shape)
    return pl.BlockSpec((None,) * len(lead) + tuple(shape), lambda *_: tuple(lead) + (0,) * nd,
                        pipeline_mode=pl.Buffered(1))


def _mod_kernel(c_ref, w_ref, b_ref, o_ref):
    c = c_ref[...]
    o_ref[...] = _dot(c * _sigmoid(c), w_ref[...]) + b_ref[...]


def _mod_call(cond, w_mod, b_mod):
    ncol = N_MOD * D_MODEL
    tn = D_MODEL
    return pl.pallas_call(
        _mod_kernel,
        grid=(DEPTH, ncol // tn),
        in_specs=[
            pl.BlockSpec((MOD_ROWS, D_MODEL), lambda l, j: (0, 0)),
            pl.BlockSpec((None, D_MODEL, tn), lambda l, j: (l, 0, j)),
            pl.BlockSpec((None, 1, tn), lambda l, j: (l, 0, j)),
        ],
        out_specs=pl.BlockSpec((None, MOD_ROWS, tn), lambda l, j: (l, 0, j)),
        out_shape=jax.ShapeDtypeStruct((DEPTH, MOD_ROWS, ncol), F32),
        compiler_params=_cparams(("arbitrary", "arbitrary")),
        name="mod",
    )(cond, w_mod, b_mod.reshape(DEPTH, 1, ncol))


def _ffn_kernel(x_ref, mod_ref, npre_ref, npost_ref, w1_ref, w3_ref, w2_ref, o_ref,
                h_scr, g_scr, *, k):
    x = x_ref[...]
    sh = mod_ref[3 * k:3 * k + 1, :]
    sc = mod_ref[3 * k + 1:3 * k + 2, :]
    gt = mod_ref[3 * k + 2:3 * k + 3, :]
    h_scr[...] = (_rms(x, npre_ref[...]) * (1.0 + sc) + sh).astype(BF16)
    for j in range(N_FF_CHUNKS):
        cols = slice(j * FF_CHUNK, (j + 1) * FF_CHUNK)
        a = jnp.dot(h_scr[...], w1_ref[:, cols], preferred_element_type=F32)
        b = jnp.dot(h_scr[...], w3_ref[:, cols], preferred_element_type=F32)
        g_scr[:, cols] = (a * _sigmoid(a) * b).astype(BF16)
    f = jnp.dot(g_scr[...], w2_ref[...], preferred_element_type=F32)
    o_ref[...] = x + (0.5 * gt) * _rms(f, npost_ref[...])


def _ffn_call(x, w, *, layer, k, tm, row_of_tile):
    n = x.shape[0]
    lj = (layer, k // 2)
    w1, w3, w2 = w['ffn']
    return pl.pallas_call(
        functools.partial(_ffn_kernel, k=k),
        grid=(n // tm,),
        in_specs=[
            pl.BlockSpec((tm, D_MODEL), lambda i: (i, 0)),
            pl.BlockSpec((None, None, N_MOD, D_MODEL), lambda i: (layer, row_of_tile(i), 0, 0)),
            _pick_spec(w['npre'], layer * N_NORM + k),
            _pick_spec(w['npost'], layer * N_NORM + k),
            _resident_spec((D_MODEL, D_FF), lj),
            _resident_spec((D_MODEL, D_FF), lj),
            _resident_spec((D_FF, D_MODEL), lj),
        ],
        out_specs=pl.BlockSpec((tm, D_MODEL), lambda i: (i, 0)),
        out_shape=jax.ShapeDtypeStruct((n, D_MODEL), F32),
        scratch_shapes=[pltpu.VMEM((tm, D_MODEL), BF16), pltpu.VMEM((tm, D_FF), BF16)],
        compiler_params=_cparams(("arbitrary",)),
        name="ffn",
    )(x, w['mod'], w['npre'], w['npost'], w1, w3, w2)


def _in_kernel(x_ref, mod_ref, npre_ref, w_ref, p_ref, h_scr, *, nseg):
    sh = mod_ref[3:4, :]
    sc = mod_ref[4:5, :]
    h_scr[...] = (_rms(x_ref[...], npre_ref[...]) * (1.0 + sc) + sh).astype(BF16)
    for j in range(nseg):
        cols = slice(j * SEG, (j + 1) * SEG)
        p_ref[:, cols] = jnp.dot(h_scr[...], w_ref[:, cols], preferred_element_type=F32).astype(BF16)


def _in_call(x, w, *, layer, nb, t, tm, nseg, row_of_batch):
    tt = t // tm
    return pl.pallas_call(
        functools.partial(_in_kernel, nseg=nseg),
        grid=(nb, tt),
        in_specs=[
            pl.BlockSpec((tm, D_MODEL), lambda b, i: (b * tt + i, 0)),
            pl.BlockSpec((None, None, N_MOD, D_MODEL), lambda b, i: (layer, row_of_batch(b), 0, 0)),
            _pick_spec(w['npre'], layer * N_NORM + 1),
            _resident_spec((D_MODEL, nseg * SEG), (layer,)),
        ],
        out_specs=pl.BlockSpec((tm, nseg * SEG), lambda b, i: (b * tt + i, 0)),
        out_shape=jax.ShapeDtypeStruct((nb * t, nseg * SEG), BF16),
        scratch_shapes=[pltpu.VMEM((tm, D_MODEL), BF16)],
        compiler_params=_cparams(("arbitrary", "arbitrary")),
        name="in_proj",
    )(x, w['mod'], w['npre'], w['w_in'])


def _s5_kernel(*refs, nb, steps, want_fin):
    if want_fin:
        u_ref, perm_ref, permt_ref, bb_ref, a_ref, cc_ref, s0_ref, y_ref, fin_ref, s_scr, bu_scr = refs
    else:
        u_ref, perm_ref, permt_ref, bb_ref, a_ref, cc_ref, s0_ref, y_ref, s_scr, bu_scr = refs
        fin_ref = None
    d = pl.program_id(0)
    c = pl.program_id(1)
    rows = nb * steps
    ngrp = nb // SUBLANES

    @pl.when(c == 0)
    def _():
        s_scr[...] = s0_ref[...]

    a_re = jnp.broadcast_to(a_ref[0:1, :], (SUBLANES, S5_LANES))
    a_im = jnp.broadcast_to(a_ref[1:2, :], (SUBLANES, S5_LANES))
    s_re = [s_scr[0, g * SUBLANES:(g + 1) * SUBLANES, :] for g in range(ngrp)]
    s_im = [s_scr[1, g * SUBLANES:(g + 1) * SUBLANES, :] for g in range(ngrp)]
    re_cols = slice(0, S5_LANES)
    im_cols = slice(S5_LANES, 2 * S5_LANES)
    for k in range(S5_SUBCHUNKS):
        sub = jnp.where(d == 0, k, S5_SUBCHUNKS - 1 - k)
        off = pl.multiple_of(sub * steps, steps)
        u_bt = u_ref[:, pl.ds(off, steps), :].reshape(rows, SEG)
        u_tb = jnp.dot(perm_ref[...], u_bt, preferred_element_type=F32).astype(BF16)
        bu_scr[k] = jnp.dot(u_tb, bb_ref[...], preferred_element_type=F32)
        for i in range(steps):
            for g in range(ngrp):
                rr = slice(i * nb + g * SUBLANES, i * nb + (g + 1) * SUBLANES)
                n_re = a_re * s_re[g] - a_im * s_im[g] + bu_scr[k, rr, re_cols]
                n_im = a_re * s_im[g] + a_im * s_re[g] + bu_scr[k, rr, im_cols]
                bu_scr[k, rr, re_cols] = n_re
                bu_scr[k, rr, im_cols] = n_im
                s_re[g], s_im[g] = n_re, n_im
        y_tb = _dot(bu_scr[k], cc_ref[...])
        hi = y_tb.astype(BF16)
        lo = (y_tb - hi.astype(F32)).astype(BF16)
        y_bt = (jnp.dot(permt_ref[...], hi, preferred_element_type=F32)
                + jnp.dot(permt_ref[...], lo, preferred_element_type=F32))
        y_ref[:, pl.ds(off, steps), :] = y_bt.reshape(nb, steps, SEG)
    for g in range(ngrp):
        s_scr[0, g * SUBLANES:(g + 1) * SUBLANES, :] = s_re[g]
        s_scr[1, g * SUBLANES:(g + 1) * SUBLANES, :] = s_im[g]
    if want_fin:
        fin_ref[...] = s_scr[...]


def _s5_call(p3, w, s0, *, layer, s0_layer, nb, t, want_fin):
    steps = S5_ROWS // nb
    span = steps * S5_SUBCHUNKS
    nch = t // span
    rows = S5_ROWS
    r = np.arange(rows)
    perm_np = np.zeros((2, rows, rows), np.float32)
    perm_np[0, r, (r % nb) * steps + r // nb] = 1.0
    perm_np[1, r, (r % nb) * steps + (steps - 1 - r // nb)] = 1.0
    perm = jnp.asarray(perm_np, BF16)
    permt = jnp.asarray(perm_np.transpose(0, 2, 1), BF16)

    def chunk(d, c):
        return jnp.where(d == 0, c, nch - 1 - c)

    out_specs = [pl.BlockSpec((None, nb, span, SEG), lambda d, c: (d, 0, chunk(d, c), 0))]
    out_shape = [jax.ShapeDtypeStruct((2, nb, t, SEG), F32)]
    if want_fin:
        out_specs.append(pl.BlockSpec((None, 2, nb, S5_LANES), lambda d, c: (d, 0, 0, 0)))
        out_shape.append(jax.ShapeDtypeStruct((2, 2, nb, S5_LANES), F32))
    return pl.pallas_call(
        functools.partial(_s5_kernel, nb=nb, steps=steps, want_fin=want_fin),
        grid=(2, nch),
        in_specs=[
            pl.BlockSpec((nb, span, SEG), lambda d, c: (0, chunk(d, c), S_U)),
            pl.BlockSpec((None, rows, rows), lambda d, c: (d, 0, 0)),
            pl.BlockSpec((None, rows, rows), lambda d, c: (d, 0, 0)),
            pl.BlockSpec((None, None, SEG, 2 * S5_LANES), lambda d, c: (layer, d, 0, 0)),
            pl.BlockSpec((None, None, 2, S5_LANES), lambda d, c: (layer, d, 0, 0)),
            _pick_spec(w['s5_cblk'], layer),
            pl.BlockSpec((None, None, 2, nb, S5_LANES), lambda d, c: (s0_layer, d, 0, 0, 0)),
        ],
        out_specs=out_specs,
        out_shape=out_shape,
        scratch_shapes=[pltpu.VMEM((2, nb, S5_LANES), F32),
                        pltpu.VMEM((S5_SUBCHUNKS, rows, 2 * S5_LANES), F32)],
        compiler_params=_cparams(("arbitrary", "arbitrary")),
        name="s5_scan",
    )(p3, perm, permt, w['s5_bblk'], w['s5_a'], w['s5_cblk'], s0)


def _ret_kernel(*refs, rope, has_s0, want_fin, nch):
    refs = list(refs)
    q_ref, k_ref, v_ref, g_ref = refs[:4]
    del refs[:4]
    if rope:
        qr_ref, kr_ref, cos_ref, sin_ref = refs[:4]
        del refs[:4]
    (dm_ref, qdf_ref, qdb_ref, kdft_ref, kdbt_ref, cdf_ref, cdb_ref, hm_ref, bd_ref, gm_ref,
     gn_ref) = refs[:11]
    del refs[:11]
    if has_s0:
        s0_ref = refs.pop(0)
    o_ref = refs.pop(0)
    fin_ref = refs.pop(0) if want_fin else None
    sb_all, sf_scr, sb_scr = refs

    p = pl.program_id(1)
    c = pl.program_id(2)
    hd = RET_DIM

    def init_state(dst, d):
        dst[...] = jnp.zeros_like(dst)
        if has_s0:
            for h in range(RET_HEADS):
                dst[h * hd:(h + 1) * hd, h * hd:(h + 1) * hd] = s0_ref[d, h]

    def write_final(src, d):
        for h in range(RET_HEADS):
            fin_ref[d, h] = src[h * hd:(h + 1) * hd, h * hd:(h + 1) * hd]

    def rope_tabs():
        cos = cos_ref[...]
        sin = sin_ref[...]
        return jnp.concatenate([cos, cos], axis=-1), jnp.concatenate([sin, sin], axis=-1)

    def roped_kt():
        k = k_ref[...].astype(F32)
        if rope:
            cos, sin = rope_tabs()
            k = k * cos + kr_ref[...].astype(F32) * sin
        return k.T

    @pl.when(p == 0)
    def _():
        ck = nch - 1 - c

        @pl.when(c == 0)
        def _():
            init_state(sb_scr, 1)

        kt = roped_kt()
        s_old = sb_scr[...]
        sb_all[ck] = s_old
        sb_scr[...] = s_old * cdb_ref[...] + _dot(kt * kdbt_ref[...], v_ref[...]) * bd_ref[...]
        if want_fin:
            @pl.when(c == nch - 1)
            def _():
                write_final(sb_scr, 1)

    @pl.when(p == 1)
    def _():
        @pl.when(c == 0)
        def _():
            init_state(sf_scr, 0)

        q = q_ref[...].astype(F32)
        if rope:
            cos, sin = rope_tabs()
            q = q * cos + qr_ref[...].astype(F32) * sin
        kt = roped_kt()
        ktb = kt.astype(BF16)
        vb = v_ref[...]
        s_f = sf_scr[...]
        o = _dot(q, s_f) * qdf_ref[...] + _dot(q, sb_all[c]) * qdb_ref[...]
        for h in range(RET_HEADS):
            hm = hm_ref[h:h + 1, :]
            att = _dot(q * hm, ktb) * dm_ref[h]
            o = o + _dot(att, vb) * hm
        sf_scr[...] = s_f * cdf_ref[...] + _dot(kt * kdft_ref[...], vb) * bd_ref[...]
        hi = o.astype(BF16)
        lo = (o - hi.astype(F32)).astype(BF16)
        mu = (jnp.dot(hi, gm_ref[...], preferred_element_type=F32)
              + jnp.dot(lo, gm_ref[...], preferred_element_type=F32))
        oc = o - mu
        var = _dot(oc * oc, gm_ref[...])
        on = oc * lax.rsqrt(var + EPS) * gn_ref[...]
        g = g_ref[...].astype(F32)
        o_ref[...] = (g * _sigmoid(g)) * on
        if want_fin:
            @pl.when(c == nch - 1)
            def _():
                write_final(sf_scr, 0)


def _ret_call(p3, w, s0, rope_tabs, *, layer, nb, t, want_fin):
    tc = RET_CHUNK
    nch = t // tc
    rope = rope_tabs is not None
    has_s0 = s0 is not None

    def kv_chunk(p, c):
        return jnp.where(p == 0, nch - 1 - c, c)

    def q_chunk(p, c):
        return jnp.where(p == 0, 0, c)

    def seg_spec(seg, cmap):
        return pl.BlockSpec((None, tc, SEG), lambda b, p, c: (b, cmap(p, c), seg))

    args = [p3, p3, p3, p3]
    in_specs = [seg_spec(S_RQ, q_chunk), seg_spec(S_RK, kv_chunk), seg_spec(S_RV, kv_chunk),
                seg_spec(S_RG, q_chunk)]
    if rope:
        args += [p3, p3, rope_tabs[0], rope_tabs[1]]
        in_specs += [seg_spec(S_RQR, q_chunk), seg_spec(S_RKR, kv_chunk),
                     pl.BlockSpec((tc, 2 * RET_DIM), lambda b, p, c: (kv_chunk(p, c), 0)),
                     pl.BlockSpec((tc, 2 * RET_DIM), lambda b, p, c: (kv_chunk(p, c), 0))]
    args += list(w['ret_layer_tabs']) + list(w['ret_const_tabs']) + [w['ret_gn']]
    in_specs += ([_pick_spec(a, layer) for a in w['ret_layer_tabs']]
                 + [_const_spec(a.shape) for a in w['ret_const_tabs']] + [_pick_spec(w['ret_gn'], layer)])
    if has_s0:
        args.append(s0)
        in_specs.append(pl.BlockSpec((None, None, 2, RET_HEADS, RET_DIM, RET_DIM),
                                     lambda b, p, c: (b, layer, 0, 0, 0, 0)))
    out_specs = [pl.BlockSpec((None, tc, SEG), lambda b, p, c: (b, q_chunk(p, c), 0))]
    out_shape = [jax.ShapeDtypeStruct((nb, t, SEG), F32)]
    if want_fin:
        out_specs.append(pl.BlockSpec((None, 2, RET_HEADS, RET_DIM, RET_DIM),
                                      lambda b, p, c: (b, 0, 0, 0, 0)))
        out_shape.append(jax.ShapeDtypeStruct((nb, 2, RET_HEADS, RET_DIM, RET_DIM), F32))
    return pl.pallas_call(
        functools.partial(_ret_kernel, rope=rope, has_s0=has_s0, want_fin=want_fin, nch=nch),
        grid=(nb, 2, nch),
        in_specs=in_specs,
        out_specs=out_specs,
        out_shape=out_shape,
        scratch_shapes=[pltpu.VMEM((nch, SEG, SEG), F32),
                        pltpu.VMEM((SEG, SEG), F32),
                        pltpu.VMEM((SEG, SEG), F32)],
        compiler_params=_cparams(("arbitrary", "arbitrary", "arbitrary")),
        name="retention",
    )(*args)


def _mla_kernel(*refs, t, tq, nctx, want_cache):
    refs = list(refs)
    cq_ref, ckv_ref, krp_ref = refs[:3]
    del refs[:3]
    has_ctx = nctx > 0
    if has_ctx:
        krr_ref, cm_ref, sm_ref, cache_ref = refs[:4]
        del refs[:4]
    qn_ref, kvn_ref, wq_ref = refs[:3]
    del refs[:3]
    if has_ctx:
        wqr_ref, wkc_ref = refs[:2]
        del refs[:2]
    wk_ref, wv_ref = refs[:2]
    del refs[:2]
    o_ref = refs.pop(0)
    lat_ref = refs.pop(0) if want_cache else None
    k_scr, v_scr = refs
    qi = pl.program_id(1)

    @pl.when(qi == 0)
    def _():
        lat = _rms(ckv_ref[...].astype(F32), kvn_ref[...])
        if want_cache:
            lat_ref[...] = lat
        latb = lat.astype(BF16)
        kr = krp_ref[...].astype(F32)
        if has_ctx:
            kr = kr * cm_ref[...] + krr_ref[...].astype(F32) * sm_ref[...]
        kk = jnp.dot(latb, wk_ref[...], preferred_element_type=F32)
        for h in range(MLA_HEADS):
            k_scr[h, 0:t, :] = (kk[:, h * MLA_SLAB:(h + 1) * MLA_SLAB] + kr).astype(BF16)
        v_scr[0:t, :] = jnp.dot(latb, wv_ref[...], preferred_element_type=F32).astype(BF16)
        if has_ctx:
            cache = cache_ref[...].astype(BF16)
            kkc = jnp.dot(cache, wkc_ref[...], preferred_element_type=F32)
            for h in range(MLA_HEADS):
                k_scr[h, t:t + nctx, :] = kkc[:, h * MLA_SLAB:(h + 1) * MLA_SLAB].astype(BF16)
            v_scr[t:t + nctx, :] = jnp.dot(cache[:, 0:MLA_KV_LORA], wv_ref[...],
                                           preferred_element_type=F32).astype(BF16)

    a = _rms(cq_ref[...].astype(F32), qn_ref[...], n=MLA_Q_LORA).astype(BF16)
    q = jnp.dot(a, wq_ref[...], preferred_element_type=F32)
    if has_ctx:
        qrot = jnp.dot(a, wqr_ref[...], preferred_element_type=F32)
        r0 = pl.multiple_of(qi * tq, tq)
        cm = cm_ref[pl.ds(r0, tq), :]
        sm = sm_ref[pl.ds(r0, tq), :]
    scale = (MLA_NOPE + MLA_ROPE) ** -0.5
    lane = lax.broadcasted_iota(jnp.int32, (tq, SEG), 1)
    acc = jnp.zeros((tq, SEG), F32)
    for h in range(MLA_HEADS):
        qh = q[:, h * MLA_SLAB:(h + 1) * MLA_SLAB]
        if has_ctx:
            qh = qh * cm + qrot[:, h * MLA_SLAB:(h + 1) * MLA_SLAB] * sm
        s = _dot_nt(qh * scale, k_scr[h])
        m = jnp.max(s, axis=-1, keepdims=True)
        e = jnp.exp(s - m)
        l = jnp.sum(e, axis=-1, keepdims=True)
        oh = jnp.dot(e.astype(BF16), v_scr[...], preferred_element_type=F32) * (1.0 / l)
        acc = jnp.where((lane >= h * MLA_V) & (lane < (h + 1) * MLA_V), oh, acc)
    o_ref[...] = acc


def _mla_call(p3, w, rope_tabs, cache, *, layer, nb, t, tq, want_cache):
    has_ctx = cache is not None
    nctx = cache.shape[2] if has_ctx else 0
    s_tot = t + nctx
    half = MLA_KV_LORA

    args = [p3, p3, p3]
    in_specs = [
        pl.BlockSpec((None, tq, SEG), lambda b, i: (b, i, S_CQ)),
        pl.BlockSpec((None, t, half), lambda b, i: (b, 0, 2 * S_KV)),
        pl.BlockSpec((None, t, half), lambda b, i: (b, 0, 2 * S_KV + 1)),
    ]
    if has_ctx:
        args += [p3, rope_tabs[0], rope_tabs[1], cache]
        in_specs += [
            pl.BlockSpec((None, t, half), lambda b, i: (b, 0, 2 * S_KRR)),
            _const_spec((t, MLA_SLAB)),
            _const_spec((t, MLA_SLAB)),
            pl.BlockSpec((None, None, nctx, MLA_KV_LORA + MLA_ROPE), lambda b, i: (b, layer, 0, 0)),
        ]
    args += [w['mla_qn'], w['mla_kvn'], w['mla_wq']]
    in_specs += [_pick_spec(w['mla_qn'], layer), _pick_spec(w['mla_kvn'], layer), _pick_spec(w['mla_wq'], layer)]
    if has_ctx:
        args += [w['mla_wqr'], w['mla_wkc']]
        in_specs += [_pick_spec(w['mla_wqr'], layer), _pick_spec(w['mla_wkc'], layer)]
    args += [w['mla_wk'], w['mla_wv']]
    in_specs += [_pick_spec(w['mla_wk'], layer), _pick_spec(w['mla_wv'], layer)]
    out_specs = [pl.BlockSpec((None, tq, SEG), lambda b, i: (b, i, 0))]
    out_shape = [jax.ShapeDtypeStruct((nb, t, SEG), F32)]
    if want_cache:
        out_specs.append(pl.BlockSpec((None, t, MLA_KV_LORA), lambda b, i: (b, 0, 0)))
        out_shape.append(jax.ShapeDtypeStruct((nb, t, MLA_KV_LORA), F32))
    return pl.pallas_call(
        functools.partial(_mla_kernel, t=t, tq=tq, nctx=nctx, want_cache=want_cache),
        grid=(nb, t // tq),
        in_specs=in_specs,
        out_specs=out_specs,
        out_shape=out_shape,
        scratch_shapes=[pltpu.VMEM((MLA_HEADS, s_tot, MLA_SLAB), BF16),
                        pltpu.VMEM((s_tot, SEG), BF16)],
        compiler_params=_cparams(("arbitrary", "arbitrary")),
        name="mla",
    )(*args)


def _merge_kernel(x_ref, mod_ref, npre_ref, npost_ref, yf_ref, yb_ref, u_ref, ret_ref,
                  cx_ref, cb_ref, cc_ref, cxp_ref, ccp_ref, cxn_ref, ccn_ref, mla_ref,
                  s5d_ref, wglu_ref, cw_ref, cbias_ref, wbr_ref, wg_ref, bg_ref, wo_ref,
                  o_ref, h_scr, *, tm, nt):
    i = pl.program_id(1)
    x = x_ref[...]
    sh = mod_ref[3:4, :]
    sc = mod_ref[4:5, :]
    gt = mod_ref[5:6, :]
    h_scr[...] = (_rms(x, npre_ref[...]) * (1.0 + sc) + sh).astype(BF16)

    y = yf_ref[...] + yb_ref[...] + s5d_ref[...] * u_ref[...].astype(F32)
    z5 = _gelu_tanh(y)
    b_s5 = z5 * _sigmoid(_dot(z5, wglu_ref[...]))

    z = cc_ref[...].astype(F32) * cx_ref[...].astype(F32)
    zprev = ((ccp_ref[...].astype(F32) * cxp_ref[...].astype(F32))[HALO - 1:HALO, :]
             * jnp.where(i > 0, 1.0, 0.0))
    znext = ((ccn_ref[...].astype(F32) * cxn_ref[...].astype(F32))[0:1, :]
             * jnp.where(i < nt - 1, 1.0, 0.0))
    row = lax.broadcasted_iota(jnp.int32, (tm, SEG), 0)
    z_dn = jnp.where(row == 0, zprev, pltpu.roll(z, 1, axis=0))
    z_up = jnp.where(row == tm - 1, znext, pltpu.roll(z, tm - 1, axis=0))
    b_conv = cb_ref[...].astype(F32) * (cbias_ref[...] + z_dn * cw_ref[0:1, :] + z * cw_ref[1:2, :]
                                        + z_up * cw_ref[2:3, :])

    branches = (b_s5, ret_ref[...], b_conv, mla_ref[...])
    merged = jnp.zeros((tm, D_MODEL), F32)
    for n in range(N_BRANCH):
        gate = _sigmoid(jnp.dot(h_scr[...], wg_ref[:, n * D_MODEL:(n + 1) * D_MODEL],
                                preferred_element_type=F32) + bg_ref[:, n * D_MODEL:(n + 1) * D_MODEL])
        merged = merged + gate * _dot(branches[n], wbr_ref[n])
    m = _dot(merged, wo_ref[...])
    o_ref[...] = x + gt * _rms(m, npost_ref[...])


def _merge_call(x, w, y_s5, ret_o, p3, mla_o, *, layer, nb, t, tm, row_of_batch):
    nt = t // tm
    hb = tm // HALO
    n_halo = t // HALO

    def dir_spec(d):
        return pl.BlockSpec((None, None, tm, SEG), lambda b, i: (d, b, i, 0))

    def seg_spec(seg):
        return pl.BlockSpec((None, tm, SEG), lambda b, i: (b, i, seg))

    def prev_spec(seg):
        return pl.BlockSpec((None, HALO, SEG), lambda b, i: (b, jnp.maximum(i * hb - 1, 0), seg))

    def next_spec(seg):
        return pl.BlockSpec((None, HALO, SEG), lambda b, i: (b, jnp.minimum((i + 1) * hb, n_halo - 1), seg))

    return pl.pallas_call(
        functools.partial(_merge_kernel, tm=tm, nt=nt),
        grid=(nb, nt),
        in_specs=[
            pl.BlockSpec((tm, D_MODEL), lambda b, i: (b * nt + i, 0)),
            pl.BlockSpec((None, None, N_MOD, D_MODEL), lambda b, i: (layer, row_of_batch(b), 0, 0)),
            _pick_spec(w['npre'], layer * N_NORM + 1),
            _pick_spec(w['npost'], layer * N_NORM + 1),
            dir_spec(0), dir_spec(1), seg_spec(S_U),
            pl.BlockSpec((None, tm, SEG), lambda b, i: (b, i, 0)),
            seg_spec(S_CX), seg_spec(S_CB), seg_spec(S_CC),
            prev_spec(S_CX), prev_spec(S_CC), next_spec(S_CX), next_spec(S_CC),
            pl.BlockSpec((None, tm, SEG), lambda b, i: (b, i, 0)),
            _pick_spec(w['s5_d'], layer), _pick_spec(w['s5_wglu'], layer), _pick_spec(w['conv_w'], layer),
            _pick_spec(w['conv_b'], layer), _resident_spec(w['w_branch'].shape[1:], (layer,)),
            _resident_spec(w['w_gate'].shape[1:], (layer,)), _pick_spec(w['b_gate'], layer),
            _resident_spec(w['w_o'].shape[1:], (layer,)),
        ],
        out_specs=pl.BlockSpec((tm, D_MODEL), lambda b, i: (b * nt + i, 0)),
        out_shape=jax.ShapeDtypeStruct((nb * t, D_MODEL), F32),
        scratch_shapes=[pltpu.VMEM((tm, D_MODEL), BF16)],
        compiler_params=_cparams(("arbitrary", "arbitrary")),
        name="merge",
    )(x, w['mod'], w['npre'], w['npost'], y_s5, y_s5, p3, ret_o, p3, p3, p3, p3, p3, p3, p3, mla_o,
      w['s5_d'], w['s5_wglu'], w['conv_w'], w['conv_b'], w['w_branch'], w['w_gate'], w['b_gate'], w['w_o'])


def _zeros_like_cols(w, n):
    return jnp.zeros(w.shape[:-1] + (n,), w.dtype)


def _rot_half(w, heads, dim):
    w4 = w.reshape(w.shape[:-1] + (heads, dim))
    half = dim // 2
    return jnp.concatenate([-w4[..., half:], w4[..., :half]], axis=-1).reshape(w.shape)


def _axial_angles(t, dim):
    rows = t // GRID_W
    row = np.repeat(np.arange(rows, dtype=np.float64), GRID_W)
    col = np.tile(np.arange(GRID_W, dtype=np.float64), rows)
    quarter = dim // 4
    inv = ROPE_BASE ** (-np.arange(quarter, dtype=np.float64) / quarter)
    ang = np.concatenate([row[:, None] * inv, col[:, None] * inv], axis=-1)
    return np.cos(ang).astype(np.float32), np.sin(ang).astype(np.float32)


def _rope_tables(t):
    cos, sin = _axial_angles(t, RET_DIM)
    ret_cos = np.tile(np.concatenate([cos, cos], -1), (1, 2))
    ret_sin = np.tile(np.concatenate([sin, sin], -1), (1, 2))
    cos, sin = _axial_angles(t, MLA_ROPE)
    pad = MLA_SLAB - MLA_NOPE - MLA_ROPE
    mla_cos = np.concatenate([np.ones((t, MLA_NOPE), np.float32), cos, cos, np.zeros((t, pad), np.float32)], -1)
    mla_sin = np.concatenate([np.zeros((t, MLA_NOPE), np.float32), sin, sin, np.zeros((t, pad), np.float32)], -1)
    return ((jnp.asarray(ret_cos), jnp.asarray(ret_sin)), (jnp.asarray(mla_cos), jnp.asarray(mla_sin)))


def _place_rope_key(w):
    return jnp.concatenate([_zeros_like_cols(w, MLA_NOPE), w,
                            _zeros_like_cols(w, MLA_SLAB - MLA_NOPE - MLA_ROPE)], -1)


def _in_weights(w_in):
    sizes = (BRANCH_WIDTH,) * 8 + (MLA_Q_LORA, MLA_KV_LORA, MLA_ROPE)
    offs = np.cumsum((0,) + sizes)
    u, rq, rk, rv, rg, cx, cb, cc, cq, ckv, kr = [w_in[..., offs[i]:offs[i + 1]] for i in range(11)]
    rk = rk * (RET_DIM ** -0.5)
    cols = [u, rq, rk, rv, rg, cx, cb, cc,
            jnp.concatenate([cq, _zeros_like_cols(cq, SEG - MLA_Q_LORA)], -1),
            jnp.concatenate([ckv, _place_rope_key(kr)], -1),
            _rot_half(rq, RET_HEADS, RET_DIM), _rot_half(rk, RET_HEADS, RET_DIM),
            jnp.concatenate([_place_rope_key(_rot_half(kr, 1, MLA_ROPE)), _zeros_like_cols(kr, SEG - MLA_SLAB)], -1)]
    return jnp.concatenate([c.astype(BF16) for c in cols], axis=-1)


def _mla_weights(w_uq, w_ukv):
    nl = w_uq.shape[0]
    dq = MLA_NOPE + MLA_ROPE
    w4 = w_uq.reshape(nl, MLA_Q_LORA, MLA_HEADS, dq)
    nope, ropew = w4[..., :MLA_NOPE], w4[..., MLA_NOPE:]
    half = MLA_ROPE // 2
    rot = jnp.concatenate([-ropew[..., half:], ropew[..., :half]], -1)
    zpad = jnp.zeros((nl, MLA_Q_LORA, MLA_HEADS, MLA_SLAB - dq), F32)
    wq = jnp.concatenate([nope, ropew, zpad], -1).reshape(nl, MLA_Q_LORA, MLA_HEADS * MLA_SLAB)
    wqr = jnp.concatenate([jnp.zeros_like(nope), rot, zpad], -1).reshape(nl, MLA_Q_LORA, MLA_HEADS * MLA_SLAB)
    rpad = jnp.zeros((nl, SEG - MLA_Q_LORA, MLA_HEADS * MLA_SLAB), F32)
    wq = jnp.concatenate([wq, rpad], 1).astype(BF16)
    wqr = jnp.concatenate([wqr, rpad], 1).astype(BF16)
    kv4 = w_ukv.reshape(nl, MLA_KV_LORA, MLA_HEADS, MLA_NOPE + MLA_V)
    wk = jnp.concatenate([kv4[..., :MLA_NOPE], jnp.zeros((nl, MLA_KV_LORA, MLA_HEADS, MLA_SLAB - MLA_NOPE), F32)],
                         -1).reshape(nl, MLA_KV_LORA, MLA_HEADS * MLA_SLAB)
    place = np.zeros((MLA_ROPE, MLA_HEADS, MLA_SLAB), np.float32)
    place[np.arange(MLA_ROPE), :, MLA_NOPE + np.arange(MLA_ROPE)] = 1.0
    place = jnp.broadcast_to(jnp.asarray(place.reshape(MLA_ROPE, MLA_HEADS * MLA_SLAB)),
                             (nl, MLA_ROPE, MLA_HEADS * MLA_SLAB))
    wkc = jnp.concatenate([wk, place], 1).astype(BF16)
    wv = kv4[..., MLA_NOPE:].reshape(nl, MLA_KV_LORA, MLA_HEADS * MLA_V).astype(BF16)
    return wq, wqr, wk.astype(BF16), wkc, wv


def _s5_tables(lam_re, lam_im, log_dt, b_re, b_im, c_re, c_im):
    nl = lam_re.shape[0]
    dt = jnp.exp(log_dt)[..., None]
    mag = jnp.exp(lam_re * dt)
    ab_re = mag * jnp.cos(lam_im * dt)
    ab_im = mag * jnp.sin(lam_im * dt)
    den = lam_re * lam_re + lam_im * lam_im
    f_re = ((ab_re - 1.0) * lam_re + ab_im * lam_im) / den
    f_im = (ab_im * lam_re - (ab_re - 1.0) * lam_im) / den
    bb_re = f_re[..., None] * b_re - f_im[..., None] * b_im
    bb_im = f_re[..., None] * b_im + f_im[..., None] * b_re
    eye = jnp.eye(S5_GROUPS, dtype=F32)
    blk_in = lambda m: jnp.einsum('ldgph,gk->ldghkp', m, eye).reshape(nl, 2, BRANCH_WIDTH, S5_LANES)
    bblk = jnp.concatenate([blk_in(bb_re), blk_in(bb_im)], -1).astype(BF16)
    blk_out = lambda m: jnp.einsum('lghp,gk->lgpkh', m, eye).reshape(nl, S5_LANES, BRANCH_WIDTH)
    cblk = jnp.concatenate([blk_out(c_re), -blk_out(c_im)], 1).astype(BF16)
    a_tab = jnp.stack([ab_re.reshape(nl, 2, S5_LANES), ab_im.reshape(nl, 2, S5_LANES)], axis=2)
    return bblk, a_tab, cblk


def _ret_tables(decay_logit, c):
    lg = jax.nn.log_sigmoid(decay_logit.astype(F32))
    lf = lg[:, 0][:, :, None, None]
    lb = lg[:, 1][:, :, None, None]
    idx = jnp.arange(c, dtype=F32)
    diff = idx[:, None] - idx[None, :]
    dm = jnp.where(diff >= 0, jnp.exp(lf * jnp.maximum(diff, 0.0)), jnp.exp(lb * jnp.maximum(-diff, 0.0)))
    lfl = jnp.repeat(lg[:, 0], RET_DIM, axis=-1)
    lbl = jnp.repeat(lg[:, 1], RET_DIM, axis=-1)
    qdf = jnp.exp(lfl[:, None, :] * (idx[:, None] + 1.0))
    qdb = jnp.exp(lbl[:, None, :] * (c - idx[:, None]))
    kdft = jnp.exp(lfl[:, :, None] * (c - 1.0 - idx[None, :]))
    kdbt = jnp.exp(lbl[:, :, None] * idx[None, :])
    nl = lg.shape[0]
    cdf = jnp.broadcast_to(jnp.exp(lfl * c)[:, :, None], (nl, SEG, SEG))
    cdb = jnp.broadcast_to(jnp.exp(lbl * c)[:, :, None], (nl, SEG, SEG))
    head = np.arange(SEG) // RET_DIM
    hm = jnp.asarray((head[None, :] == np.arange(RET_HEADS)[:, None]).astype(np.float32))
    bd_np = (head[:, None] == head[None, :]).astype(np.float32)
    bd = jnp.asarray(bd_np)
    gm = jnp.asarray(bd_np / RET_DIM, BF16)
    return (dm, qdf, qdb, kdft, kdbt, cdf, cdb), (hm, bd, gm)


def _layer(x, w, *, layer, nb, t, tm, tm_ffn, row_of_batch, ctx, rope_tabs):
    is_ctx_pass = ctx is None
    row_of_tile = lambda i: row_of_batch((i * tm_ffn) // t)
    x = _ffn_call(x, w, layer=layer, k=0, tm=tm_ffn, row_of_tile=row_of_tile)

    nseg = NSEG_PLAIN if is_ctx_pass else NSEG_ROPE
    p = _in_call(x, w, layer=layer, nb=nb, t=t, tm=tm, nseg=nseg, row_of_batch=row_of_batch)
    p3 = p.reshape(nb, t, nseg * SEG)

    if is_ctx_pass:
        s5_0, s0_layer = jnp.zeros((1, 2, 2, nb, S5_LANES), F32), 0
        ret_0, mla_cache, ret_rope, mla_rope = None, None, None, None
    else:
        s5_0, ret_0, mla_cache = ctx
        s0_layer = layer
        ret_rope, mla_rope = rope_tabs
    s5_out = _s5_call(p3, w, s5_0, layer=layer, s0_layer=s0_layer, nb=nb, t=t, want_fin=is_ctx_pass)
    ret_out = _ret_call(p3, w, ret_0, ret_rope, layer=layer, nb=nb, t=t, want_fin=is_ctx_pass)
    mla_out = _mla_call(p3, w, mla_rope, mla_cache, layer=layer, nb=nb, t=t, tq=min(t, 512),
                        want_cache=is_ctx_pass)

    x = _merge_call(x, w, s5_out[0], ret_out[0], p3, mla_out[0], layer=layer, nb=nb, t=t, tm=tm,
                    row_of_batch=row_of_batch)
    x = _ffn_call(x, w, layer=layer, k=2, tm=tm_ffn, row_of_tile=row_of_tile)

    state = None
    if is_ctx_pass:
        kr = p3[:, :, S_KV * SEG + MLA_KV_LORA + MLA_NOPE:S_KV * SEG + MLA_KV_LORA + MLA_NOPE + MLA_ROPE]
        cache = jnp.concatenate([mla_out[1], kr.astype(F32)], axis=-1)
        state = (s5_out[1], ret_out[1], cache)
    return x, state


def kernel(x_prompt, x_sample, state_s5, state_ret, cache_mla, c, c_ctx, w_mod, b_mod, norm_pre, norm_post, ffn_w1, ffn_w3, ffn_w2, w_in, s5_lam_re, s5_lam_im, s5_log_dt, s5_b_re, s5_b_im, s5_c_re, s5_c_im, s5_d, s5_w_glu, ret_decay, ret_gn, conv_w, conv_b, mla_q_norm, mla_w_uq, mla_kv_norm, mla_w_ukv, w_branch, w_gate, b_gate, w_o):
    bp, tp, _ = x_prompt.shape
    bs, ts, _ = x_sample.shape
    assert 1 + bs <= MOD_ROWS

    cond = jnp.concatenate([c_ctx[None, :], c, jnp.zeros((MOD_ROWS - 1 - bs, D_MODEL), F32)], axis=0)
    mod = _mod_call(cond, w_mod, b_mod).reshape(DEPTH, MOD_ROWS, N_MOD, D_MODEL)

    wq, wqr, wk, wkc, wv = _mla_weights(mla_w_uq, mla_w_ukv)
    bblk, a_tab, cblk = _s5_tables(s5_lam_re, s5_lam_im, s5_log_dt, s5_b_re, s5_b_im, s5_c_re, s5_c_im)
    ret_layer_tabs, ret_const_tabs = _ret_tables(ret_decay, RET_CHUNK)
    w = dict(
        mod=mod,
        npre=norm_pre.reshape(DEPTH * N_NORM, 1, D_MODEL),
        npost=norm_post.reshape(DEPTH * N_NORM, 1, D_MODEL),
        ffn=(ffn_w1.astype(BF16), ffn_w3.astype(BF16), ffn_w2.astype(BF16)),
        w_in=_in_weights(w_in),
        s5_bblk=bblk, s5_a=a_tab, s5_cblk=cblk,
        s5_d=s5_d[:, None, :], s5_wglu=s5_w_glu.astype(BF16),
        ret_layer_tabs=ret_layer_tabs, ret_const_tabs=ret_const_tabs, ret_gn=ret_gn[:, None, :],
        conv_w=conv_w, conv_b=conv_b[:, None, :],
        mla_qn=jnp.concatenate([mla_q_norm, jnp.zeros((DEPTH, SEG - MLA_Q_LORA), F32)], -1)[:, None, :],
        mla_kvn=mla_kv_norm[:, None, :],
        mla_wq=wq, mla_wqr=wqr, mla_wk=wk, mla_wkc=wkc, mla_wv=wv,
        w_branch=w_branch.astype(BF16), w_gate=w_gate.astype(BF16), b_gate=b_gate[:, None, :],
        w_o=w_o.astype(BF16),
    )
    rope_tabs = _rope_tables(ts)
    s5_0 = state_s5.transpose(1, 2, 5, 0, 3, 4).reshape(DEPTH, 2, 2, bs, S5_LANES)

    xp = x_prompt.reshape(bp * tp, D_MODEL)
    xs = x_sample.reshape(bs * ts, D_MODEL)
    s5_list, ret_list, mla_list = [], [], []
    for l in range(DEPTH):
        xp, (s5_s, ret_s, mla_c) = _layer(xp, w, layer=l, nb=bp, t=tp, tm=256, tm_ffn=512,
                                          row_of_batch=lambda b: 0, ctx=None, rope_tabs=None)
        s5_list.append(s5_s)
        ret_list.append(ret_s)
        mla_list.append(mla_c)
        xs, _ = _layer(xs, w, layer=l, nb=bs, t=ts, tm=512, tm_ffn=1024, row_of_batch=lambda b: 1 + b,
                       ctx=(s5_0, state_ret, cache_mla), rope_tabs=rope_tabs)
    new_s5 = jnp.stack(s5_list, 0).reshape(DEPTH, 2, 2, bp, S5_GROUPS, S5_STATE).transpose(3, 0, 1, 4, 5, 2)
    return (xp.reshape(bp, tp, D_MODEL), xs.reshape(bs, ts, D_MODEL),
            new_s5, jnp.stack(ret_list, axis=1), jnp.stack(mla_list, axis=1))
```

```python
import functools

import numpy as np
import jax
import jax.numpy as jnp
from jax import lax
from jax.experimental import pallas as pl
from jax.experimental.pallas import tpu as pltpu

F32 = jnp.float32
BF16 = jnp.bfloat16

D_MODEL = 1024
DEPTH = 2
GRID_W = 64
EPS = 1e-6
LOG2_E = float(np.log2(np.e))
ROPE_BASE = 10000.0
N_BRANCH = 4
BRANCH_WIDTH = D_MODEL // 4
N_MOD = 9
N_NORM = 3
D_FF = 2816
S5_GROUP = 16
S5_GROUPS = BRANCH_WIDTH // S5_GROUP
S5_STATE = 64
S5_LANES = S5_GROUPS * S5_STATE
RET_HEADS = 4
RET_DIM = BRANCH_WIDTH // RET_HEADS
CONV_K = 3
MLA_HEADS = 4
MLA_Q_LORA = 192
MLA_KV_LORA = 128
MLA_NOPE = 64
MLA_ROPE = 32
MLA_V = BRANCH_WIDTH // MLA_HEADS
MLA_SLAB = 128
MLA_VT_ROWS = MLA_V + 16

SUBLANES = 8
S5_SUBCHUNKS = 4
S5_ROWS = 512
SEG = 256
FF_CHUNK = 256
N_FF_CHUNKS = D_FF // FF_CHUNK
MOD_ROWS = 16
HALO = 16
RET_CHUNK = 256
V7X_VMEM_LIMIT = 56 * 1024 * 1024

S_U, S_RQ, S_RK, S_RV, S_RG, S_CX, S_CB, S_CC, S_CQ, S_KV, S_RQR, S_RKR, S_KRR = range(13)
NSEG_PLAIN = 10
NSEG_ROPE = 13


def _cparams(sem):
    return pltpu.CompilerParams(dimension_semantics=sem, vmem_limit_bytes=V7X_VMEM_LIMIT)


def _dot(a, b):
    return jnp.dot(a.astype(BF16), b.astype(BF16), preferred_element_type=F32)


def _sigmoid(x):
    return 1.0 / (1.0 + jnp.exp(-x))


def _rms(x, g, n=None):
    n = x.shape[-1] if n is None else n
    ms = jnp.sum(x * x, axis=-1, keepdims=True) * (1.0 / n)
    return x * lax.rsqrt(ms + EPS) * g


def _gelu_tanh(x):
    return 0.5 * x * (1.0 + jnp.tanh(np.sqrt(2.0 / np.pi) * (x + 0.044715 * (x * x * x))))


def _const_spec(shape):
    nd = len(shape)
    return pl.BlockSpec(shape, lambda *_: (0,) * nd)


def _pick_spec(arr, *lead):
    shape = arr.shape[len(lead):]
    return pl.BlockSpec((None,) * len(lead) + tuple(shape), lambda *_: tuple(lead) + (0,) * len(shape))


def _resident_spec(shape, lead=()):
    nd = len(shape)
    return pl.BlockSpec((None,) * len(lead) + tuple(shape), lambda *_: tuple(lead) + (0,) * nd,
                        pipeline_mode=pl.Buffered(1))


def _mod_kernel(c_ref, w_ref, b_ref, o_ref):
    c = c_ref[...]
    o_ref[...] = _dot(c * _sigmoid(c), w_ref[...]) + b_ref[...]


def _mod_call(cond, w_mod, b_mod):
    ncol = N_MOD * D_MODEL
    tn = D_MODEL
    return pl.pallas_call(
        _mod_kernel,
        grid=(DEPTH, ncol // tn),
        in_specs=[
            pl.BlockSpec((MOD_ROWS, D_MODEL), lambda l, j: (0, 0)),
            pl.BlockSpec((None, D_MODEL, tn), lambda l, j: (l, 0, j)),
            pl.BlockSpec((None, 1, tn), lambda l, j: (l, 0, j)),
        ],
        out_specs=pl.BlockSpec((None, MOD_ROWS, tn), lambda l, j: (l, 0, j)),
        out_shape=jax.ShapeDtypeStruct((DEPTH, MOD_ROWS, ncol), F32),
        compiler_params=_cparams(("arbitrary", "arbitrary")),
        name="mod",
    )(cond, w_mod, b_mod.reshape(DEPTH, 1, ncol))


def _ffn_kernel(x_ref, mod_ref, npre_ref, npost_ref, w1_ref, w3_ref, w2_ref, o_ref,
                h_scr, g_scr, *, k):
    x = x_ref[...]
    sh = mod_ref[3 * k:3 * k + 1, :]
    sc = mod_ref[3 * k + 1:3 * k + 2, :]
    gt = mod_ref[3 * k + 2:3 * k + 3, :]
    h_scr[...] = (_rms(x, npre_ref[...]) * (1.0 + sc) + sh).astype(BF16)
    for j in range(N_FF_CHUNKS):
        cols = slice(j * FF_CHUNK, (j + 1) * FF_CHUNK)
        a = jnp.dot(h_scr[...], w1_ref[:, cols], preferred_element_type=F32)
        b = jnp.dot(h_scr[...], w3_ref[:, cols], preferred_element_type=F32)
        g_scr[:, cols] = (a * _sigmoid(a) * b).astype(BF16)
    f = jnp.dot(g_scr[...], w2_ref[...], preferred_element_type=F32)
    o_ref[...] = x + (0.5 * gt) * _rms(f, npost_ref[...])


def _ffn_call(x, w, *, layer, k, tm, row_of_tile):
    n = x.shape[0]
    lj = (layer, k // 2)
    w1, w3, w2 = w['ffn']
    return pl.pallas_call(
        functools.partial(_ffn_kernel, k=k),
        grid=(n // tm,),
        in_specs=[
            pl.BlockSpec((tm, D_MODEL), lambda i: (i, 0)),
            pl.BlockSpec((None, None, N_MOD, D_MODEL), lambda i: (layer, row_of_tile(i), 0, 0)),
            _pick_spec(w['npre'], layer * N_NORM + k),
            _pick_spec(w['npost'], layer * N_NORM + k),
            _resident_spec((D_MODEL, D_FF), lj),
            _resident_spec((D_MODEL, D_FF), lj),
            _resident_spec((D_FF, D_MODEL), lj),
        ],
        out_specs=pl.BlockSpec((tm, D_MODEL), lambda i: (i, 0)),
        out_shape=jax.ShapeDtypeStruct((n, D_MODEL), F32),
        scratch_shapes=[pltpu.VMEM((tm, D_MODEL), BF16), pltpu.VMEM((tm, D_FF), BF16)],
        compiler_params=_cparams(("arbitrary",)),
        name="ffn",
    )(x, w['mod'], w['npre'], w['npost'], w1, w3, w2)


def _in_kernel(x_ref, mod_ref, npre_ref, w_ref, p_ref, h_scr, *, nseg):
    sh = mod_ref[3:4, :]
    sc = mod_ref[4:5, :]
    h_scr[...] = (_rms(x_ref[...], npre_ref[...]) * (1.0 + sc) + sh).astype(BF16)
    for j in range(nseg):
        cols = slice(j * SEG, (j + 1) * SEG)
        p_ref[:, cols] = jnp.dot(h_scr[...], w_ref[:, cols], preferred_element_type=F32).astype(BF16)


def _in_call(x, w, *, layer, nb, t, tm, nseg, row_of_batch):
    tt = t // tm
    return pl.pallas_call(
        functools.partial(_in_kernel, nseg=nseg),
        grid=(nb, tt),
        in_specs=[
            pl.BlockSpec((tm, D_MODEL), lambda b, i: (b * tt + i, 0)),
            pl.BlockSpec((None, None, N_MOD, D_MODEL), lambda b, i: (layer, row_of_batch(b), 0, 0)),
            _pick_spec(w['npre'], layer * N_NORM + 1),
            _resident_spec((D_MODEL, nseg * SEG), (layer,)),
        ],
        out_specs=pl.BlockSpec((tm, nseg * SEG), lambda b, i: (b * tt + i, 0)),
        out_shape=jax.ShapeDtypeStruct((nb * t, nseg * SEG), BF16),
        scratch_shapes=[pltpu.VMEM((tm, D_MODEL), BF16)],
        compiler_params=_cparams(("arbitrary", "arbitrary")),
        name="in_proj",
    )(x, w['mod'], w['npre'], w['w_in'])


def _s5_kernel(*refs, nb, steps, want_fin):
    if want_fin:
        u_ref, perm_ref, permt_ref, bb_ref, a_ref, cc_ref, s0_ref, y_ref, fin_ref, s_scr, bu_scr = refs
    else:
        u_ref, perm_ref, permt_ref, bb_ref, a_ref, cc_ref, s0_ref, y_ref, s_scr, bu_scr = refs
        fin_ref = None
    d = pl.program_id(0)
    c = pl.program_id(1)
    rows = nb * steps
    ngrp = nb // SUBLANES

    @pl.when(c == 0)
    def _():
        s_scr[...] = s0_ref[...]

    a_re = jnp.broadcast_to(a_ref[0:1, :], (SUBLANES, S5_LANES))
    a_im = jnp.broadcast_to(a_ref[1:2, :], (SUBLANES, S5_LANES))
    s_re = [s_scr[0, g * SUBLANES:(g + 1) * SUBLANES, :] for g in range(ngrp)]
    s_im = [s_scr[1, g * SUBLANES:(g + 1) * SUBLANES, :] for g in range(ngrp)]
    re_cols = slice(0, S5_LANES)
    im_cols = slice(S5_LANES, 2 * S5_LANES)
    def sub_offset(k):
        sub = jnp.where(d == 0, k, S5_SUBCHUNKS - 1 - k)
        return pl.multiple_of(sub * steps, steps)

    def project_in(k):
        u_bt = u_ref[:, pl.ds(sub_offset(k), steps), :].reshape(rows, SEG)
        u_tb = jnp.dot(perm_ref[...], u_bt, preferred_element_type=F32).astype(BF16)
        bu_scr[k] = jnp.dot(u_tb, bb_ref[...], preferred_element_type=F32)

    project_in(0)
    for k in range(S5_SUBCHUNKS):
        off = sub_offset(k)
        if k + 1 < S5_SUBCHUNKS:
            project_in(k + 1)
        for i in range(steps):
            for g in range(ngrp):
                rr = slice(i * nb + g * SUBLANES, i * nb + (g + 1) * SUBLANES)
                n_re = a_re * s_re[g] - a_im * s_im[g] + bu_scr[k, rr, re_cols]
                n_im = a_re * s_im[g] + a_im * s_re[g] + bu_scr[k, rr, im_cols]
                bu_scr[k, rr, re_cols] = n_re
                bu_scr[k, rr, im_cols] = n_im
                s_re[g], s_im[g] = n_re, n_im
        y_tb = _dot(bu_scr[k], cc_ref[...])
        hi = y_tb.astype(BF16)
        lo = (y_tb - hi.astype(F32)).astype(BF16)
        y_bt = (jnp.dot(permt_ref[...], hi, preferred_element_type=F32)
                + jnp.dot(permt_ref[...], lo, preferred_element_type=F32))
        y_ref[:, pl.ds(off, steps), :] = y_bt.reshape(nb, steps, SEG)
    for g in range(ngrp):
        s_scr[0, g * SUBLANES:(g + 1) * SUBLANES, :] = s_re[g]
        s_scr[1, g * SUBLANES:(g + 1) * SUBLANES, :] = s_im[g]
    if want_fin:
        fin_ref[...] = s_scr[...]


def _s5_call(p3, w, s0, *, layer, s0_layer, nb, t, want_fin):
    steps = S5_ROWS // nb
    span = steps * S5_SUBCHUNKS
    nch = t // span
    rows = S5_ROWS
    r = np.arange(rows)
    perm_np = np.zeros((2, rows, rows), np.float32)
    perm_np[0, r, (r % nb) * steps + r // nb] = 1.0
    perm_np[1, r, (r % nb) * steps + (steps - 1 - r // nb)] = 1.0
    perm = jnp.asarray(perm_np, BF16)
    permt = jnp.asarray(perm_np.transpose(0, 2, 1), BF16)

    def chunk(d, c):
        return jnp.where(d == 0, c, nch - 1 - c)

    out_specs = [pl.BlockSpec((None, nb, span, SEG), lambda d, c: (d, 0, chunk(d, c), 0))]
    out_shape = [jax.ShapeDtypeStruct((2, nb, t, SEG), F32)]
    if want_fin:
        out_specs.append(pl.BlockSpec((None, 2, nb, S5_LANES), lambda d, c: (d, 0, 0, 0)))
        out_shape.append(jax.ShapeDtypeStruct((2, 2, nb, S5_LANES), F32))
    return pl.pallas_call(
        functools.partial(_s5_kernel, nb=nb, steps=steps, want_fin=want_fin),
        grid=(2, nch),
        in_specs=[
            pl.BlockSpec((nb, span, SEG), lambda d, c: (0, chunk(d, c), S_U)),
            pl.BlockSpec((None, rows, rows), lambda d, c: (d, 0, 0)),
            pl.BlockSpec((None, rows, rows), lambda d, c: (d, 0, 0)),
            pl.BlockSpec((None, None, SEG, 2 * S5_LANES), lambda d, c: (layer, d, 0, 0)),
            pl.BlockSpec((None, None, 2, S5_LANES), lambda d, c: (layer, d, 0, 0)),
            _pick_spec(w['s5_cblk'], layer),
            pl.BlockSpec((None, None, 2, nb, S5_LANES), lambda d, c: (s0_layer, d, 0, 0, 0)),
        ],
        out_specs=out_specs,
        out_shape=out_shape,
        scratch_shapes=[pltpu.VMEM((2, nb, S5_LANES), F32),
                        pltpu.VMEM((S5_SUBCHUNKS, rows, 2 * S5_LANES), F32)],
        compiler_params=_cparams(("arbitrary", "arbitrary")),
        name="s5_scan",
    )(p3, perm, permt, w['s5_bblk'], w['s5_a'], w['s5_cblk'], s0)


def _ret_kernel(*refs, rope, has_s0, want_fin, nch):
    refs = list(refs)
    q_ref, k_ref, v_ref, g_ref = refs[:4]
    del refs[:4]
    if rope:
        qr_ref, kr_ref, cos_ref, sin_ref = refs[:4]
        del refs[:4]
    (dm_ref, qdf_ref, qdb_ref, kdft_ref, kdbt_ref, cdf_ref, cdb_ref, hm_ref, bd_ref, gm_ref,
     gn_ref) = refs[:11]
    del refs[:11]
    if has_s0:
        s0_ref = refs.pop(0)
    o_ref = refs.pop(0)
    fin_ref = refs.pop(0) if want_fin else None
    sb_all, sf_scr, sb_scr = refs

    p = pl.program_id(1)
    c = pl.program_id(2)
    hd = RET_DIM

    def init_state(dst, d):
        dst[...] = jnp.zeros_like(dst)
        if has_s0:
            for h in range(RET_HEADS):
                dst[h * hd:(h + 1) * hd, h * hd:(h + 1) * hd] = s0_ref[d, h]

    def write_final(src, d):
        for h in range(RET_HEADS):
            fin_ref[d, h] = src[h * hd:(h + 1) * hd, h * hd:(h + 1) * hd]

    def rope_tabs():
        cos = cos_ref[...]
        sin = sin_ref[...]
        return jnp.concatenate([cos, cos], axis=-1), jnp.concatenate([sin, sin], axis=-1)

    def roped_kt():
        k = k_ref[...].astype(F32)
        if rope:
            cos, sin = rope_tabs()
            k = k * cos + kr_ref[...].astype(F32) * sin
        return k.T

    @pl.when(p == 0)
    def _():
        ck = nch - 1 - c

        @pl.when(c == 0)
        def _():
            init_state(sb_scr, 1)

        kt = roped_kt()
        s_old = sb_scr[...]
        sb_all[ck] = s_old
        sb_scr[...] = s_old * cdb_ref[...] + _dot(kt * kdbt_ref[...], v_ref[...]) * bd_ref[...]
        if want_fin:
            @pl.when(c == nch - 1)
            def _():
                write_final(sb_scr, 1)

    @pl.when(p == 1)
    def _():
        @pl.when(c == 0)
        def _():
            init_state(sf_scr, 0)

        q = q_ref[...].astype(F32)
        if rope:
            cos, sin = rope_tabs()
            q = q * cos + qr_ref[...].astype(F32) * sin
        kt = roped_kt()
        ktb = kt.astype(BF16)
        vb = v_ref[...]
        s_f = sf_scr[...]
        o = _dot(q, s_f) * qdf_ref[...] + _dot(q, sb_all[c]) * qdb_ref[...]
        for h in range(RET_HEADS):
            hm = hm_ref[h:h + 1, :]
            att = _dot(q * hm, ktb) * dm_ref[h]
            o = o + _dot(att, vb) * hm
        sf_scr[...] = s_f * cdf_ref[...] + _dot(kt * kdft_ref[...], vb) * bd_ref[...]
        hi = o.astype(BF16)
        lo = (o - hi.astype(F32)).astype(BF16)
        mu = (jnp.dot(hi, gm_ref[...], preferred_element_type=F32)
              + jnp.dot(lo, gm_ref[...], preferred_element_type=F32))
        oc = o - mu
        var = _dot(oc * oc, gm_ref[...])
        on = oc * lax.rsqrt(var + EPS) * gn_ref[...]
        g = g_ref[...].astype(F32)
        o_ref[...] = (g * _sigmoid(g)) * on
        if want_fin:
            @pl.when(c == nch - 1)
            def _():
                write_final(sf_scr, 0)


def _ret_call(p3, w, s0, rope_tabs, *, layer, nb, t, want_fin):
    tc = RET_CHUNK
    nch = t // tc
    rope = rope_tabs is not None
    has_s0 = s0 is not None

    def kv_chunk(p, c):
        return jnp.where(p == 0, nch - 1 - c, c)

    def q_chunk(p, c):
        return jnp.where(p == 0, 0, c)

    def seg_spec(seg, cmap):
        return pl.BlockSpec((None, tc, SEG), lambda b, p, c: (b, cmap(p, c), seg))

    args = [p3, p3, p3, p3]
    in_specs = [seg_spec(S_RQ, q_chunk), seg_spec(S_RK, kv_chunk), seg_spec(S_RV, kv_chunk),
                seg_spec(S_RG, q_chunk)]
    if rope:
        args += [p3, p3, rope_tabs[0], rope_tabs[1]]
        in_specs += [seg_spec(S_RQR, q_chunk), seg_spec(S_RKR, kv_chunk),
                     pl.BlockSpec((tc, 2 * RET_DIM), lambda b, p, c: (kv_chunk(p, c), 0)),
                     pl.BlockSpec((tc, 2 * RET_DIM), lambda b, p, c: (kv_chunk(p, c), 0))]
    args += list(w['ret_layer_tabs']) + list(w['ret_const_tabs']) + [w['ret_gn']]
    in_specs += ([_pick_spec(a, layer) for a in w['ret_layer_tabs']]
                 + [_const_spec(a.shape) for a in w['ret_const_tabs']] + [_pick_spec(w['ret_gn'], layer)])
    if has_s0:
        args.append(s0)
        in_specs.append(pl.BlockSpec((None, None, 2, RET_HEADS, RET_DIM, RET_DIM),
                                     lambda b, p, c: (b, layer, 0, 0, 0, 0)))
    out_specs = [pl.BlockSpec((None, tc, SEG), lambda b, p, c: (b, q_chunk(p, c), 0))]
    out_shape = [jax.ShapeDtypeStruct((nb, t, SEG), F32)]
    if want_fin:
        out_specs.append(pl.BlockSpec((None, 2, RET_HEADS, RET_DIM, RET_DIM),
                                      lambda b, p, c: (b, 0, 0, 0, 0)))
        out_shape.append(jax.ShapeDtypeStruct((nb, 2, RET_HEADS, RET_DIM, RET_DIM), F32))
    return pl.pallas_call(
        functools.partial(_ret_kernel, rope=rope, has_s0=has_s0, want_fin=want_fin, nch=nch),
        grid=(nb, 2, nch),
        in_specs=in_specs,
        out_specs=out_specs,
        out_shape=out_shape,
        scratch_shapes=[pltpu.VMEM((nch, SEG, SEG), F32),
                        pltpu.VMEM((SEG, SEG), F32),
                        pltpu.VMEM((SEG, SEG), F32)],
        compiler_params=_cparams(("arbitrary", "arbitrary", "arbitrary")),
        name="retention",
    )(*args)


def _mla_kernel(*refs, t, tq, nctx, want_cache):
    refs = list(refs)
    cq_ref, ckv_ref, krp_ref = refs[:3]
    del refs[:3]
    has_ctx = nctx > 0
    if has_ctx:
        krr_ref, cm_ref, sm_ref, cache_ref = refs[:4]
        del refs[:4]
    qn_ref, kvn_ref, wq_ref = refs[:3]
    del refs[:3]
    if has_ctx:
        wqr_ref, wkc_ref = refs[:2]
        del refs[:2]
    wk_ref, wv_ref = refs[:2]
    del refs[:2]
    o_ref = refs.pop(0)
    lat_ref = refs.pop(0) if want_cache else None
    k_scr, vt_scr = refs
    qi = pl.program_id(1)
    s_tot = t + nctx

    def store_vt(v, c0, n):
        vt = v.T
        for h in range(MLA_HEADS):
            vt_scr[h, 0:MLA_V, c0:c0 + n] = vt[h * MLA_V:(h + 1) * MLA_V, :].astype(BF16)

    @pl.when(qi == 0)
    def _():
        lat = _rms(ckv_ref[...].astype(F32), kvn_ref[...])
        if want_cache:
            lat_ref[...] = lat
        latb = lat.astype(BF16)
        kr = krp_ref[...].astype(F32)
        if has_ctx:
            kr = kr * cm_ref[...] + krr_ref[...].astype(F32) * sm_ref[...]
        kk = jnp.dot(latb, wk_ref[...], preferred_element_type=F32)
        for h in range(MLA_HEADS):
            k_scr[h, 0:t, :] = (kk[:, h * MLA_SLAB:(h + 1) * MLA_SLAB] + kr).astype(BF16)
        store_vt(jnp.dot(latb, wv_ref[...], preferred_element_type=F32), 0, t)
        if has_ctx:
            cache = cache_ref[...].astype(BF16)
            kkc = jnp.dot(cache, wkc_ref[...], preferred_element_type=F32)
            for h in range(MLA_HEADS):
                k_scr[h, t:t + nctx, :] = kkc[:, h * MLA_SLAB:(h + 1) * MLA_SLAB].astype(BF16)
            store_vt(jnp.dot(cache[:, 0:MLA_KV_LORA], wv_ref[...], preferred_element_type=F32), t, nctx)
        tail = lax.broadcasted_iota(jnp.int32, (MLA_VT_ROWS - MLA_V, s_tot), 0)
        for h in range(MLA_HEADS):
            vt_scr[h, MLA_V:MLA_VT_ROWS, :] = jnp.where(tail == 0, 1.0, 0.0).astype(BF16)

    a = _rms(cq_ref[...].astype(F32), qn_ref[...], n=MLA_Q_LORA).astype(BF16)
    q = jnp.dot(a, wq_ref[...], preferred_element_type=F32)
    if has_ctx:
        qrot = jnp.dot(a, wqr_ref[...], preferred_element_type=F32)
        r0 = pl.multiple_of(qi * tq, tq)
        cm = cm_ref[pl.ds(r0, tq), :]
        sm = sm_ref[pl.ds(r0, tq), :]
    scale = (MLA_NOPE + MLA_ROPE) ** -0.5
    def scores_t(h):
        qh = q[:, h * MLA_SLAB:(h + 1) * MLA_SLAB]
        if has_ctx:
            qh = qh * cm + qrot[:, h * MLA_SLAB:(h + 1) * MLA_SLAB] * sm
        qt = (qh * (scale * LOG2_E)).T.astype(BF16)
        return jnp.dot(k_scr[h], qt, preferred_element_type=F32)

    outs = []
    st_next = scores_t(0)
    for h in range(MLA_HEADS):
        st = st_next
        if h + 1 < MLA_HEADS:
            st_next = scores_t(h + 1)
        m = jnp.max(st, axis=0, keepdims=True)
        e = jnp.exp2(st - m).astype(BF16)
        ot = jnp.dot(vt_scr[h], e, preferred_element_type=F32)
        outs.append(ot[0:MLA_V, :] * (1.0 / ot[MLA_V:MLA_V + 1, :]))
    o_ref[...] = jnp.concatenate(outs, axis=0).T


def _mla_call(p3, w, rope_tabs, cache, *, layer, nb, t, tq, want_cache):
    has_ctx = cache is not None
    nctx = cache.shape[2] if has_ctx else 0
    s_tot = t + nctx
    half = MLA_KV_LORA

    args = [p3, p3, p3]
    in_specs = [
        pl.BlockSpec((None, tq, SEG), lambda b, i: (b, i, S_CQ)),
        pl.BlockSpec((None, t, half), lambda b, i: (b, 0, 2 * S_KV)),
        pl.BlockSpec((None, t, half), lambda b, i: (b, 0, 2 * S_KV + 1)),
    ]
    if has_ctx:
        args += [p3, rope_tabs[0], rope_tabs[1], cache]
        in_specs += [
            pl.BlockSpec((None, t, half), lambda b, i: (b, 0, 2 * S_KRR)),
            _const_spec((t, MLA_SLAB)),
            _const_spec((t, MLA_SLAB)),
            pl.BlockSpec((None, None, nctx, MLA_KV_LORA + MLA_ROPE), lambda b, i: (b, layer, 0, 0)),
        ]
    args += [w['mla_qn'], w['mla_kvn'], w['mla_wq']]
    in_specs += [_pick_spec(w['mla_qn'], layer), _pick_spec(w['mla_kvn'], layer), _pick_spec(w['mla_wq'], layer)]
    if has_ctx:
        args += [w['mla_wqr'], w['mla_wkc']]
        in_specs += [_pick_spec(w['mla_wqr'], layer), _pick_spec(w['mla_wkc'], layer)]
    args += [w['mla_wk'], w['mla_wv']]
    in_specs += [_pick_spec(w['mla_wk'], layer), _pick_spec(w['mla_wv'], layer)]
    out_specs = [pl.BlockSpec((None, tq, SEG), lambda b, i: (b, i, 0))]
    out_shape = [jax.ShapeDtypeStruct((nb, t, SEG), F32)]
    if want_cache:
        out_specs.append(pl.BlockSpec((None, t, MLA_KV_LORA), lambda b, i: (b, 0, 0)))
        out_shape.append(jax.ShapeDtypeStruct((nb, t, MLA_KV_LORA), F32))
    return pl.pallas_call(
        functools.partial(_mla_kernel, t=t, tq=tq, nctx=nctx, want_cache=want_cache),
        grid=(nb, t // tq),
        in_specs=in_specs,
        out_specs=out_specs,
        out_shape=out_shape,
        scratch_shapes=[pltpu.VMEM((MLA_HEADS, s_tot, MLA_SLAB), BF16),
                        pltpu.VMEM((MLA_HEADS, MLA_VT_ROWS, s_tot), BF16)],
        compiler_params=_cparams(("arbitrary", "arbitrary")),
        name="mla",
    )(*args)


def _merge_kernel(x_ref, mod_ref, npre_ref, npost_ref, yf_ref, yb_ref, u_ref, ret_ref,
                  cx_ref, cb_ref, cc_ref, cxp_ref, ccp_ref, cxn_ref, ccn_ref, mla_ref,
                  s5d_ref, wglu_ref, cw_ref, cbias_ref, wbr_ref, wg_ref, bg_ref, wo_ref,
                  o_ref, h_scr, *, tm, nt):
    i = pl.program_id(1)
    x = x_ref[...]
    sh = mod_ref[3:4, :]
    sc = mod_ref[4:5, :]
    gt = mod_ref[5:6, :]
    h_scr[...] = (_rms(x, npre_ref[...]) * (1.0 + sc) + sh).astype(BF16)

    y = yf_ref[...] + yb_ref[...] + s5d_ref[...] * u_ref[...].astype(F32)
    z5 = _gelu_tanh(y)
    b_s5 = z5 * _sigmoid(_dot(z5, wglu_ref[...]))

    z = cc_ref[...].astype(F32) * cx_ref[...].astype(F32)
    zprev = ((ccp_ref[...].astype(F32) * cxp_ref[...].astype(F32))[HALO - 1:HALO, :]
             * jnp.where(i > 0, 1.0, 0.0))
    znext = ((ccn_ref[...].astype(F32) * cxn_ref[...].astype(F32))[0:1, :]
             * jnp.where(i < nt - 1, 1.0, 0.0))
    row = lax.broadcasted_iota(jnp.int32, (tm, SEG), 0)
    z_dn = jnp.where(row == 0, zprev, pltpu.roll(z, 1, axis=0))
    z_up = jnp.where(row == tm - 1, znext, pltpu.roll(z, tm - 1, axis=0))
    b_conv = cb_ref[...].astype(F32) * (cbias_ref[...] + z_dn * cw_ref[0:1, :] + z * cw_ref[1:2, :]
                                        + z_up * cw_ref[2:3, :])

    branches = (b_s5, ret_ref[...], b_conv, mla_ref[...])
    merged = jnp.zeros((tm, D_MODEL), F32)
    for n in range(N_BRANCH):
        gate = _sigmoid(jnp.dot(h_scr[...], wg_ref[:, n * D_MODEL:(n + 1) * D_MODEL],
                                preferred_element_type=F32) + bg_ref[:, n * D_MODEL:(n + 1) * D_MODEL])
        merged = merged + gate * _dot(branches[n], wbr_ref[n])
    m = _dot(merged, wo_ref[...])
    o_ref[...] = x + gt * _rms(m, npost_ref[...])


def _merge_call(x, w, y_s5, ret_o, p3, mla_o, *, layer, nb, t, tm, row_of_batch):
    nt = t // tm
    hb = tm // HALO
    n_halo = t // HALO

    def dir_spec(d):
        return pl.BlockSpec((None, None, tm, SEG), lambda b, i: (d, b, i, 0))

    def seg_spec(seg):
        return pl.BlockSpec((None, tm, SEG), lambda b, i: (b, i, seg))

    def prev_spec(seg):
        return pl.BlockSpec((None, HALO, SEG), lambda b, i: (b, jnp.maximum(i * hb - 1, 0), seg))

    def next_spec(seg):
        return pl.BlockSpec((None, HALO, SEG), lambda b, i: (b, jnp.minimum((i + 1) * hb, n_halo - 1), seg))

    return pl.pallas_call(
        functools.partial(_merge_kernel, tm=tm, nt=nt),
        grid=(nb, nt),
        in_specs=[
            pl.BlockSpec((tm, D_MODEL), lambda b, i: (b * nt + i, 0)),
            pl.BlockSpec((None, None, N_MOD, D_MODEL), lambda b, i: (layer, row_of_batch(b), 0, 0)),
            _pick_spec(w['npre'], layer * N_NORM + 1),
            _pick_spec(w['npost'], layer * N_NORM + 1),
            dir_spec(0), dir_spec(1), seg_spec(S_U),
            pl.BlockSpec((None, tm, SEG), lambda b, i: (b, i, 0)),
            seg_spec(S_CX), seg_spec(S_CB), seg_spec(S_CC),
            prev_spec(S_CX), prev_spec(S_CC), next_spec(S_CX), next_spec(S_CC),
            pl.BlockSpec((None, tm, SEG), lambda b, i: (b, i, 0)),
            _pick_spec(w['s5_d'], layer), _pick_spec(w['s5_wglu'], layer), _pick_spec(w['conv_w'], layer),
            _pick_spec(w['conv_b'], layer), _resident_spec(w['w_branch'].shape[1:], (layer,)),
            _resident_spec(w['w_gate'].shape[1:], (layer,)), _pick_spec(w['b_gate'], layer),
            _resident_spec(w['w_o'].shape[1:], (layer,)),
        ],
        out_specs=pl.BlockSpec((tm, D_MODEL), lambda b, i: (b * nt + i, 0)),
        out_shape=jax.ShapeDtypeStruct((nb * t, D_MODEL), F32),
        scratch_shapes=[pltpu.VMEM((tm, D_MODEL), BF16)],
        compiler_params=_cparams(("arbitrary", "arbitrary")),
        name="merge",
    )(x, w['mod'], w['npre'], w['npost'], y_s5, y_s5, p3, ret_o, p3, p3, p3, p3, p3, p3, p3, mla_o,
      w['s5_d'], w['s5_wglu'], w['conv_w'], w['conv_b'], w['w_branch'], w['w_gate'], w['b_gate'], w['w_o'])


def _zeros_like_cols(w, n):
    return jnp.zeros(w.shape[:-1] + (n,), w.dtype)


def _rot_half(w, heads, dim):
    w4 = w.reshape(w.shape[:-1] + (heads, dim))
    half = dim // 2
    return jnp.concatenate([-w4[..., half:], w4[..., :half]], axis=-1).reshape(w.shape)


def _axial_angles(t, dim):
    rows = t // GRID_W
    row = np.repeat(np.arange(rows, dtype=np.float64), GRID_W)
    col = np.tile(np.arange(GRID_W, dtype=np.float64), rows)
    quarter = dim // 4
    inv = ROPE_BASE ** (-np.arange(quarter, dtype=np.float64) / quarter)
    ang = np.concatenate([row[:, None] * inv, col[:, None] * inv], axis=-1)
    return np.cos(ang).astype(np.float32), np.sin(ang).astype(np.float32)


def _rope_tables(t):
    cos, sin = _axial_angles(t, RET_DIM)
    ret_cos = np.tile(np.concatenate([cos, cos], -1), (1, 2))
    ret_sin = np.tile(np.concatenate([sin, sin], -1), (1, 2))
    cos, sin = _axial_angles(t, MLA_ROPE)
    pad = MLA_SLAB - MLA_NOPE - MLA_ROPE
    mla_cos = np.concatenate([np.ones((t, MLA_NOPE), np.float32), cos, cos, np.zeros((t, pad), np.float32)], -1)
    mla_sin = np.concatenate([np.zeros((t, MLA_NOPE), np.float32), sin, sin, np.zeros((t, pad), np.float32)], -1)
    return ((jnp.asarray(ret_cos), jnp.asarray(ret_sin)), (jnp.asarray(mla_cos), jnp.asarray(mla_sin)))


def _place_rope_key(w):
    return jnp.concatenate([_zeros_like_cols(w, MLA_NOPE), w,
                            _zeros_like_cols(w, MLA_SLAB - MLA_NOPE - MLA_ROPE)], -1)


def _in_weights(w_in):
    sizes = (BRANCH_WIDTH,) * 8 + (MLA_Q_LORA, MLA_KV_LORA, MLA_ROPE)
    offs = np.cumsum((0,) + sizes)
    u, rq, rk, rv, rg, cx, cb, cc, cq, ckv, kr = [w_in[..., offs[i]:offs[i + 1]] for i in range(11)]
    rk = rk * (RET_DIM ** -0.5)
    cols = [u, rq, rk, rv, rg, cx, cb, cc,
            jnp.concatenate([cq, _zeros_like_cols(cq, SEG - MLA_Q_LORA)], -1),
            jnp.concatenate([ckv, _place_rope_key(kr)], -1),
            _rot_half(rq, RET_HEADS, RET_DIM), _rot_half(rk, RET_HEADS, RET_DIM),
            jnp.concatenate([_place_rope_key(_rot_half(kr, 1, MLA_ROPE)), _zeros_like_cols(kr, SEG - MLA_SLAB)], -1)]
    return jnp.concatenate([c.astype(BF16) for c in cols], axis=-1)


def _mla_weights(w_uq, w_ukv):
    nl = w_uq.shape[0]
    dq = MLA_NOPE + MLA_ROPE
    w4 = w_uq.reshape(nl, MLA_Q_LORA, MLA_HEADS, dq)
    nope, ropew = w4[..., :MLA_NOPE], w4[..., MLA_NOPE:]
    half = MLA_ROPE // 2
    rot = jnp.concatenate([-ropew[..., half:], ropew[..., :half]], -1)
    zpad = jnp.zeros((nl, MLA_Q_LORA, MLA_HEADS, MLA_SLAB - dq), F32)
    wq = jnp.concatenate([nope, ropew, zpad], -1).reshape(nl, MLA_Q_LORA, MLA_HEADS * MLA_SLAB)
    wqr = jnp.concatenate([jnp.zeros_like(nope), rot, zpad], -1).reshape(nl, MLA_Q_LORA, MLA_HEADS * MLA_SLAB)
    rpad = jnp.zeros((nl, SEG - MLA_Q_LORA, MLA_HEADS * MLA_SLAB), F32)
    wq = jnp.concatenate([wq, rpad], 1).astype(BF16)
    wqr = jnp.concatenate([wqr, rpad], 1).astype(BF16)
    kv4 = w_ukv.reshape(nl, MLA_KV_LORA, MLA_HEADS, MLA_NOPE + MLA_V)
    wk = jnp.concatenate([kv4[..., :MLA_NOPE], jnp.zeros((nl, MLA_KV_LORA, MLA_HEADS, MLA_SLAB - MLA_NOPE), F32)],
                         -1).reshape(nl, MLA_KV_LORA, MLA_HEADS * MLA_SLAB)
    place = np.zeros((MLA_ROPE, MLA_HEADS, MLA_SLAB), np.float32)
    place[np.arange(MLA_ROPE), :, MLA_NOPE + np.arange(MLA_ROPE)] = 1.0
    place = jnp.broadcast_to(jnp.asarray(place.reshape(MLA_ROPE, MLA_HEADS * MLA_SLAB)),
                             (nl, MLA_ROPE, MLA_HEADS * MLA_SLAB))
    wkc = jnp.concatenate([wk, place], 1).astype(BF16)
    wv = kv4[..., MLA_NOPE:].reshape(nl, MLA_KV_LORA, MLA_HEADS * MLA_V).astype(BF16)
    return wq, wqr, wk.astype(BF16), wkc, wv


def _s5_tables(lam_re, lam_im, log_dt, b_re, b_im, c_re, c_im):
    nl = lam_re.shape[0]
    dt = jnp.exp(log_dt)[..., None]
    mag = jnp.exp(lam_re * dt)
    ab_re = mag * jnp.cos(lam_im * dt)
    ab_im = mag * jnp.sin(lam_im * dt)
    den = lam_re * lam_re + lam_im * lam_im
    f_re = ((ab_re - 1.0) * lam_re + ab_im * lam_im) / den
    f_im = (ab_im * lam_re - (ab_re - 1.0) * lam_im) / den
    bb_re = f_re[..., None] * b_re - f_im[..., None] * b_im
    bb_im = f_re[..., None] * b_im + f_im[..., None] * b_re
    eye = jnp.eye(S5_GROUPS, dtype=F32)
    blk_in = lambda m: jnp.einsum('ldgph,gk->ldghkp', m, eye).reshape(nl, 2, BRANCH_WIDTH, S5_LANES)
    bblk = jnp.concatenate([blk_in(bb_re), blk_in(bb_im)], -1).astype(BF16)
    blk_out = lambda m: jnp.einsum('lghp,gk->lgpkh', m, eye).reshape(nl, S5_LANES, BRANCH_WIDTH)
    cblk = jnp.concatenate([blk_out(c_re), -blk_out(c_im)], 1).astype(BF16)
    a_tab = jnp.stack([ab_re.reshape(nl, 2, S5_LANES), ab_im.reshape(nl, 2, S5_LANES)], axis=2)
    return bblk, a_tab, cblk


def _ret_tables(decay_logit, c):
    lg = jax.nn.log_sigmoid(decay_logit.astype(F32))
    lf = lg[:, 0][:, :, None, None]
    lb = lg[:, 1][:, :, None, None]
    idx = jnp.arange(c, dtype=F32)
    diff = idx[:, None] - idx[None, :]
    dm = jnp.where(diff >= 0, jnp.exp(lf * jnp.maximum(diff, 0.0)), jnp.exp(lb * jnp.maximum(-diff, 0.0)))
    lfl = jnp.repeat(lg[:, 0], RET_DIM, axis=-1)
    lbl = jnp.repeat(lg[:, 1], RET_DIM, axis=-1)
    qdf = jnp.exp(lfl[:, None, :] * (idx[:, None] + 1.0))
    qdb = jnp.exp(lbl[:, None, :] * (c - idx[:, None]))
    kdft = jnp.exp(lfl[:, :, None] * (c - 1.0 - idx[None, :]))
    kdbt = jnp.exp(lbl[:, :, None] * idx[None, :])
    nl = lg.shape[0]
    cdf = jnp.broadcast_to(jnp.exp(lfl * c)[:, :, None], (nl, SEG, SEG))
    cdb = jnp.broadcast_to(jnp.exp(lbl * c)[:, :, None], (nl, SEG, SEG))
    head = np.arange(SEG) // RET_DIM
    hm = jnp.asarray((head[None, :] == np.arange(RET_HEADS)[:, None]).astype(np.float32))
    bd_np = (head[:, None] == head[None, :]).astype(np.float32)
    bd = jnp.asarray(bd_np)
    gm = jnp.asarray(bd_np / RET_DIM, BF16)
    return (dm, qdf, qdb, kdft, kdbt, cdf, cdb), (hm, bd, gm)


def _layer(x, w, *, layer, nb, t, tm, tm_ffn, row_of_batch, ctx, rope_tabs):
    is_ctx_pass = ctx is None
    row_of_tile = lambda i: row_of_batch((i * tm_ffn) // t)
    x = _ffn_call(x, w, layer=layer, k=0, tm=tm_ffn, row_of_tile=row_of_tile)

    nseg = NSEG_PLAIN if is_ctx_pass else NSEG_ROPE
    p = _in_call(x, w, layer=layer, nb=nb, t=t, tm=tm, nseg=nseg, row_of_batch=row_of_batch)
    p3 = p.reshape(nb, t, nseg * SEG)

    if is_ctx_pass:
        s5_0, s0_layer = jnp.zeros((1, 2, 2, nb, S5_LANES), F32), 0
        ret_0, mla_cache, ret_rope, mla_rope = None, None, None, None
    else:
        s5_0, ret_0, mla_cache = ctx
        s0_layer = layer
        ret_rope, mla_rope = rope_tabs
    s5_out = _s5_call(p3, w, s5_0, layer=layer, s0_layer=s0_layer, nb=nb, t=t, want_fin=is_ctx_pass)
    ret_out = _ret_call(p3, w, ret_0, ret_rope, layer=layer, nb=nb, t=t, want_fin=is_ctx_pass)
    mla_out = _mla_call(p3, w, mla_rope, mla_cache, layer=layer, nb=nb, t=t, tq=min(t, 512),
                        want_cache=is_ctx_pass)

    x = _merge_call(x, w, s5_out[0], ret_out[0], p3, mla_out[0], layer=layer, nb=nb, t=t, tm=tm,
                    row_of_batch=row_of_batch)
    x = _ffn_call(x, w, layer=layer, k=2, tm=tm_ffn, row_of_tile=row_of_tile)

    state = None
    if is_ctx_pass:
        kr = p3[:, :, S_KV * SEG + MLA_KV_LORA + MLA_NOPE:S_KV * SEG + MLA_KV_LORA + MLA_NOPE + MLA_ROPE]
        cache = jnp.concatenate([mla_out[1], kr.astype(F32)], axis=-1)
        state = (s5_out[1], ret_out[1], cache)
    return x, state


def kernel(x_prompt, x_sample, state_s5, state_ret, cache_mla, c, c_ctx, w_mod, b_mod, norm_pre, norm_post, ffn_w1, ffn_w3, ffn_w2, w_in, s5_lam_re, s5_lam_im, s5_log_dt, s5_b_re, s5_b_im, s5_c_re, s5_c_im, s5_d, s5_w_glu, ret_decay, ret_gn, conv_w, conv_b, mla_q_norm, mla_w_uq, mla_kv_norm, mla_w_ukv, w_branch, w_gate, b_gate, w_o):
    bp, tp, _ = x_prompt.shape
    bs, ts, _ = x_sample.shape
    assert 1 + bs <= MOD_ROWS

    cond = jnp.concatenate([c_ctx[None, :], c, jnp.zeros((MOD_ROWS - 1 - bs, D_MODEL), F32)], axis=0)
    mod = _mod_call(cond, w_mod, b_mod).reshape(DEPTH, MOD_ROWS, N_MOD, D_MODEL)

    wq, wqr, wk, wkc, wv = _mla_weights(mla_w_uq, mla_w_ukv)
    bblk, a_tab, cblk = _s5_tables(s5_lam_re, s5_lam_im, s5_log_dt, s5_b_re, s5_b_im, s5_c_re, s5_c_im)
    ret_layer_tabs, ret_const_tabs = _ret_tables(ret_decay, RET_CHUNK)
    w = dict(
        mod=mod,
        npre=norm_pre.reshape(DEPTH * N_NORM, 1, D_MODEL),
        npost=norm_post.reshape(DEPTH * N_NORM, 1, D_MODEL),
        ffn=(ffn_w1.astype(BF16), ffn_w3.astype(BF16), ffn_w2.astype(BF16)),
        w_in=_in_weights(w_in),
        s5_bblk=bblk, s5_a=a_tab, s5_cblk=cblk,
        s5_d=s5_d[:, None, :], s5_wglu=s5_w_glu.astype(BF16),
        ret_layer_tabs=ret_layer_tabs, ret_const_tabs=ret_const_tabs, ret_gn=ret_gn[:, None, :],
        conv_w=conv_w, conv_b=conv_b[:, None, :],
        mla_qn=jnp.concatenate([mla_q_norm, jnp.zeros((DEPTH, SEG - MLA_Q_LORA), F32)], -1)[:, None, :],
        mla_kvn=mla_kv_norm[:, None, :],
        mla_wq=wq, mla_wqr=wqr, mla_wk=wk, mla_wkc=wkc, mla_wv=wv,
        w_branch=w_branch.astype(BF16), w_gate=w_gate.astype(BF16), b_gate=b_gate[:, None, :],
        w_o=w_o.astype(BF16),
    )
    rope_tabs = _rope_tables(ts)
    s5_0 = state_s5.transpose(1, 2, 5, 0, 3, 4).reshape(DEPTH, 2, 2, bs, S5_LANES)

    xp = x_prompt.reshape(bp * tp, D_MODEL)
    xs = x_sample.reshape(bs * ts, D_MODEL)
    s5_list, ret_list, mla_list = [], [], []
    for l in range(DEPTH):
        xp, (s5_s, ret_s, mla_c) = _layer(xp, w, layer=l, nb=bp, t=tp, tm=256, tm_ffn=512,
                                          row_of_batch=lambda b: 0, ctx=None, rope_tabs=None)
        s5_list.append(s5_s)
        ret_list.append(ret_s)
        mla_list.append(mla_c)
        xs, _ = _layer(xs, w, layer=l, nb=bs, t=ts, tm=512, tm_ffn=1024, row_of_batch=lambda b: 1 + b,
                       ctx=(s5_0, state_ret, cache_mla), rope_tabs=rope_tabs)
    new_s5 = jnp.stack(s5_list, 0).reshape(DEPTH, 2, 2, bp, S5_GROUPS, S5_STATE).transpose(3, 0, 1, 4, 5, 2)
    return (xp.reshape(bp, tp, D_MODEL), xs.reshape(bs, ts, D_MODEL),
            new_s5, jnp.stack(ret_list, axis=1), jnp.stack(mla_list, axis=1))
```

```python
import functools

import numpy as np
import jax
import jax.numpy as jnp
from jax import lax
from jax.experimental import pallas as pl
from jax.experimental.pallas import tpu as pltpu

F32 = jnp.float32
BF16 = jnp.bfloat16

D_MODEL = 1024
DEPTH = 2
GRID_W = 64
EPS = 1e-6
LOG2_E = float(np.log2(np.e))
ROPE_BASE = 10000.0
N_BRANCH = 4
BRANCH_WIDTH = D_MODEL // 4
N_MOD = 9
N_NORM = 3
D_FF = 2816
S5_GROUP = 16
S5_GROUPS = BRANCH_WIDTH // S5_GROUP
S5_STATE = 64
S5_LANES = S5_GROUPS * S5_STATE
RET_HEADS = 4
RET_DIM = BRANCH_WIDTH // RET_HEADS
CONV_K = 3
MLA_HEADS = 4
MLA_Q_LORA = 192
MLA_KV_LORA = 128
MLA_NOPE = 64
MLA_ROPE = 32
MLA_V = BRANCH_WIDTH // MLA_HEADS
MLA_SLAB = 128
MLA_VT_ROWS = MLA_V + 16

SUBLANES = 8
S5_SUBCHUNKS = 4
S5_ROWS = 512
SEG = 256
FF_CHUNK = 256
N_FF_CHUNKS = D_FF // FF_CHUNK
MOD_ROWS = 16
HALO = 16
RET_CHUNK = 256
V7X_VMEM_LIMIT = 56 * 1024 * 1024

S_U, S_RQ, S_RK, S_RV, S_RG, S_CX, S_CB, S_CC, S_CQ, S_KV = range(10)
NSEG = 10
LANES = 128


def _cparams(sem):
    return pltpu.CompilerParams(dimension_semantics=sem, vmem_limit_bytes=V7X_VMEM_LIMIT)


def _dot(a, b):
    return jnp.dot(a.astype(BF16), b.astype(BF16), preferred_element_type=F32)


def _sigmoid(x):
    return 1.0 / (1.0 + jnp.exp(-x))


def _rms(x, g, n=None):
    n = x.shape[-1] if n is None else n
    ms = jnp.sum(x * x, axis=-1, keepdims=True) * (1.0 / n)
    return x * lax.rsqrt(ms + EPS) * g


def _rot_half_lanes(x, width):
    half = width // 2
    lane = lax.broadcasted_iota(jnp.int32, (x.shape[0], LANES), 1)
    first = (lane & (width - 1)) < half
    outs = []
    for j in range(x.shape[1] // LANES):
        v = x[:, j * LANES:(j + 1) * LANES]
        outs.append(jnp.where(first, -pltpu.roll(v, LANES - half, axis=1), pltpu.roll(v, half, axis=1)))
    return outs[0] if len(outs) == 1 else jnp.concatenate(outs, axis=-1)


def _gelu_tanh(x):
    return 0.5 * x * (1.0 + jnp.tanh(np.sqrt(2.0 / np.pi) * (x + 0.044715 * (x * x * x))))


def _const_spec(shape):
    nd = len(shape)
    return pl.BlockSpec(shape, lambda *_: (0,) * nd)


def _pick_spec(arr, *lead):
    shape = arr.shape[len(lead):]
    return pl.BlockSpec((None,) * len(lead) + tuple(shape), lambda *_: tuple(lead) + (0,) * len(shape))


def _resident_spec(shape, lead=()):
    nd = len(shape)
    return pl.BlockSpec((None,) * len(lead) + tuple(shape), lambda *_: tuple(lead) + (0,) * nd,
                        pipeline_mode=pl.Buffered(1))


def _mod_kernel(c_ref, w_ref, b_ref, o_ref):
    c = c_ref[...]
    o_ref[...] = _dot(c * _sigmoid(c), w_ref[...]) + b_ref[...]


def _mod_call(cond, w_mod, b_mod):
    ncol = N_MOD * D_MODEL
    tn = D_MODEL
    return pl.pallas_call(
        _mod_kernel,
        grid=(DEPTH, ncol // tn),
        in_specs=[
            pl.BlockSpec((MOD_ROWS, D_MODEL), lambda l, j: (0, 0)),
            pl.BlockSpec((None, D_MODEL, tn), lambda l, j: (l, 0, j)),
            pl.BlockSpec((None, 1, tn), lambda l, j: (l, 0, j)),
        ],
        out_specs=pl.BlockSpec((None, MOD_ROWS, tn), lambda l, j: (l, 0, j)),
        out_shape=jax.ShapeDtypeStruct((DEPTH, MOD_ROWS, ncol), F32),
        compiler_params=_cparams(("arbitrary", "arbitrary")),
        name="mod",
    )(cond, w_mod, b_mod.reshape(DEPTH, 1, ncol))


def _ffn_kernel(x_ref, mod_ref, npre_ref, npost_ref, w1_ref, w3_ref, w2_ref, o_ref,
                h_scr, g_scr, *, k):
    x = x_ref[...]
    sh = mod_ref[3 * k:3 * k + 1, :]
    sc = mod_ref[3 * k + 1:3 * k + 2, :]
    gt = mod_ref[3 * k + 2:3 * k + 3, :]
    h_scr[...] = (_rms(x, npre_ref[...]) * (1.0 + sc) + sh).astype(BF16)
    for j in range(N_FF_CHUNKS):
        cols = slice(j * FF_CHUNK, (j + 1) * FF_CHUNK)
        a = jnp.dot(h_scr[...], w1_ref[:, cols], preferred_element_type=F32)
        b = jnp.dot(h_scr[...], w3_ref[:, cols], preferred_element_type=F32)
        g_scr[:, cols] = (a * _sigmoid(a) * b).astype(BF16)
    f = jnp.dot(g_scr[...], w2_ref[...], preferred_element_type=F32)
    o_ref[...] = x + (0.5 * gt) * _rms(f, npost_ref[...])


def _ffn_call(x, w, *, layer, k, tm, row_of_tile):
    n = x.shape[0]
    lj = (layer, k // 2)
    w1, w3, w2 = w['ffn']
    return pl.pallas_call(
        functools.partial(_ffn_kernel, k=k),
        grid=(n // tm,),
        in_specs=[
            pl.BlockSpec((tm, D_MODEL), lambda i: (i, 0)),
            pl.BlockSpec((None, None, N_MOD, D_MODEL), lambda i: (layer, row_of_tile(i), 0, 0)),
            _pick_spec(w['npre'], layer * N_NORM + k),
            _pick_spec(w['npost'], layer * N_NORM + k),
            _resident_spec((D_MODEL, D_FF), lj),
            _resident_spec((D_MODEL, D_FF), lj),
            _resident_spec((D_FF, D_MODEL), lj),
        ],
        out_specs=pl.BlockSpec((tm, D_MODEL), lambda i: (i, 0)),
        out_shape=jax.ShapeDtypeStruct((n, D_MODEL), F32),
        scratch_shapes=[pltpu.VMEM((tm, D_MODEL), BF16), pltpu.VMEM((tm, D_FF), BF16)],
        compiler_params=_cparams(("arbitrary",)),
        name="ffn",
    )(x, w['mod'], w['npre'], w['npost'], w1, w3, w2)


def _in_kernel(x_ref, mod_ref, npre_ref, w_ref, p_ref, h_scr):
    sh = mod_ref[3:4, :]
    sc = mod_ref[4:5, :]
    h_scr[...] = (_rms(x_ref[...], npre_ref[...]) * (1.0 + sc) + sh).astype(BF16)
    for j in range(NSEG):
        cols = slice(j * SEG, (j + 1) * SEG)
        p_ref[:, cols] = jnp.dot(h_scr[...], w_ref[:, cols], preferred_element_type=F32).astype(BF16)


def _in_call(x, w, *, layer, nb, t, tm, row_of_batch):
    tt = t // tm
    nseg = NSEG
    return pl.pallas_call(
        _in_kernel,
        grid=(nb, tt),
        in_specs=[
            pl.BlockSpec((tm, D_MODEL), lambda b, i: (b * tt + i, 0)),
            pl.BlockSpec((None, None, N_MOD, D_MODEL), lambda b, i: (layer, row_of_batch(b), 0, 0)),
            _pick_spec(w['npre'], layer * N_NORM + 1),
            _resident_spec((D_MODEL, nseg * SEG), (layer,)),
        ],
        out_specs=pl.BlockSpec((tm, nseg * SEG), lambda b, i: (b * tt + i, 0)),
        out_shape=jax.ShapeDtypeStruct((nb * t, nseg * SEG), BF16),
        scratch_shapes=[pltpu.VMEM((tm, D_MODEL), BF16)],
        compiler_params=_cparams(("arbitrary", "arbitrary")),
        name="in_proj",
    )(x, w['mod'], w['npre'], w['w_in'])


def _s5_kernel(*refs, nb, steps, want_fin):
    if want_fin:
        u_ref, perm_ref, permt_ref, bb_ref, a_ref, cc_ref, s0_ref, y_ref, fin_ref, s_scr, bu_scr = refs
    else:
        u_ref, perm_ref, permt_ref, bb_ref, a_ref, cc_ref, s0_ref, y_ref, s_scr, bu_scr = refs
        fin_ref = None
    d = pl.program_id(0)
    c = pl.program_id(1)
    rows = nb * steps
    ngrp = nb // SUBLANES

    @pl.when(c == 0)
    def _():
        s_scr[...] = s0_ref[...]

    a_re = jnp.broadcast_to(a_ref[0:1, :], (SUBLANES, S5_LANES))
    a_im = jnp.broadcast_to(a_ref[1:2, :], (SUBLANES, S5_LANES))
    s_re = [s_scr[0, g * SUBLANES:(g + 1) * SUBLANES, :] for g in range(ngrp)]
    s_im = [s_scr[1, g * SUBLANES:(g + 1) * SUBLANES, :] for g in range(ngrp)]
    re_cols = slice(0, S5_LANES)
    im_cols = slice(S5_LANES, 2 * S5_LANES)
    def sub_offset(k):
        sub = jnp.where(d == 0, k, S5_SUBCHUNKS - 1 - k)
        return pl.multiple_of(sub * steps, steps)

    def project_in(k):
        u_bt = u_ref[:, pl.ds(sub_offset(k), steps), :].reshape(rows, SEG)
        u_tb = jnp.dot(perm_ref[...], u_bt, preferred_element_type=F32).astype(BF16)
        bu_scr[k] = jnp.dot(u_tb, bb_ref[...], preferred_element_type=F32)

    project_in(0)
    for k in range(S5_SUBCHUNKS):
        off = sub_offset(k)
        if k + 1 < S5_SUBCHUNKS:
            project_in(k + 1)
        for i in range(steps):
            for g in range(ngrp):
                rr = slice(i * nb + g * SUBLANES, i * nb + (g + 1) * SUBLANES)
                n_re = a_re * s_re[g] - a_im * s_im[g] + bu_scr[k, rr, re_cols]
                n_im = a_re * s_im[g] + a_im * s_re[g] + bu_scr[k, rr, im_cols]
                bu_scr[k, rr, re_cols] = n_re
                bu_scr[k, rr, im_cols] = n_im
                s_re[g], s_im[g] = n_re, n_im
        y_tb = _dot(bu_scr[k], cc_ref[...])
        hi = y_tb.astype(BF16)
        lo = (y_tb - hi.astype(F32)).astype(BF16)
        y_bt = (jnp.dot(permt_ref[...], hi, preferred_element_type=F32)
                + jnp.dot(permt_ref[...], lo, preferred_element_type=F32))
        y_ref[:, pl.ds(off, steps), :] = y_bt.reshape(nb, steps, SEG)
    for g in range(ngrp):
        s_scr[0, g * SUBLANES:(g + 1) * SUBLANES, :] = s_re[g]
        s_scr[1, g * SUBLANES:(g + 1) * SUBLANES, :] = s_im[g]
    if want_fin:
        fin_ref[...] = s_scr[...]


def _s5_call(p3, w, s0, *, layer, s0_layer, nb, t, want_fin):
    steps = S5_ROWS // nb
    span = steps * S5_SUBCHUNKS
    nch = t // span
    rows = S5_ROWS
    r = np.arange(rows)
    perm_np = np.zeros((2, rows, rows), np.float32)
    perm_np[0, r, (r % nb) * steps + r // nb] = 1.0
    perm_np[1, r, (r % nb) * steps + (steps - 1 - r // nb)] = 1.0
    perm = jnp.asarray(perm_np, BF16)
    permt = jnp.asarray(perm_np.transpose(0, 2, 1), BF16)

    def chunk(d, c):
        return jnp.where(d == 0, c, nch - 1 - c)

    out_specs = [pl.BlockSpec((None, nb, span, SEG), lambda d, c: (d, 0, chunk(d, c), 0))]
    out_shape = [jax.ShapeDtypeStruct((2, nb, t, SEG), F32)]
    if want_fin:
        out_specs.append(pl.BlockSpec((None, 2, nb, S5_LANES), lambda d, c: (d, 0, 0, 0)))
        out_shape.append(jax.ShapeDtypeStruct((2, 2, nb, S5_LANES), F32))
    return pl.pallas_call(
        functools.partial(_s5_kernel, nb=nb, steps=steps, want_fin=want_fin),
        grid=(2, nch),
        in_specs=[
            pl.BlockSpec((nb, span, SEG), lambda d, c: (0, chunk(d, c), S_U)),
            pl.BlockSpec((None, rows, rows), lambda d, c: (d, 0, 0)),
            pl.BlockSpec((None, rows, rows), lambda d, c: (d, 0, 0)),
            pl.BlockSpec((None, None, SEG, 2 * S5_LANES), lambda d, c: (layer, d, 0, 0)),
            pl.BlockSpec((None, None, 2, S5_LANES), lambda d, c: (layer, d, 0, 0)),
            _pick_spec(w['s5_cblk'], layer),
            pl.BlockSpec((None, None, 2, nb, S5_LANES), lambda d, c: (s0_layer, d, 0, 0, 0)),
        ],
        out_specs=out_specs,
        out_shape=out_shape,
        scratch_shapes=[pltpu.VMEM((2, nb, S5_LANES), F32),
                        pltpu.VMEM((S5_SUBCHUNKS, rows, 2 * S5_LANES), F32)],
        compiler_params=_cparams(("arbitrary", "arbitrary")),
        name="s5_scan",
    )(p3, perm, permt, w['s5_bblk'], w['s5_a'], w['s5_cblk'], s0)


def _ret_kernel(*refs, rope, has_s0, want_fin, nch):
    refs = list(refs)
    q_ref, k_ref, v_ref, g_ref = refs[:4]
    del refs[:4]
    if rope:
        cos_ref, sin_ref = refs[:2]
        del refs[:2]
    (dm_ref, qdf_ref, qdb_ref, kdft_ref, kdbt_ref, cdf_ref, cdb_ref, hm_ref, bd_ref, gm_ref,
     gn_ref) = refs[:11]
    del refs[:11]
    if has_s0:
        s0_ref = refs.pop(0)
    o_ref = refs.pop(0)
    fin_ref = refs.pop(0) if want_fin else None
    sb_all, sf_scr, sb_scr = refs

    p = pl.program_id(1)
    c = pl.program_id(2)
    hd = RET_DIM

    def init_state(dst, d):
        dst[...] = jnp.zeros_like(dst)
        if has_s0:
            for h in range(RET_HEADS):
                dst[h * hd:(h + 1) * hd, h * hd:(h + 1) * hd] = s0_ref[d, h]

    def write_final(src, d):
        for h in range(RET_HEADS):
            fin_ref[d, h] = src[h * hd:(h + 1) * hd, h * hd:(h + 1) * hd]

    def rope_tabs():
        cos = cos_ref[...]
        sin = sin_ref[...]
        return jnp.concatenate([cos, cos], axis=-1), jnp.concatenate([sin, sin], axis=-1)

    def roped_kt():
        k = k_ref[...].astype(F32)
        if rope:
            cos, sin = rope_tabs()
            k = k * cos + _rot_half_lanes(k, RET_DIM) * sin
        return k.T

    @pl.when(p == 0)
    def _():
        ck = nch - 1 - c

        @pl.when(c == 0)
        def _():
            init_state(sb_scr, 1)

        kt = roped_kt()
        s_old = sb_scr[...]
        sb_all[ck] = s_old
        sb_scr[...] = s_old * cdb_ref[...] + _dot(kt * kdbt_ref[...], v_ref[...]) * bd_ref[...]
        if want_fin:
            @pl.when(c == nch - 1)
            def _():
                write_final(sb_scr, 1)

    @pl.when(p == 1)
    def _():
        @pl.when(c == 0)
        def _():
            init_state(sf_scr, 0)

        q = q_ref[...].astype(F32)
        if rope:
            cos, sin = rope_tabs()
            q = q * cos + _rot_half_lanes(q, RET_DIM) * sin
        kt = roped_kt()
        ktb = kt.astype(BF16)
        vb = v_ref[...]
        s_f = sf_scr[...]
        o = _dot(q, s_f) * qdf_ref[...] + _dot(q, sb_all[c]) * qdb_ref[...]
        for h in range(RET_HEADS):
            hm = hm_ref[h:h + 1, :]
            att = _dot(q * hm, ktb) * dm_ref[h]
            o = o + _dot(att, vb) * hm
        sf_scr[...] = s_f * cdf_ref[...] + _dot(kt * kdft_ref[...], vb) * bd_ref[...]
        hi = o.astype(BF16)
        lo = (o - hi.astype(F32)).astype(BF16)
        mu = (jnp.dot(hi, gm_ref[...], preferred_element_type=F32)
              + jnp.dot(lo, gm_ref[...], preferred_element_type=F32))
        oc = o - mu
        var = _dot(oc * oc, gm_ref[...])
        on = oc * lax.rsqrt(var + EPS) * gn_ref[...]
        g = g_ref[...].astype(F32)
        o_ref[...] = (g * _sigmoid(g)) * on
        if want_fin:
            @pl.when(c == nch - 1)
            def _():
                write_final(sf_scr, 0)


def _ret_call(p3, w, s0, rope_tabs, *, layer, nb, t, want_fin):
    tc = RET_CHUNK
    nch = t // tc
    rope = rope_tabs is not None
    has_s0 = s0 is not None

    def kv_chunk(p, c):
        return jnp.where(p == 0, nch - 1 - c, c)

    def q_chunk(p, c):
        return jnp.where(p == 0, 0, c)

    def seg_spec(seg, cmap):
        return pl.BlockSpec((None, tc, SEG), lambda b, p, c: (b, cmap(p, c), seg))

    args = [p3, p3, p3, p3]
    in_specs = [seg_spec(S_RQ, q_chunk), seg_spec(S_RK, kv_chunk), seg_spec(S_RV, kv_chunk),
                seg_spec(S_RG, q_chunk)]
    if rope:
        args += [rope_tabs[0], rope_tabs[1]]
        in_specs += [pl.BlockSpec((tc, 2 * RET_DIM), lambda b, p, c: (kv_chunk(p, c), 0)),
                     pl.BlockSpec((tc, 2 * RET_DIM), lambda b, p, c: (kv_chunk(p, c), 0))]
    args += list(w['ret_layer_tabs']) + list(w['ret_const_tabs']) + [w['ret_gn']]
    in_specs += ([_pick_spec(a, layer) for a in w['ret_layer_tabs']]
                 + [_const_spec(a.shape) for a in w['ret_const_tabs']] + [_pick_spec(w['ret_gn'], layer)])
    if has_s0:
        args.append(s0)
        in_specs.append(pl.BlockSpec((None, None, 2, RET_HEADS, RET_DIM, RET_DIM),
                                     lambda b, p, c: (b, layer, 0, 0, 0, 0)))
    out_specs = [pl.BlockSpec((None, tc, SEG), lambda b, p, c: (b, q_chunk(p, c), 0))]
    out_shape = [jax.ShapeDtypeStruct((nb, t, SEG), F32)]
    if want_fin:
        out_specs.append(pl.BlockSpec((None, 2, RET_HEADS, RET_DIM, RET_DIM),
                                      lambda b, p, c: (b, 0, 0, 0, 0)))
        out_shape.append(jax.ShapeDtypeStruct((nb, 2, RET_HEADS, RET_DIM, RET_DIM), F32))
    return pl.pallas_call(
        functools.partial(_ret_kernel, rope=rope, has_s0=has_s0, want_fin=want_fin, nch=nch),
        grid=(nb, 2, nch),
        in_specs=in_specs,
        out_specs=out_specs,
        out_shape=out_shape,
        scratch_shapes=[pltpu.VMEM((nch, SEG, SEG), F32),
                        pltpu.VMEM((SEG, SEG), F32),
                        pltpu.VMEM((SEG, SEG), F32)],
        compiler_params=_cparams(("arbitrary", "arbitrary", "arbitrary")),
        name="retention",
    )(*args)


def _mla_kernel(*refs, t, tq, nctx, want_cache):
    refs = list(refs)
    cq_ref, ckv_ref, krp_ref = refs[:3]
    del refs[:3]
    has_ctx = nctx > 0
    if has_ctx:
        cm_ref, sm_ref, cache_ref = refs[:3]
        del refs[:3]
    qn_ref, kvn_ref, wq_ref = refs[:3]
    del refs[:3]
    if has_ctx:
        wqr_ref, wkc_ref = refs[:2]
        del refs[:2]
    wk_ref, wv_ref = refs[:2]
    del refs[:2]
    o_ref = refs.pop(0)
    lat_ref = refs.pop(0) if want_cache else None
    k_scr, vt_scr = refs
    qi = pl.program_id(1)
    s_tot = t + nctx

    def store_vt(v, c0, n):
        vt = v.T
        for h in range(MLA_HEADS):
            vt_scr[h, 0:MLA_V, c0:c0 + n] = vt[h * MLA_V:(h + 1) * MLA_V, :].astype(BF16)

    @pl.when(qi == 0)
    def _():
        lat = _rms(ckv_ref[...].astype(F32), kvn_ref[...])
        if want_cache:
            lat_ref[...] = lat
        latb = lat.astype(BF16)
        kr = krp_ref[...].astype(F32)
        if has_ctx:
            kr = kr * cm_ref[...] + _rot_half_lanes(kr, MLA_ROPE) * sm_ref[...]
        kk = jnp.dot(latb, wk_ref[...], preferred_element_type=F32)
        for h in range(MLA_HEADS):
            k_scr[h, 0:t, :] = (kk[:, h * MLA_SLAB:(h + 1) * MLA_SLAB] + kr).astype(BF16)
        store_vt(jnp.dot(latb, wv_ref[...], preferred_element_type=F32), 0, t)
        if has_ctx:
            cache = cache_ref[...].astype(BF16)
            kkc = jnp.dot(cache, wkc_ref[...], preferred_element_type=F32)
            for h in range(MLA_HEADS):
                k_scr[h, t:t + nctx, :] = kkc[:, h * MLA_SLAB:(h + 1) * MLA_SLAB].astype(BF16)
            store_vt(jnp.dot(cache[:, 0:MLA_KV_LORA], wv_ref[...], preferred_element_type=F32), t, nctx)
        tail = lax.broadcasted_iota(jnp.int32, (MLA_VT_ROWS - MLA_V, s_tot), 0)
        for h in range(MLA_HEADS):
            vt_scr[h, MLA_V:MLA_VT_ROWS, :] = jnp.where(tail == 0, 1.0, 0.0).astype(BF16)

    a = _rms(cq_ref[...].astype(F32), qn_ref[...], n=MLA_Q_LORA).astype(BF16)
    q = jnp.dot(a, wq_ref[...], preferred_element_type=F32)
    if has_ctx:
        qrot = jnp.dot(a, wqr_ref[...], preferred_element_type=F32)
        r0 = pl.multiple_of(qi * tq, tq)
        cm = cm_ref[pl.ds(r0, tq), :]
        sm = sm_ref[pl.ds(r0, tq), :]
    scale = (MLA_NOPE + MLA_ROPE) ** -0.5
    def scores_t(h):
        qh = q[:, h * MLA_SLAB:(h + 1) * MLA_SLAB]
        if has_ctx:
            qh = qh * cm + qrot[:, h * MLA_SLAB:(h + 1) * MLA_SLAB] * sm
        qt = (qh * (scale * LOG2_E)).T.astype(BF16)
        return jnp.dot(k_scr[h], qt, preferred_element_type=F32)

    outs = []
    st_next = scores_t(0)
    for h in range(MLA_HEADS):
        st = st_next
        if h + 1 < MLA_HEADS:
            st_next = scores_t(h + 1)
        m = jnp.max(st, axis=0, keepdims=True)
        e = jnp.exp2(st - m).astype(BF16)
        ot = jnp.dot(vt_scr[h], e, preferred_element_type=F32)
        outs.append(ot[0:MLA_V, :] * (1.0 / ot[MLA_V:MLA_V + 1, :]))
    o_ref[...] = jnp.concatenate(outs, axis=0).T


def _mla_call(p3, w, rope_tabs, cache, *, layer, nb, t, tq, want_cache):
    has_ctx = cache is not None
    nctx = cache.shape[2] if has_ctx else 0
    s_tot = t + nctx
    half = MLA_KV_LORA

    args = [p3, p3, p3]
    in_specs = [
        pl.BlockSpec((None, tq, SEG), lambda b, i: (b, i, S_CQ)),
        pl.BlockSpec((None, t, half), lambda b, i: (b, 0, 2 * S_KV)),
        pl.BlockSpec((None, t, half), lambda b, i: (b, 0, 2 * S_KV + 1)),
    ]
    if has_ctx:
        args += [rope_tabs[0], rope_tabs[1], cache]
        in_specs += [
            _const_spec((t, MLA_SLAB)),
            _const_spec((t, MLA_SLAB)),
            pl.BlockSpec((None, None, nctx, MLA_KV_LORA + MLA_ROPE), lambda b, i: (b, layer, 0, 0)),
        ]
    args += [w['mla_qn'], w['mla_kvn'], w['mla_wq']]
    in_specs += [_pick_spec(w['mla_qn'], layer), _pick_spec(w['mla_kvn'], layer), _pick_spec(w['mla_wq'], layer)]
    if has_ctx:
        args += [w['mla_wqr'], w['mla_wkc']]
        in_specs += [_pick_spec(w['mla_wqr'], layer), _pick_spec(w['mla_wkc'], layer)]
    args += [w['mla_wk'], w['mla_wv']]
    in_specs += [_pick_spec(w['mla_wk'], layer), _pick_spec(w['mla_wv'], layer)]
    out_specs = [pl.BlockSpec((None, tq, SEG), lambda b, i: (b, i, 0))]
    out_shape = [jax.ShapeDtypeStruct((nb, t, SEG), F32)]
    if want_cache:
        out_specs.append(pl.BlockSpec((None, t, MLA_KV_LORA), lambda b, i: (b, 0, 0)))
        out_shape.append(jax.ShapeDtypeStruct((nb, t, MLA_KV_LORA), F32))
    return pl.pallas_call(
        functools.partial(_mla_kernel, t=t, tq=tq, nctx=nctx, want_cache=want_cache),
        grid=(nb, t // tq),
        in_specs=in_specs,
        out_specs=out_specs,
        out_shape=out_shape,
        scratch_shapes=[pltpu.VMEM((MLA_HEADS, s_tot, MLA_SLAB), BF16),
                        pltpu.VMEM((MLA_HEADS, MLA_VT_ROWS, s_tot), BF16)],
        compiler_params=_cparams(("arbitrary", "arbitrary")),
        name="mla",
    )(*args)


def _merge_kernel(x_ref, mod_ref, npre_ref, npost_ref, yf_ref, yb_ref, u_ref, ret_ref,
                  cx_ref, cb_ref, cc_ref, cxp_ref, ccp_ref, cxn_ref, ccn_ref, mla_ref,
                  s5d_ref, wglu_ref, cw_ref, cbias_ref, wbr_ref, wg_ref, bg_ref, wo_ref,
                  o_ref, h_scr, *, tm, nt):
    i = pl.program_id(1)
    x = x_ref[...]
    sh = mod_ref[3:4, :]
    sc = mod_ref[4:5, :]
    gt = mod_ref[5:6, :]
    h_scr[...] = (_rms(x, npre_ref[...]) * (1.0 + sc) + sh).astype(BF16)

    y = yf_ref[...] + yb_ref[...] + s5d_ref[...] * u_ref[...].astype(F32)
    z5 = _gelu_tanh(y)
    b_s5 = z5 * _sigmoid(_dot(z5, wglu_ref[...]))

    z = cc_ref[...].astype(F32) * cx_ref[...].astype(F32)
    zprev = ((ccp_ref[...].astype(F32) * cxp_ref[...].astype(F32))[HALO - 1:HALO, :]
             * jnp.where(i > 0, 1.0, 0.0))
    znext = ((ccn_ref[...].astype(F32) * cxn_ref[...].astype(F32))[0:1, :]
             * jnp.where(i < nt - 1, 1.0, 0.0))
    row = lax.broadcasted_iota(jnp.int32, (tm, SEG), 0)
    z_dn = jnp.where(row == 0, zprev, pltpu.roll(z, 1, axis=0))
    z_up = jnp.where(row == tm - 1, znext, pltpu.roll(z, tm - 1, axis=0))
    b_conv = cb_ref[...].astype(F32) * (cbias_ref[...] + z_dn * cw_ref[0:1, :] + z * cw_ref[1:2, :]
                                        + z_up * cw_ref[2:3, :])

    branches = (b_s5, ret_ref[...], b_conv, mla_ref[...])
    merged = jnp.zeros((tm, D_MODEL), F32)
    for n in range(N_BRANCH):
        gate = _sigmoid(jnp.dot(h_scr[...], wg_ref[:, n * D_MODEL:(n + 1) * D_MODEL],
                                preferred_element_type=F32) + bg_ref[:, n * D_MODEL:(n + 1) * D_MODEL])
        merged = merged + gate * _dot(branches[n], wbr_ref[n])
    m = _dot(merged, wo_ref[...])
    o_ref[...] = x + gt * _rms(m, npost_ref[...])


def _merge_call(x, w, y_s5, ret_o, p3, mla_o, *, layer, nb, t, tm, row_of_batch):
    nt = t // tm
    hb = tm // HALO
    n_halo = t // HALO

    def dir_spec(d):
        return pl.BlockSpec((None, None, tm, SEG), lambda b, i: (d, b, i, 0))

    def seg_spec(seg):
        return pl.BlockSpec((None, tm, SEG), lambda b, i: (b, i, seg))

    def prev_spec(seg):
        return pl.BlockSpec((None, HALO, SEG), lambda b, i: (b, jnp.maximum(i * hb - 1, 0), seg))

    def next_spec(seg):
        return pl.BlockSpec((None, HALO, SEG), lambda b, i: (b, jnp.minimum((i + 1) * hb, n_halo - 1), seg))

    return pl.pallas_call(
        functools.partial(_merge_kernel, tm=tm, nt=nt),
        grid=(nb, nt),
        in_specs=[
            pl.BlockSpec((tm, D_MODEL), lambda b, i: (b * nt + i, 0)),
            pl.BlockSpec((None, None, N_MOD, D_MODEL), lambda b, i: (layer, row_of_batch(b), 0, 0)),
            _pick_spec(w['npre'], layer * N_NORM + 1),
            _pick_spec(w['npost'], layer * N_NORM + 1),
            dir_spec(0), dir_spec(1), seg_spec(S_U),
            pl.BlockSpec((None, tm, SEG), lambda b, i: (b, i, 0)),
            seg_spec(S_CX), seg_spec(S_CB), seg_spec(S_CC),
            prev_spec(S_CX), prev_spec(S_CC), next_spec(S_CX), next_spec(S_CC),
            pl.BlockSpec((None, tm, SEG), lambda b, i: (b, i, 0)),
            _pick_spec(w['s5_d'], layer), _pick_spec(w['s5_wglu'], layer), _pick_spec(w['conv_w'], layer),
            _pick_spec(w['conv_b'], layer), _resident_spec(w['w_branch'].shape[1:], (layer,)),
            _resident_spec(w['w_gate'].shape[1:], (layer,)), _pick_spec(w['b_gate'], layer),
            _resident_spec(w['w_o'].shape[1:], (layer,)),
        ],
        out_specs=pl.BlockSpec((tm, D_MODEL), lambda b, i: (b * nt + i, 0)),
        out_shape=jax.ShapeDtypeStruct((nb * t, D_MODEL), F32),
        scratch_shapes=[pltpu.VMEM((tm, D_MODEL), BF16)],
        compiler_params=_cparams(("arbitrary", "arbitrary")),
        name="merge",
    )(x, w['mod'], w['npre'], w['npost'], y_s5, y_s5, p3, ret_o, p3, p3, p3, p3, p3, p3, p3, mla_o,
      w['s5_d'], w['s5_wglu'], w['conv_w'], w['conv_b'], w['w_branch'], w['w_gate'], w['b_gate'], w['w_o'])


def _zeros_like_cols(w, n):
    return jnp.zeros(w.shape[:-1] + (n,), w.dtype)


def _axial_angles(t, dim):
    rows = t // GRID_W
    row = np.repeat(np.arange(rows, dtype=np.float64), GRID_W)
    col = np.tile(np.arange(GRID_W, dtype=np.float64), rows)
    quarter = dim // 4
    inv = ROPE_BASE ** (-np.arange(quarter, dtype=np.float64) / quarter)
    ang = np.concatenate([row[:, None] * inv, col[:, None] * inv], axis=-1)
    return np.cos(ang).astype(np.float32), np.sin(ang).astype(np.float32)


def _rope_tables(t):
    cos, sin = _axial_angles(t, RET_DIM)
    ret_cos = np.tile(np.concatenate([cos, cos], -1), (1, 2))
    ret_sin = np.tile(np.concatenate([sin, sin], -1), (1, 2))
    cos, sin = _axial_angles(t, MLA_ROPE)
    pad = MLA_SLAB - MLA_NOPE - MLA_ROPE
    mla_cos = np.concatenate([np.ones((t, MLA_NOPE), np.float32), cos, cos, np.zeros((t, pad), np.float32)], -1)
    mla_sin = np.concatenate([np.zeros((t, MLA_NOPE), np.float32), sin, sin, np.zeros((t, pad), np.float32)], -1)
    return ((jnp.asarray(ret_cos), jnp.asarray(ret_sin)), (jnp.asarray(mla_cos), jnp.asarray(mla_sin)))


def _place_rope_key(w):
    return jnp.concatenate([_zeros_like_cols(w, MLA_NOPE), w,
                            _zeros_like_cols(w, MLA_SLAB - MLA_NOPE - MLA_ROPE)], -1)


def _in_weights(w_in):
    sizes = (BRANCH_WIDTH,) * 8 + (MLA_Q_LORA, MLA_KV_LORA, MLA_ROPE)
    offs = np.cumsum((0,) + sizes)
    u, rq, rk, rv, rg, cx, cb, cc, cq, ckv, kr = [w_in[..., offs[i]:offs[i + 1]] for i in range(11)]
    rk = rk * (RET_DIM ** -0.5)
    cols = [u, rq, rk, rv, rg, cx, cb, cc,
            jnp.concatenate([cq, _zeros_like_cols(cq, SEG - MLA_Q_LORA)], -1),
            jnp.concatenate([ckv, _place_rope_key(kr)], -1)]
    return jnp.concatenate([c.astype(BF16) for c in cols], axis=-1)


def _mla_weights(w_uq, w_ukv):
    nl = w_uq.shape[0]
    dq = MLA_NOPE + MLA_ROPE
    w4 = w_uq.reshape(nl, MLA_Q_LORA, MLA_HEADS, dq)
    nope, ropew = w4[..., :MLA_NOPE], w4[..., MLA_NOPE:]
    half = MLA_ROPE // 2
    rot = jnp.concatenate([-ropew[..., half:], ropew[..., :half]], -1)
    zpad = jnp.zeros((nl, MLA_Q_LORA, MLA_HEADS, MLA_SLAB - dq), F32)
    wq = jnp.concatenate([nope, ropew, zpad], -1).reshape(nl, MLA_Q_LORA, MLA_HEADS * MLA_SLAB)
    wqr = jnp.concatenate([jnp.zeros_like(nope), rot, zpad], -1).reshape(nl, MLA_Q_LORA, MLA_HEADS * MLA_SLAB)
    rpad = jnp.zeros((nl, SEG - MLA_Q_LORA, MLA_HEADS * MLA_SLAB), F32)
    wq = jnp.concatenate([wq, rpad], 1).astype(BF16)
    wqr = jnp.concatenate([wqr, rpad], 1).astype(BF16)
    kv4 = w_ukv.reshape(nl, MLA_KV_LORA, MLA_HEADS, MLA_NOPE + MLA_V)
    wk = jnp.concatenate([kv4[..., :MLA_NOPE], jnp.zeros((nl, MLA_KV_LORA, MLA_HEADS, MLA_SLAB - MLA_NOPE), F32)],
                         -1).reshape(nl, MLA_KV_LORA, MLA_HEADS * MLA_SLAB)
    place = np.zeros((MLA_ROPE, MLA_HEADS, MLA_SLAB), np.float32)
    place[np.arange(MLA_ROPE), :, MLA_NOPE + np.arange(MLA_ROPE)] = 1.0
    place = jnp.broadcast_to(jnp.asarray(place.reshape(MLA_ROPE, MLA_HEADS * MLA_SLAB)),
                             (nl, MLA_ROPE, MLA_HEADS * MLA_SLAB))
    wkc = jnp.concatenate([wk, place], 1).astype(BF16)
    wv = kv4[..., MLA_NOPE:].reshape(nl, MLA_KV_LORA, MLA_HEADS * MLA_V).astype(BF16)
    return wq, wqr, wk.astype(BF16), wkc, wv


def _s5_tables(lam_re, lam_im, log_dt, b_re, b_im, c_re, c_im):
    nl = lam_re.shape[0]
    dt = jnp.exp(log_dt)[..., None]
    mag = jnp.exp(lam_re * dt)
    ab_re = mag * jnp.cos(lam_im * dt)
    ab_im = mag * jnp.sin(lam_im * dt)
    den = lam_re * lam_re + lam_im * lam_im
    f_re = ((ab_re - 1.0) * lam_re + ab_im * lam_im) / den
    f_im = (ab_im * lam_re - (ab_re - 1.0) * lam_im) / den
    bb_re = f_re[..., None] * b_re - f_im[..., None] * b_im
    bb_im = f_re[..., None] * b_im + f_im[..., None] * b_re
    eye = jnp.eye(S5_GROUPS, dtype=F32)
    blk_in = lambda m: jnp.einsum('ldgph,gk->ldghkp', m, eye).reshape(nl, 2, BRANCH_WIDTH, S5_LANES)
    bblk = jnp.concatenate([blk_in(bb_re), blk_in(bb_im)], -1).astype(BF16)
    blk_out = lambda m: jnp.einsum('lghp,gk->lgpkh', m, eye).reshape(nl, S5_LANES, BRANCH_WIDTH)
    cblk = jnp.concatenate([blk_out(c_re), -blk_out(c_im)], 1).astype(BF16)
    a_tab = jnp.stack([ab_re.reshape(nl, 2, S5_LANES), ab_im.reshape(nl, 2, S5_LANES)], axis=2)
    return bblk, a_tab, cblk


def _ret_tables(decay_logit, c):
    lg = jax.nn.log_sigmoid(decay_logit.astype(F32))
    lf = lg[:, 0][:, :, None, None]
    lb = lg[:, 1][:, :, None, None]
    idx = jnp.arange(c, dtype=F32)
    diff = idx[:, None] - idx[None, :]
    dm = jnp.where(diff >= 0, jnp.exp(lf * jnp.maximum(diff, 0.0)), jnp.exp(lb * jnp.maximum(-diff, 0.0)))
    lfl = jnp.repeat(lg[:, 0], RET_DIM, axis=-1)
    lbl = jnp.repeat(lg[:, 1], RET_DIM, axis=-1)
    qdf = jnp.exp(lfl[:, None, :] * (idx[:, None] + 1.0))
    qdb = jnp.exp(lbl[:, None, :] * (c - idx[:, None]))
    kdft = jnp.exp(lfl[:, :, None] * (c - 1.0 - idx[None, :]))
    kdbt = jnp.exp(lbl[:, :, None] * idx[None, :])
    nl = lg.shape[0]
    cdf = jnp.broadcast_to(jnp.exp(lfl * c)[:, :, None], (nl, SEG, SEG))
    cdb = jnp.broadcast_to(jnp.exp(lbl * c)[:, :, None], (nl, SEG, SEG))
    head = np.arange(SEG) // RET_DIM
    hm = jnp.asarray((head[None, :] == np.arange(RET_HEADS)[:, None]).astype(np.float32))
    bd_np = (head[:, None] == head[None, :]).astype(np.float32)
    bd = jnp.asarray(bd_np)
    gm = jnp.asarray(bd_np / RET_DIM, BF16)
    return (dm, qdf, qdb, kdft, kdbt, cdf, cdb), (hm, bd, gm)


def _layer(x, w, *, layer, nb, t, tm, tm_ffn, row_of_batch, ctx, rope_tabs):
    is_ctx_pass = ctx is None
    row_of_tile = lambda i: row_of_batch((i * tm_ffn) // t)
    x = _ffn_call(x, w, layer=layer, k=0, tm=tm_ffn, row_of_tile=row_of_tile)

    p = _in_call(x, w, layer=layer, nb=nb, t=t, tm=tm, row_of_batch=row_of_batch)
    p3 = p.reshape(nb, t, NSEG * SEG)

    if is_ctx_pass:
        s5_0, s0_layer = jnp.zeros((1, 2, 2, nb, S5_LANES), F32), 0
        ret_0, mla_cache, ret_rope, mla_rope = None, None, None, None
    else:
        s5_0, ret_0, mla_cache = ctx
        s0_layer = layer
        ret_rope, mla_rope = rope_tabs
    s5_out = _s5_call(p3, w, s5_0, layer=layer, s0_layer=s0_layer, nb=nb, t=t, want_fin=is_ctx_pass)
    ret_out = _ret_call(p3, w, ret_0, ret_rope, layer=layer, nb=nb, t=t, want_fin=is_ctx_pass)
    mla_out = _mla_call(p3, w, mla_rope, mla_cache, layer=layer, nb=nb, t=t, tq=min(t, 512),
                        want_cache=is_ctx_pass)

    x = _merge_call(x, w, s5_out[0], ret_out[0], p3, mla_out[0], layer=layer, nb=nb, t=t, tm=tm,
                    row_of_batch=row_of_batch)
    x = _ffn_call(x, w, layer=layer, k=2, tm=tm_ffn, row_of_tile=row_of_tile)

    state = None
    if is_ctx_pass:
        kr = p3[:, :, S_KV * SEG + MLA_KV_LORA + MLA_NOPE:S_KV * SEG + MLA_KV_LORA + MLA_NOPE + MLA_ROPE]
        cache = jnp.concatenate([mla_out[1], kr.astype(F32)], axis=-1)
        state = (s5_out[1], ret_out[1], cache)
    return x, state


def kernel(x_prompt, x_sample, state_s5, state_ret, cache_mla, c, c_ctx, w_mod, b_mod, norm_pre, norm_post, ffn_w1, ffn_w3, ffn_w2, w_in, s5_lam_re, s5_lam_im, s5_log_dt, s5_b_re, s5_b_im, s5_c_re, s5_c_im, s5_d, s5_w_glu, ret_decay, ret_gn, conv_w, conv_b, mla_q_norm, mla_w_uq, mla_kv_norm, mla_w_ukv, w_branch, w_gate, b_gate, w_o):
    bp, tp, _ = x_prompt.shape
    bs, ts, _ = x_sample.shape
    assert 1 + bs <= MOD_ROWS

    cond = jnp.concatenate([c_ctx[None, :], c, jnp.zeros((MOD_ROWS - 1 - bs, D_MODEL), F32)], axis=0)
    mod = _mod_call(cond, w_mod, b_mod).reshape(DEPTH, MOD_ROWS, N_MOD, D_MODEL)

    wq, wqr, wk, wkc, wv = _mla_weights(mla_w_uq, mla_w_ukv)
    bblk, a_tab, cblk = _s5_tables(s5_lam_re, s5_lam_im, s5_log_dt, s5_b_re, s5_b_im, s5_c_re, s5_c_im)
    ret_layer_tabs, ret_const_tabs = _ret_tables(ret_decay, RET_CHUNK)
    w = dict(
        mod=mod,
        npre=norm_pre.reshape(DEPTH * N_NORM, 1, D_MODEL),
        npost=norm_post.reshape(DEPTH * N_NORM, 1, D_MODEL),
        ffn=(ffn_w1.astype(BF16), ffn_w3.astype(BF16), ffn_w2.astype(BF16)),
        w_in=_in_weights(w_in),
        s5_bblk=bblk, s5_a=a_tab, s5_cblk=cblk,
        s5_d=s5_d[:, None, :], s5_wglu=s5_w_glu.astype(BF16),
        ret_layer_tabs=ret_layer_tabs, ret_const_tabs=ret_const_tabs, ret_gn=ret_gn[:, None, :],
        conv_w=conv_w, conv_b=conv_b[:, None, :],
        mla_qn=jnp.concatenate([mla_q_norm, jnp.zeros((DEPTH, SEG - MLA_Q_LORA), F32)], -1)[:, None, :],
        mla_kvn=mla_kv_norm[:, None, :],
        mla_wq=wq, mla_wqr=wqr, mla_wk=wk, mla_wkc=wkc, mla_wv=wv,
        w_branch=w_branch.astype(BF16), w_gate=w_gate.astype(BF16), b_gate=b_gate[:, None, :],
        w_o=w_o.astype(BF16),
    )
    rope_tabs = _rope_tables(ts)
    s5_0 = state_s5.transpose(1, 2, 5, 0, 3, 4).reshape(DEPTH, 2, 2, bs, S5_LANES)

    xp = x_prompt.reshape(bp * tp, D_MODEL)
    xs = x_sample.reshape(bs * ts, D_MODEL)
    s5_list, ret_list, mla_list = [], [], []
    for l in range(DEPTH):
        xp, (s5_s, ret_s, mla_c) = _layer(xp, w, layer=l, nb=bp, t=tp, tm=256, tm_ffn=512,
                                          row_of_batch=lambda b: 0, ctx=None, rope_tabs=None)
        s5_list.append(s5_s)
        ret_list.append(ret_s)
        mla_list.append(mla_c)
        xs, _ = _layer(xs, w, layer=l, nb=bs, t=ts, tm=512, tm_ffn=1024, row_of_batch=lambda b: 1 + b,
                       ctx=(s5_0, state_ret, cache_mla), rope_tabs=rope_tabs)
    new_s5 = jnp.stack(s5_list, 0).reshape(DEPTH, 2, 2, bp, S5_GROUPS, S5_STATE).transpose(3, 0, 1, 4, 5, 2)
    return (xp.reshape(bp, tp, D_MODEL), xs.reshape(bs, ts, D_MODEL),
            new_s5, jnp.stack(ret_list, axis=1), jnp.stack(mla_list, axis=1))
```

```python
import functools

import numpy as np
import jax
import jax.numpy as jnp
from jax import lax
from jax.experimental import pallas as pl
from jax.experimental.pallas import tpu as pltpu

F32 = jnp.float32
BF16 = jnp.bfloat16

D_MODEL = 1024
DEPTH = 2
GRID_W = 64
EPS = 1e-6
LOG2_E = float(np.log2(np.e))
ROPE_BASE = 10000.0
N_BRANCH = 4
BRANCH_WIDTH = D_MODEL // 4
N_MOD = 9
N_NORM = 3
D_FF = 2816
S5_GROUP = 16
S5_GROUPS = BRANCH_WIDTH // S5_GROUP
S5_STATE = 64
S5_LANES = S5_GROUPS * S5_STATE
RET_HEADS = 4
RET_DIM = BRANCH_WIDTH // RET_HEADS
CONV_K = 3
MLA_HEADS = 4
MLA_Q_LORA = 192
MLA_KV_LORA = 128
MLA_NOPE = 64
MLA_ROPE = 32
MLA_V = BRANCH_WIDTH // MLA_HEADS
MLA_SLAB = 128
MLA_VT_ROWS = MLA_V + 16

SUBLANES = 8
S5_SUBCHUNKS = 4
S5_ROWS = 512
SEG = 256
FF_CHUNK = 256
N_FF_CHUNKS = D_FF // FF_CHUNK
MOD_ROWS = 16
HALO = 16
RET_CHUNK = 256
V7X_VMEM_LIMIT = 56 * 1024 * 1024

S_U, S_RQ, S_RK, S_RV, S_RG, S_CX, S_CB, S_CC, S_CQ, S_KV = range(10)
NSEG = 10
LANES = 128


def _cparams(sem):
    return pltpu.CompilerParams(dimension_semantics=sem, vmem_limit_bytes=V7X_VMEM_LIMIT)


def _dot(a, b):
    return jnp.dot(a.astype(BF16), b.astype(BF16), preferred_element_type=F32)


def _sigmoid(x):
    return 1.0 / (1.0 + jnp.exp(-x))


def _rms(x, g, n=None):
    n = x.shape[-1] if n is None else n
    ms = jnp.sum(x * x, axis=-1, keepdims=True) * (1.0 / n)
    return x * lax.rsqrt(ms + EPS) * g


def _rot_half_lanes(x, width):
    half = width // 2
    lane = lax.broadcasted_iota(jnp.int32, (x.shape[0], LANES), 1)
    first = (lane & (width - 1)) < half
    outs = []
    for j in range(x.shape[1] // LANES):
        v = x[:, j * LANES:(j + 1) * LANES]
        outs.append(jnp.where(first, -pltpu.roll(v, LANES - half, axis=1), pltpu.roll(v, half, axis=1)))
    return outs[0] if len(outs) == 1 else jnp.concatenate(outs, axis=-1)


def _gelu_tanh(x):
    return 0.5 * x * (1.0 + jnp.tanh(np.sqrt(2.0 / np.pi) * (x + 0.044715 * (x * x * x))))


def _const_spec(shape):
    nd = len(shape)
    return pl.BlockSpec(shape, lambda *_: (0,) * nd)


def _pick_spec(arr, *lead):
    shape = arr.shape[len(lead):]
    return pl.BlockSpec((None,) * len(lead) + tuple(shape), lambda *_: tuple(lead) + (0,) * len(shape))


def _resident_spec(shape, lead=()):
    nd = len(shape)
    return pl.BlockSpec((None,) * len(lead) + tuple(shape), lambda *_: tuple(lead) + (0,) * nd,
                        pipeline_mode=pl.Buffered(1))


def _mod_kernel(c_ref, w_ref, b_ref, o_ref):
    c = c_ref[...]
    o_ref[...] = _dot(c * _sigmoid(c), w_ref[...]) + b_ref[...]


def _mod_call(cond, w_mod, b_mod):
    ncol = N_MOD * D_MODEL
    tn = D_MODEL
    return pl.pallas_call(
        _mod_kernel,
        grid=(DEPTH, ncol // tn),
        in_specs=[
            pl.BlockSpec((MOD_ROWS, D_MODEL), lambda l, j: (0, 0)),
            pl.BlockSpec((None, D_MODEL, tn), lambda l, j: (l, 0, j)),
            pl.BlockSpec((None, 1, tn), lambda l, j: (l, 0, j)),
        ],
        out_specs=pl.BlockSpec((None, MOD_ROWS, tn), lambda l, j: (l, 0, j)),
        out_shape=jax.ShapeDtypeStruct((DEPTH, MOD_ROWS, ncol), F32),
        compiler_params=_cparams(("arbitrary", "arbitrary")),
        name="mod",
    )(cond, w_mod, b_mod.reshape(DEPTH, 1, ncol))


def _ffn_kernel(x_ref, mod_ref, npre_ref, npost_ref, w1_ref, w3_ref, w2_ref, o_ref,
                h_scr, g_scr, *, k):
    x = x_ref[...]
    sh = mod_ref[3 * k:3 * k + 1, :]
    sc = mod_ref[3 * k + 1:3 * k + 2, :]
    gt = mod_ref[3 * k + 2:3 * k + 3, :]
    h_scr[...] = (_rms(x, npre_ref[...]) * (1.0 + sc) + sh).astype(BF16)
    for j in range(N_FF_CHUNKS):
        cols = slice(j * FF_CHUNK, (j + 1) * FF_CHUNK)
        a = jnp.dot(h_scr[...], w1_ref[:, cols], preferred_element_type=F32)
        b = jnp.dot(h_scr[...], w3_ref[:, cols], preferred_element_type=F32)
        g_scr[:, cols] = (a * _sigmoid(a) * b).astype(BF16)
    f = jnp.dot(g_scr[...], w2_ref[...], preferred_element_type=F32)
    o_ref[...] = x + (0.5 * gt) * _rms(f, npost_ref[...])


def _ffn_call(x, w, *, layer, k, tm, row_of_tile):
    n = x.shape[0]
    lj = (layer, k // 2)
    w1, w3, w2 = w['ffn']
    return pl.pallas_call(
        functools.partial(_ffn_kernel, k=k),
        grid=(n // tm,),
        in_specs=[
            pl.BlockSpec((tm, D_MODEL), lambda i: (i, 0)),
            pl.BlockSpec((None, None, N_MOD, D_MODEL), lambda i: (layer, row_of_tile(i), 0, 0)),
            _pick_spec(w['npre'], layer * N_NORM + k),
            _pick_spec(w['npost'], layer * N_NORM + k),
            _resident_spec((D_MODEL, D_FF), lj),
            _resident_spec((D_MODEL, D_FF), lj),
            _resident_spec((D_FF, D_MODEL), lj),
        ],
        out_specs=pl.BlockSpec((tm, D_MODEL), lambda i: (i, 0)),
        out_shape=jax.ShapeDtypeStruct((n, D_MODEL), F32),
        scratch_shapes=[pltpu.VMEM((tm, D_MODEL), BF16), pltpu.VMEM((tm, D_FF), BF16)],
        compiler_params=_cparams(("arbitrary",)),
        name="ffn",
    )(x, w['mod'], w['npre'], w['npost'], w1, w3, w2)


def _in_kernel(x_ref, mod_ref, npre_ref, w_ref, p_ref, h_scr):
    sh = mod_ref[3:4, :]
    sc = mod_ref[4:5, :]
    h_scr[...] = (_rms(x_ref[...], npre_ref[...]) * (1.0 + sc) + sh).astype(BF16)
    for j in range(NSEG):
        cols = slice(j * SEG, (j + 1) * SEG)
        p_ref[:, cols] = jnp.dot(h_scr[...], w_ref[:, cols], preferred_element_type=F32).astype(BF16)


def _in_call(x, w, *, layer, nb, t, tm, row_of_batch):
    tt = t // tm
    nseg = NSEG
    return pl.pallas_call(
        _in_kernel,
        grid=(nb, tt),
        in_specs=[
            pl.BlockSpec((tm, D_MODEL), lambda b, i: (b * tt + i, 0)),
            pl.BlockSpec((None, None, N_MOD, D_MODEL), lambda b, i: (layer, row_of_batch(b), 0, 0)),
            _pick_spec(w['npre'], layer * N_NORM + 1),
            _resident_spec((D_MODEL, nseg * SEG), (layer,)),
        ],
        out_specs=pl.BlockSpec((tm, nseg * SEG), lambda b, i: (b * tt + i, 0)),
        out_shape=jax.ShapeDtypeStruct((nb * t, nseg * SEG), BF16),
        scratch_shapes=[pltpu.VMEM((tm, D_MODEL), BF16)],
        compiler_params=_cparams(("arbitrary", "arbitrary")),
        name="in_proj",
    )(x, w['mod'], w['npre'], w['w_in'])


def _s5_kernel(*refs, nb, steps, want_fin):
    if want_fin:
        u_ref, bb_ref, a_ref, cc_ref, s0_ref, y_ref, fin_ref, s_scr, bu_scr, u2_scr, y2_scr = refs
    else:
        u_ref, bb_ref, a_ref, cc_ref, s0_ref, y_ref, s_scr, bu_scr, u2_scr, y2_scr = refs
        fin_ref = None
    d = pl.program_id(0)
    c = pl.program_id(1)
    span = steps * S5_SUBCHUNKS
    ngrp = nb // SUBLANES

    @pl.when(c == 0)
    def _():
        s_scr[...] = s0_ref[...]

    u_all = u_ref[...].astype(F32).reshape(nb * span, SEG)
    for j in range(SEG // LANES):
        u2_scr[j] = u_all[:, j * LANES:(j + 1) * LANES]

    a_re = jnp.broadcast_to(a_ref[0:1, :], (SUBLANES, S5_LANES))
    a_im = jnp.broadcast_to(a_ref[1:2, :], (SUBLANES, S5_LANES))
    s_re = [s_scr[0, g * SUBLANES:(g + 1) * SUBLANES, :] for g in range(ngrp)]
    s_im = [s_scr[1, g * SUBLANES:(g + 1) * SUBLANES, :] for g in range(ngrp)]
    re_cols = slice(0, S5_LANES)
    im_cols = slice(S5_LANES, 2 * S5_LANES)

    def time_of(k, i):
        j = k * steps + i
        return jnp.where(d == 0, j, span - 1 - j)

    def project_in(k):
        u_tb = jnp.concatenate(
            [jnp.concatenate([u2_scr[j, pl.ds(time_of(k, i), nb, stride=span), :]
                              for j in range(SEG // LANES)], axis=-1) for i in range(steps)], axis=0)
        bu_scr[k] = _dot(u_tb, bb_ref[...])

    project_in(0)
    for k in range(S5_SUBCHUNKS):
        if k + 1 < S5_SUBCHUNKS:
            project_in(k + 1)
        for i in range(steps):
            for g in range(ngrp):
                rr = slice(i * nb + g * SUBLANES, i * nb + (g + 1) * SUBLANES)
                n_re = a_re * s_re[g] - a_im * s_im[g] + bu_scr[k, rr, re_cols]
                n_im = a_re * s_im[g] + a_im * s_re[g] + bu_scr[k, rr, im_cols]
                bu_scr[k, rr, re_cols] = n_re
                bu_scr[k, rr, im_cols] = n_im
                s_re[g], s_im[g] = n_re, n_im
        y_tb = _dot(bu_scr[k], cc_ref[...])
        for i in range(steps):
            for j in range(SEG // LANES):
                y2_scr[j, pl.ds(time_of(k, i), nb, stride=span), :] = (
                    y_tb[i * nb:(i + 1) * nb, j * LANES:(j + 1) * LANES])
    y_ref[...] = jnp.concatenate([y2_scr[j] for j in range(SEG // LANES)], axis=-1).reshape(nb, span, SEG)
    for g in range(ngrp):
        s_scr[0, g * SUBLANES:(g + 1) * SUBLANES, :] = s_re[g]
        s_scr[1, g * SUBLANES:(g + 1) * SUBLANES, :] = s_im[g]
    if want_fin:
        fin_ref[...] = s_scr[...]


def _s5_call(p3, w, s0, *, layer, s0_layer, nb, t, want_fin):
    steps = S5_ROWS // nb
    span = steps * S5_SUBCHUNKS
    nch = t // span
    rows = S5_ROWS

    def chunk(d, c):
        return jnp.where(d == 0, c, nch - 1 - c)

    out_specs = [pl.BlockSpec((None, nb, span, SEG), lambda d, c: (d, 0, chunk(d, c), 0))]
    out_shape = [jax.ShapeDtypeStruct((2, nb, t, SEG), F32)]
    if want_fin:
        out_specs.append(pl.BlockSpec((None, 2, nb, S5_LANES), lambda d, c: (d, 0, 0, 0)))
        out_shape.append(jax.ShapeDtypeStruct((2, 2, nb, S5_LANES), F32))
    return pl.pallas_call(
        functools.partial(_s5_kernel, nb=nb, steps=steps, want_fin=want_fin),
        grid=(2, nch),
        in_specs=[
            pl.BlockSpec((nb, span, SEG), lambda d, c: (0, chunk(d, c), S_U)),
            pl.BlockSpec((None, None, SEG, 2 * S5_LANES), lambda d, c: (layer, d, 0, 0)),
            pl.BlockSpec((None, None, 2, S5_LANES), lambda d, c: (layer, d, 0, 0)),
            _pick_spec(w['s5_cblk'], layer),
            pl.BlockSpec((None, None, 2, nb, S5_LANES), lambda d, c: (s0_layer, d, 0, 0, 0)),
        ],
        out_specs=out_specs,
        out_shape=out_shape,
        scratch_shapes=[pltpu.VMEM((2, nb, S5_LANES), F32),
                        pltpu.VMEM((S5_SUBCHUNKS, rows, 2 * S5_LANES), F32),
                        pltpu.VMEM((SEG // LANES, nb * span, LANES), F32),
                        pltpu.VMEM((SEG // LANES, nb * span, LANES), F32)],
        compiler_params=_cparams(("arbitrary", "arbitrary")),
        name="s5_scan",
    )(p3, w['s5_bblk'], w['s5_a'], w['s5_cblk'], s0)


def _ret_kernel(*refs, rope, has_s0, want_fin, nch):
    refs = list(refs)
    q_ref, k_ref, v_ref, g_ref = refs[:4]
    del refs[:4]
    if rope:
        cos_ref, sin_ref = refs[:2]
        del refs[:2]
    (dm_ref, qdf_ref, qdb_ref, kdft_ref, kdbt_ref, cdf_ref, cdb_ref, hm_ref, bd_ref, gm_ref,
     gn_ref) = refs[:11]
    del refs[:11]
    if has_s0:
        s0_ref = refs.pop(0)
    o_ref = refs.pop(0)
    fin_ref = refs.pop(0) if want_fin else None
    sb_all, sf_scr, sb_scr = refs

    p = pl.program_id(1)
    c = pl.program_id(2)
    hd = RET_DIM

    def init_state(dst, d):
        dst[...] = jnp.zeros_like(dst)
        if has_s0:
            for h in range(RET_HEADS):
                dst[h * hd:(h + 1) * hd, h * hd:(h + 1) * hd] = s0_ref[d, h]

    def write_final(src, d):
        for h in range(RET_HEADS):
            fin_ref[d, h] = src[h * hd:(h + 1) * hd, h * hd:(h + 1) * hd]

    def rope_tabs():
        cos = cos_ref[...]
        sin = sin_ref[...]
        return jnp.concatenate([cos, cos], axis=-1), jnp.concatenate([sin, sin], axis=-1)

    def roped_kt():
        k = k_ref[...].astype(F32)
        if rope:
            cos, sin = rope_tabs()
            k = k * cos + _rot_half_lanes(k, RET_DIM) * sin
        return k.T

    @pl.when(p == 0)
    def _():
        ck = nch - 1 - c

        @pl.when(c == 0)
        def _():
            init_state(sb_scr, 1)

        kt = roped_kt()
        s_old = sb_scr[...]
        sb_all[ck] = s_old
        sb_scr[...] = s_old * cdb_ref[...] + _dot(kt * kdbt_ref[...], v_ref[...]) * bd_ref[...]
        if want_fin:
            @pl.when(c == nch - 1)
            def _():
                write_final(sb_scr, 1)

    @pl.when(p == 1)
    def _():
        @pl.when(c == 0)
        def _():
            init_state(sf_scr, 0)

        q = q_ref[...].astype(F32)
        if rope:
            cos, sin = rope_tabs()
            q = q * cos + _rot_half_lanes(q, RET_DIM) * sin
        kt = roped_kt()
        ktb = kt.astype(BF16)
        vb = v_ref[...]
        s_f = sf_scr[...]
        o = _dot(q, s_f) * qdf_ref[...] + _dot(q, sb_all[c]) * qdb_ref[...]
        for h in range(RET_HEADS):
            hm = hm_ref[h:h + 1, :]
            att = _dot(q * hm, ktb) * dm_ref[h]
            o = o + _dot(att, vb) * hm
        sf_scr[...] = s_f * cdf_ref[...] + _dot(kt * kdft_ref[...], vb) * bd_ref[...]
        hi = o.astype(BF16)
        lo = (o - hi.astype(F32)).astype(BF16)
        mu = (jnp.dot(hi, gm_ref[...], preferred_element_type=F32)
              + jnp.dot(lo, gm_ref[...], preferred_element_type=F32))
        oc = o - mu
        var = _dot(oc * oc, gm_ref[...])
        on = oc * lax.rsqrt(var + EPS) * gn_ref[...]
        g = g_ref[...].astype(F32)
        o_ref[...] = (g * _sigmoid(g)) * on
        if want_fin:
            @pl.when(c == nch - 1)
            def _():
                write_final(sf_scr, 0)


def _ret_call(p3, w, s0, rope_tabs, *, layer, nb, t, want_fin):
    tc = RET_CHUNK
    nch = t // tc
    rope = rope_tabs is not None
    has_s0 = s0 is not None

    def kv_chunk(p, c):
        return jnp.where(p == 0, nch - 1 - c, c)

    def q_chunk(p, c):
        return jnp.where(p == 0, 0, c)

    def seg_spec(seg, cmap):
        return pl.BlockSpec((None, tc, SEG), lambda b, p, c: (b, cmap(p, c), seg))

    args = [p3, p3, p3, p3]
    in_specs = [seg_spec(S_RQ, q_chunk), seg_spec(S_RK, kv_chunk), seg_spec(S_RV, kv_chunk),
                seg_spec(S_RG, q_chunk)]
    if rope:
        args += [rope_tabs[0], rope_tabs[1]]
        in_specs += [pl.BlockSpec((tc, 2 * RET_DIM), lambda b, p, c: (kv_chunk(p, c), 0)),
                     pl.BlockSpec((tc, 2 * RET_DIM), lambda b, p, c: (kv_chunk(p, c), 0))]
    args += list(w['ret_layer_tabs']) + list(w['ret_const_tabs']) + [w['ret_gn']]
    in_specs += ([_pick_spec(a, layer) for a in w['ret_layer_tabs']]
                 + [_const_spec(a.shape) for a in w['ret_const_tabs']] + [_pick_spec(w['ret_gn'], layer)])
    if has_s0:
        args.append(s0)
        in_specs.append(pl.BlockSpec((None, None, 2, RET_HEADS, RET_DIM, RET_DIM),
                                     lambda b, p, c: (b, layer, 0, 0, 0, 0)))
    out_specs = [pl.BlockSpec((None, tc, SEG), lambda b, p, c: (b, q_chunk(p, c), 0))]
    out_shape = [jax.ShapeDtypeStruct((nb, t, SEG), F32)]
    if want_fin:
        out_specs.append(pl.BlockSpec((None, 2, RET_HEADS, RET_DIM, RET_DIM),
                                      lambda b, p, c: (b, 0, 0, 0, 0)))
        out_shape.append(jax.ShapeDtypeStruct((nb, 2, RET_HEADS, RET_DIM, RET_DIM), F32))
    return pl.pallas_call(
        functools.partial(_ret_kernel, rope=rope, has_s0=has_s0, want_fin=want_fin, nch=nch),
        grid=(nb, 2, nch),
        in_specs=in_specs,
        out_specs=out_specs,
        out_shape=out_shape,
        scratch_shapes=[pltpu.VMEM((nch, SEG, SEG), F32),
                        pltpu.VMEM((SEG, SEG), F32),
                        pltpu.VMEM((SEG, SEG), F32)],
        compiler_params=_cparams(("arbitrary", "arbitrary", "arbitrary")),
        name="retention",
    )(*args)


def _mla_kernel(*refs, t, tq, nctx, want_cache):
    refs = list(refs)
    cq_ref, ckv_ref, krp_ref = refs[:3]
    del refs[:3]
    has_ctx = nctx > 0
    if has_ctx:
        cm_ref, sm_ref, cache_ref = refs[:3]
        del refs[:3]
    qn_ref, kvn_ref, wq_ref = refs[:3]
    del refs[:3]
    if has_ctx:
        wqr_ref, wkc_ref = refs[:2]
        del refs[:2]
    wk_ref, wv_ref = refs[:2]
    del refs[:2]
    o_ref = refs.pop(0)
    lat_ref = refs.pop(0) if want_cache else None
    k_scr, vt_scr = refs
    qi = pl.program_id(1)
    s_tot = t + nctx

    def store_vt(v, c0, n):
        vt = v.T
        for h in range(MLA_HEADS):
            vt_scr[h, 0:MLA_V, c0:c0 + n] = vt[h * MLA_V:(h + 1) * MLA_V, :].astype(BF16)

    @pl.when(qi == 0)
    def _():
        lat = _rms(ckv_ref[...].astype(F32), kvn_ref[...])
        if want_cache:
            lat_ref[...] = lat
        latb = lat.astype(BF16)
        kr = krp_ref[...].astype(F32)
        if has_ctx:
            kr = kr * cm_ref[...] + _rot_half_lanes(kr, MLA_ROPE) * sm_ref[...]
        kk = jnp.dot(latb, wk_ref[...], preferred_element_type=F32)
        for h in range(MLA_HEADS):
            k_scr[h, 0:t, :] = (kk[:, h * MLA_SLAB:(h + 1) * MLA_SLAB] + kr).astype(BF16)
        store_vt(jnp.dot(latb, wv_ref[...], preferred_element_type=F32), 0, t)
        if has_ctx:
            cache = cache_ref[...].astype(BF16)
            kkc = jnp.dot(cache, wkc_ref[...], preferred_element_type=F32)
            for h in range(MLA_HEADS):
                k_scr[h, t:t + nctx, :] = kkc[:, h * MLA_SLAB:(h + 1) * MLA_SLAB].astype(BF16)
            store_vt(jnp.dot(cache[:, 0:MLA_KV_LORA], wv_ref[...], preferred_element_type=F32), t, nctx)
        tail = lax.broadcasted_iota(jnp.int32, (MLA_VT_ROWS - MLA_V, s_tot), 0)
        for h in range(MLA_HEADS):
            vt_scr[h, MLA_V:MLA_VT_ROWS, :] = jnp.where(tail == 0, 1.0, 0.0).astype(BF16)

    a = _rms(cq_ref[...].astype(F32), qn_ref[...], n=MLA_Q_LORA).astype(BF16)
    q = jnp.dot(a, wq_ref[...], preferred_element_type=F32)
    if has_ctx:
        qrot = jnp.dot(a, wqr_ref[...], preferred_element_type=F32)
        r0 = pl.multiple_of(qi * tq, tq)
        cm = cm_ref[pl.ds(r0, tq), :]
        sm = sm_ref[pl.ds(r0, tq), :]
    scale = (MLA_NOPE + MLA_ROPE) ** -0.5
    def scores_t(h):
        qh = q[:, h * MLA_SLAB:(h + 1) * MLA_SLAB]
        if has_ctx:
            qh = qh * cm + qrot[:, h * MLA_SLAB:(h + 1) * MLA_SLAB] * sm
        qt = (qh * (scale * LOG2_E)).T.astype(BF16)
        return jnp.dot(k_scr[h], qt, preferred_element_type=F32)

    outs = []
    st_next = scores_t(0)
    for h in range(MLA_HEADS):
        st = st_next
        if h + 1 < MLA_HEADS:
            st_next = scores_t(h + 1)
        m = jnp.max(st, axis=0, keepdims=True)
        e = jnp.exp2(st - m).astype(BF16)
        ot = jnp.dot(vt_scr[h], e, preferred_element_type=F32)
        outs.append(ot[0:MLA_V, :] * (1.0 / ot[MLA_V:MLA_V + 1, :]))
    o_ref[...] = jnp.concatenate(outs, axis=0).T


def _mla_call(p3, w, rope_tabs, cache, *, layer, nb, t, tq, want_cache):
    has_ctx = cache is not None
    nctx = cache.shape[2] if has_ctx else 0
    s_tot = t + nctx
    half = MLA_KV_LORA

    args = [p3, p3, p3]
    in_specs = [
        pl.BlockSpec((None, tq, SEG), lambda b, i: (b, i, S_CQ)),
        pl.BlockSpec((None, t, half), lambda b, i: (b, 0, 2 * S_KV)),
        pl.BlockSpec((None, t, half), lambda b, i: (b, 0, 2 * S_KV + 1)),
    ]
    if has_ctx:
        args += [rope_tabs[0], rope_tabs[1], cache]
        in_specs += [
            _const_spec((t, MLA_SLAB)),
            _const_spec((t, MLA_SLAB)),
            pl.BlockSpec((None, None, nctx, MLA_KV_LORA + MLA_ROPE), lambda b, i: (b, layer, 0, 0)),
        ]
    args += [w['mla_qn'], w['mla_kvn'], w['mla_wq']]
    in_specs += [_pick_spec(w['mla_qn'], layer), _pick_spec(w['mla_kvn'], layer), _pick_spec(w['mla_wq'], layer)]
    if has_ctx:
        args += [w['mla_wqr'], w['mla_wkc']]
        in_specs += [_pick_spec(w['mla_wqr'], layer), _pick_spec(w['mla_wkc'], layer)]
    args += [w['mla_wk'], w['mla_wv']]
    in_specs += [_pick_spec(w['mla_wk'], layer), _pick_spec(w['mla_wv'], layer)]
    out_specs = [pl.BlockSpec((None, tq, SEG), lambda b, i: (b, i, 0))]
    out_shape = [jax.ShapeDtypeStruct((nb, t, SEG), F32)]
    if want_cache:
        out_specs.append(pl.BlockSpec((None, t, MLA_KV_LORA), lambda b, i: (b, 0, 0)))
        out_shape.append(jax.ShapeDtypeStruct((nb, t, MLA_KV_LORA), F32))
    return pl.pallas_call(
        functools.partial(_mla_kernel, t=t, tq=tq, nctx=nctx, want_cache=want_cache),
        grid=(nb, t // tq),
        in_specs=in_specs,
        out_specs=out_specs,
        out_shape=out_shape,
        scratch_shapes=[pltpu.VMEM((MLA_HEADS, s_tot, MLA_SLAB), BF16),
                        pltpu.VMEM((MLA_HEADS, MLA_VT_ROWS, s_tot), BF16)],
        compiler_params=_cparams(("arbitrary", "arbitrary")),
        name="mla",
    )(*args)


def _merge_kernel(x_ref, mod_ref, npre_ref, npost_ref, yf_ref, yb_ref, u_ref, ret_ref,
                  cx_ref, cb_ref, cc_ref, cxp_ref, ccp_ref, cxn_ref, ccn_ref, mla_ref,
                  s5d_ref, wglu_ref, cw_ref, cbias_ref, wbr_ref, wg_ref, bg_ref, wo_ref,
                  o_ref, h_scr, *, tm, nt):
    i = pl.program_id(1)
    x = x_ref[...]
    sh = mod_ref[3:4, :]
    sc = mod_ref[4:5, :]
    gt = mod_ref[5:6, :]
    h_scr[...] = (_rms(x, npre_ref[...]) * (1.0 + sc) + sh).astype(BF16)

    y = yf_ref[...] + yb_ref[...] + s5d_ref[...] * u_ref[...].astype(F32)
    z5 = _gelu_tanh(y)
    b_s5 = z5 * _sigmoid(_dot(z5, wglu_ref[...]))

    z = cc_ref[...].astype(F32) * cx_ref[...].astype(F32)
    zprev = ((ccp_ref[...].astype(F32) * cxp_ref[...].astype(F32))[HALO - 1:HALO, :]
             * jnp.where(i > 0, 1.0, 0.0))
    znext = ((ccn_ref[...].astype(F32) * cxn_ref[...].astype(F32))[0:1, :]
             * jnp.where(i < nt - 1, 1.0, 0.0))
    row = lax.broadcasted_iota(jnp.int32, (tm, SEG), 0)
    z_dn = jnp.where(row == 0, zprev, pltpu.roll(z, 1, axis=0))
    z_up = jnp.where(row == tm - 1, znext, pltpu.roll(z, tm - 1, axis=0))
    b_conv = cb_ref[...].astype(F32) * (cbias_ref[...] + z_dn * cw_ref[0:1, :] + z * cw_ref[1:2, :]
                                        + z_up * cw_ref[2:3, :])

    branches = (b_s5, ret_ref[...], b_conv, mla_ref[...])
    merged = jnp.zeros((tm, D_MODEL), F32)
    for n in range(N_BRANCH):
        gate = _sigmoid(jnp.dot(h_scr[...], wg_ref[:, n * D_MODEL:(n + 1) * D_MODEL],
                                preferred_element_type=F32) + bg_ref[:, n * D_MODEL:(n + 1) * D_MODEL])
        merged = merged + gate * _dot(branches[n], wbr_ref[n])
    m = _dot(merged, wo_ref[...])
    o_ref[...] = x + gt * _rms(m, npost_ref[...])


def _merge_call(x, w, y_s5, ret_o, p3, mla_o, *, layer, nb, t, tm, row_of_batch):
    nt = t // tm
    hb = tm // HALO
    n_halo = t // HALO

    def dir_spec(d):
        return pl.BlockSpec((None, None, tm, SEG), lambda b, i: (d, b, i, 0))

    def seg_spec(seg):
        return pl.BlockSpec((None, tm, SEG), lambda b, i: (b, i, seg))

    def prev_spec(seg):
        return pl.BlockSpec((None, HALO, SEG), lambda b, i: (b, jnp.maximum(i * hb - 1, 0), seg))

    def next_spec(seg):
        return pl.BlockSpec((None, HALO, SEG), lambda b, i: (b, jnp.minimum((i + 1) * hb, n_halo - 1), seg))

    return pl.pallas_call(
        functools.partial(_merge_kernel, tm=tm, nt=nt),
        grid=(nb, nt),
        in_specs=[
            pl.BlockSpec((tm, D_MODEL), lambda b, i: (b * nt + i, 0)),
            pl.BlockSpec((None, None, N_MOD, D_MODEL), lambda b, i: (layer, row_of_batch(b), 0, 0)),
            _pick_spec(w['npre'], layer * N_NORM + 1),
            _pick_spec(w['npost'], layer * N_NORM + 1),
            dir_spec(0), dir_spec(1), seg_spec(S_U),
            pl.BlockSpec((None, tm, SEG), lambda b, i: (b, i, 0)),
            seg_spec(S_CX), seg_spec(S_CB), seg_spec(S_CC),
            prev_spec(S_CX), prev_spec(S_CC), next_spec(S_CX), next_spec(S_CC),
            pl.BlockSpec((None, tm, SEG), lambda b, i: (b, i, 0)),
            _pick_spec(w['s5_d'], layer), _pick_spec(w['s5_wglu'], layer), _pick_spec(w['conv_w'], layer),
            _pick_spec(w['conv_b'], layer), _resident_spec(w['w_branch'].shape[1:], (layer,)),
            _resident_spec(w['w_gate'].shape[1:], (layer,)), _pick_spec(w['b_gate'], layer),
            _resident_spec(w['w_o'].shape[1:], (layer,)),
        ],
        out_specs=pl.BlockSpec((tm, D_MODEL), lambda b, i: (b * nt + i, 0)),
        out_shape=jax.ShapeDtypeStruct((nb * t, D_MODEL), F32),
        scratch_shapes=[pltpu.VMEM((tm, D_MODEL), BF16)],
        compiler_params=_cparams(("arbitrary", "arbitrary")),
        name="merge",
    )(x, w['mod'], w['npre'], w['npost'], y_s5, y_s5, p3, ret_o, p3, p3, p3, p3, p3, p3, p3, mla_o,
      w['s5_d'], w['s5_wglu'], w['conv_w'], w['conv_b'], w['w_branch'], w['w_gate'], w['b_gate'], w['w_o'])


def _zeros_like_cols(w, n):
    return jnp.zeros(w.shape[:-1] + (n,), w.dtype)


def _axial_angles(t, dim):
    rows = t // GRID_W
    row = np.repeat(np.arange(rows, dtype=np.float64), GRID_W)
    col = np.tile(np.arange(GRID_W, dtype=np.float64), rows)
    quarter = dim // 4
    inv = ROPE_BASE ** (-np.arange(quarter, dtype=np.float64) / quarter)
    ang = np.concatenate([row[:, None] * inv, col[:, None] * inv], axis=-1)
    return np.cos(ang).astype(np.float32), np.sin(ang).astype(np.float32)


def _rope_tables(t):
    cos, sin = _axial_angles(t, RET_DIM)
    ret_cos = np.tile(np.concatenate([cos, cos], -1), (1, 2))
    ret_sin = np.tile(np.concatenate([sin, sin], -1), (1, 2))
    cos, sin = _axial_angles(t, MLA_ROPE)
    pad = MLA_SLAB - MLA_NOPE - MLA_ROPE
    mla_cos = np.concatenate([np.ones((t, MLA_NOPE), np.float32), cos, cos, np.zeros((t, pad), np.float32)], -1)
    mla_sin = np.concatenate([np.zeros((t, MLA_NOPE), np.float32), sin, sin, np.zeros((t, pad), np.float32)], -1)
    return ((jnp.asarray(ret_cos), jnp.asarray(ret_sin)), (jnp.asarray(mla_cos), jnp.asarray(mla_sin)))


def _place_rope_key(w):
    return jnp.concatenate([_zeros_like_cols(w, MLA_NOPE), w,
                            _zeros_like_cols(w, MLA_SLAB - MLA_NOPE - MLA_ROPE)], -1)


def _in_weights(w_in):
    sizes = (BRANCH_WIDTH,) * 8 + (MLA_Q_LORA, MLA_KV_LORA, MLA_ROPE)
    offs = np.cumsum((0,) + sizes)
    u, rq, rk, rv, rg, cx, cb, cc, cq, ckv, kr = [w_in[..., offs[i]:offs[i + 1]] for i in range(11)]
    rk = rk * (RET_DIM ** -0.5)
    cols = [u, rq, rk, rv, rg, cx, cb, cc,
            jnp.concatenate([cq, _zeros_like_cols(cq, SEG - MLA_Q_LORA)], -1),
            jnp.concatenate([ckv, _place_rope_key(kr)], -1)]
    return jnp.concatenate([c.astype(BF16) for c in cols], axis=-1)


def _mla_weights(w_uq, w_ukv):
    nl = w_uq.shape[0]
    dq = MLA_NOPE + MLA_ROPE
    w4 = w_uq.reshape(nl, MLA_Q_LORA, MLA_HEADS, dq)
    nope, ropew = w4[..., :MLA_NOPE], w4[..., MLA_NOPE:]
    half = MLA_ROPE // 2
    rot = jnp.concatenate([-ropew[..., half:], ropew[..., :half]], -1)
    zpad = jnp.zeros((nl, MLA_Q_LORA, MLA_HEADS, MLA_SLAB - dq), F32)
    wq = jnp.concatenate([nope, ropew, zpad], -1).reshape(nl, MLA_Q_LORA, MLA_HEADS * MLA_SLAB)
    wqr = jnp.concatenate([jnp.zeros_like(nope), rot, zpad], -1).reshape(nl, MLA_Q_LORA, MLA_HEADS * MLA_SLAB)
    rpad = jnp.zeros((nl, SEG - MLA_Q_LORA, MLA_HEADS * MLA_SLAB), F32)
    wq = jnp.concatenate([wq, rpad], 1).astype(BF16)
    wqr = jnp.concatenate([wqr, rpad], 1).astype(BF16)
    kv4 = w_ukv.reshape(nl, MLA_KV_LORA, MLA_HEADS, MLA_NOPE + MLA_V)
    wk = jnp.concatenate([kv4[..., :MLA_NOPE], jnp.zeros((nl, MLA_KV_LORA, MLA_HEADS, MLA_SLAB - MLA_NOPE), F32)],
                         -1).reshape(nl, MLA_KV_LORA, MLA_HEADS * MLA_SLAB)
    place = np.zeros((MLA_ROPE, MLA_HEADS, MLA_SLAB), np.float32)
    place[np.arange(MLA_ROPE), :, MLA_NOPE + np.arange(MLA_ROPE)] = 1.0
    place = jnp.broadcast_to(jnp.asarray(place.reshape(MLA_ROPE, MLA_HEADS * MLA_SLAB)),
                             (nl, MLA_ROPE, MLA_HEADS * MLA_SLAB))
    wkc = jnp.concatenate([wk, place], 1).astype(BF16)
    wv = kv4[..., MLA_NOPE:].reshape(nl, MLA_KV_LORA, MLA_HEADS * MLA_V).astype(BF16)
    return wq, wqr, wk.astype(BF16), wkc, wv


def _s5_tables(lam_re, lam_im, log_dt, b_re, b_im, c_re, c_im):
    nl = lam_re.shape[0]
    dt = jnp.exp(log_dt)[..., None]
    mag = jnp.exp(lam_re * dt)
    ab_re = mag * jnp.cos(lam_im * dt)
    ab_im = mag * jnp.sin(lam_im * dt)
    den = lam_re * lam_re + lam_im * lam_im
    f_re = ((ab_re - 1.0) * lam_re + ab_im * lam_im) / den
    f_im = (ab_im * lam_re - (ab_re - 1.0) * lam_im) / den
    bb_re = f_re[..., None] * b_re - f_im[..., None] * b_im
    bb_im = f_re[..., None] * b_im + f_im[..., None] * b_re
    eye = jnp.eye(S5_GROUPS, dtype=F32)
    blk_in = lambda m: jnp.einsum('ldgph,gk->ldghkp', m, eye).reshape(nl, 2, BRANCH_WIDTH, S5_LANES)
    bblk = jnp.concatenate([blk_in(bb_re), blk_in(bb_im)], -1).astype(BF16)
    blk_out = lambda m: jnp.einsum('lghp,gk->lgpkh', m, eye).reshape(nl, S5_LANES, BRANCH_WIDTH)
    cblk = jnp.concatenate([blk_out(c_re), -blk_out(c_im)], 1).astype(BF16)
    a_tab = jnp.stack([ab_re.reshape(nl, 2, S5_LANES), ab_im.reshape(nl, 2, S5_LANES)], axis=2)
    return bblk, a_tab, cblk


def _ret_tables(decay_logit, c):
    lg = jax.nn.log_sigmoid(decay_logit.astype(F32))
    lf = lg[:, 0][:, :, None, None]
    lb = lg[:, 1][:, :, None, None]
    idx = jnp.arange(c, dtype=F32)
    diff = idx[:, None] - idx[None, :]
    dm = jnp.where(diff >= 0, jnp.exp(lf * jnp.maximum(diff, 0.0)), jnp.exp(lb * jnp.maximum(-diff, 0.0)))
    lfl = jnp.repeat(lg[:, 0], RET_DIM, axis=-1)
    lbl = jnp.repeat(lg[:, 1], RET_DIM, axis=-1)
    qdf = jnp.exp(lfl[:, None, :] * (idx[:, None] + 1.0))
    qdb = jnp.exp(lbl[:, None, :] * (c - idx[:, None]))
    kdft = jnp.exp(lfl[:, :, None] * (c - 1.0 - idx[None, :]))
    kdbt = jnp.exp(lbl[:, :, None] * idx[None, :])
    nl = lg.shape[0]
    cdf = jnp.broadcast_to(jnp.exp(lfl * c)[:, :, None], (nl, SEG, SEG))
    cdb = jnp.broadcast_to(jnp.exp(lbl * c)[:, :, None], (nl, SEG, SEG))
    head = np.arange(SEG) // RET_DIM
    hm = jnp.asarray((head[None, :] == np.arange(RET_HEADS)[:, None]).astype(np.float32))
    bd_np = (head[:, None] == head[None, :]).astype(np.float32)
    bd = jnp.asarray(bd_np)
    gm = jnp.asarray(bd_np / RET_DIM, BF16)
    return (dm, qdf, qdb, kdft, kdbt, cdf, cdb), (hm, bd, gm)


def _layer(x, w, *, layer, nb, t, tm, tm_ffn, row_of_batch, ctx, rope_tabs):
    is_ctx_pass = ctx is None
    row_of_tile = lambda i: row_of_batch((i * tm_ffn) // t)
    x = _ffn_call(x, w, layer=layer, k=0, tm=tm_ffn, row_of_tile=row_of_tile)

    p = _in_call(x, w, layer=layer, nb=nb, t=t, tm=tm, row_of_batch=row_of_batch)
    p3 = p.reshape(nb, t, NSEG * SEG)

    if is_ctx_pass:
        s5_0, s0_layer = jnp.zeros((1, 2, 2, nb, S5_LANES), F32), 0
        ret_0, mla_cache, ret_rope, mla_rope = None, None, None, None
    else:
        s5_0, ret_0, mla_cache = ctx
        s0_layer = layer
        ret_rope, mla_rope = rope_tabs
    s5_out = _s5_call(p3, w, s5_0, layer=layer, s0_layer=s0_layer, nb=nb, t=t, want_fin=is_ctx_pass)
    ret_out = _ret_call(p3, w, ret_0, ret_rope, layer=layer, nb=nb, t=t, want_fin=is_ctx_pass)
    mla_out = _mla_call(p3, w, mla_rope, mla_cache, layer=layer, nb=nb, t=t, tq=min(t, 512),
                        want_cache=is_ctx_pass)

    x = _merge_call(x, w, s5_out[0], ret_out[0], p3, mla_out[0], layer=layer, nb=nb, t=t, tm=tm,
                    row_of_batch=row_of_batch)
    x = _ffn_call(x, w, layer=layer, k=2, tm=tm_ffn, row_of_tile=row_of_tile)

    state = None
    if is_ctx_pass:
        kr = p3[:, :, S_KV * SEG + MLA_KV_LORA + MLA_NOPE:S_KV * SEG + MLA_KV_LORA + MLA_NOPE + MLA_ROPE]
        cache = jnp.concatenate([mla_out[1], kr.astype(F32)], axis=-1)
        state = (s5_out[1], ret_out[1], cache)
    return x, state


def kernel(x_prompt, x_sample, state_s5, state_ret, cache_mla, c, c_ctx, w_mod, b_mod, norm_pre, norm_post, ffn_w1, ffn_w3, ffn_w2, w_in, s5_lam_re, s5_lam_im, s5_log_dt, s5_b_re, s5_b_im, s5_c_re, s5_c_im, s5_d, s5_w_glu, ret_decay, ret_gn, conv_w, conv_b, mla_q_norm, mla_w_uq, mla_kv_norm, mla_w_ukv, w_branch, w_gate, b_gate, w_o):
    bp, tp, _ = x_prompt.shape
    bs, ts, _ = x_sample.shape
    assert 1 + bs <= MOD_ROWS

    cond = jnp.concatenate([c_ctx[None, :], c, jnp.zeros((MOD_ROWS - 1 - bs, D_MODEL), F32)], axis=0)
    mod = _mod_call(cond, w_mod, b_mod).reshape(DEPTH, MOD_ROWS, N_MOD, D_MODEL)

    wq, wqr, wk, wkc, wv = _mla_weights(mla_w_uq, mla_w_ukv)
    bblk, a_tab, cblk = _s5_tables(s5_lam_re, s5_lam_im, s5_log_dt, s5_b_re, s5_b_im, s5_c_re, s5_c_im)
    ret_layer_tabs, ret_const_tabs = _ret_tables(ret_decay, RET_CHUNK)
    w = dict(
        mod=mod,
        npre=norm_pre.reshape(DEPTH * N_NORM, 1, D_MODEL),
        npost=norm_post.reshape(DEPTH * N_NORM, 1, D_MODEL),
        ffn=(ffn_w1.astype(BF16), ffn_w3.astype(BF16), ffn_w2.astype(BF16)),
        w_in=_in_weights(w_in),
        s5_bblk=bblk, s5_a=a_tab, s5_cblk=cblk,
        s5_d=s5_d[:, None, :], s5_wglu=s5_w_glu.astype(BF16),
        ret_layer_tabs=ret_layer_tabs, ret_const_tabs=ret_const_tabs, ret_gn=ret_gn[:, None, :],
        conv_w=conv_w, conv_b=conv_b[:, None, :],
        mla_qn=jnp.concatenate([mla_q_norm, jnp.zeros((DEPTH, SEG - MLA_Q_LORA), F32)], -1)[:, None, :],
        mla_kvn=mla_kv_norm[:, None, :],
        mla_wq=wq, mla_wqr=wqr, mla_wk=wk, mla_wkc=wkc, mla_wv=wv,
        w_branch=w_branch.astype(BF16), w_gate=w_gate.astype(BF16), b_gate=b_gate[:, None, :],
        w_o=w_o.astype(BF16),
    )
    rope_tabs = _rope_tables(ts)
    s5_0 = state_s5.transpose(1, 2, 5, 0, 3, 4).reshape(DEPTH, 2, 2, bs, S5_LANES)

    xp = x_prompt.reshape(bp * tp, D_MODEL)
    xs = x_sample.reshape(bs * ts, D_MODEL)
    s5_list, ret_list, mla_list = [], [], []
    for l in range(DEPTH):
        xp, (s5_s, ret_s, mla_c) = _layer(xp, w, layer=l, nb=bp, t=tp, tm=256, tm_ffn=512,
                                          row_of_batch=lambda b: 0, ctx=None, rope_tabs=None)
        s5_list.append(s5_s)
        ret_list.append(ret_s)
        mla_list.append(mla_c)
        xs, _ = _layer(xs, w, layer=l, nb=bs, t=ts, tm=512, tm_ffn=1024, row_of_batch=lambda b: 1 + b,
                       ctx=(s5_0, state_ret, cache_mla), rope_tabs=rope_tabs)
    new_s5 = jnp.stack(s5_list, 0).reshape(DEPTH, 2, 2, bp, S5_GROUPS, S5_STATE).transpose(3, 0, 1, 4, 5, 2)
    return (xp.reshape(bp, tp, D_MODEL), xs.reshape(bs, ts, D_MODEL),
            new_s5, jnp.stack(ret_list, axis=1), jnp.stack(mla_list, axis=1))
```

```python
import functools

import numpy as np
import jax
import jax.numpy as jnp
from jax import lax
from jax.experimental import pallas as pl
from jax.experimental.pallas import tpu as pltpu

F32 = jnp.float32
BF16 = jnp.bfloat16

D_MODEL = 1024
DEPTH = 2
GRID_W = 64
EPS = 1e-6
LOG2_E = float(np.log2(np.e))
ROPE_BASE = 10000.0
N_BRANCH = 4
BRANCH_WIDTH = D_MODEL // 4
N_MOD = 9
N_NORM = 3
D_FF = 2816
S5_GROUP = 16
S5_GROUPS = BRANCH_WIDTH // S5_GROUP
S5_STATE = 64
S5_LANES = S5_GROUPS * S5_STATE
RET_HEADS = 4
RET_DIM = BRANCH_WIDTH // RET_HEADS
CONV_K = 3
MLA_HEADS = 4
MLA_Q_LORA = 192
MLA_KV_LORA = 128
MLA_NOPE = 64
MLA_ROPE = 32
MLA_V = BRANCH_WIDTH // MLA_HEADS
MLA_SLAB = 128
MLA_VT_ROWS = MLA_V + 16

SUBLANES = 8
S5_SUBCHUNKS = 4
S5_ROWS = 512
SEG = 256
FF_CHUNK = 256
N_FF_CHUNKS = D_FF // FF_CHUNK
MOD_ROWS = 16
HALO = 16
RET_CHUNK = 256
MLA_TQ = 512
TM_CTX, TM_CTX_FFN = 256, 512
TM_LAT, TM_LAT_FFN = 512, 1024
V7X_VMEM_LIMIT = 56 * 1024 * 1024

S_U, S_RQ, S_RK, S_RV, S_RG, S_CX, S_CB, S_CC, S_CQ, S_KV = range(10)
NSEG = 10
LANES = 128


def _cparams(sem):
    return pltpu.CompilerParams(dimension_semantics=sem, vmem_limit_bytes=V7X_VMEM_LIMIT)


def _dot(a, b):
    return jnp.dot(a.astype(BF16), b.astype(BF16), preferred_element_type=F32)


def _sigmoid(x):
    return 1.0 / (1.0 + jnp.exp(-x))


def _rms(x, g, n=None):
    n = x.shape[-1] if n is None else n
    ms = jnp.sum(x * x, axis=-1, keepdims=True) * (1.0 / n)
    return x * lax.rsqrt(ms + EPS) * g


def _rot_half_lanes(x, width):
    half = width // 2
    lane = lax.broadcasted_iota(jnp.int32, (x.shape[0], LANES), 1)
    first = (lane & (width - 1)) < half
    outs = []
    for j in range(x.shape[1] // LANES):
        v = x[:, j * LANES:(j + 1) * LANES]
        outs.append(jnp.where(first, -pltpu.roll(v, LANES - half, axis=1), pltpu.roll(v, half, axis=1)))
    return outs[0] if len(outs) == 1 else jnp.concatenate(outs, axis=-1)


def _gelu_tanh(x):
    return 0.5 * x * (1.0 + jnp.tanh(np.sqrt(2.0 / np.pi) * (x + 0.044715 * (x * x * x))))


def _const_spec(shape):
    nd = len(shape)
    return pl.BlockSpec(shape, lambda *_: (0,) * nd)


def _pick_spec(arr, *lead):
    shape = arr.shape[len(lead):]
    return pl.BlockSpec((None,) * len(lead) + tuple(shape), lambda *_: tuple(lead) + (0,) * len(shape))


def _resident_spec(shape, lead=()):
    nd = len(shape)
    return pl.BlockSpec((None,) * len(lead) + tuple(shape), lambda *_: tuple(lead) + (0,) * nd,
                        pipeline_mode=pl.Buffered(1))


def _mod_kernel(c_ref, w_ref, b_ref, o_ref):
    c = c_ref[...]
    o_ref[...] = _dot(c * _sigmoid(c), w_ref[...]) + b_ref[...]


def _mod_call(cond, w_mod, b_mod):
    ncol = N_MOD * D_MODEL
    tn = D_MODEL
    return pl.pallas_call(
        _mod_kernel,
        grid=(DEPTH, ncol // tn),
        in_specs=[
            pl.BlockSpec((MOD_ROWS, D_MODEL), lambda l, j: (0, 0)),
            pl.BlockSpec((None, D_MODEL, tn), lambda l, j: (l, 0, j)),
            pl.BlockSpec((None, 1, tn), lambda l, j: (l, 0, j)),
        ],
        out_specs=pl.BlockSpec((None, MOD_ROWS, tn), lambda l, j: (l, 0, j)),
        out_shape=jax.ShapeDtypeStruct((DEPTH, MOD_ROWS, ncol), F32),
        compiler_params=_cparams(("arbitrary", "arbitrary")),
        name="mod",
    )(cond, w_mod, b_mod.reshape(DEPTH, 1, ncol))


def _ffn_kernel(x_ref, mod_ref, npre_ref, npost_ref, w1_ref, w3_ref, w2_ref, o_ref,
                h_scr, g_scr, *, k):
    x = x_ref[...]
    sh = mod_ref[3 * k:3 * k + 1, :]
    sc = mod_ref[3 * k + 1:3 * k + 2, :]
    gt = mod_ref[3 * k + 2:3 * k + 3, :]
    h_scr[...] = (_rms(x, npre_ref[...]) * (1.0 + sc) + sh).astype(BF16)
    for j in range(N_FF_CHUNKS):
        cols = slice(j * FF_CHUNK, (j + 1) * FF_CHUNK)
        a = jnp.dot(h_scr[...], w1_ref[:, cols], preferred_element_type=F32)
        b = jnp.dot(h_scr[...], w3_ref[:, cols], preferred_element_type=F32)
        g_scr[:, cols] = (a * _sigmoid(a) * b).astype(BF16)
    f = jnp.dot(g_scr[...], w2_ref[...], preferred_element_type=F32)
    o_ref[...] = x + (0.5 * gt) * _rms(f, npost_ref[...])


def _ffn_call(x, w, *, layer, k, tm, row_of_tile):
    n = x.shape[0]
    lj = (layer, k // 2)
    w1, w3, w2 = w['ffn']
    return pl.pallas_call(
        functools.partial(_ffn_kernel, k=k),
        grid=(n // tm,),
        in_specs=[
            pl.BlockSpec((tm, D_MODEL), lambda i: (i, 0)),
            pl.BlockSpec((None, None, N_MOD, D_MODEL), lambda i: (layer, row_of_tile(i), 0, 0)),
            _pick_spec(w['npre'], layer * N_NORM + k),
            _pick_spec(w['npost'], layer * N_NORM + k),
            _resident_spec((D_MODEL, D_FF), lj),
            _resident_spec((D_MODEL, D_FF), lj),
            _resident_spec((D_FF, D_MODEL), lj),
        ],
        out_specs=pl.BlockSpec((tm, D_MODEL), lambda i: (i, 0)),
        out_shape=jax.ShapeDtypeStruct((n, D_MODEL), F32),
        scratch_shapes=[pltpu.VMEM((tm, D_MODEL), BF16), pltpu.VMEM((tm, D_FF), BF16)],
        compiler_params=_cparams(("arbitrary",)),
        name="ffn",
    )(x, w['mod'], w['npre'], w['npost'], w1, w3, w2)


def _in_kernel(x_ref, mod_ref, npre_ref, w_ref, p_ref, h_scr):
    sh = mod_ref[3:4, :]
    sc = mod_ref[4:5, :]
    h_scr[...] = (_rms(x_ref[...], npre_ref[...]) * (1.0 + sc) + sh).astype(BF16)
    for j in range(NSEG):
        cols = slice(j * SEG, (j + 1) * SEG)
        p_ref[:, cols] = jnp.dot(h_scr[...], w_ref[:, cols], preferred_element_type=F32).astype(BF16)


def _in_call(x, w, *, layer, nb, t, tm, row_of_batch):
    tt = t // tm
    nseg = NSEG
    return pl.pallas_call(
        _in_kernel,
        grid=(nb, tt),
        in_specs=[
            pl.BlockSpec((tm, D_MODEL), lambda b, i: (b * tt + i, 0)),
            pl.BlockSpec((None, None, N_MOD, D_MODEL), lambda b, i: (layer, row_of_batch(b), 0, 0)),
            _pick_spec(w['npre'], layer * N_NORM + 1),
            _resident_spec((D_MODEL, nseg * SEG), (layer,)),
        ],
        out_specs=pl.BlockSpec((tm, nseg * SEG), lambda b, i: (b * tt + i, 0)),
        out_shape=jax.ShapeDtypeStruct((nb * t, nseg * SEG), BF16),
        scratch_shapes=[pltpu.VMEM((tm, D_MODEL), BF16)],
        compiler_params=_cparams(("arbitrary", "arbitrary")),
        name="in_proj",
    )(x, w['mod'], w['npre'], w['w_in'])


def _s5_kernel(*refs, nb, steps, want_fin):
    if want_fin:
        u_ref, bb_ref, a_ref, cc_ref, s0_ref, y_ref, fin_ref, s_scr, bu_scr, u2_scr, y2_scr = refs
    else:
        u_ref, bb_ref, a_ref, cc_ref, s0_ref, y_ref, s_scr, bu_scr, u2_scr, y2_scr = refs
        fin_ref = None
    d = pl.program_id(0)
    c = pl.program_id(1)
    span = steps * S5_SUBCHUNKS
    ngrp = nb // SUBLANES

    @pl.when(c == 0)
    def _():
        s_scr[...] = s0_ref[...]

    u_all = u_ref[...].astype(F32).reshape(nb * span, SEG)
    for j in range(SEG // LANES):
        u2_scr[j] = u_all[:, j * LANES:(j + 1) * LANES]

    a_re = jnp.broadcast_to(a_ref[0:1, :], (SUBLANES, S5_LANES))
    a_im = jnp.broadcast_to(a_ref[1:2, :], (SUBLANES, S5_LANES))
    s_re = [s_scr[0, g * SUBLANES:(g + 1) * SUBLANES, :] for g in range(ngrp)]
    s_im = [s_scr[1, g * SUBLANES:(g + 1) * SUBLANES, :] for g in range(ngrp)]
    re_cols = slice(0, S5_LANES)
    im_cols = slice(S5_LANES, 2 * S5_LANES)

    def time_of(k, i):
        j = k * steps + i
        return jnp.where(d == 0, j, span - 1 - j)

    def project_in(k):
        u_tb = jnp.concatenate(
            [jnp.concatenate([u2_scr[j, pl.ds(time_of(k, i), nb, stride=span), :]
                              for j in range(SEG // LANES)], axis=-1) for i in range(steps)], axis=0)
        bu_scr[k] = _dot(u_tb, bb_ref[...])

    project_in(0)
    for k in range(S5_SUBCHUNKS):
        if k + 1 < S5_SUBCHUNKS:
            project_in(k + 1)
        for i in range(steps):
            for g in range(ngrp):
                rr = slice(i * nb + g * SUBLANES, i * nb + (g + 1) * SUBLANES)
                n_re = a_re * s_re[g] - a_im * s_im[g] + bu_scr[k, rr, re_cols]
                n_im = a_re * s_im[g] + a_im * s_re[g] + bu_scr[k, rr, im_cols]
                bu_scr[k, rr, re_cols] = n_re
                bu_scr[k, rr, im_cols] = n_im
                s_re[g], s_im[g] = n_re, n_im
        y_tb = _dot(bu_scr[k], cc_ref[...])
        for i in range(steps):
            for j in range(SEG // LANES):
                y2_scr[j, pl.ds(time_of(k, i), nb, stride=span), :] = (
                    y_tb[i * nb:(i + 1) * nb, j * LANES:(j + 1) * LANES])
    y_ref[...] = jnp.concatenate([y2_scr[j] for j in range(SEG // LANES)], axis=-1).reshape(nb, span, SEG)
    for g in range(ngrp):
        s_scr[0, g * SUBLANES:(g + 1) * SUBLANES, :] = s_re[g]
        s_scr[1, g * SUBLANES:(g + 1) * SUBLANES, :] = s_im[g]
    if want_fin:
        fin_ref[...] = s_scr[...]


def _s5_call(p3, w, s0, *, layer, s0_layer, nb, t, want_fin):
    steps = S5_ROWS // nb
    span = steps * S5_SUBCHUNKS
    nch = t // span
    rows = S5_ROWS

    def chunk(d, c):
        return jnp.where(d == 0, c, nch - 1 - c)

    out_specs = [pl.BlockSpec((None, nb, span, SEG), lambda d, c: (d, 0, chunk(d, c), 0))]
    out_shape = [jax.ShapeDtypeStruct((2, nb, t, SEG), F32)]
    if want_fin:
        out_specs.append(pl.BlockSpec((None, 2, nb, S5_LANES), lambda d, c: (d, 0, 0, 0)))
        out_shape.append(jax.ShapeDtypeStruct((2, 2, nb, S5_LANES), F32))
    return pl.pallas_call(
        functools.partial(_s5_kernel, nb=nb, steps=steps, want_fin=want_fin),
        grid=(2, nch),
        in_specs=[
            pl.BlockSpec((nb, span, SEG), lambda d, c: (0, chunk(d, c), S_U)),
            pl.BlockSpec((None, None, SEG, 2 * S5_LANES), lambda d, c: (layer, d, 0, 0)),
            pl.BlockSpec((None, None, 2, S5_LANES), lambda d, c: (layer, d, 0, 0)),
            _pick_spec(w['s5_cblk'], layer),
            pl.BlockSpec((None, None, 2, nb, S5_LANES), lambda d, c: (s0_layer, d, 0, 0, 0)),
        ],
        out_specs=out_specs,
        out_shape=out_shape,
        scratch_shapes=[pltpu.VMEM((2, nb, S5_LANES), F32),
                        pltpu.VMEM((S5_SUBCHUNKS, rows, 2 * S5_LANES), F32),
                        pltpu.VMEM((SEG // LANES, nb * span, LANES), F32),
                        pltpu.VMEM((SEG // LANES, nb * span, LANES), F32)],
        compiler_params=_cparams(("arbitrary", "arbitrary")),
        name="s5_scan",
    )(p3, w['s5_bblk'], w['s5_a'], w['s5_cblk'], s0)


def _ret_kernel(*refs, rope, has_s0, want_fin, nch):
    refs = list(refs)
    q_ref, k_ref, v_ref, g_ref = refs[:4]
    del refs[:4]
    if rope:
        cos_ref, sin_ref = refs[:2]
        del refs[:2]
    (dm_ref, qdf_ref, qdb_ref, kdft_ref, kdbt_ref, cdf_ref, cdb_ref, hm_ref, bd_ref, gm_ref,
     gn_ref) = refs[:11]
    del refs[:11]
    if has_s0:
        s0_ref = refs.pop(0)
    o_ref = refs.pop(0)
    fin_ref = refs.pop(0) if want_fin else None
    sb_all, sf_scr, sb_scr = refs

    p = pl.program_id(1)
    c = pl.program_id(2)
    hd = RET_DIM

    def init_state(dst, d):
        dst[...] = jnp.zeros_like(dst)
        if has_s0:
            for h in range(RET_HEADS):
                dst[h * hd:(h + 1) * hd, h * hd:(h + 1) * hd] = s0_ref[d, h]

    def write_final(src, d):
        for h in range(RET_HEADS):
            fin_ref[d, h] = src[h * hd:(h + 1) * hd, h * hd:(h + 1) * hd]

    def rope_tabs():
        cos = cos_ref[...]
        sin = sin_ref[...]
        return jnp.concatenate([cos, cos], axis=-1), jnp.concatenate([sin, sin], axis=-1)

    def roped_kt():
        k = k_ref[...].astype(F32)
        if rope:
            cos, sin = rope_tabs()
            k = k * cos + _rot_half_lanes(k, RET_DIM) * sin
        return k.T

    @pl.when(p == 0)
    def _():
        ck = nch - 1 - c

        @pl.when(c == 0)
        def _():
            init_state(sb_scr, 1)

        kt = roped_kt()
        s_old = sb_scr[...]
        sb_all[ck] = s_old
        sb_scr[...] = s_old * cdb_ref[...] + _dot(kt * kdbt_ref[...], v_ref[...]) * bd_ref[...]
        if want_fin:
            @pl.when(c == nch - 1)
            def _():
                write_final(sb_scr, 1)

    @pl.when(p == 1)
    def _():
        @pl.when(c == 0)
        def _():
            init_state(sf_scr, 0)

        q = q_ref[...].astype(F32)
        if rope:
            cos, sin = rope_tabs()
            q = q * cos + _rot_half_lanes(q, RET_DIM) * sin
        kt = roped_kt()
        ktb = kt.astype(BF16)
        vb = v_ref[...]
        s_f = sf_scr[...]
        o = _dot(q, s_f) * qdf_ref[...] + _dot(q, sb_all[c]) * qdb_ref[...]
        for h in range(RET_HEADS):
            hm = hm_ref[h:h + 1, :]
            att = _dot(q * hm, ktb) * dm_ref[h]
            o = o + _dot(att, vb) * hm
        sf_scr[...] = s_f * cdf_ref[...] + _dot(kt * kdft_ref[...], vb) * bd_ref[...]
        hi = o.astype(BF16)
        lo = (o - hi.astype(F32)).astype(BF16)
        mu = (jnp.dot(hi, gm_ref[...], preferred_element_type=F32)
              + jnp.dot(lo, gm_ref[...], preferred_element_type=F32))
        oc = o - mu
        var = _dot(oc * oc, gm_ref[...])
        on = oc * lax.rsqrt(var + EPS) * gn_ref[...]
        g = g_ref[...].astype(F32)
        o_ref[...] = ((g * _sigmoid(g)) * on).astype(o_ref.dtype)
        if want_fin:
            @pl.when(c == nch - 1)
            def _():
                write_final(sf_scr, 0)


def _ret_call(p3, w, s0, rope_tabs, *, layer, nb, t, want_fin):
    tc = RET_CHUNK
    nch = t // tc
    rope = rope_tabs is not None
    has_s0 = s0 is not None

    def kv_chunk(p, c):
        return jnp.where(p == 0, nch - 1 - c, c)

    def q_chunk(p, c):
        return jnp.where(p == 0, 0, c)

    def seg_spec(seg, cmap):
        return pl.BlockSpec((None, tc, SEG), lambda b, p, c: (b, cmap(p, c), seg))

    args = [p3, p3, p3, p3]
    in_specs = [seg_spec(S_RQ, q_chunk), seg_spec(S_RK, kv_chunk), seg_spec(S_RV, kv_chunk),
                seg_spec(S_RG, q_chunk)]
    if rope:
        args += [rope_tabs[0], rope_tabs[1]]
        in_specs += [pl.BlockSpec((tc, 2 * RET_DIM), lambda b, p, c: (kv_chunk(p, c), 0)),
                     pl.BlockSpec((tc, 2 * RET_DIM), lambda b, p, c: (kv_chunk(p, c), 0))]
    args += list(w['ret_layer_tabs']) + list(w['ret_const_tabs']) + [w['ret_gn']]
    in_specs += ([_pick_spec(a, layer) for a in w['ret_layer_tabs']]
                 + [_const_spec(a.shape) for a in w['ret_const_tabs']] + [_pick_spec(w['ret_gn'], layer)])
    if has_s0:
        args.append(s0)
        in_specs.append(pl.BlockSpec((None, None, 2, RET_HEADS, RET_DIM, RET_DIM),
                                     lambda b, p, c: (b, layer, 0, 0, 0, 0)))
    out_specs = [pl.BlockSpec((None, tc, SEG), lambda b, p, c: (b, q_chunk(p, c), 0))]
    out_shape = [jax.ShapeDtypeStruct((nb, t, SEG), BF16)]
    if want_fin:
        out_specs.append(pl.BlockSpec((None, 2, RET_HEADS, RET_DIM, RET_DIM),
                                      lambda b, p, c: (b, 0, 0, 0, 0)))
        out_shape.append(jax.ShapeDtypeStruct((nb, 2, RET_HEADS, RET_DIM, RET_DIM), F32))
    return pl.pallas_call(
        functools.partial(_ret_kernel, rope=rope, has_s0=has_s0, want_fin=want_fin, nch=nch),
        grid=(nb, 2, nch),
        in_specs=in_specs,
        out_specs=out_specs,
        out_shape=out_shape,
        scratch_shapes=[pltpu.VMEM((nch, SEG, SEG), F32),
                        pltpu.VMEM((SEG, SEG), F32),
                        pltpu.VMEM((SEG, SEG), F32)],
        compiler_params=_cparams(("arbitrary", "arbitrary", "arbitrary")),
        name="retention",
    )(*args)


def _mla_kernel(*refs, t, tq, nctx, want_cache):
    refs = list(refs)
    cq_ref, ckv_ref, krp_ref = refs[:3]
    del refs[:3]
    has_ctx = nctx > 0
    if has_ctx:
        cm_ref, sm_ref, cache_ref = refs[:3]
        del refs[:3]
    qn_ref, kvn_ref, wq_ref = refs[:3]
    del refs[:3]
    if has_ctx:
        wqr_ref, wkc_ref = refs[:2]
        del refs[:2]
    wk_ref, wv_ref = refs[:2]
    del refs[:2]
    o_ref = refs.pop(0)
    lat_ref = refs.pop(0) if want_cache else None
    k_scr, vt_scr = refs
    qi = pl.program_id(1)
    s_tot = t + nctx

    def store_vt(v, c0, n):
        vt = v.T
        for h in range(MLA_HEADS):
            vt_scr[h, 0:MLA_V, c0:c0 + n] = vt[h * MLA_V:(h + 1) * MLA_V, :].astype(BF16)

    @pl.when(qi == 0)
    def _():
        lat = _rms(ckv_ref[...].astype(F32), kvn_ref[...])
        if want_cache:
            lat_ref[...] = lat
        latb = lat.astype(BF16)
        kr = krp_ref[...].astype(F32)
        if has_ctx:
            kr = kr * cm_ref[...] + _rot_half_lanes(kr, MLA_ROPE) * sm_ref[...]
        kk = jnp.dot(latb, wk_ref[...], preferred_element_type=F32)
        for h in range(MLA_HEADS):
            k_scr[h, 0:t, :] = (kk[:, h * MLA_SLAB:(h + 1) * MLA_SLAB] + kr).astype(BF16)
        store_vt(jnp.dot(latb, wv_ref[...], preferred_element_type=F32), 0, t)
        if has_ctx:
            cache = cache_ref[...].astype(BF16)
            kkc = jnp.dot(cache, wkc_ref[...], preferred_element_type=F32)
            for h in range(MLA_HEADS):
                k_scr[h, t:t + nctx, :] = kkc[:, h * MLA_SLAB:(h + 1) * MLA_SLAB].astype(BF16)
            store_vt(jnp.dot(cache[:, 0:MLA_KV_LORA], wv_ref[...], preferred_element_type=F32), t, nctx)
        tail = lax.broadcasted_iota(jnp.int32, (MLA_VT_ROWS - MLA_V, s_tot), 0)
        for h in range(MLA_HEADS):
            vt_scr[h, MLA_V:MLA_VT_ROWS, :] = jnp.where(tail == 0, 1.0, 0.0).astype(BF16)

    a = _rms(cq_ref[...].astype(F32), qn_ref[...], n=MLA_Q_LORA).astype(BF16)
    q = jnp.dot(a, wq_ref[...], preferred_element_type=F32)
    if has_ctx:
        qrot = jnp.dot(a, wqr_ref[...], preferred_element_type=F32)
        r0 = pl.multiple_of(qi * tq, tq)
        cm = cm_ref[pl.ds(r0, tq), :]
        sm = sm_ref[pl.ds(r0, tq), :]
    scale = (MLA_NOPE + MLA_ROPE) ** -0.5
    def scores_t(h):
        qh = q[:, h * MLA_SLAB:(h + 1) * MLA_SLAB]
        if has_ctx:
            qh = qh * cm + qrot[:, h * MLA_SLAB:(h + 1) * MLA_SLAB] * sm
        qt = (qh * (scale * LOG2_E)).T.astype(BF16)
        return jnp.dot(k_scr[h], qt, preferred_element_type=F32)

    outs = []
    st_next = scores_t(0)
    for h in range(MLA_HEADS):
        st = st_next
        if h + 1 < MLA_HEADS:
            st_next = scores_t(h + 1)
        m = jnp.max(st, axis=0, keepdims=True)
        e = jnp.exp2(st - m).astype(BF16)
        ot = jnp.dot(vt_scr[h], e, preferred_element_type=F32)
        outs.append(ot[0:MLA_V, :] * (1.0 / ot[MLA_V:MLA_V + 1, :]))
    o_ref[...] = jnp.concatenate(outs, axis=0).T.astype(o_ref.dtype)


def _mla_call(p3, w, rope_tabs, cache, *, layer, nb, t, tq, want_cache):
    has_ctx = cache is not None
    nctx = cache.shape[2] if has_ctx else 0
    s_tot = t + nctx
    half = MLA_KV_LORA

    args = [p3, p3, p3]
    in_specs = [
        pl.BlockSpec((None, tq, SEG), lambda b, i: (b, i, S_CQ)),
        pl.BlockSpec((None, t, half), lambda b, i: (b, 0, 2 * S_KV)),
        pl.BlockSpec((None, t, half), lambda b, i: (b, 0, 2 * S_KV + 1)),
    ]
    if has_ctx:
        args += [rope_tabs[0], rope_tabs[1], cache]
        in_specs += [
            _const_spec((t, MLA_SLAB)),
            _const_spec((t, MLA_SLAB)),
            pl.BlockSpec((None, None, nctx, MLA_KV_LORA + MLA_ROPE), lambda b, i: (b, layer, 0, 0)),
        ]
    args += [w['mla_qn'], w['mla_kvn'], w['mla_wq']]
    in_specs += [_pick_spec(w['mla_qn'], layer), _pick_spec(w['mla_kvn'], layer), _pick_spec(w['mla_wq'], layer)]
    if has_ctx:
        args += [w['mla_wqr'], w['mla_wkc']]
        in_specs += [_pick_spec(w['mla_wqr'], layer), _pick_spec(w['mla_wkc'], layer)]
    args += [w['mla_wk'], w['mla_wv']]
    in_specs += [_pick_spec(w['mla_wk'], layer), _pick_spec(w['mla_wv'], layer)]
    out_specs = [pl.BlockSpec((None, tq, SEG), lambda b, i: (b, i, 0))]
    out_shape = [jax.ShapeDtypeStruct((nb, t, SEG), BF16)]
    if want_cache:
        out_specs.append(pl.BlockSpec((None, t, MLA_KV_LORA), lambda b, i: (b, 0, 0)))
        out_shape.append(jax.ShapeDtypeStruct((nb, t, MLA_KV_LORA), F32))
    return pl.pallas_call(
        functools.partial(_mla_kernel, t=t, tq=tq, nctx=nctx, want_cache=want_cache),
        grid=(nb, t // tq),
        in_specs=in_specs,
        out_specs=out_specs,
        out_shape=out_shape,
        scratch_shapes=[pltpu.VMEM((MLA_HEADS, s_tot, MLA_SLAB), BF16),
                        pltpu.VMEM((MLA_HEADS, MLA_VT_ROWS, s_tot), BF16)],
        compiler_params=_cparams(("arbitrary", "arbitrary")),
        name="mla",
    )(*args)


def _merge_kernel(x_ref, mod_ref, npre_ref, npost_ref, yf_ref, yb_ref, u_ref, ret_ref,
                  cx_ref, cb_ref, cc_ref, cxp_ref, ccp_ref, cxn_ref, ccn_ref, mla_ref,
                  s5d_ref, wglu_ref, cw_ref, cbias_ref, wbr_ref, wg_ref, bg_ref, wo_ref,
                  o_ref, h_scr, *, tm, nt):
    i = pl.program_id(1)
    x = x_ref[...]
    sh = mod_ref[3:4, :]
    sc = mod_ref[4:5, :]
    gt = mod_ref[5:6, :]
    h_scr[...] = (_rms(x, npre_ref[...]) * (1.0 + sc) + sh).astype(BF16)

    y = yf_ref[...] + yb_ref[...] + s5d_ref[...] * u_ref[...].astype(F32)
    z5 = _gelu_tanh(y)
    b_s5 = z5 * _sigmoid(_dot(z5, wglu_ref[...]))

    z = cc_ref[...].astype(F32) * cx_ref[...].astype(F32)
    zprev = ((ccp_ref[...].astype(F32) * cxp_ref[...].astype(F32))[HALO - 1:HALO, :]
             * jnp.where(i > 0, 1.0, 0.0))
    znext = ((ccn_ref[...].astype(F32) * cxn_ref[...].astype(F32))[0:1, :]
             * jnp.where(i < nt - 1, 1.0, 0.0))
    row = lax.broadcasted_iota(jnp.int32, (tm, SEG), 0)
    z_dn = jnp.where(row == 0, zprev, pltpu.roll(z, 1, axis=0))
    z_up = jnp.where(row == tm - 1, znext, pltpu.roll(z, tm - 1, axis=0))
    b_conv = cb_ref[...].astype(F32) * (cbias_ref[...] + z_dn * cw_ref[0:1, :] + z * cw_ref[1:2, :]
                                        + z_up * cw_ref[2:3, :])

    branches = (b_s5, ret_ref[...], b_conv, mla_ref[...])
    merged = jnp.zeros((tm, D_MODEL), F32)
    for n in range(N_BRANCH):
        gate = _sigmoid(jnp.dot(h_scr[...], wg_ref[:, n * D_MODEL:(n + 1) * D_MODEL],
                                preferred_element_type=F32) + bg_ref[:, n * D_MODEL:(n + 1) * D_MODEL])
        merged = merged + gate * _dot(branches[n], wbr_ref[n])
    m = _dot(merged, wo_ref[...])
    o_ref[...] = x + gt * _rms(m, npost_ref[...])


def _merge_call(x, w, y_s5, ret_o, p3, mla_o, *, layer, nb, t, tm, row_of_batch):
    nt = t // tm
    hb = tm // HALO
    n_halo = t // HALO

    def dir_spec(d):
        return pl.BlockSpec((None, None, tm, SEG), lambda b, i: (d, b, i, 0))

    def seg_spec(seg):
        return pl.BlockSpec((None, tm, SEG), lambda b, i: (b, i, seg))

    def prev_spec(seg):
        return pl.BlockSpec((None, HALO, SEG), lambda b, i: (b, jnp.maximum(i * hb - 1, 0), seg))

    def next_spec(seg):
        return pl.BlockSpec((None, HALO, SEG), lambda b, i: (b, jnp.minimum((i + 1) * hb, n_halo - 1), seg))

    return pl.pallas_call(
        functools.partial(_merge_kernel, tm=tm, nt=nt),
        grid=(nb, nt),
        in_specs=[
            pl.BlockSpec((tm, D_MODEL), lambda b, i: (b * nt + i, 0)),
            pl.BlockSpec((None, None, N_MOD, D_MODEL), lambda b, i: (layer, row_of_batch(b), 0, 0)),
            _pick_spec(w['npre'], layer * N_NORM + 1),
            _pick_spec(w['npost'], layer * N_NORM + 1),
            dir_spec(0), dir_spec(1), seg_spec(S_U),
            pl.BlockSpec((None, tm, SEG), lambda b, i: (b, i, 0)),
            seg_spec(S_CX), seg_spec(S_CB), seg_spec(S_CC),
            prev_spec(S_CX), prev_spec(S_CC), next_spec(S_CX), next_spec(S_CC),
            pl.BlockSpec((None, tm, SEG), lambda b, i: (b, i, 0)),
            _pick_spec(w['s5_d'], layer), _pick_spec(w['s5_wglu'], layer), _pick_spec(w['conv_w'], layer),
            _pick_spec(w['conv_b'], layer), _resident_spec(w['w_branch'].shape[1:], (layer,)),
            _resident_spec(w['w_gate'].shape[1:], (layer,)), _pick_spec(w['b_gate'], layer),
            _resident_spec(w['w_o'].shape[1:], (layer,)),
        ],
        out_specs=pl.BlockSpec((tm, D_MODEL), lambda b, i: (b * nt + i, 0)),
        out_shape=jax.ShapeDtypeStruct((nb * t, D_MODEL), F32),
        scratch_shapes=[pltpu.VMEM((tm, D_MODEL), BF16)],
        compiler_params=_cparams(("arbitrary", "arbitrary")),
        name="merge",
    )(x, w['mod'], w['npre'], w['npost'], y_s5, y_s5, p3, ret_o, p3, p3, p3, p3, p3, p3, p3, mla_o,
      w['s5_d'], w['s5_wglu'], w['conv_w'], w['conv_b'], w['w_branch'], w['w_gate'], w['b_gate'], w['w_o'])


def _zeros_like_cols(w, n):
    return jnp.zeros(w.shape[:-1] + (n,), w.dtype)


def _axial_angles(t, dim):
    rows = t // GRID_W
    row = np.repeat(np.arange(rows, dtype=np.float64), GRID_W)
    col = np.tile(np.arange(GRID_W, dtype=np.float64), rows)
    quarter = dim // 4
    inv = ROPE_BASE ** (-np.arange(quarter, dtype=np.float64) / quarter)
    ang = np.concatenate([row[:, None] * inv, col[:, None] * inv], axis=-1)
    return np.cos(ang).astype(np.float32), np.sin(ang).astype(np.float32)


def _rope_tables(t):
    cos, sin = _axial_angles(t, RET_DIM)
    ret_cos = np.tile(np.concatenate([cos, cos], -1), (1, 2))
    ret_sin = np.tile(np.concatenate([sin, sin], -1), (1, 2))
    cos, sin = _axial_angles(t, MLA_ROPE)
    pad = MLA_SLAB - MLA_NOPE - MLA_ROPE
    mla_cos = np.concatenate([np.ones((t, MLA_NOPE), np.float32), cos, cos, np.zeros((t, pad), np.float32)], -1)
    mla_sin = np.concatenate([np.zeros((t, MLA_NOPE), np.float32), sin, sin, np.zeros((t, pad), np.float32)], -1)
    return ((jnp.asarray(ret_cos), jnp.asarray(ret_sin)), (jnp.asarray(mla_cos), jnp.asarray(mla_sin)))


def _place_rope_key(w):
    return jnp.concatenate([_zeros_like_cols(w, MLA_NOPE), w,
                            _zeros_like_cols(w, MLA_SLAB - MLA_NOPE - MLA_ROPE)], -1)


def _in_weights(w_in):
    sizes = (BRANCH_WIDTH,) * 8 + (MLA_Q_LORA, MLA_KV_LORA, MLA_ROPE)
    offs = np.cumsum((0,) + sizes)
    u, rq, rk, rv, rg, cx, cb, cc, cq, ckv, kr = [w_in[..., offs[i]:offs[i + 1]] for i in range(11)]
    rk = rk * (RET_DIM ** -0.5)
    cols = [u, rq, rk, rv, rg, cx, cb, cc,
            jnp.concatenate([cq, _zeros_like_cols(cq, SEG - MLA_Q_LORA)], -1),
            jnp.concatenate([ckv, _place_rope_key(kr)], -1)]
    return jnp.concatenate([c.astype(BF16) for c in cols], axis=-1)


def _mla_weights(w_uq, w_ukv):
    nl = w_uq.shape[0]
    dq = MLA_NOPE + MLA_ROPE
    w4 = w_uq.reshape(nl, MLA_Q_LORA, MLA_HEADS, dq)
    nope, ropew = w4[..., :MLA_NOPE], w4[..., MLA_NOPE:]
    half = MLA_ROPE // 2
    rot = jnp.concatenate([-ropew[..., half:], ropew[..., :half]], -1)
    zpad = jnp.zeros((nl, MLA_Q_LORA, MLA_HEADS, MLA_SLAB - dq), F32)
    wq = jnp.concatenate([nope, ropew, zpad], -1).reshape(nl, MLA_Q_LORA, MLA_HEADS * MLA_SLAB)
    wqr = jnp.concatenate([jnp.zeros_like(nope), rot, zpad], -1).reshape(nl, MLA_Q_LORA, MLA_HEADS * MLA_SLAB)
    rpad = jnp.zeros((nl, SEG - MLA_Q_LORA, MLA_HEADS * MLA_SLAB), F32)
    wq = jnp.concatenate([wq, rpad], 1).astype(BF16)
    wqr = jnp.concatenate([wqr, rpad], 1).astype(BF16)
    kv4 = w_ukv.reshape(nl, MLA_KV_LORA, MLA_HEADS, MLA_NOPE + MLA_V)
    wk = jnp.concatenate([kv4[..., :MLA_NOPE], jnp.zeros((nl, MLA_KV_LORA, MLA_HEADS, MLA_SLAB - MLA_NOPE), F32)],
                         -1).reshape(nl, MLA_KV_LORA, MLA_HEADS * MLA_SLAB)
    place = np.zeros((MLA_ROPE, MLA_HEADS, MLA_SLAB), np.float32)
    place[np.arange(MLA_ROPE), :, MLA_NOPE + np.arange(MLA_ROPE)] = 1.0
    place = jnp.broadcast_to(jnp.asarray(place.reshape(MLA_ROPE, MLA_HEADS * MLA_SLAB)),
                             (nl, MLA_ROPE, MLA_HEADS * MLA_SLAB))
    wkc = jnp.concatenate([wk, place], 1).astype(BF16)
    wv = kv4[..., MLA_NOPE:].reshape(nl, MLA_KV_LORA, MLA_HEADS * MLA_V).astype(BF16)
    return wq, wqr, wk.astype(BF16), wkc, wv


def _s5_tables(lam_re, lam_im, log_dt, b_re, b_im, c_re, c_im):
    nl = lam_re.shape[0]
    dt = jnp.exp(log_dt)[..., None]
    mag = jnp.exp(lam_re * dt)
    ab_re = mag * jnp.cos(lam_im * dt)
    ab_im = mag * jnp.sin(lam_im * dt)
    den = lam_re * lam_re + lam_im * lam_im
    f_re = ((ab_re - 1.0) * lam_re + ab_im * lam_im) / den
    f_im = (ab_im * lam_re - (ab_re - 1.0) * lam_im) / den
    bb_re = f_re[..., None] * b_re - f_im[..., None] * b_im
    bb_im = f_re[..., None] * b_im + f_im[..., None] * b_re
    eye = jnp.eye(S5_GROUPS, dtype=F32)
    blk_in = lambda m: jnp.einsum('ldgph,gk->ldghkp', m, eye).reshape(nl, 2, BRANCH_WIDTH, S5_LANES)
    bblk = jnp.concatenate([blk_in(bb_re), blk_in(bb_im)], -1).astype(BF16)
    blk_out = lambda m: jnp.einsum('lghp,gk->lgpkh', m, eye).reshape(nl, S5_LANES, BRANCH_WIDTH)
    cblk = jnp.concatenate([blk_out(c_re), -blk_out(c_im)], 1).astype(BF16)
    a_tab = jnp.stack([ab_re.reshape(nl, 2, S5_LANES), ab_im.reshape(nl, 2, S5_LANES)], axis=2)
    return bblk, a_tab, cblk


def _ret_tables(decay_logit, c):
    lg = jax.nn.log_sigmoid(decay_logit.astype(F32))
    lf = lg[:, 0][:, :, None, None]
    lb = lg[:, 1][:, :, None, None]
    idx = jnp.arange(c, dtype=F32)
    diff = idx[:, None] - idx[None, :]
    dm = jnp.where(diff >= 0, jnp.exp(lf * jnp.maximum(diff, 0.0)), jnp.exp(lb * jnp.maximum(-diff, 0.0)))
    lfl = jnp.repeat(lg[:, 0], RET_DIM, axis=-1)
    lbl = jnp.repeat(lg[:, 1], RET_DIM, axis=-1)
    qdf = jnp.exp(lfl[:, None, :] * (idx[:, None] + 1.0))
    qdb = jnp.exp(lbl[:, None, :] * (c - idx[:, None]))
    kdft = jnp.exp(lfl[:, :, None] * (c - 1.0 - idx[None, :]))
    kdbt = jnp.exp(lbl[:, :, None] * idx[None, :])
    nl = lg.shape[0]
    cdf = jnp.broadcast_to(jnp.exp(lfl * c)[:, :, None], (nl, SEG, SEG))
    cdb = jnp.broadcast_to(jnp.exp(lbl * c)[:, :, None], (nl, SEG, SEG))
    head = np.arange(SEG) // RET_DIM
    hm = jnp.asarray((head[None, :] == np.arange(RET_HEADS)[:, None]).astype(np.float32))
    bd_np = (head[:, None] == head[None, :]).astype(np.float32)
    bd = jnp.asarray(bd_np)
    gm = jnp.asarray(bd_np / RET_DIM, BF16)
    return (dm, qdf, qdb, kdft, kdbt, cdf, cdb), (hm, bd, gm)


def _layer(x, w, *, layer, nb, t, tm, tm_ffn, row_of_batch, ctx, rope_tabs):
    is_ctx_pass = ctx is None
    row_of_tile = lambda i: row_of_batch((i * tm_ffn) // t)
    x = _ffn_call(x, w, layer=layer, k=0, tm=tm_ffn, row_of_tile=row_of_tile)

    p = _in_call(x, w, layer=layer, nb=nb, t=t, tm=tm, row_of_batch=row_of_batch)
    p3 = p.reshape(nb, t, NSEG * SEG)

    if is_ctx_pass:
        s5_0, s0_layer = jnp.zeros((1, 2, 2, nb, S5_LANES), F32), 0
        ret_0, mla_cache, ret_rope, mla_rope = None, None, None, None
    else:
        s5_0, ret_0, mla_cache = ctx
        s0_layer = layer
        ret_rope, mla_rope = rope_tabs
    s5_out = _s5_call(p3, w, s5_0, layer=layer, s0_layer=s0_layer, nb=nb, t=t, want_fin=is_ctx_pass)
    ret_out = _ret_call(p3, w, ret_0, ret_rope, layer=layer, nb=nb, t=t, want_fin=is_ctx_pass)
    mla_out = _mla_call(p3, w, mla_rope, mla_cache, layer=layer, nb=nb, t=t, tq=min(t, MLA_TQ),
                        want_cache=is_ctx_pass)

    x = _merge_call(x, w, s5_out[0], ret_out[0], p3, mla_out[0], layer=layer, nb=nb, t=t, tm=tm,
                    row_of_batch=row_of_batch)
    x = _ffn_call(x, w, layer=layer, k=2, tm=tm_ffn, row_of_tile=row_of_tile)

    state = None
    if is_ctx_pass:
        kr = p3[:, :, S_KV * SEG + MLA_KV_LORA + MLA_NOPE:S_KV * SEG + MLA_KV_LORA + MLA_NOPE + MLA_ROPE]
        cache = jnp.concatenate([mla_out[1], kr.astype(F32)], axis=-1)
        state = (s5_out[1], ret_out[1], cache)
    return x, state


def kernel(x_prompt, x_sample, state_s5, state_ret, cache_mla, c, c_ctx, w_mod, b_mod, norm_pre, norm_post, ffn_w1, ffn_w3, ffn_w2, w_in, s5_lam_re, s5_lam_im, s5_log_dt, s5_b_re, s5_b_im, s5_c_re, s5_c_im, s5_d, s5_w_glu, ret_decay, ret_gn, conv_w, conv_b, mla_q_norm, mla_w_uq, mla_kv_norm, mla_w_ukv, w_branch, w_gate, b_gate, w_o):
    bp, tp, _ = x_prompt.shape
    bs, ts, _ = x_sample.shape
    assert 1 + bs <= MOD_ROWS

    cond = jnp.concatenate([c_ctx[None, :], c, jnp.zeros((MOD_ROWS - 1 - bs, D_MODEL), F32)], axis=0)
    mod = _mod_call(cond, w_mod, b_mod).reshape(DEPTH, MOD_ROWS, N_MOD, D_MODEL)

    wq, wqr, wk, wkc, wv = _mla_weights(mla_w_uq, mla_w_ukv)
    bblk, a_tab, cblk = _s5_tables(s5_lam_re, s5_lam_im, s5_log_dt, s5_b_re, s5_b_im, s5_c_re, s5_c_im)
    ret_layer_tabs, ret_const_tabs = _ret_tables(ret_decay, RET_CHUNK)
    w = dict(
        mod=mod,
        npre=norm_pre.reshape(DEPTH * N_NORM, 1, D_MODEL),
        npost=norm_post.reshape(DEPTH * N_NORM, 1, D_MODEL),
        ffn=(ffn_w1.astype(BF16), ffn_w3.astype(BF16), ffn_w2.astype(BF16)),
        w_in=_in_weights(w_in),
        s5_bblk=bblk, s5_a=a_tab, s5_cblk=cblk,
        s5_d=s5_d[:, None, :], s5_wglu=s5_w_glu.astype(BF16),
        ret_layer_tabs=ret_layer_tabs, ret_const_tabs=ret_const_tabs, ret_gn=ret_gn[:, None, :],
        conv_w=conv_w, conv_b=conv_b[:, None, :],
        mla_qn=jnp.concatenate([mla_q_norm, jnp.zeros((DEPTH, SEG - MLA_Q_LORA), F32)], -1)[:, None, :],
        mla_kvn=mla_kv_norm[:, None, :],
        mla_wq=wq, mla_wqr=wqr, mla_wk=wk, mla_wkc=wkc, mla_wv=wv,
        w_branch=w_branch.astype(BF16), w_gate=w_gate.astype(BF16), b_gate=b_gate[:, None, :],
        w_o=w_o.astype(BF16),
    )
    rope_tabs = _rope_tables(ts)
    s5_0 = state_s5.transpose(1, 2, 5, 0, 3, 4).reshape(DEPTH, 2, 2, bs, S5_LANES)

    xp = x_prompt.reshape(bp * tp, D_MODEL)
    xs = x_sample.reshape(bs * ts, D_MODEL)
    s5_list, ret_list, mla_list = [], [], []
    for l in range(DEPTH):
        xp, (s5_s, ret_s, mla_c) = _layer(xp, w, layer=l, nb=bp, t=tp, tm=TM_CTX, tm_ffn=TM_CTX_FFN,
                                          row_of_batch=lambda b: 0, ctx=None, rope_tabs=None)
        s5_list.append(s5_s)
        ret_list.append(ret_s)
        mla_list.append(mla_c)
        xs, _ = _layer(xs, w, layer=l, nb=bs, t=ts, tm=TM_LAT, tm_ffn=TM_LAT_FFN, row_of_batch=lambda b: 1 + b,
                       ctx=(s5_0, state_ret, cache_mla), rope_tabs=rope_tabs)
    new_s5 = jnp.stack(s5_list, 0).reshape(DEPTH, 2, 2, bp, S5_GROUPS, S5_STATE).transpose(3, 0, 1, 4, 5, 2)
    return (xp.reshape(bp, tp, D_MODEL), xs.reshape(bs, ts, D_MODEL),
            new_s5, jnp.stack(ret_list, axis=1), jnp.stack(mla_list, axis=1))
```

```python
import functools

import numpy as np
import jax
import jax.numpy as jnp
from jax import lax
from jax.experimental import pallas as pl
from jax.experimental.pallas import tpu as pltpu

F32 = jnp.float32
BF16 = jnp.bfloat16

D_MODEL = 1024
DEPTH = 2
GRID_W = 64
EPS = 1e-6
LOG2_E = float(np.log2(np.e))
ROPE_BASE = 10000.0
N_BRANCH = 4
BRANCH_WIDTH = D_MODEL // 4
N_MOD = 9
N_NORM = 3
D_FF = 2816
S5_GROUP = 16
S5_GROUPS = BRANCH_WIDTH // S5_GROUP
S5_STATE = 64
S5_LANES = S5_GROUPS * S5_STATE
RET_HEADS = 4
RET_DIM = BRANCH_WIDTH // RET_HEADS
CONV_K = 3
MLA_HEADS = 4
MLA_Q_LORA = 192
MLA_KV_LORA = 128
MLA_NOPE = 64
MLA_ROPE = 32
MLA_V = BRANCH_WIDTH // MLA_HEADS
MLA_SLAB = 128
MLA_VT_ROWS = MLA_V + 16

SUBLANES = 8
S5_SUBCHUNKS = 4
S5_ROWS = 512
SEG = 256
FF_CHUNK = 256
N_FF_CHUNKS = D_FF // FF_CHUNK
MOD_ROWS = 16
HALO = 16
RET_CHUNK = 256
MLA_TQ = 1024
TM_CTX, TM_CTX_FFN = 256, 512
TM_LAT, TM_LAT_FFN = 512, 1024
V7X_VMEM_LIMIT = 56 * 1024 * 1024

S_U, S_RQ, S_RK, S_RV, S_RG, S_CX, S_CB, S_CC, S_CQ, S_KV = range(10)
NSEG = 10
LANES = 128


def _cparams(sem):
    return pltpu.CompilerParams(dimension_semantics=sem, vmem_limit_bytes=V7X_VMEM_LIMIT)


def _dot(a, b):
    return jnp.dot(a.astype(BF16), b.astype(BF16), preferred_element_type=F32)


def _sigmoid(x):
    return 1.0 / (1.0 + jnp.exp(-x))


def _rms(x, g, n=None):
    n = x.shape[-1] if n is None else n
    ms = jnp.sum(x * x, axis=-1, keepdims=True) * (1.0 / n)
    return x * lax.rsqrt(ms + EPS) * g


def _rot_half_lanes(x, width):
    half = width // 2
    lane = lax.broadcasted_iota(jnp.int32, (x.shape[0], LANES), 1)
    first = (lane & (width - 1)) < half
    outs = []
    for j in range(x.shape[1] // LANES):
        v = x[:, j * LANES:(j + 1) * LANES]
        outs.append(jnp.where(first, -pltpu.roll(v, LANES - half, axis=1), pltpu.roll(v, half, axis=1)))
    return outs[0] if len(outs) == 1 else jnp.concatenate(outs, axis=-1)


def _gelu_tanh(x):
    return 0.5 * x * (1.0 + jnp.tanh(np.sqrt(2.0 / np.pi) * (x + 0.044715 * (x * x * x))))


def _const_spec(shape):
    nd = len(shape)
    return pl.BlockSpec(shape, lambda *_: (0,) * nd)


def _pick_spec(arr, *lead):
    shape = arr.shape[len(lead):]
    return pl.BlockSpec((None,) * len(lead) + tuple(shape), lambda *_: tuple(lead) + (0,) * len(shape))


def _resident_spec(shape, lead=()):
    nd = len(shape)
    return pl.BlockSpec((None,) * len(lead) + tuple(shape), lambda *_: tuple(lead) + (0,) * nd,
                        pipeline_mode=pl.Buffered(1))


def _mod_kernel(c_ref, w_ref, b_ref, o_ref):
    c = c_ref[...]
    o_ref[...] = _dot(c * _sigmoid(c), w_ref[...]) + b_ref[...]


def _mod_call(cond, w_mod, b_mod):
    ncol = N_MOD * D_MODEL
    tn = D_MODEL
    return pl.pallas_call(
        _mod_kernel,
        grid=(DEPTH, ncol // tn),
        in_specs=[
            pl.BlockSpec((MOD_ROWS, D_MODEL), lambda l, j: (0, 0)),
            pl.BlockSpec((None, D_MODEL, tn), lambda l, j: (l, 0, j)),
            pl.BlockSpec((None, 1, tn), lambda l, j: (l, 0, j)),
        ],
        out_specs=pl.BlockSpec((None, MOD_ROWS, tn), lambda l, j: (l, 0, j)),
        out_shape=jax.ShapeDtypeStruct((DEPTH, MOD_ROWS, ncol), F32),
        compiler_params=_cparams(("arbitrary", "arbitrary")),
        name="mod",
    )(cond, w_mod, b_mod.reshape(DEPTH, 1, ncol))


def _ffn_kernel(x_ref, mod_ref, npre_ref, npost_ref, w1_ref, w3_ref, w2_ref, o_ref,
                h_scr, g_scr, *, k):
    x = x_ref[...]
    sh = mod_ref[3 * k:3 * k + 1, :]
    sc = mod_ref[3 * k + 1:3 * k + 2, :]
    gt = mod_ref[3 * k + 2:3 * k + 3, :]
    h_scr[...] = (_rms(x, npre_ref[...]) * (1.0 + sc) + sh).astype(BF16)
    for j in range(N_FF_CHUNKS):
        cols = slice(j * FF_CHUNK, (j + 1) * FF_CHUNK)
        a = jnp.dot(h_scr[...], w1_ref[:, cols], preferred_element_type=F32)
        b = jnp.dot(h_scr[...], w3_ref[:, cols], preferred_element_type=F32)
        g_scr[:, cols] = (a * _sigmoid(a) * b).astype(BF16)
    f = jnp.dot(g_scr[...], w2_ref[...], preferred_element_type=F32)
    o_ref[...] = x + (0.5 * gt) * _rms(f, npost_ref[...])


def _ffn_call(x, w, *, layer, k, tm, row_of_tile):
    n = x.shape[0]
    lj = (layer, k // 2)
    w1, w3, w2 = w['ffn']
    return pl.pallas_call(
        functools.partial(_ffn_kernel, k=k),
        grid=(n // tm,),
        in_specs=[
            pl.BlockSpec((tm, D_MODEL), lambda i: (i, 0)),
            pl.BlockSpec((None, None, N_MOD, D_MODEL), lambda i: (layer, row_of_tile(i), 0, 0)),
            _pick_spec(w['npre'], layer * N_NORM + k),
            _pick_spec(w['npost'], layer * N_NORM + k),
            _resident_spec((D_MODEL, D_FF), lj),
            _resident_spec((D_MODEL, D_FF), lj),
            _resident_spec((D_FF, D_MODEL), lj),
        ],
        out_specs=pl.BlockSpec((tm, D_MODEL), lambda i: (i, 0)),
        out_shape=jax.ShapeDtypeStruct((n, D_MODEL), F32),
        scratch_shapes=[pltpu.VMEM((tm, D_MODEL), BF16), pltpu.VMEM((tm, D_FF), BF16)],
        compiler_params=_cparams(("arbitrary",)),
        name="ffn",
    )(x, w['mod'], w['npre'], w['npost'], w1, w3, w2)


def _in_kernel(x_ref, mod_ref, npre_ref, w_ref, p_ref, h_scr):
    sh = mod_ref[3:4, :]
    sc = mod_ref[4:5, :]
    h_scr[...] = (_rms(x_ref[...], npre_ref[...]) * (1.0 + sc) + sh).astype(BF16)
    for j in range(NSEG):
        cols = slice(j * SEG, (j + 1) * SEG)
        p_ref[:, cols] = jnp.dot(h_scr[...], w_ref[:, cols], preferred_element_type=F32).astype(BF16)


def _in_call(x, w, *, layer, nb, t, tm, row_of_batch):
    tt = t // tm
    nseg = NSEG
    return pl.pallas_call(
        _in_kernel,
        grid=(nb, tt),
        in_specs=[
            pl.BlockSpec((tm, D_MODEL), lambda b, i: (b * tt + i, 0)),
            pl.BlockSpec((None, None, N_MOD, D_MODEL), lambda b, i: (layer, row_of_batch(b), 0, 0)),
            _pick_spec(w['npre'], layer * N_NORM + 1),
            _resident_spec((D_MODEL, nseg * SEG), (layer,)),
        ],
        out_specs=pl.BlockSpec((tm, nseg * SEG), lambda b, i: (b * tt + i, 0)),
        out_shape=jax.ShapeDtypeStruct((nb * t, nseg * SEG), BF16),
        scratch_shapes=[pltpu.VMEM((tm, D_MODEL), BF16)],
        compiler_params=_cparams(("arbitrary", "arbitrary")),
        name="in_proj",
    )(x, w['mod'], w['npre'], w['w_in'])


def _s5_kernel(*refs, nb, steps, want_fin):
    if want_fin:
        u_ref, bb_ref, a_ref, cc_ref, s0_ref, y_ref, fin_ref, s_scr, bu_scr, u2_scr, y2_scr = refs
    else:
        u_ref, bb_ref, a_ref, cc_ref, s0_ref, y_ref, s_scr, bu_scr, u2_scr, y2_scr = refs
        fin_ref = None
    d = pl.program_id(0)
    c = pl.program_id(1)
    span = steps * S5_SUBCHUNKS
    ngrp = nb // SUBLANES

    @pl.when(c == 0)
    def _():
        s_scr[...] = s0_ref[...]

    u_all = u_ref[...].astype(F32).reshape(nb * span, SEG)
    for j in range(SEG // LANES):
        u2_scr[j] = u_all[:, j * LANES:(j + 1) * LANES]

    a_re = jnp.broadcast_to(a_ref[0:1, :], (SUBLANES, S5_LANES))
    a_im = jnp.broadcast_to(a_ref[1:2, :], (SUBLANES, S5_LANES))
    s_re = [s_scr[0, g * SUBLANES:(g + 1) * SUBLANES, :] for g in range(ngrp)]
    s_im = [s_scr[1, g * SUBLANES:(g + 1) * SUBLANES, :] for g in range(ngrp)]
    re_cols = slice(0, S5_LANES)
    im_cols = slice(S5_LANES, 2 * S5_LANES)

    def time_of(k, i):
        j = k * steps + i
        return jnp.where(d == 0, j, span - 1 - j)

    def project_in(k):
        u_tb = jnp.concatenate(
            [jnp.concatenate([u2_scr[j, pl.ds(time_of(k, i), nb, stride=span), :]
                              for j in range(SEG // LANES)], axis=-1) for i in range(steps)], axis=0)
        bu_scr[k] = _dot(u_tb, bb_ref[...])

    project_in(0)
    for k in range(S5_SUBCHUNKS):
        if k + 1 < S5_SUBCHUNKS:
            project_in(k + 1)
        for i in range(steps):
            for g in range(ngrp):
                rr = slice(i * nb + g * SUBLANES, i * nb + (g + 1) * SUBLANES)
                n_re = a_re * s_re[g] - a_im * s_im[g] + bu_scr[k, rr, re_cols]
                n_im = a_re * s_im[g] + a_im * s_re[g] + bu_scr[k, rr, im_cols]
                bu_scr[k, rr, re_cols] = n_re
                bu_scr[k, rr, im_cols] = n_im
                s_re[g], s_im[g] = n_re, n_im
        y_tb = _dot(bu_scr[k], cc_ref[...])
        for i in range(steps):
            for j in range(SEG // LANES):
                y2_scr[j, pl.ds(time_of(k, i), nb, stride=span), :] = (
                    y_tb[i * nb:(i + 1) * nb, j * LANES:(j + 1) * LANES])
    y_ref[...] = jnp.concatenate([y2_scr[j] for j in range(SEG // LANES)], axis=-1).reshape(nb, span, SEG)
    for g in range(ngrp):
        s_scr[0, g * SUBLANES:(g + 1) * SUBLANES, :] = s_re[g]
        s_scr[1, g * SUBLANES:(g + 1) * SUBLANES, :] = s_im[g]
    if want_fin:
        fin_ref[...] = s_scr[...]


def _s5_call(p3, w, s0, *, layer, s0_layer, nb, t, want_fin):
    steps = S5_ROWS // nb
    span = steps * S5_SUBCHUNKS
    nch = t // span
    rows = S5_ROWS

    def chunk(d, c):
        return jnp.where(d == 0, c, nch - 1 - c)

    out_specs = [pl.BlockSpec((None, nb, span, SEG), lambda d, c: (d, 0, chunk(d, c), 0))]
    out_shape = [jax.ShapeDtypeStruct((2, nb, t, SEG), F32)]
    if want_fin:
        out_specs.append(pl.BlockSpec((None, 2, nb, S5_LANES), lambda d, c: (d, 0, 0, 0)))
        out_shape.append(jax.ShapeDtypeStruct((2, 2, nb, S5_LANES), F32))
    return pl.pallas_call(
        functools.partial(_s5_kernel, nb=nb, steps=steps, want_fin=want_fin),
        grid=(2, nch),
        in_specs=[
            pl.BlockSpec((nb, span, SEG), lambda d, c: (0, chunk(d, c), S_U)),
            pl.BlockSpec((None, None, SEG, 2 * S5_LANES), lambda d, c: (layer, d, 0, 0)),
            pl.BlockSpec((None, None, 2, S5_LANES), lambda d, c: (layer, d, 0, 0)),
            _pick_spec(w['s5_cblk'], layer),
            pl.BlockSpec((None, None, 2, nb, S5_LANES), lambda d, c: (s0_layer, d, 0, 0, 0)),
        ],
        out_specs=out_specs,
        out_shape=out_shape,
        scratch_shapes=[pltpu.VMEM((2, nb, S5_LANES), F32),
                        pltpu.VMEM((S5_SUBCHUNKS, rows, 2 * S5_LANES), F32),
                        pltpu.VMEM((SEG // LANES, nb * span, LANES), F32),
                        pltpu.VMEM((SEG // LANES, nb * span, LANES), F32)],
        compiler_params=_cparams(("arbitrary", "arbitrary")),
        name="s5_scan",
    )(p3, w['s5_bblk'], w['s5_a'], w['s5_cblk'], s0)


def _ret_kernel(*refs, rope, has_s0, want_fin, nch):
    refs = list(refs)
    q_ref, k_ref, v_ref, g_ref = refs[:4]
    del refs[:4]
    if rope:
        cos_ref, sin_ref = refs[:2]
        del refs[:2]
    (dm_ref, qdf_ref, qdb_ref, kdft_ref, kdbt_ref, cdf_ref, cdb_ref, hm_ref, bd_ref, gm_ref,
     gn_ref) = refs[:11]
    del refs[:11]
    if has_s0:
        s0_ref = refs.pop(0)
    o_ref = refs.pop(0)
    fin_ref = refs.pop(0) if want_fin else None
    sb_all, sf_scr, sb_scr = refs

    p = pl.program_id(1)
    c = pl.program_id(2)
    hd = RET_DIM

    def init_state(dst, d):
        dst[...] = jnp.zeros_like(dst)
        if has_s0:
            for h in range(RET_HEADS):
                dst[h * hd:(h + 1) * hd, h * hd:(h + 1) * hd] = s0_ref[d, h]

    def write_final(src, d):
        for h in range(RET_HEADS):
            fin_ref[d, h] = src[h * hd:(h + 1) * hd, h * hd:(h + 1) * hd]

    def rope_tabs():
        cos = cos_ref[...]
        sin = sin_ref[...]
        return jnp.concatenate([cos, cos], axis=-1), jnp.concatenate([sin, sin], axis=-1)

    def roped_kt():
        k = k_ref[...].astype(F32)
        if rope:
            cos, sin = rope_tabs()
            k = k * cos + _rot_half_lanes(k, RET_DIM) * sin
        return k.T

    @pl.when(p == 0)
    def _():
        ck = nch - 1 - c

        @pl.when(c == 0)
        def _():
            init_state(sb_scr, 1)

        kt = roped_kt()
        s_old = sb_scr[...]
        sb_all[ck] = s_old
        sb_scr[...] = s_old * cdb_ref[...] + _dot(kt * kdbt_ref[...], v_ref[...]) * bd_ref[...]
        if want_fin:
            @pl.when(c == nch - 1)
            def _():
                write_final(sb_scr, 1)

    @pl.when(p == 1)
    def _():
        @pl.when(c == 0)
        def _():
            init_state(sf_scr, 0)

        q = q_ref[...].astype(F32)
        if rope:
            cos, sin = rope_tabs()
            q = q * cos + _rot_half_lanes(q, RET_DIM) * sin
        kt = roped_kt()
        ktb = kt.astype(BF16)
        vb = v_ref[...]
        s_f = sf_scr[...]
        o = _dot(q, s_f) * qdf_ref[...] + _dot(q, sb_all[c]) * qdb_ref[...]
        for h in range(RET_HEADS):
            hm = hm_ref[h:h + 1, :]
            att = _dot(q * hm, ktb) * dm_ref[h]
            o = o + _dot(att, vb) * hm
        sf_scr[...] = s_f * cdf_ref[...] + _dot(kt * kdft_ref[...], vb) * bd_ref[...]
        hi = o.astype(BF16)
        lo = (o - hi.astype(F32)).astype(BF16)
        mu = (jnp.dot(hi, gm_ref[...], preferred_element_type=F32)
              + jnp.dot(lo, gm_ref[...], preferred_element_type=F32))
        oc = o - mu
        var = _dot(oc * oc, gm_ref[...])
        on = oc * lax.rsqrt(var + EPS) * gn_ref[...]
        g = g_ref[...].astype(F32)
        o_ref[...] = (g * _sigmoid(g)) * on
        if want_fin:
            @pl.when(c == nch - 1)
            def _():
                write_final(sf_scr, 0)


def _ret_call(p3, w, s0, rope_tabs, *, layer, nb, t, want_fin):
    tc = RET_CHUNK
    nch = t // tc
    rope = rope_tabs is not None
    has_s0 = s0 is not None

    def kv_chunk(p, c):
        return jnp.where(p == 0, nch - 1 - c, c)

    def q_chunk(p, c):
        return jnp.where(p == 0, 0, c)

    def seg_spec(seg, cmap):
        return pl.BlockSpec((None, tc, SEG), lambda b, p, c: (b, cmap(p, c), seg))

    args = [p3, p3, p3, p3]
    in_specs = [seg_spec(S_RQ, q_chunk), seg_spec(S_RK, kv_chunk), seg_spec(S_RV, kv_chunk),
                seg_spec(S_RG, q_chunk)]
    if rope:
        args += [rope_tabs[0], rope_tabs[1]]
        in_specs += [pl.BlockSpec((tc, 2 * RET_DIM), lambda b, p, c: (kv_chunk(p, c), 0)),
                     pl.BlockSpec((tc, 2 * RET_DIM), lambda b, p, c: (kv_chunk(p, c), 0))]
    args += list(w['ret_layer_tabs']) + list(w['ret_const_tabs']) + [w['ret_gn']]
    in_specs += ([_pick_spec(a, layer) for a in w['ret_layer_tabs']]
                 + [_const_spec(a.shape) for a in w['ret_const_tabs']] + [_pick_spec(w['ret_gn'], layer)])
    if has_s0:
        args.append(s0)
        in_specs.append(pl.BlockSpec((None, None, 2, RET_HEADS, RET_DIM, RET_DIM),
                                     lambda b, p, c: (b, layer, 0, 0, 0, 0)))
    out_specs = [pl.BlockSpec((None, tc, SEG), lambda b, p, c: (b, q_chunk(p, c), 0))]
    out_shape = [jax.ShapeDtypeStruct((nb, t, SEG), F32)]
    if want_fin:
        out_specs.append(pl.BlockSpec((None, 2, RET_HEADS, RET_DIM, RET_DIM),
                                      lambda b, p, c: (b, 0, 0, 0, 0)))
        out_shape.append(jax.ShapeDtypeStruct((nb, 2, RET_HEADS, RET_DIM, RET_DIM), F32))
    return pl.pallas_call(
        functools.partial(_ret_kernel, rope=rope, has_s0=has_s0, want_fin=want_fin, nch=nch),
        grid=(nb, 2, nch),
        in_specs=in_specs,
        out_specs=out_specs,
        out_shape=out_shape,
        scratch_shapes=[pltpu.VMEM((nch, SEG, SEG), F32),
                        pltpu.VMEM((SEG, SEG), F32),
                        pltpu.VMEM((SEG, SEG), F32)],
        compiler_params=_cparams(("arbitrary", "arbitrary", "arbitrary")),
        name="retention",
    )(*args)


def _mla_kernel(*refs, t, tq, nctx, want_cache):
    refs = list(refs)
    cq_ref, ckv_ref, krp_ref = refs[:3]
    del refs[:3]
    has_ctx = nctx > 0
    if has_ctx:
        cm_ref, sm_ref, cache_ref = refs[:3]
        del refs[:3]
    qn_ref, kvn_ref, wq_ref = refs[:3]
    del refs[:3]
    if has_ctx:
        wqr_ref, wkc_ref = refs[:2]
        del refs[:2]
    wk_ref, wv_ref = refs[:2]
    del refs[:2]
    o_ref = refs.pop(0)
    lat_ref = refs.pop(0) if want_cache else None
    k_scr, vt_scr = refs
    qi = pl.program_id(1)
    s_tot = t + nctx

    def store_vt(v, c0, n):
        vt = v.T
        for h in range(MLA_HEADS):
            vt_scr[h, 0:MLA_V, c0:c0 + n] = vt[h * MLA_V:(h + 1) * MLA_V, :].astype(BF16)

    @pl.when(qi == 0)
    def _():
        lat = _rms(ckv_ref[...].astype(F32), kvn_ref[...])
        if want_cache:
            lat_ref[...] = lat
        latb = lat.astype(BF16)
        kr = krp_ref[...].astype(F32)
        if has_ctx:
            kr = kr * cm_ref[...] + _rot_half_lanes(kr, MLA_ROPE) * sm_ref[...]
        kk = jnp.dot(latb, wk_ref[...], preferred_element_type=F32)
        for h in range(MLA_HEADS):
            k_scr[h, 0:t, :] = (kk[:, h * MLA_SLAB:(h + 1) * MLA_SLAB] + kr).astype(BF16)
        store_vt(jnp.dot(latb, wv_ref[...], preferred_element_type=F32), 0, t)
        if has_ctx:
            cache = cache_ref[...].astype(BF16)
            kkc = jnp.dot(cache, wkc_ref[...], preferred_element_type=F32)
            for h in range(MLA_HEADS):
                k_scr[h, t:t + nctx, :] = kkc[:, h * MLA_SLAB:(h + 1) * MLA_SLAB].astype(BF16)
            store_vt(jnp.dot(cache[:, 0:MLA_KV_LORA], wv_ref[...], preferred_element_type=F32), t, nctx)
        tail = lax.broadcasted_iota(jnp.int32, (MLA_VT_ROWS - MLA_V, s_tot), 0)
        for h in range(MLA_HEADS):
            vt_scr[h, MLA_V:MLA_VT_ROWS, :] = jnp.where(tail == 0, 1.0, 0.0).astype(BF16)

    a = _rms(cq_ref[...].astype(F32), qn_ref[...], n=MLA_Q_LORA).astype(BF16)
    q = jnp.dot(a, wq_ref[...], preferred_element_type=F32)
    if has_ctx:
        qrot = jnp.dot(a, wqr_ref[...], preferred_element_type=F32)
        r0 = pl.multiple_of(qi * tq, tq)
        cm = cm_ref[pl.ds(r0, tq), :]
        sm = sm_ref[pl.ds(r0, tq), :]
    scale = (MLA_NOPE + MLA_ROPE) ** -0.5
    def scores_t(h):
        qh = q[:, h * MLA_SLAB:(h + 1) * MLA_SLAB]
        if has_ctx:
            qh = qh * cm + qrot[:, h * MLA_SLAB:(h + 1) * MLA_SLAB] * sm
        qt = (qh * (scale * LOG2_E)).T.astype(BF16)
        return jnp.dot(k_scr[h], qt, preferred_element_type=F32)

    outs = []
    st_next = scores_t(0)
    for h in range(MLA_HEADS):
        st = st_next
        if h + 1 < MLA_HEADS:
            st_next = scores_t(h + 1)
        m = jnp.max(st, axis=0, keepdims=True)
        e = jnp.exp2(st - m).astype(BF16)
        ot = jnp.dot(vt_scr[h], e, preferred_element_type=F32)
        outs.append(ot[0:MLA_V, :] * (1.0 / ot[MLA_V:MLA_V + 1, :]))
    o_ref[...] = jnp.concatenate(outs, axis=0).T


def _mla_call(p3, w, rope_tabs, cache, *, layer, nb, t, tq, want_cache):
    has_ctx = cache is not None
    nctx = cache.shape[2] if has_ctx else 0
    s_tot = t + nctx
    half = MLA_KV_LORA

    args = [p3, p3, p3]
    in_specs = [
        pl.BlockSpec((None, tq, SEG), lambda b, i: (b, i, S_CQ)),
        pl.BlockSpec((None, t, half), lambda b, i: (b, 0, 2 * S_KV)),
        pl.BlockSpec((None, t, half), lambda b, i: (b, 0, 2 * S_KV + 1)),
    ]
    if has_ctx:
        args += [rope_tabs[0], rope_tabs[1], cache]
        in_specs += [
            _const_spec((t, MLA_SLAB)),
            _const_spec((t, MLA_SLAB)),
            pl.BlockSpec((None, None, nctx, MLA_KV_LORA + MLA_ROPE), lambda b, i: (b, layer, 0, 0)),
        ]
    args += [w['mla_qn'], w['mla_kvn'], w['mla_wq']]
    in_specs += [_pick_spec(w['mla_qn'], layer), _pick_spec(w['mla_kvn'], layer), _pick_spec(w['mla_wq'], layer)]
    if has_ctx:
        args += [w['mla_wqr'], w['mla_wkc']]
        in_specs += [_pick_spec(w['mla_wqr'], layer), _pick_spec(w['mla_wkc'], layer)]
    args += [w['mla_wk'], w['mla_wv']]
    in_specs += [_pick_spec(w['mla_wk'], layer), _pick_spec(w['mla_wv'], layer)]
    out_specs = [pl.BlockSpec((None, tq, SEG), lambda b, i: (b, i, 0))]
    out_shape = [jax.ShapeDtypeStruct((nb, t, SEG), F32)]
    if want_cache:
        out_specs.append(pl.BlockSpec((None, t, MLA_KV_LORA), lambda b, i: (b, 0, 0)))
        out_shape.append(jax.ShapeDtypeStruct((nb, t, MLA_KV_LORA), F32))
    return pl.pallas_call(
        functools.partial(_mla_kernel, t=t, tq=tq, nctx=nctx, want_cache=want_cache),
        grid=(nb, t // tq),
        in_specs=in_specs,
        out_specs=out_specs,
        out_shape=out_shape,
        scratch_shapes=[pltpu.VMEM((MLA_HEADS, s_tot, MLA_SLAB), BF16),
                        pltpu.VMEM((MLA_HEADS, MLA_VT_ROWS, s_tot), BF16)],
        compiler_params=_cparams(("arbitrary", "arbitrary")),
        name="mla",
    )(*args)


def _merge_kernel(x_ref, mod_ref, npre_ref, npost_ref, yf_ref, yb_ref, u_ref, ret_ref,
                  cx_ref, cb_ref, cc_ref, cxp_ref, ccp_ref, cxn_ref, ccn_ref, mla_ref,
                  s5d_ref, wglu_ref, cw_ref, cbias_ref, wbr_ref, wg_ref, bg_ref, wo_ref,
                  o_ref, h_scr, *, tm, nt):
    i = pl.program_id(1)
    x = x_ref[...]
    sh = mod_ref[3:4, :]
    sc = mod_ref[4:5, :]
    gt = mod_ref[5:6, :]
    h_scr[...] = (_rms(x, npre_ref[...]) * (1.0 + sc) + sh).astype(BF16)

    y = yf_ref[...] + yb_ref[...] + s5d_ref[...] * u_ref[...].astype(F32)
    z5 = _gelu_tanh(y)
    b_s5 = z5 * _sigmoid(_dot(z5, wglu_ref[...]))

    z = cc_ref[...].astype(F32) * cx_ref[...].astype(F32)
    zprev = ((ccp_ref[...].astype(F32) * cxp_ref[...].astype(F32))[HALO - 1:HALO, :]
             * jnp.where(i > 0, 1.0, 0.0))
    znext = ((ccn_ref[...].astype(F32) * cxn_ref[...].astype(F32))[0:1, :]
             * jnp.where(i < nt - 1, 1.0, 0.0))
    row = lax.broadcasted_iota(jnp.int32, (tm, SEG), 0)
    z_dn = jnp.where(row == 0, zprev, pltpu.roll(z, 1, axis=0))
    z_up = jnp.where(row == tm - 1, znext, pltpu.roll(z, tm - 1, axis=0))
    b_conv = cb_ref[...].astype(F32) * (cbias_ref[...] + z_dn * cw_ref[0:1, :] + z * cw_ref[1:2, :]
                                        + z_up * cw_ref[2:3, :])

    branches = (b_s5, ret_ref[...], b_conv, mla_ref[...])
    merged = jnp.zeros((tm, D_MODEL), F32)
    for n in range(N_BRANCH):
        gate = _sigmoid(jnp.dot(h_scr[...], wg_ref[:, n * D_MODEL:(n + 1) * D_MODEL],
                                preferred_element_type=F32) + bg_ref[:, n * D_MODEL:(n + 1) * D_MODEL])
        merged = merged + gate * _dot(branches[n], wbr_ref[n])
    m = _dot(merged, wo_ref[...])
    o_ref[...] = x + gt * _rms(m, npost_ref[...])


def _merge_call(x, w, y_s5, ret_o, p3, mla_o, *, layer, nb, t, tm, row_of_batch):
    nt = t // tm
    hb = tm // HALO
    n_halo = t // HALO

    def dir_spec(d):
        return pl.BlockSpec((None, None, tm, SEG), lambda b, i: (d, b, i, 0))

    def seg_spec(seg):
        return pl.BlockSpec((None, tm, SEG), lambda b, i: (b, i, seg))

    def prev_spec(seg):
        return pl.BlockSpec((None, HALO, SEG), lambda b, i: (b, jnp.maximum(i * hb - 1, 0), seg))

    def next_spec(seg):
        return pl.BlockSpec((None, HALO, SEG), lambda b, i: (b, jnp.minimum((i + 1) * hb, n_halo - 1), seg))

    return pl.pallas_call(
        functools.partial(_merge_kernel, tm=tm, nt=nt),
        grid=(nb, nt),
        in_specs=[
            pl.BlockSpec((tm, D_MODEL), lambda b, i: (b * nt + i, 0)),
            pl.BlockSpec((None, None, N_MOD, D_MODEL), lambda b, i: (layer, row_of_batch(b), 0, 0)),
            _pick_spec(w['npre'], layer * N_NORM + 1),
            _pick_spec(w['npost'], layer * N_NORM + 1),
            dir_spec(0), dir_spec(1), seg_spec(S_U),
            pl.BlockSpec((None, tm, SEG), lambda b, i: (b, i, 0)),
            seg_spec(S_CX), seg_spec(S_CB), seg_spec(S_CC),
            prev_spec(S_CX), prev_spec(S_CC), next_spec(S_CX), next_spec(S_CC),
            pl.BlockSpec((None, tm, SEG), lambda b, i: (b, i, 0)),
            _pick_spec(w['s5_d'], layer), _pick_spec(w['s5_wglu'], layer), _pick_spec(w['conv_w'], layer),
            _pick_spec(w['conv_b'], layer), _resident_spec(w['w_branch'].shape[1:], (layer,)),
            _resident_spec(w['w_gate'].shape[1:], (layer,)), _pick_spec(w['b_gate'], layer),
            _resident_spec(w['w_o'].shape[1:], (layer,)),
        ],
        out_specs=pl.BlockSpec((tm, D_MODEL), lambda b, i: (b * nt + i, 0)),
        out_shape=jax.ShapeDtypeStruct((nb * t, D_MODEL), F32),
        scratch_shapes=[pltpu.VMEM((tm, D_MODEL), BF16)],
        compiler_params=_cparams(("arbitrary", "arbitrary")),
        name="merge",
    )(x, w['mod'], w['npre'], w['npost'], y_s5, y_s5, p3, ret_o, p3, p3, p3, p3, p3, p3, p3, mla_o,
      w['s5_d'], w['s5_wglu'], w['conv_w'], w['conv_b'], w['w_branch'], w['w_gate'], w['b_gate'], w['w_o'])


def _zeros_like_cols(w, n):
    return jnp.zeros(w.shape[:-1] + (n,), w.dtype)


def _axial_angles(t, dim):
    rows = t // GRID_W
    row = np.repeat(np.arange(rows, dtype=np.float64), GRID_W)
    col = np.tile(np.arange(GRID_W, dtype=np.float64), rows)
    quarter = dim // 4
    inv = ROPE_BASE ** (-np.arange(quarter, dtype=np.float64) / quarter)
    ang = np.concatenate([row[:, None] * inv, col[:, None] * inv], axis=-1)
    return np.cos(ang).astype(np.float32), np.sin(ang).astype(np.float32)


def _rope_tables(t):
    cos, sin = _axial_angles(t, RET_DIM)
    ret_cos = np.tile(np.concatenate([cos, cos], -1), (1, 2))
    ret_sin = np.tile(np.concatenate([sin, sin], -1), (1, 2))
    cos, sin = _axial_angles(t, MLA_ROPE)
    pad = MLA_SLAB - MLA_NOPE - MLA_ROPE
    mla_cos = np.concatenate([np.ones((t, MLA_NOPE), np.float32), cos, cos, np.zeros((t, pad), np.float32)], -1)
    mla_sin = np.concatenate([np.zeros((t, MLA_NOPE), np.float32), sin, sin, np.zeros((t, pad), np.float32)], -1)
    return ((jnp.asarray(ret_cos), jnp.asarray(ret_sin)), (jnp.asarray(mla_cos), jnp.asarray(mla_sin)))


def _place_rope_key(w):
    return jnp.concatenate([_zeros_like_cols(w, MLA_NOPE), w,
                            _zeros_like_cols(w, MLA_SLAB - MLA_NOPE - MLA_ROPE)], -1)


def _in_weights(w_in):
    sizes = (BRANCH_WIDTH,) * 8 + (MLA_Q_LORA, MLA_KV_LORA, MLA_ROPE)
    offs = np.cumsum((0,) + sizes)
    u, rq, rk, rv, rg, cx, cb, cc, cq, ckv, kr = [w_in[..., offs[i]:offs[i + 1]] for i in range(11)]
    rk = rk * (RET_DIM ** -0.5)
    cols = [u, rq, rk, rv, rg, cx, cb, cc,
            jnp.concatenate([cq, _zeros_like_cols(cq, SEG - MLA_Q_LORA)], -1),
            jnp.concatenate([ckv, _place_rope_key(kr)], -1)]
    return jnp.concatenate([c.astype(BF16) for c in cols], axis=-1)


def _mla_weights(w_uq, w_ukv):
    nl = w_uq.shape[0]
    dq = MLA_NOPE + MLA_ROPE
    w4 = w_uq.reshape(nl, MLA_Q_LORA, MLA_HEADS, dq)
    nope, ropew = w4[..., :MLA_NOPE], w4[..., MLA_NOPE:]
    half = MLA_ROPE // 2
    rot = jnp.concatenate([-ropew[..., half:], ropew[..., :half]], -1)
    zpad = jnp.zeros((nl, MLA_Q_LORA, MLA_HEADS, MLA_SLAB - dq), F32)
    wq = jnp.concatenate([nope, ropew, zpad], -1).reshape(nl, MLA_Q_LORA, MLA_HEADS * MLA_SLAB)
    wqr = jnp.concatenate([jnp.zeros_like(nope), rot, zpad], -1).reshape(nl, MLA_Q_LORA, MLA_HEADS * MLA_SLAB)
    rpad = jnp.zeros((nl, SEG - MLA_Q_LORA, MLA_HEADS * MLA_SLAB), F32)
    wq = jnp.concatenate([wq, rpad], 1).astype(BF16)
    wqr = jnp.concatenate([wqr, rpad], 1).astype(BF16)
    kv4 = w_ukv.reshape(nl, MLA_KV_LORA, MLA_HEADS, MLA_NOPE + MLA_V)
    wk = jnp.concatenate([kv4[..., :MLA_NOPE], jnp.zeros((nl, MLA_KV_LORA, MLA_HEADS, MLA_SLAB - MLA_NOPE), F32)],
                         -1).reshape(nl, MLA_KV_LORA, MLA_HEADS * MLA_SLAB)
    place = np.zeros((MLA_ROPE, MLA_HEADS, MLA_SLAB), np.float32)
    place[np.arange(MLA_ROPE), :, MLA_NOPE + np.arange(MLA_ROPE)] = 1.0
    place = jnp.broadcast_to(jnp.asarray(place.reshape(MLA_ROPE, MLA_HEADS * MLA_SLAB)),
                             (nl, MLA_ROPE, MLA_HEADS * MLA_SLAB))
    wkc = jnp.concatenate([wk, place], 1).astype(BF16)
    wv = kv4[..., MLA_NOPE:].reshape(nl, MLA_KV_LORA, MLA_HEADS * MLA_V).astype(BF16)
    return wq, wqr, wk.astype(BF16), wkc, wv


def _s5_tables(lam_re, lam_im, log_dt, b_re, b_im, c_re, c_im):
    nl = lam_re.shape[0]
    dt = jnp.exp(log_dt)[..., None]
    mag = jnp.exp(lam_re * dt)
    ab_re = mag * jnp.cos(lam_im * dt)
    ab_im = mag * jnp.sin(lam_im * dt)
    den = lam_re * lam_re + lam_im * lam_im
    f_re = ((ab_re - 1.0) * lam_re + ab_im * lam_im) / den
    f_im = (ab_im * lam_re - (ab_re - 1.0) * lam_im) / den
    bb_re = f_re[..., None] * b_re - f_im[..., None] * b_im
    bb_im = f_re[..., None] * b_im + f_im[..., None] * b_re
    eye = jnp.eye(S5_GROUPS, dtype=F32)
    blk_in = lambda m: jnp.einsum('ldgph,gk->ldghkp', m, eye).reshape(nl, 2, BRANCH_WIDTH, S5_LANES)
    bblk = jnp.concatenate([blk_in(bb_re), blk_in(bb_im)], -1).astype(BF16)
    blk_out = lambda m: jnp.einsum('lghp,gk->lgpkh', m, eye).reshape(nl, S5_LANES, BRANCH_WIDTH)
    cblk = jnp.concatenate([blk_out(c_re), -blk_out(c_im)], 1).astype(BF16)
    a_tab = jnp.stack([ab_re.reshape(nl, 2, S5_LANES), ab_im.reshape(nl, 2, S5_LANES)], axis=2)
    return bblk, a_tab, cblk


def _ret_tables(decay_logit, c):
    lg = jax.nn.log_sigmoid(decay_logit.astype(F32))
    lf = lg[:, 0][:, :, None, None]
    lb = lg[:, 1][:, :, None, None]
    idx = jnp.arange(c, dtype=F32)
    diff = idx[:, None] - idx[None, :]
    dm = jnp.where(diff >= 0, jnp.exp(lf * jnp.maximum(diff, 0.0)), jnp.exp(lb * jnp.maximum(-diff, 0.0)))
    lfl = jnp.repeat(lg[:, 0], RET_DIM, axis=-1)
    lbl = jnp.repeat(lg[:, 1], RET_DIM, axis=-1)
    qdf = jnp.exp(lfl[:, None, :] * (idx[:, None] + 1.0))
    qdb = jnp.exp(lbl[:, None, :] * (c - idx[:, None]))
    kdft = jnp.exp(lfl[:, :, None] * (c - 1.0 - idx[None, :]))
    kdbt = jnp.exp(lbl[:, :, None] * idx[None, :])
    nl = lg.shape[0]
    cdf = jnp.broadcast_to(jnp.exp(lfl * c)[:, :, None], (nl, SEG, SEG))
    cdb = jnp.broadcast_to(jnp.exp(lbl * c)[:, :, None], (nl, SEG, SEG))
    head = np.arange(SEG) // RET_DIM
    hm = jnp.asarray((head[None, :] == np.arange(RET_HEADS)[:, None]).astype(np.float32))
    bd_np = (head[:, None] == head[None, :]).astype(np.float32)
    bd = jnp.asarray(bd_np)
    gm = jnp.asarray(bd_np / RET_DIM, BF16)
    return (dm, qdf, qdb, kdft, kdbt, cdf, cdb), (hm, bd, gm)


def _layer(x, w, *, layer, nb, t, tm, tm_ffn, row_of_batch, ctx, rope_tabs):
    is_ctx_pass = ctx is None
    row_of_tile = lambda i: row_of_batch((i * tm_ffn) // t)
    x = _ffn_call(x, w, layer=layer, k=0, tm=tm_ffn, row_of_tile=row_of_tile)

    p = _in_call(x, w, layer=layer, nb=nb, t=t, tm=tm, row_of_batch=row_of_batch)
    p3 = p.reshape(nb, t, NSEG * SEG)

    if is_ctx_pass:
        s5_0, s0_layer = jnp.zeros((1, 2, 2, nb, S5_LANES), F32), 0
        ret_0, mla_cache, ret_rope, mla_rope = None, None, None, None
    else:
        s5_0, ret_0, mla_cache = ctx
        s0_layer = layer
        ret_rope, mla_rope = rope_tabs
    s5_out = _s5_call(p3, w, s5_0, layer=layer, s0_layer=s0_layer, nb=nb, t=t, want_fin=is_ctx_pass)
    ret_out = _ret_call(p3, w, ret_0, ret_rope, layer=layer, nb=nb, t=t, want_fin=is_ctx_pass)
    mla_out = _mla_call(p3, w, mla_rope, mla_cache, layer=layer, nb=nb, t=t, tq=min(t, MLA_TQ),
                        want_cache=is_ctx_pass)

    x = _merge_call(x, w, s5_out[0], ret_out[0], p3, mla_out[0], layer=layer, nb=nb, t=t, tm=tm,
                    row_of_batch=row_of_batch)
    x = _ffn_call(x, w, layer=layer, k=2, tm=tm_ffn, row_of_tile=row_of_tile)

    state = None
    if is_ctx_pass:
        kr = p3[:, :, S_KV * SEG + MLA_KV_LORA + MLA_NOPE:S_KV * SEG + MLA_KV_LORA + MLA_NOPE + MLA_ROPE]
        cache = jnp.concatenate([mla_out[1], kr.astype(F32)], axis=-1)
        state = (s5_out[1], ret_out[1], cache)
    return x, state


def kernel(x_prompt, x_sample, state_s5, state_ret, cache_mla, c, c_ctx, w_mod, b_mod, norm_pre, norm_post, ffn_w1, ffn_w3, ffn_w2, w_in, s5_lam_re, s5_lam_im, s5_log_dt, s5_b_re, s5_b_im, s5_c_re, s5_c_im, s5_d, s5_w_glu, ret_decay, ret_gn, conv_w, conv_b, mla_q_norm, mla_w_uq, mla_kv_norm, mla_w_ukv, w_branch, w_gate, b_gate, w_o):
    bp, tp, _ = x_prompt.shape
    bs, ts, _ = x_sample.shape
    assert 1 + bs <= MOD_ROWS

    cond = jnp.concatenate([c_ctx[None, :], c, jnp.zeros((MOD_ROWS - 1 - bs, D_MODEL), F32)], axis=0)
    mod = _mod_call(cond, w_mod, b_mod).reshape(DEPTH, MOD_ROWS, N_MOD, D_MODEL)

    wq, wqr, wk, wkc, wv = _mla_weights(mla_w_uq, mla_w_ukv)
    bblk, a_tab, cblk = _s5_tables(s5_lam_re, s5_lam_im, s5_log_dt, s5_b_re, s5_b_im, s5_c_re, s5_c_im)
    ret_layer_tabs, ret_const_tabs = _ret_tables(ret_decay, RET_CHUNK)
    w = dict(
        mod=mod,
        npre=norm_pre.reshape(DEPTH * N_NORM, 1, D_MODEL),
        npost=norm_post.reshape(DEPTH * N_NORM, 1, D_MODEL),
        ffn=(ffn_w1.astype(BF16), ffn_w3.astype(BF16), ffn_w2.astype(BF16)),
        w_in=_in_weights(w_in),
        s5_bblk=bblk, s5_a=a_tab, s5_cblk=cblk,
        s5_d=s5_d[:, None, :], s5_wglu=s5_w_glu.astype(BF16),
        ret_layer_tabs=ret_layer_tabs, ret_const_tabs=ret_const_tabs, ret_gn=ret_gn[:, None, :],
        conv_w=conv_w, conv_b=conv_b[:, None, :],
        mla_qn=jnp.concatenate([mla_q_norm, jnp.zeros((DEPTH, SEG - MLA_Q_LORA), F32)], -1)[:, None, :],
        mla_kvn=mla_kv_norm[:, None, :],
        mla_wq=wq, mla_wqr=wqr, mla_wk=wk, mla_wkc=wkc, mla_wv=wv,
        w_branch=w_branch.astype(BF16), w_gate=w_gate.astype(BF16), b_gate=b_gate[:, None, :],
        w_o=w_o.astype(BF16),
    )
    rope_tabs = _rope_tables(ts)
    s5_0 = state_s5.transpose(1, 2, 5, 0, 3, 4).reshape(DEPTH, 2, 2, bs, S5_LANES)

    xp = x_prompt.reshape(bp * tp, D_MODEL)
    xs = x_sample.reshape(bs * ts, D_MODEL)
    s5_list, ret_list, mla_list = [], [], []
    for l in range(DEPTH):
        xp, (s5_s, ret_s, mla_c) = _layer(xp, w, layer=l, nb=bp, t=tp, tm=TM_CTX, tm_ffn=TM_CTX_FFN,
                                          row_of_batch=lambda b: 0, ctx=None, rope_tabs=None)
        s5_list.append(s5_s)
        ret_list.append(ret_s)
        mla_list.append(mla_c)
        xs, _ = _layer(xs, w, layer=l, nb=bs, t=ts, tm=TM_LAT, tm_ffn=TM_LAT_FFN, row_of_batch=lambda b: 1 + b,
                       ctx=(s5_0, state_ret, cache_mla), rope_tabs=rope_tabs)
    new_s5 = jnp.stack(s5_list, 0).reshape(DEPTH, 2, 2, bp, S5_GROUPS, S5_STATE).transpose(3, 0, 1, 4, 5, 2)
    return (xp.reshape(bp, tp, D_MODEL), xs.reshape(bs, ts, D_MODEL),
            new_s5, jnp.stack(ret_list, axis=1), jnp.stack(mla_list, axis=1))
```

```python
import functools

import numpy as np
import jax
import jax.numpy as jnp
from jax import lax
from jax.experimental import pallas as pl
from jax.experimental.pallas import tpu as pltpu

F32 = jnp.float32
BF16 = jnp.bfloat16

D_MODEL = 1024
DEPTH = 2
GRID_W = 64
EPS = 1e-6
LOG2_E = float(np.log2(np.e))
ROPE_BASE = 10000.0
N_BRANCH = 4
BRANCH_WIDTH = D_MODEL // 4
N_MOD = 9
N_NORM = 3
D_FF = 2816
S5_GROUP = 16
S5_GROUPS = BRANCH_WIDTH // S5_GROUP
S5_STATE = 64
S5_LANES = S5_GROUPS * S5_STATE
RET_HEADS = 4
RET_DIM = BRANCH_WIDTH // RET_HEADS
CONV_K = 3
MLA_HEADS = 4
MLA_Q_LORA = 192
MLA_KV_LORA = 128
MLA_NOPE = 64
MLA_ROPE = 32
MLA_V = BRANCH_WIDTH // MLA_HEADS
MLA_SLAB = 128
MLA_VT_ROWS = MLA_V + 16

SUBLANES = 8
S5_SUBCHUNKS = 8
S5_ROWS = 256
SEG = 256
FF_CHUNK = 256
N_FF_CHUNKS = D_FF // FF_CHUNK
MOD_ROWS = 16
HALO = 16
RET_CHUNK = 256
MLA_TQ = 1024
TM_CTX, TM_CTX_FFN = 256, 512
TM_LAT, TM_LAT_FFN = 1024, 1024
V7X_VMEM_LIMIT = 56 * 1024 * 1024

S_U, S_RQ, S_RK, S_RV, S_RG, S_CX, S_CB, S_CC, S_CQ, S_KV = range(10)
NSEG = 10
LANES = 128


def _cparams(sem):
    return pltpu.CompilerParams(dimension_semantics=sem, vmem_limit_bytes=V7X_VMEM_LIMIT)


def _dot(a, b):
    return jnp.dot(a.astype(BF16), b.astype(BF16), preferred_element_type=F32)


def _sigmoid(x):
    return 1.0 / (1.0 + jnp.exp(-x))


def _rms(x, g, n=None):
    n = x.shape[-1] if n is None else n
    ms = jnp.sum(x * x, axis=-1, keepdims=True) * (1.0 / n)
    return x * lax.rsqrt(ms + EPS) * g


def _rot_half_lanes(x, width):
    half = width // 2
    lane = lax.broadcasted_iota(jnp.int32, (x.shape[0], LANES), 1)
    first = (lane & (width - 1)) < half
    outs = []
    for j in range(x.shape[1] // LANES):
        v = x[:, j * LANES:(j + 1) * LANES]
        outs.append(jnp.where(first, -pltpu.roll(v, LANES - half, axis=1), pltpu.roll(v, half, axis=1)))
    return outs[0] if len(outs) == 1 else jnp.concatenate(outs, axis=-1)


def _gelu_tanh(x):
    return 0.5 * x * (1.0 + jnp.tanh(np.sqrt(2.0 / np.pi) * (x + 0.044715 * (x * x * x))))


def _const_spec(shape):
    nd = len(shape)
    return pl.BlockSpec(shape, lambda *_: (0,) * nd)


def _pick_spec(arr, *lead):
    shape = arr.shape[len(lead):]
    return pl.BlockSpec((None,) * len(lead) + tuple(shape), lambda *_: tuple(lead) + (0,) * len(shape))


def _resident_spec(shape, lead=()):
    nd = len(shape)
    return pl.BlockSpec((None,) * len(lead) + tuple(shape), lambda *_: tuple(lead) + (0,) * nd,
                        pipeline_mode=pl.Buffered(1))


def _mod_kernel(c_ref, w_ref, b_ref, o_ref):
    c = c_ref[...]
    o_ref[...] = _dot(c * _sigmoid(c), w_ref[...]) + b_ref[...]


def _mod_call(cond, w_mod, b_mod):
    ncol = N_MOD * D_MODEL
    tn = D_MODEL
    return pl.pallas_call(
        _mod_kernel,
        grid=(DEPTH, ncol // tn),
        in_specs=[
            pl.BlockSpec((MOD_ROWS, D_MODEL), lambda l, j: (0, 0)),
            pl.BlockSpec((None, D_MODEL, tn), lambda l, j: (l, 0, j)),
            pl.BlockSpec((None, 1, tn), lambda l, j: (l, 0, j)),
        ],
        out_specs=pl.BlockSpec((None, MOD_ROWS, tn), lambda l, j: (l, 0, j)),
        out_shape=jax.ShapeDtypeStruct((DEPTH, MOD_ROWS, ncol), F32),
        compiler_params=_cparams(("arbitrary", "arbitrary")),
        name="mod",
    )(cond, w_mod, b_mod.reshape(DEPTH, 1, ncol))


def _ffn_kernel(x_ref, mod_ref, npre_ref, npost_ref, w1_ref, w3_ref, w2_ref, o_ref,
                h_scr, g_scr, *, k):
    x = x_ref[...]
    sh = mod_ref[3 * k:3 * k + 1, :]
    sc = mod_ref[3 * k + 1:3 * k + 2, :]
    gt = mod_ref[3 * k + 2:3 * k + 3, :]
    h_scr[...] = (_rms(x, npre_ref[...]) * (1.0 + sc) + sh).astype(BF16)
    for j in range(N_FF_CHUNKS):
        cols = slice(j * FF_CHUNK, (j + 1) * FF_CHUNK)
        a = jnp.dot(h_scr[...], w1_ref[:, cols], preferred_element_type=F32)
        b = jnp.dot(h_scr[...], w3_ref[:, cols], preferred_element_type=F32)
        g_scr[:, cols] = (a * _sigmoid(a) * b).astype(BF16)
    f = jnp.dot(g_scr[...], w2_ref[...], preferred_element_type=F32)
    o_ref[...] = x + (0.5 * gt) * _rms(f, npost_ref[...])


def _ffn_call(x, w, *, layer, k, tm, row_of_tile):
    n = x.shape[0]
    lj = (layer, k // 2)
    w1, w3, w2 = w['ffn']
    return pl.pallas_call(
        functools.partial(_ffn_kernel, k=k),
        grid=(n // tm,),
        in_specs=[
            pl.BlockSpec((tm, D_MODEL), lambda i: (i, 0)),
            pl.BlockSpec((None, None, N_MOD, D_MODEL), lambda i: (layer, row_of_tile(i), 0, 0)),
            _pick_spec(w['npre'], layer * N_NORM + k),
            _pick_spec(w['npost'], layer * N_NORM + k),
            _resident_spec((D_MODEL, D_FF), lj),
            _resident_spec((D_MODEL, D_FF), lj),
            _resident_spec((D_FF, D_MODEL), lj),
        ],
        out_specs=pl.BlockSpec((tm, D_MODEL), lambda i: (i, 0)),
        out_shape=jax.ShapeDtypeStruct((n, D_MODEL), F32),
        scratch_shapes=[pltpu.VMEM((tm, D_MODEL), BF16), pltpu.VMEM((tm, D_FF), BF16)],
        compiler_params=_cparams(("arbitrary",)),
        name="ffn",
    )(x, w['mod'], w['npre'], w['npost'], w1, w3, w2)


def _in_kernel(x_ref, mod_ref, npre_ref, w_ref, p_ref, h_scr):
    sh = mod_ref[3:4, :]
    sc = mod_ref[4:5, :]
    h_scr[...] = (_rms(x_ref[...], npre_ref[...]) * (1.0 + sc) + sh).astype(BF16)
    for j in range(NSEG):
        cols = slice(j * SEG, (j + 1) * SEG)
        p_ref[:, cols] = jnp.dot(h_scr[...], w_ref[:, cols], preferred_element_type=F32).astype(BF16)


def _in_call(x, w, *, layer, nb, t, tm, row_of_batch):
    tt = t // tm
    nseg = NSEG
    return pl.pallas_call(
        _in_kernel,
        grid=(nb, tt),
        in_specs=[
            pl.BlockSpec((tm, D_MODEL), lambda b, i: (b * tt + i, 0)),
            pl.BlockSpec((None, None, N_MOD, D_MODEL), lambda b, i: (layer, row_of_batch(b), 0, 0)),
            _pick_spec(w['npre'], layer * N_NORM + 1),
            _resident_spec((D_MODEL, nseg * SEG), (layer,)),
        ],
        out_specs=pl.BlockSpec((tm, nseg * SEG), lambda b, i: (b * tt + i, 0)),
        out_shape=jax.ShapeDtypeStruct((nb * t, nseg * SEG), BF16),
        scratch_shapes=[pltpu.VMEM((tm, D_MODEL), BF16)],
        compiler_params=_cparams(("arbitrary", "arbitrary")),
        name="in_proj",
    )(x, w['mod'], w['npre'], w['w_in'])


def _s5_kernel(*refs, nb, steps, want_fin):
    if want_fin:
        u_ref, bb_ref, a_ref, cc_ref, s0_ref, y_ref, fin_ref, s_scr, bu_scr, u2_scr, y2_scr = refs
    else:
        u_ref, bb_ref, a_ref, cc_ref, s0_ref, y_ref, s_scr, bu_scr, u2_scr, y2_scr = refs
        fin_ref = None
    d = pl.program_id(0)
    c = pl.program_id(1)
    span = steps * S5_SUBCHUNKS
    ngrp = nb // SUBLANES

    @pl.when(c == 0)
    def _():
        s_scr[...] = s0_ref[...]

    u_all = u_ref[...].astype(F32).reshape(nb * span, SEG)
    for j in range(SEG // LANES):
        u2_scr[j] = u_all[:, j * LANES:(j + 1) * LANES]

    a_re = jnp.broadcast_to(a_ref[0:1, :], (SUBLANES, S5_LANES))
    a_im = jnp.broadcast_to(a_ref[1:2, :], (SUBLANES, S5_LANES))
    s_re = [s_scr[0, g * SUBLANES:(g + 1) * SUBLANES, :] for g in range(ngrp)]
    s_im = [s_scr[1, g * SUBLANES:(g + 1) * SUBLANES, :] for g in range(ngrp)]
    re_cols = slice(0, S5_LANES)
    im_cols = slice(S5_LANES, 2 * S5_LANES)

    def time_of(k, i):
        j = k * steps + i
        return jnp.where(d == 0, j, span - 1 - j)

    def project_in(k):
        u_tb = jnp.concatenate(
            [jnp.concatenate([u2_scr[j, pl.ds(time_of(k, i), nb, stride=span), :]
                              for j in range(SEG // LANES)], axis=-1) for i in range(steps)], axis=0)
        bu_scr[k] = _dot(u_tb, bb_ref[...])

    project_in(0)
    for k in range(S5_SUBCHUNKS):
        if k + 1 < S5_SUBCHUNKS:
            project_in(k + 1)
        for i in range(steps):
            for g in range(ngrp):
                rr = slice(i * nb + g * SUBLANES, i * nb + (g + 1) * SUBLANES)
                n_re = a_re * s_re[g] - a_im * s_im[g] + bu_scr[k, rr, re_cols]
                n_im = a_re * s_im[g] + a_im * s_re[g] + bu_scr[k, rr, im_cols]
                bu_scr[k, rr, re_cols] = n_re
                bu_scr[k, rr, im_cols] = n_im
                s_re[g], s_im[g] = n_re, n_im
        y_tb = _dot(bu_scr[k], cc_ref[...])
        for i in range(steps):
            for j in range(SEG // LANES):
                y2_scr[j, pl.ds(time_of(k, i), nb, stride=span), :] = (
                    y_tb[i * nb:(i + 1) * nb, j * LANES:(j + 1) * LANES])
    y_ref[...] = jnp.concatenate([y2_scr[j] for j in range(SEG // LANES)], axis=-1).reshape(nb, span, SEG)
    for g in range(ngrp):
        s_scr[0, g * SUBLANES:(g + 1) * SUBLANES, :] = s_re[g]
        s_scr[1, g * SUBLANES:(g + 1) * SUBLANES, :] = s_im[g]
    if want_fin:
        fin_ref[...] = s_scr[...]


def _s5_call(p3, w, s0, *, layer, s0_layer, nb, t, want_fin):
    steps = S5_ROWS // nb
    span = steps * S5_SUBCHUNKS
    nch = t // span
    rows = S5_ROWS

    def chunk(d, c):
        return jnp.where(d == 0, c, nch - 1 - c)

    out_specs = [pl.BlockSpec((None, nb, span, SEG), lambda d, c: (d, 0, chunk(d, c), 0))]
    out_shape = [jax.ShapeDtypeStruct((2, nb, t, SEG), F32)]
    if want_fin:
        out_specs.append(pl.BlockSpec((None, 2, nb, S5_LANES), lambda d, c: (d, 0, 0, 0)))
        out_shape.append(jax.ShapeDtypeStruct((2, 2, nb, S5_LANES), F32))
    return pl.pallas_call(
        functools.partial(_s5_kernel, nb=nb, steps=steps, want_fin=want_fin),
        grid=(2, nch),
        in_specs=[
            pl.BlockSpec((nb, span, SEG), lambda d, c: (0, chunk(d, c), S_U)),
            pl.BlockSpec((None, None, SEG, 2 * S5_LANES), lambda d, c: (layer, d, 0, 0)),
            pl.BlockSpec((None, None, 2, S5_LANES), lambda d, c: (layer, d, 0, 0)),
            _pick_spec(w['s5_cblk'], layer),
            pl.BlockSpec((None, None, 2, nb, S5_LANES), lambda d, c: (s0_layer, d, 0, 0, 0)),
        ],
        out_specs=out_specs,
        out_shape=out_shape,
        scratch_shapes=[pltpu.VMEM((2, nb, S5_LANES), F32),
                        pltpu.VMEM((S5_SUBCHUNKS, rows, 2 * S5_LANES), F32),
                        pltpu.VMEM((SEG // LANES, nb * span, LANES), F32),
                        pltpu.VMEM((SEG // LANES, nb * span, LANES), F32)],
        compiler_params=_cparams(("arbitrary", "arbitrary")),
        name="s5_scan",
    )(p3, w['s5_bblk'], w['s5_a'], w['s5_cblk'], s0)


def _ret_kernel(*refs, rope, has_s0, want_fin, nch):
    refs = list(refs)
    q_ref, k_ref, v_ref, g_ref = refs[:4]
    del refs[:4]
    if rope:
        cos_ref, sin_ref = refs[:2]
        del refs[:2]
    (dm_ref, qdf_ref, qdb_ref, kdft_ref, kdbt_ref, cdf_ref, cdb_ref, hm_ref, bd_ref, gm_ref,
     gn_ref) = refs[:11]
    del refs[:11]
    if has_s0:
        s0_ref = refs.pop(0)
    o_ref = refs.pop(0)
    fin_ref = refs.pop(0) if want_fin else None
    sb_all, sf_scr, sb_scr = refs

    p = pl.program_id(1)
    c = pl.program_id(2)
    hd = RET_DIM

    def init_state(dst, d):
        dst[...] = jnp.zeros_like(dst)
        if has_s0:
            for h in range(RET_HEADS):
                dst[h * hd:(h + 1) * hd, h * hd:(h + 1) * hd] = s0_ref[d, h]

    def write_final(src, d):
        for h in range(RET_HEADS):
            fin_ref[d, h] = src[h * hd:(h + 1) * hd, h * hd:(h + 1) * hd]

    def rope_tabs():
        cos = cos_ref[...]
        sin = sin_ref[...]
        return jnp.concatenate([cos, cos], axis=-1), jnp.concatenate([sin, sin], axis=-1)

    def roped_kt():
        k = k_ref[...].astype(F32)
        if rope:
            cos, sin = rope_tabs()
            k = k * cos + _rot_half_lanes(k, RET_DIM) * sin
        return k.T

    @pl.when(p == 0)
    def _():
        ck = nch - 1 - c

        @pl.when(c == 0)
        def _():
            init_state(sb_scr, 1)

        kt = roped_kt()
        s_old = sb_scr[...]
        sb_all[ck] = s_old
        sb_scr[...] = s_old * cdb_ref[...] + _dot(kt * kdbt_ref[...], v_ref[...]) * bd_ref[...]
        if want_fin:
            @pl.when(c == nch - 1)
            def _():
                write_final(sb_scr, 1)

    @pl.when(p == 1)
    def _():
        @pl.when(c == 0)
        def _():
            init_state(sf_scr, 0)

        q = q_ref[...].astype(F32)
        if rope:
            cos, sin = rope_tabs()
            q = q * cos + _rot_half_lanes(q, RET_DIM) * sin
        kt = roped_kt()
        ktb = kt.astype(BF16)
        vb = v_ref[...]
        s_f = sf_scr[...]
        o = _dot(q, s_f) * qdf_ref[...] + _dot(q, sb_all[c]) * qdb_ref[...]
        for h in range(RET_HEADS):
            hm = hm_ref[h:h + 1, :]
            att = _dot(q * hm, ktb) * dm_ref[h]
            o = o + _dot(att, vb) * hm
        sf_scr[...] = s_f * cdf_ref[...] + _dot(kt * kdft_ref[...], vb) * bd_ref[...]
        hi = o.astype(BF16)
        lo = (o - hi.astype(F32)).astype(BF16)
        mu = (jnp.dot(hi, gm_ref[...], preferred_element_type=F32)
              + jnp.dot(lo, gm_ref[...], preferred_element_type=F32))
        oc = o - mu
        var = _dot(oc * oc, gm_ref[...])
        on = oc * lax.rsqrt(var + EPS) * gn_ref[...]
        g = g_ref[...].astype(F32)
        o_ref[...] = (g * _sigmoid(g)) * on
        if want_fin:
            @pl.when(c == nch - 1)
            def _():
                write_final(sf_scr, 0)


def _ret_call(p3, w, s0, rope_tabs, *, layer, nb, t, want_fin):
    tc = RET_CHUNK
    nch = t // tc
    rope = rope_tabs is not None
    has_s0 = s0 is not None

    def kv_chunk(p, c):
        return jnp.where(p == 0, nch - 1 - c, c)

    def q_chunk(p, c):
        return jnp.where(p == 0, 0, c)

    def seg_spec(seg, cmap):
        return pl.BlockSpec((None, tc, SEG), lambda b, p, c: (b, cmap(p, c), seg))

    args = [p3, p3, p3, p3]
    in_specs = [seg_spec(S_RQ, q_chunk), seg_spec(S_RK, kv_chunk), seg_spec(S_RV, kv_chunk),
                seg_spec(S_RG, q_chunk)]
    if rope:
        args += [rope_tabs[0], rope_tabs[1]]
        in_specs += [pl.BlockSpec((tc, 2 * RET_DIM), lambda b, p, c: (kv_chunk(p, c), 0)),
                     pl.BlockSpec((tc, 2 * RET_DIM), lambda b, p, c: (kv_chunk(p, c), 0))]
    args += list(w['ret_layer_tabs']) + list(w['ret_const_tabs']) + [w['ret_gn']]
    in_specs += ([_pick_spec(a, layer) for a in w['ret_layer_tabs']]
                 + [_const_spec(a.shape) for a in w['ret_const_tabs']] + [_pick_spec(w['ret_gn'], layer)])
    if has_s0:
        args.append(s0)
        in_specs.append(pl.BlockSpec((None, None, 2, RET_HEADS, RET_DIM, RET_DIM),
                                     lambda b, p, c: (b, layer, 0, 0, 0, 0)))
    out_specs = [pl.BlockSpec((None, tc, SEG), lambda b, p, c: (b, q_chunk(p, c), 0))]
    out_shape = [jax.ShapeDtypeStruct((nb, t, SEG), F32)]
    if want_fin:
        out_specs.append(pl.BlockSpec((None, 2, RET_HEADS, RET_DIM, RET_DIM),
                                      lambda b, p, c: (b, 0, 0, 0, 0)))
        out_shape.append(jax.ShapeDtypeStruct((nb, 2, RET_HEADS, RET_DIM, RET_DIM), F32))
    return pl.pallas_call(
        functools.partial(_ret_kernel, rope=rope, has_s0=has_s0, want_fin=want_fin, nch=nch),
        grid=(nb, 2, nch),
        in_specs=in_specs,
        out_specs=out_specs,
        out_shape=out_shape,
        scratch_shapes=[pltpu.VMEM((nch, SEG, SEG), F32),
                        pltpu.VMEM((SEG, SEG), F32),
                        pltpu.VMEM((SEG, SEG), F32)],
        compiler_params=_cparams(("arbitrary", "arbitrary", "arbitrary")),
        name="retention",
    )(*args)


def _mla_kernel(*refs, t, tq, nctx, want_cache):
    refs = list(refs)
    cq_ref, ckv_ref, krp_ref = refs[:3]
    del refs[:3]
    has_ctx = nctx > 0
    if has_ctx:
        cm_ref, sm_ref, cache_ref = refs[:3]
        del refs[:3]
    qn_ref, kvn_ref, wq_ref = refs[:3]
    del refs[:3]
    if has_ctx:
        wqr_ref, wkc_ref = refs[:2]
        del refs[:2]
    wk_ref, wv_ref = refs[:2]
    del refs[:2]
    o_ref = refs.pop(0)
    lat_ref = refs.pop(0) if want_cache else None
    k_scr, vt_scr = refs
    qi = pl.program_id(1)
    s_tot = t + nctx

    def store_vt(v, c0, n):
        vt = v.T
        for h in range(MLA_HEADS):
            vt_scr[h, 0:MLA_V, c0:c0 + n] = vt[h * MLA_V:(h + 1) * MLA_V, :].astype(BF16)

    @pl.when(qi == 0)
    def _():
        lat = _rms(ckv_ref[...].astype(F32), kvn_ref[...])
        if want_cache:
            lat_ref[...] = lat
        latb = lat.astype(BF16)
        kr = krp_ref[...].astype(F32)
        if has_ctx:
            kr = kr * cm_ref[...] + _rot_half_lanes(kr, MLA_ROPE) * sm_ref[...]
        kk = jnp.dot(latb, wk_ref[...], preferred_element_type=F32)
        for h in range(MLA_HEADS):
            k_scr[h, 0:t, :] = (kk[:, h * MLA_SLAB:(h + 1) * MLA_SLAB] + kr).astype(BF16)
        store_vt(jnp.dot(latb, wv_ref[...], preferred_element_type=F32), 0, t)
        if has_ctx:
            cache = cache_ref[...].astype(BF16)
            kkc = jnp.dot(cache, wkc_ref[...], preferred_element_type=F32)
            for h in range(MLA_HEADS):
                k_scr[h, t:t + nctx, :] = kkc[:, h * MLA_SLAB:(h + 1) * MLA_SLAB].astype(BF16)
            store_vt(jnp.dot(cache[:, 0:MLA_KV_LORA], wv_ref[...], preferred_element_type=F32), t, nctx)
        tail = lax.broadcasted_iota(jnp.int32, (MLA_VT_ROWS - MLA_V, s_tot), 0)
        for h in range(MLA_HEADS):
            vt_scr[h, MLA_V:MLA_VT_ROWS, :] = jnp.where(tail == 0, 1.0, 0.0).astype(BF16)

    a = _rms(cq_ref[...].astype(F32), qn_ref[...], n=MLA_Q_LORA).astype(BF16)
    q = jnp.dot(a, wq_ref[...], preferred_element_type=F32)
    if has_ctx:
        qrot = jnp.dot(a, wqr_ref[...], preferred_element_type=F32)
        r0 = pl.multiple_of(qi * tq, tq)
        cm = cm_ref[pl.ds(r0, tq), :]
        sm = sm_ref[pl.ds(r0, tq), :]
    scale = (MLA_NOPE + MLA_ROPE) ** -0.5
    def scores_t(h):
        qh = q[:, h * MLA_SLAB:(h + 1) * MLA_SLAB]
        if has_ctx:
            qh = qh * cm + qrot[:, h * MLA_SLAB:(h + 1) * MLA_SLAB] * sm
        qt = (qh * (scale * LOG2_E)).T.astype(BF16)
        return jnp.dot(k_scr[h], qt, preferred_element_type=F32)

    outs = []
    st_next = scores_t(0)
    for h in range(MLA_HEADS):
        st = st_next
        if h + 1 < MLA_HEADS:
            st_next = scores_t(h + 1)
        m = jnp.max(st, axis=0, keepdims=True)
        e = jnp.exp2(st - m).astype(BF16)
        ot = jnp.dot(vt_scr[h], e, preferred_element_type=F32)
        outs.append(ot[0:MLA_V, :] * (1.0 / ot[MLA_V:MLA_V + 1, :]))
    o_ref[...] = jnp.concatenate(outs, axis=0).T


def _mla_call(p3, w, rope_tabs, cache, *, layer, nb, t, tq, want_cache):
    has_ctx = cache is not None
    nctx = cache.shape[2] if has_ctx else 0
    s_tot = t + nctx
    half = MLA_KV_LORA

    args = [p3, p3, p3]
    in_specs = [
        pl.BlockSpec((None, tq, SEG), lambda b, i: (b, i, S_CQ)),
        pl.BlockSpec((None, t, half), lambda b, i: (b, 0, 2 * S_KV)),
        pl.BlockSpec((None, t, half), lambda b, i: (b, 0, 2 * S_KV + 1)),
    ]
    if has_ctx:
        args += [rope_tabs[0], rope_tabs[1], cache]
        in_specs += [
            _const_spec((t, MLA_SLAB)),
            _const_spec((t, MLA_SLAB)),
            pl.BlockSpec((None, None, nctx, MLA_KV_LORA + MLA_ROPE), lambda b, i: (b, layer, 0, 0)),
        ]
    args += [w['mla_qn'], w['mla_kvn'], w['mla_wq']]
    in_specs += [_pick_spec(w['mla_qn'], layer), _pick_spec(w['mla_kvn'], layer), _pick_spec(w['mla_wq'], layer)]
    if has_ctx:
        args += [w['mla_wqr'], w['mla_wkc']]
        in_specs += [_pick_spec(w['mla_wqr'], layer), _pick_spec(w['mla_wkc'], layer)]
    args += [w['mla_wk'], w['mla_wv']]
    in_specs += [_pick_spec(w['mla_wk'], layer), _pick_spec(w['mla_wv'], layer)]
    out_specs = [pl.BlockSpec((None, tq, SEG), lambda b, i: (b, i, 0))]
    out_shape = [jax.ShapeDtypeStruct((nb, t, SEG), F32)]
    if want_cache:
        out_specs.append(pl.BlockSpec((None, t, MLA_KV_LORA), lambda b, i: (b, 0, 0)))
        out_shape.append(jax.ShapeDtypeStruct((nb, t, MLA_KV_LORA), F32))
    return pl.pallas_call(
        functools.partial(_mla_kernel, t=t, tq=tq, nctx=nctx, want_cache=want_cache),
        grid=(nb, t // tq),
        in_specs=in_specs,
        out_specs=out_specs,
        out_shape=out_shape,
        scratch_shapes=[pltpu.VMEM((MLA_HEADS, s_tot, MLA_SLAB), BF16),
                        pltpu.VMEM((MLA_HEADS, MLA_VT_ROWS, s_tot), BF16)],
        compiler_params=_cparams(("arbitrary", "arbitrary")),
        name="mla",
    )(*args)


def _merge_kernel(x_ref, mod_ref, npre_ref, npost_ref, yf_ref, yb_ref, u_ref, ret_ref,
                  cx_ref, cb_ref, cc_ref, cxp_ref, ccp_ref, cxn_ref, ccn_ref, mla_ref,
                  s5d_ref, wglu_ref, cw_ref, cbias_ref, wbr_ref, wg_ref, bg_ref, wo_ref,
                  o_ref, h_scr, *, tm, nt):
    i = pl.program_id(1)
    x = x_ref[...]
    sh = mod_ref[3:4, :]
    sc = mod_ref[4:5, :]
    gt = mod_ref[5:6, :]
    h_scr[...] = (_rms(x, npre_ref[...]) * (1.0 + sc) + sh).astype(BF16)

    y = yf_ref[...] + yb_ref[...] + s5d_ref[...] * u_ref[...].astype(F32)
    z5 = _gelu_tanh(y)
    b_s5 = z5 * _sigmoid(_dot(z5, wglu_ref[...]))

    z = cc_ref[...].astype(F32) * cx_ref[...].astype(F32)
    zprev = ((ccp_ref[...].astype(F32) * cxp_ref[...].astype(F32))[HALO - 1:HALO, :]
             * jnp.where(i > 0, 1.0, 0.0))
    znext = ((ccn_ref[...].astype(F32) * cxn_ref[...].astype(F32))[0:1, :]
             * jnp.where(i < nt - 1, 1.0, 0.0))
    row = lax.broadcasted_iota(jnp.int32, (tm, SEG), 0)
    z_dn = jnp.where(row == 0, zprev, pltpu.roll(z, 1, axis=0))
    z_up = jnp.where(row == tm - 1, znext, pltpu.roll(z, tm - 1, axis=0))
    b_conv = cb_ref[...].astype(F32) * (cbias_ref[...] + z_dn * cw_ref[0:1, :] + z * cw_ref[1:2, :]
                                        + z_up * cw_ref[2:3, :])

    branches = (b_s5, ret_ref[...], b_conv, mla_ref[...])
    merged = jnp.zeros((tm, D_MODEL), F32)
    for n in range(N_BRANCH):
        gate = _sigmoid(jnp.dot(h_scr[...], wg_ref[:, n * D_MODEL:(n + 1) * D_MODEL],
                                preferred_element_type=F32) + bg_ref[:, n * D_MODEL:(n + 1) * D_MODEL])
        merged = merged + gate * _dot(branches[n], wbr_ref[n])
    m = _dot(merged, wo_ref[...])
    o_ref[...] = x + gt * _rms(m, npost_ref[...])


def _merge_call(x, w, y_s5, ret_o, p3, mla_o, *, layer, nb, t, tm, row_of_batch):
    nt = t // tm
    hb = tm // HALO
    n_halo = t // HALO

    def dir_spec(d):
        return pl.BlockSpec((None, None, tm, SEG), lambda b, i: (d, b, i, 0))

    def seg_spec(seg):
        return pl.BlockSpec((None, tm, SEG), lambda b, i: (b, i, seg))

    def prev_spec(seg):
        return pl.BlockSpec((None, HALO, SEG), lambda b, i: (b, jnp.maximum(i * hb - 1, 0), seg))

    def next_spec(seg):
        return pl.BlockSpec((None, HALO, SEG), lambda b, i: (b, jnp.minimum((i + 1) * hb, n_halo - 1), seg))

    return pl.pallas_call(
        functools.partial(_merge_kernel, tm=tm, nt=nt),
        grid=(nb, nt),
        in_specs=[
            pl.BlockSpec((tm, D_MODEL), lambda b, i: (b * nt + i, 0)),
            pl.BlockSpec((None, None, N_MOD, D_MODEL), lambda b, i: (layer, row_of_batch(b), 0, 0)),
            _pick_spec(w['npre'], layer * N_NORM + 1),
            _pick_spec(w['npost'], layer * N_NORM + 1),
            dir_spec(0), dir_spec(1), seg_spec(S_U),
            pl.BlockSpec((None, tm, SEG), lambda b, i: (b, i, 0)),
            seg_spec(S_CX), seg_spec(S_CB), seg_spec(S_CC),
            prev_spec(S_CX), prev_spec(S_CC), next_spec(S_CX), next_spec(S_CC),
            pl.BlockSpec((None, tm, SEG), lambda b, i: (b, i, 0)),
            _pick_spec(w['s5_d'], layer), _pick_spec(w['s5_wglu'], layer), _pick_spec(w['conv_w'], layer),
            _pick_spec(w['conv_b'], layer), _resident_spec(w['w_branch'].shape[1:], (layer,)),
            _resident_spec(w['w_gate'].shape[1:], (layer,)), _pick_spec(w['b_gate'], layer),
            _resident_spec(w['w_o'].shape[1:], (layer,)),
        ],
        out_specs=pl.BlockSpec((tm, D_MODEL), lambda b, i: (b * nt + i, 0)),
        out_shape=jax.ShapeDtypeStruct((nb * t, D_MODEL), F32),
        scratch_shapes=[pltpu.VMEM((tm, D_MODEL), BF16)],
        compiler_params=_cparams(("arbitrary", "arbitrary")),
        name="merge",
    )(x, w['mod'], w['npre'], w['npost'], y_s5, y_s5, p3, ret_o, p3, p3, p3, p3, p3, p3, p3, mla_o,
      w['s5_d'], w['s5_wglu'], w['conv_w'], w['conv_b'], w['w_branch'], w['w_gate'], w['b_gate'], w['w_o'])


def _zeros_like_cols(w, n):
    return jnp.zeros(w.shape[:-1] + (n,), w.dtype)


def _axial_angles(t, dim):
    rows = t // GRID_W
    row = np.repeat(np.arange(rows, dtype=np.float64), GRID_W)
    col = np.tile(np.arange(GRID_W, dtype=np.float64), rows)
    quarter = dim // 4
    inv = ROPE_BASE ** (-np.arange(quarter, dtype=np.float64) / quarter)
    ang = np.concatenate([row[:, None] * inv, col[:, None] * inv], axis=-1)
    return np.cos(ang).astype(np.float32), np.sin(ang).astype(np.float32)


def _rope_tables(t):
    cos, sin = _axial_angles(t, RET_DIM)
    ret_cos = np.tile(np.concatenate([cos, cos], -1), (1, 2))
    ret_sin = np.tile(np.concatenate([sin, sin], -1), (1, 2))
    cos, sin = _axial_angles(t, MLA_ROPE)
    pad = MLA_SLAB - MLA_NOPE - MLA_ROPE
    mla_cos = np.concatenate([np.ones((t, MLA_NOPE), np.float32), cos, cos, np.zeros((t, pad), np.float32)], -1)
    mla_sin = np.concatenate([np.zeros((t, MLA_NOPE), np.float32), sin, sin, np.zeros((t, pad), np.float32)], -1)
    return ((jnp.asarray(ret_cos), jnp.asarray(ret_sin)), (jnp.asarray(mla_cos), jnp.asarray(mla_sin)))


def _place_rope_key(w):
    return jnp.concatenate([_zeros_like_cols(w, MLA_NOPE), w,
                            _zeros_like_cols(w, MLA_SLAB - MLA_NOPE - MLA_ROPE)], -1)


def _in_weights(w_in):
    sizes = (BRANCH_WIDTH,) * 8 + (MLA_Q_LORA, MLA_KV_LORA, MLA_ROPE)
    offs = np.cumsum((0,) + sizes)
    u, rq, rk, rv, rg, cx, cb, cc, cq, ckv, kr = [w_in[..., offs[i]:offs[i + 1]] for i in range(11)]
    rk = rk * (RET_DIM ** -0.5)
    cols = [u, rq, rk, rv, rg, cx, cb, cc,
            jnp.concatenate([cq, _zeros_like_cols(cq, SEG - MLA_Q_LORA)], -1),
            jnp.concatenate([ckv, _place_rope_key(kr)], -1)]
    return jnp.concatenate([c.astype(BF16) for c in cols], axis=-1)


def _mla_weights(w_uq, w_ukv):
    nl = w_uq.shape[0]
    dq = MLA_NOPE + MLA_ROPE
    w4 = w_uq.reshape(nl, MLA_Q_LORA, MLA_HEADS, dq)
    nope, ropew = w4[..., :MLA_NOPE], w4[..., MLA_NOPE:]
    half = MLA_ROPE // 2
    rot = jnp.concatenate([-ropew[..., half:], ropew[..., :half]], -1)
    zpad = jnp.zeros((nl, MLA_Q_LORA, MLA_HEADS, MLA_SLAB - dq), F32)
    wq = jnp.concatenate([nope, ropew, zpad], -1).reshape(nl, MLA_Q_LORA, MLA_HEADS * MLA_SLAB)
    wqr = jnp.concatenate([jnp.zeros_like(nope), rot, zpad], -1).reshape(nl, MLA_Q_LORA, MLA_HEADS * MLA_SLAB)
    rpad = jnp.zeros((nl, SEG - MLA_Q_LORA, MLA_HEADS * MLA_SLAB), F32)
    wq = jnp.concatenate([wq, rpad], 1).astype(BF16)
    wqr = jnp.concatenate([wqr, rpad], 1).astype(BF16)
    kv4 = w_ukv.reshape(nl, MLA_KV_LORA, MLA_HEADS, MLA_NOPE + MLA_V)
    wk = jnp.concatenate([kv4[..., :MLA_NOPE], jnp.zeros((nl, MLA_KV_LORA, MLA_HEADS, MLA_SLAB - MLA_NOPE), F32)],
                         -1).reshape(nl, MLA_KV_LORA, MLA_HEADS * MLA_SLAB)
    place = np.zeros((MLA_ROPE, MLA_HEADS, MLA_SLAB), np.float32)
    place[np.arange(MLA_ROPE), :, MLA_NOPE + np.arange(MLA_ROPE)] = 1.0
    place = jnp.broadcast_to(jnp.asarray(place.reshape(MLA_ROPE, MLA_HEADS * MLA_SLAB)),
                             (nl, MLA_ROPE, MLA_HEADS * MLA_SLAB))
    wkc = jnp.concatenate([wk, place], 1).astype(BF16)
    wv = kv4[..., MLA_NOPE:].reshape(nl, MLA_KV_LORA, MLA_HEADS * MLA_V).astype(BF16)
    return wq, wqr, wk.astype(BF16), wkc, wv


def _s5_tables(lam_re, lam_im, log_dt, b_re, b_im, c_re, c_im):
    nl = lam_re.shape[0]
    dt = jnp.exp(log_dt)[..., None]
    mag = jnp.exp(lam_re * dt)
    ab_re = mag * jnp.cos(lam_im * dt)
    ab_im = mag * jnp.sin(lam_im * dt)
    den = lam_re * lam_re + lam_im * lam_im
    f_re = ((ab_re - 1.0) * lam_re + ab_im * lam_im) / den
    f_im = (ab_im * lam_re - (ab_re - 1.0) * lam_im) / den
    bb_re = f_re[..., None] * b_re - f_im[..., None] * b_im
    bb_im = f_re[..., None] * b_im + f_im[..., None] * b_re
    eye = jnp.eye(S5_GROUPS, dtype=F32)
    blk_in = lambda m: jnp.einsum('ldgph,gk->ldghkp', m, eye).reshape(nl, 2, BRANCH_WIDTH, S5_LANES)
    bblk = jnp.concatenate([blk_in(bb_re), blk_in(bb_im)], -1).astype(BF16)
    blk_out = lambda m: jnp.einsum('lghp,gk->lgpkh', m, eye).reshape(nl, S5_LANES, BRANCH_WIDTH)
    cblk = jnp.concatenate([blk_out(c_re), -blk_out(c_im)], 1).astype(BF16)
    a_tab = jnp.stack([ab_re.reshape(nl, 2, S5_LANES), ab_im.reshape(nl, 2, S5_LANES)], axis=2)
    return bblk, a_tab, cblk


def _ret_tables(decay_logit, c):
    lg = jax.nn.log_sigmoid(decay_logit.astype(F32))
    lf = lg[:, 0][:, :, None, None]
    lb = lg[:, 1][:, :, None, None]
    idx = jnp.arange(c, dtype=F32)
    diff = idx[:, None] - idx[None, :]
    dm = jnp.where(diff >= 0, jnp.exp(lf * jnp.maximum(diff, 0.0)), jnp.exp(lb * jnp.maximum(-diff, 0.0)))
    lfl = jnp.repeat(lg[:, 0], RET_DIM, axis=-1)
    lbl = jnp.repeat(lg[:, 1], RET_DIM, axis=-1)
    qdf = jnp.exp(lfl[:, None, :] * (idx[:, None] + 1.0))
    qdb = jnp.exp(lbl[:, None, :] * (c - idx[:, None]))
    kdft = jnp.exp(lfl[:, :, None] * (c - 1.0 - idx[None, :]))
    kdbt = jnp.exp(lbl[:, :, None] * idx[None, :])
    nl = lg.shape[0]
    cdf = jnp.broadcast_to(jnp.exp(lfl * c)[:, :, None], (nl, SEG, SEG))
    cdb = jnp.broadcast_to(jnp.exp(lbl * c)[:, :, None], (nl, SEG, SEG))
    head = np.arange(SEG) // RET_DIM
    hm = jnp.asarray((head[None, :] == np.arange(RET_HEADS)[:, None]).astype(np.float32))
    bd_np = (head[:, None] == head[None, :]).astype(np.float32)
    bd = jnp.asarray(bd_np)
    gm = jnp.asarray(bd_np / RET_DIM, BF16)
    return (dm, qdf, qdb, kdft, kdbt, cdf, cdb), (hm, bd, gm)


def _layer(x, w, *, layer, nb, t, tm, tm_ffn, row_of_batch, ctx, rope_tabs):
    is_ctx_pass = ctx is None
    row_of_tile = lambda i: row_of_batch((i * tm_ffn) // t)
    x = _ffn_call(x, w, layer=layer, k=0, tm=tm_ffn, row_of_tile=row_of_tile)

    p = _in_call(x, w, layer=layer, nb=nb, t=t, tm=tm, row_of_batch=row_of_batch)
    p3 = p.reshape(nb, t, NSEG * SEG)

    if is_ctx_pass:
        s5_0, s0_layer = jnp.zeros((1, 2, 2, nb, S5_LANES), F32), 0
        ret_0, mla_cache, ret_rope, mla_rope = None, None, None, None
    else:
        s5_0, ret_0, mla_cache = ctx
        s0_layer = layer
        ret_rope, mla_rope = rope_tabs
    s5_out = _s5_call(p3, w, s5_0, layer=layer, s0_layer=s0_layer, nb=nb, t=t, want_fin=is_ctx_pass)
    ret_out = _ret_call(p3, w, ret_0, ret_rope, layer=layer, nb=nb, t=t, want_fin=is_ctx_pass)
    mla_out = _mla_call(p3, w, mla_rope, mla_cache, layer=layer, nb=nb, t=t, tq=min(t, MLA_TQ),
                        want_cache=is_ctx_pass)

    x = _merge_call(x, w, s5_out[0], ret_out[0], p3, mla_out[0], layer=layer, nb=nb, t=t, tm=tm,
                    row_of_batch=row_of_batch)
    x = _ffn_call(x, w, layer=layer, k=2, tm=tm_ffn, row_of_tile=row_of_tile)

    state = None
    if is_ctx_pass:
        kr = p3[:, :, S_KV * SEG + MLA_KV_LORA + MLA_NOPE:S_KV * SEG + MLA_KV_LORA + MLA_NOPE + MLA_ROPE]
        cache = jnp.concatenate([mla_out[1], kr.astype(F32)], axis=-1)
        state = (s5_out[1], ret_out[1], cache)
    return x, state


def kernel(x_prompt, x_sample, state_s5, state_ret, cache_mla, c, c_ctx, w_mod, b_mod, norm_pre, norm_post, ffn_w1, ffn_w3, ffn_w2, w_in, s5_lam_re, s5_lam_im, s5_log_dt, s5_b_re, s5_b_im, s5_c_re, s5_c_im, s5_d, s5_w_glu, ret_decay, ret_gn, conv_w, conv_b, mla_q_norm, mla_w_uq, mla_kv_norm, mla_w_ukv, w_branch, w_gate, b_gate, w_o):
    bp, tp, _ = x_prompt.shape
    bs, ts, _ = x_sample.shape
    assert 1 + bs <= MOD_ROWS

    cond = jnp.concatenate([c_ctx[None, :], c, jnp.zeros((MOD_ROWS - 1 - bs, D_MODEL), F32)], axis=0)
    mod = _mod_call(cond, w_mod, b_mod).reshape(DEPTH, MOD_ROWS, N_MOD, D_MODEL)

    wq, wqr, wk, wkc, wv = _mla_weights(mla_w_uq, mla_w_ukv)
    bblk, a_tab, cblk = _s5_tables(s5_lam_re, s5_lam_im, s5_log_dt, s5_b_re, s5_b_im, s5_c_re, s5_c_im)
    ret_layer_tabs, ret_const_tabs = _ret_tables(ret_decay, RET_CHUNK)
    w = dict(
        mod=mod,
        npre=norm_pre.reshape(DEPTH * N_NORM, 1, D_MODEL),
        npost=norm_post.reshape(DEPTH * N_NORM, 1, D_MODEL),
        ffn=(ffn_w1.astype(BF16), ffn_w3.astype(BF16), ffn_w2.astype(BF16)),
        w_in=_in_weights(w_in),
        s5_bblk=bblk, s5_a=a_tab, s5_cblk=cblk,
        s5_d=s5_d[:, None, :], s5_wglu=s5_w_glu.astype(BF16),
        ret_layer_tabs=ret_layer_tabs, ret_const_tabs=ret_const_tabs, ret_gn=ret_gn[:, None, :],
        conv_w=conv_w, conv_b=conv_b[:, None, :],
        mla_qn=jnp.concatenate([mla_q_norm, jnp.zeros((DEPTH, SEG - MLA_Q_LORA), F32)], -1)[:, None, :],
        mla_kvn=mla_kv_norm[:, None, :],
        mla_wq=wq, mla_wqr=wqr, mla_wk=wk, mla_wkc=wkc, mla_wv=wv,
        w_branch=w_branch.astype(BF16), w_gate=w_gate.astype(BF16), b_gate=b_gate[:, None, :],
        w_o=w_o.astype(BF16),
    )
    rope_tabs = _rope_tables(ts)
    s5_0 = state_s5.transpose(1, 2, 5, 0, 3, 4).reshape(DEPTH, 2, 2, bs, S5_LANES)

    xp = x_prompt.reshape(bp * tp, D_MODEL)
    xs = x_sample.reshape(bs * ts, D_MODEL)
    s5_list, ret_list, mla_list = [], [], []
    for l in range(DEPTH):
        xp, (s5_s, ret_s, mla_c) = _layer(xp, w, layer=l, nb=bp, t=tp, tm=TM_CTX, tm_ffn=TM_CTX_FFN,
                                          row_of_batch=lambda b: 0, ctx=None, rope_tabs=None)
        s5_list.append(s5_s)
        ret_list.append(ret_s)
        mla_list.append(mla_c)
        xs, _ = _layer(xs, w, layer=l, nb=bs, t=ts, tm=TM_LAT, tm_ffn=TM_LAT_FFN, row_of_batch=lambda b: 1 + b,
                       ctx=(s5_0, state_ret, cache_mla), rope_tabs=rope_tabs)
    new_s5 = jnp.stack(s5_list, 0).reshape(DEPTH, 2, 2, bp, S5_GROUPS, S5_STATE).transpose(3, 0, 1, 4, 5, 2)
    return (xp.reshape(bp, tp, D_MODEL), xs.reshape(bs, ts, D_MODEL),
            new_s5, jnp.stack(ret_list, axis=1), jnp.stack(mla_list, axis=1))
```

```python
import functools

import numpy as np
import jax
import jax.numpy as jnp
from jax import lax
from jax.experimental import pallas as pl
from jax.experimental.pallas import tpu as pltpu

F32 = jnp.float32
BF16 = jnp.bfloat16

D_MODEL = 1024
DEPTH = 2
GRID_W = 64
EPS = 1e-6
LOG2_E = float(np.log2(np.e))
ROPE_BASE = 10000.0
N_BRANCH = 4
BRANCH_WIDTH = D_MODEL // 4
N_MOD = 9
N_NORM = 3
D_FF = 2816
S5_GROUP = 16
S5_GROUPS = BRANCH_WIDTH // S5_GROUP
S5_STATE = 64
S5_LANES = S5_GROUPS * S5_STATE
RET_HEADS = 4
RET_DIM = BRANCH_WIDTH // RET_HEADS
CONV_K = 3
MLA_HEADS = 4
MLA_Q_LORA = 192
MLA_KV_LORA = 128
MLA_NOPE = 64
MLA_ROPE = 32
MLA_V = BRANCH_WIDTH // MLA_HEADS
MLA_SLAB = 128
MLA_VT_ROWS = MLA_V + 16

SUBLANES = 8
S5_SUBCHUNKS = 8
S5_ROWS = 256
SEG = 256
FF_CHUNK = 256
N_FF_CHUNKS = D_FF // FF_CHUNK
MOD_ROWS = 16
HALO = 16
RET_CHUNK = 256
MLA_TQ = 1024
TM_CTX, TM_CTX_FFN = 256, 512
TM_LAT, TM_LAT_FFN = 1024, 1024
TM_FFN_IN = 512
V7X_VMEM_LIMIT = 56 * 1024 * 1024

S_U, S_RQ, S_RK, S_RV, S_RG, S_CX, S_CB, S_CC, S_CQ, S_KV = range(10)
NSEG = 10
LANES = 128


def _cparams(sem):
    return pltpu.CompilerParams(dimension_semantics=sem, vmem_limit_bytes=V7X_VMEM_LIMIT)


def _dot(a, b):
    return jnp.dot(a.astype(BF16), b.astype(BF16), preferred_element_type=F32)


def _sigmoid(x):
    return 1.0 / (1.0 + jnp.exp(-x))


def _rms(x, g, n=None):
    n = x.shape[-1] if n is None else n
    ms = jnp.sum(x * x, axis=-1, keepdims=True) * (1.0 / n)
    return x * lax.rsqrt(ms + EPS) * g


def _rot_half_lanes(x, width):
    half = width // 2
    lane = lax.broadcasted_iota(jnp.int32, (x.shape[0], LANES), 1)
    first = (lane & (width - 1)) < half
    outs = []
    for j in range(x.shape[1] // LANES):
        v = x[:, j * LANES:(j + 1) * LANES]
        outs.append(jnp.where(first, -pltpu.roll(v, LANES - half, axis=1), pltpu.roll(v, half, axis=1)))
    return outs[0] if len(outs) == 1 else jnp.concatenate(outs, axis=-1)


def _gelu_tanh(x):
    return 0.5 * x * (1.0 + jnp.tanh(np.sqrt(2.0 / np.pi) * (x + 0.044715 * (x * x * x))))


def _const_spec(shape):
    nd = len(shape)
    return pl.BlockSpec(shape, lambda *_: (0,) * nd)


def _pick_spec(arr, *lead):
    shape = arr.shape[len(lead):]
    return pl.BlockSpec((None,) * len(lead) + tuple(shape), lambda *_: tuple(lead) + (0,) * len(shape))


def _resident_spec(shape, lead=()):
    nd = len(shape)
    return pl.BlockSpec((None,) * len(lead) + tuple(shape), lambda *_: tuple(lead) + (0,) * nd,
                        pipeline_mode=pl.Buffered(1))


def _mod_kernel(c_ref, w_ref, b_ref, o_ref):
    c = c_ref[...]
    o_ref[...] = _dot(c * _sigmoid(c), w_ref[...]) + b_ref[...]


def _mod_call(cond, w_mod, b_mod):
    ncol = N_MOD * D_MODEL
    tn = D_MODEL
    return pl.pallas_call(
        _mod_kernel,
        grid=(DEPTH, ncol // tn),
        in_specs=[
            pl.BlockSpec((MOD_ROWS, D_MODEL), lambda l, j: (0, 0)),
            pl.BlockSpec((None, D_MODEL, tn), lambda l, j: (l, 0, j)),
            pl.BlockSpec((None, 1, tn), lambda l, j: (l, 0, j)),
        ],
        out_specs=pl.BlockSpec((None, MOD_ROWS, tn), lambda l, j: (l, 0, j)),
        out_shape=jax.ShapeDtypeStruct((DEPTH, MOD_ROWS, ncol), F32),
        compiler_params=_cparams(("arbitrary", "arbitrary")),
        name="mod",
    )(cond, w_mod, b_mod.reshape(DEPTH, 1, ncol))


def _ffn_kernel(x_ref, mod_ref, npre_ref, npost_ref, w1_ref, w3_ref, w2_ref, o_ref,
                h_scr, g_scr, *, k):
    x = x_ref[...]
    sh = mod_ref[3 * k:3 * k + 1, :]
    sc = mod_ref[3 * k + 1:3 * k + 2, :]
    gt = mod_ref[3 * k + 2:3 * k + 3, :]
    h_scr[...] = (_rms(x, npre_ref[...]) * (1.0 + sc) + sh).astype(BF16)
    for j in range(N_FF_CHUNKS):
        cols = slice(j * FF_CHUNK, (j + 1) * FF_CHUNK)
        a = jnp.dot(h_scr[...], w1_ref[:, cols], preferred_element_type=F32)
        b = jnp.dot(h_scr[...], w3_ref[:, cols], preferred_element_type=F32)
        g_scr[:, cols] = (a * _sigmoid(a) * b).astype(BF16)
    f = jnp.dot(g_scr[...], w2_ref[...], preferred_element_type=F32)
    o_ref[...] = x + (0.5 * gt) * _rms(f, npost_ref[...])


def _ffn_call(x, w, *, layer, k, tm, row_of_tile):
    n = x.shape[0]
    lj = (layer, k // 2)
    w1, w3, w2 = w['ffn']
    return pl.pallas_call(
        functools.partial(_ffn_kernel, k=k),
        grid=(n // tm,),
        in_specs=[
            pl.BlockSpec((tm, D_MODEL), lambda i: (i, 0)),
            pl.BlockSpec((None, None, N_MOD, D_MODEL), lambda i: (layer, row_of_tile(i), 0, 0)),
            _pick_spec(w['npre'], layer * N_NORM + k),
            _pick_spec(w['npost'], layer * N_NORM + k),
            _resident_spec((D_MODEL, D_FF), lj),
            _resident_spec((D_MODEL, D_FF), lj),
            _resident_spec((D_FF, D_MODEL), lj),
        ],
        out_specs=pl.BlockSpec((tm, D_MODEL), lambda i: (i, 0)),
        out_shape=jax.ShapeDtypeStruct((n, D_MODEL), F32),
        scratch_shapes=[pltpu.VMEM((tm, D_MODEL), BF16), pltpu.VMEM((tm, D_FF), BF16)],
        compiler_params=_cparams(("arbitrary",)),
        name="ffn",
    )(x, w['mod'], w['npre'], w['npost'], w1, w3, w2)


def _ffn_in_kernel(x_ref, mod_ref, npre_ref, npost_ref, w1_ref, w3_ref, w2_ref, npre2_ref, win_ref,
                   o_ref, p_ref, h_scr, g_scr):
    _ffn_kernel(x_ref, mod_ref, npre_ref, npost_ref, w1_ref, w3_ref, w2_ref, o_ref, h_scr, g_scr, k=0)
    _in_kernel(o_ref, mod_ref, npre2_ref, win_ref, p_ref, h_scr)


def _ffn_in_call(x, w, *, layer, tm, row_of_tile):
    n = x.shape[0]
    lj = (layer, 0)
    w1, w3, w2 = w['ffn']
    return pl.pallas_call(
        _ffn_in_kernel,
        grid=(n // tm,),
        in_specs=[
            pl.BlockSpec((tm, D_MODEL), lambda i: (i, 0)),
            pl.BlockSpec((None, None, N_MOD, D_MODEL), lambda i: (layer, row_of_tile(i), 0, 0)),
            _pick_spec(w['npre'], layer * N_NORM),
            _pick_spec(w['npost'], layer * N_NORM),
            _resident_spec((D_MODEL, D_FF), lj),
            _resident_spec((D_MODEL, D_FF), lj),
            _resident_spec((D_FF, D_MODEL), lj),
            _pick_spec(w['npre'], layer * N_NORM + 1),
            _resident_spec((D_MODEL, NSEG * SEG), (layer,)),
        ],
        out_specs=[pl.BlockSpec((tm, D_MODEL), lambda i: (i, 0)),
                   pl.BlockSpec((tm, NSEG * SEG), lambda i: (i, 0))],
        out_shape=[jax.ShapeDtypeStruct((n, D_MODEL), F32),
                   jax.ShapeDtypeStruct((n, NSEG * SEG), BF16)],
        scratch_shapes=[pltpu.VMEM((tm, D_MODEL), BF16), pltpu.VMEM((tm, D_FF), BF16)],
        compiler_params=_cparams(("arbitrary",)),
        name="ffn_in",
    )(x, w['mod'], w['npre'], w['npost'], w1, w3, w2, w['npre'], w['w_in'])


def _in_kernel(x_ref, mod_ref, npre_ref, w_ref, p_ref, h_scr):
    sh = mod_ref[3:4, :]
    sc = mod_ref[4:5, :]
    h_scr[...] = (_rms(x_ref[...], npre_ref[...]) * (1.0 + sc) + sh).astype(BF16)
    for j in range(NSEG):
        cols = slice(j * SEG, (j + 1) * SEG)
        p_ref[:, cols] = jnp.dot(h_scr[...], w_ref[:, cols], preferred_element_type=F32).astype(BF16)


def _in_call(x, w, *, layer, nb, t, tm, row_of_batch):
    tt = t // tm
    nseg = NSEG
    return pl.pallas_call(
        _in_kernel,
        grid=(nb, tt),
        in_specs=[
            pl.BlockSpec((tm, D_MODEL), lambda b, i: (b * tt + i, 0)),
            pl.BlockSpec((None, None, N_MOD, D_MODEL), lambda b, i: (layer, row_of_batch(b), 0, 0)),
            _pick_spec(w['npre'], layer * N_NORM + 1),
            _resident_spec((D_MODEL, nseg * SEG), (layer,)),
        ],
        out_specs=pl.BlockSpec((tm, nseg * SEG), lambda b, i: (b * tt + i, 0)),
        out_shape=jax.ShapeDtypeStruct((nb * t, nseg * SEG), BF16),
        scratch_shapes=[pltpu.VMEM((tm, D_MODEL), BF16)],
        compiler_params=_cparams(("arbitrary", "arbitrary")),
        name="in_proj",
    )(x, w['mod'], w['npre'], w['w_in'])


def _s5_kernel(*refs, nb, steps, want_fin):
    if want_fin:
        u_ref, bb_ref, a_ref, cc_ref, s0_ref, y_ref, fin_ref, s_scr, bu_scr, u2_scr, y2_scr = refs
    else:
        u_ref, bb_ref, a_ref, cc_ref, s0_ref, y_ref, s_scr, bu_scr, u2_scr, y2_scr = refs
        fin_ref = None
    d = pl.program_id(0)
    c = pl.program_id(1)
    span = steps * S5_SUBCHUNKS
    ngrp = nb // SUBLANES

    @pl.when(c == 0)
    def _():
        s_scr[...] = s0_ref[...]

    u_all = u_ref[...].astype(F32).reshape(nb * span, SEG)
    for j in range(SEG // LANES):
        u2_scr[j] = u_all[:, j * LANES:(j + 1) * LANES]

    a_re = jnp.broadcast_to(a_ref[0:1, :], (SUBLANES, S5_LANES))
    a_im = jnp.broadcast_to(a_ref[1:2, :], (SUBLANES, S5_LANES))
    s_re = [s_scr[0, g * SUBLANES:(g + 1) * SUBLANES, :] for g in range(ngrp)]
    s_im = [s_scr[1, g * SUBLANES:(g + 1) * SUBLANES, :] for g in range(ngrp)]
    re_cols = slice(0, S5_LANES)
    im_cols = slice(S5_LANES, 2 * S5_LANES)

    def time_of(k, i):
        j = k * steps + i
        return jnp.where(d == 0, j, span - 1 - j)

    def project_in(k):
        u_tb = jnp.concatenate(
            [jnp.concatenate([u2_scr[j, pl.ds(time_of(k, i), nb, stride=span), :]
                              for j in range(SEG // LANES)], axis=-1) for i in range(steps)], axis=0)
        bu_scr[k] = _dot(u_tb, bb_ref[...])

    project_in(0)
    for k in range(S5_SUBCHUNKS):
        if k + 1 < S5_SUBCHUNKS:
            project_in(k + 1)
        for i in range(steps):
            for g in range(ngrp):
                rr = slice(i * nb + g * SUBLANES, i * nb + (g + 1) * SUBLANES)
                n_re = a_re * s_re[g] - a_im * s_im[g] + bu_scr[k, rr, re_cols]
                n_im = a_re * s_im[g] + a_im * s_re[g] + bu_scr[k, rr, im_cols]
                bu_scr[k, rr, re_cols] = n_re
                bu_scr[k, rr, im_cols] = n_im
                s_re[g], s_im[g] = n_re, n_im
        y_tb = _dot(bu_scr[k], cc_ref[...])
        for i in range(steps):
            for j in range(SEG // LANES):
                y2_scr[j, pl.ds(time_of(k, i), nb, stride=span), :] = (
                    y_tb[i * nb:(i + 1) * nb, j * LANES:(j + 1) * LANES])
    y_ref[...] = jnp.concatenate([y2_scr[j] for j in range(SEG // LANES)], axis=-1).reshape(nb, span, SEG)
    for g in range(ngrp):
        s_scr[0, g * SUBLANES:(g + 1) * SUBLANES, :] = s_re[g]
        s_scr[1, g * SUBLANES:(g + 1) * SUBLANES, :] = s_im[g]
    if want_fin:
        fin_ref[...] = s_scr[...]


def _s5_call(p3, w, s0, *, layer, s0_layer, nb, t, want_fin):
    steps = S5_ROWS // nb
    span = steps * S5_SUBCHUNKS
    nch = t // span
    rows = S5_ROWS

    def chunk(d, c):
        return jnp.where(d == 0, c, nch - 1 - c)

    out_specs = [pl.BlockSpec((None, nb, span, SEG), lambda d, c: (d, 0, chunk(d, c), 0))]
    out_shape = [jax.ShapeDtypeStruct((2, nb, t, SEG), F32)]
    if want_fin:
        out_specs.append(pl.BlockSpec((None, 2, nb, S5_LANES), lambda d, c: (d, 0, 0, 0)))
        out_shape.append(jax.ShapeDtypeStruct((2, 2, nb, S5_LANES), F32))
    return pl.pallas_call(
        functools.partial(_s5_kernel, nb=nb, steps=steps, want_fin=want_fin),
        grid=(2, nch),
        in_specs=[
            pl.BlockSpec((nb, span, SEG), lambda d, c: (0, chunk(d, c), S_U)),
            pl.BlockSpec((None, None, SEG, 2 * S5_LANES), lambda d, c: (layer, d, 0, 0)),
            pl.BlockSpec((None, None, 2, S5_LANES), lambda d, c: (layer, d, 0, 0)),
            _pick_spec(w['s5_cblk'], layer),
            pl.BlockSpec((None, None, 2, nb, S5_LANES), lambda d, c: (s0_layer, d, 0, 0, 0)),
        ],
        out_specs=out_specs,
        out_shape=out_shape,
        scratch_shapes=[pltpu.VMEM((2, nb, S5_LANES), F32),
                        pltpu.VMEM((S5_SUBCHUNKS, rows, 2 * S5_LANES), F32),
                        pltpu.VMEM((SEG // LANES, nb * span, LANES), F32),
                        pltpu.VMEM((SEG // LANES, nb * span, LANES), F32)],
        compiler_params=_cparams(("arbitrary", "arbitrary")),
        name="s5_scan",
    )(p3, w['s5_bblk'], w['s5_a'], w['s5_cblk'], s0)


def _ret_kernel(*refs, rope, has_s0, want_fin, nch):
    refs = list(refs)
    q_ref, k_ref, v_ref, g_ref = refs[:4]
    del refs[:4]
    if rope:
        cos_ref, sin_ref = refs[:2]
        del refs[:2]
    (dm_ref, qdf_ref, qdb_ref, kdft_ref, kdbt_ref, cdf_ref, cdb_ref, hm_ref, bd_ref, gm_ref,
     gn_ref) = refs[:11]
    del refs[:11]
    if has_s0:
        s0_ref = refs.pop(0)
    o_ref = refs.pop(0)
    fin_ref = refs.pop(0) if want_fin else None
    sb_all, sf_scr, sb_scr = refs

    p = pl.program_id(1)
    c = pl.program_id(2)
    hd = RET_DIM

    def init_state(dst, d):
        dst[...] = jnp.zeros_like(dst)
        if has_s0:
            for h in range(RET_HEADS):
                dst[h * hd:(h + 1) * hd, h * hd:(h + 1) * hd] = s0_ref[d, h]

    def write_final(src, d):
        for h in range(RET_HEADS):
            fin_ref[d, h] = src[h * hd:(h + 1) * hd, h * hd:(h + 1) * hd]

    def rope_tabs():
        cos = cos_ref[...]
        sin = sin_ref[...]
        return jnp.concatenate([cos, cos], axis=-1), jnp.concatenate([sin, sin], axis=-1)

    def roped_kt():
        k = k_ref[...].astype(F32)
        if rope:
            cos, sin = rope_tabs()
            k = k * cos + _rot_half_lanes(k, RET_DIM) * sin
        return k.T

    @pl.when(p == 0)
    def _():
        ck = nch - 1 - c

        @pl.when(c == 0)
        def _():
            init_state(sb_scr, 1)

        kt = roped_kt()
        s_old = sb_scr[...]
        sb_all[ck] = s_old
        sb_scr[...] = s_old * cdb_ref[...] + _dot(kt * kdbt_ref[...], v_ref[...]) * bd_ref[...]
        if want_fin:
            @pl.when(c == nch - 1)
            def _():
                write_final(sb_scr, 1)

    @pl.when(p == 1)
    def _():
        @pl.when(c == 0)
        def _():
            init_state(sf_scr, 0)

        q = q_ref[...].astype(F32)
        if rope:
            cos, sin = rope_tabs()
            q = q * cos + _rot_half_lanes(q, RET_DIM) * sin
        kt = roped_kt()
        ktb = kt.astype(BF16)
        vb = v_ref[...]
        s_f = sf_scr[...]
        o = _dot(q, s_f) * qdf_ref[...] + _dot(q, sb_all[c]) * qdb_ref[...]
        for h in range(RET_HEADS):
            hm = hm_ref[h:h + 1, :]
            att = _dot(q * hm, ktb) * dm_ref[h]
            o = o + _dot(att, vb) * hm
        sf_scr[...] = s_f * cdf_ref[...] + _dot(kt * kdft_ref[...], vb) * bd_ref[...]
        hi = o.astype(BF16)
        lo = (o - hi.astype(F32)).astype(BF16)
        mu = (jnp.dot(hi, gm_ref[...], preferred_element_type=F32)
              + jnp.dot(lo, gm_ref[...], preferred_element_type=F32))
        oc = o - mu
        var = _dot(oc * oc, gm_ref[...])
        on = oc * lax.rsqrt(var + EPS) * gn_ref[...]
        g = g_ref[...].astype(F32)
        o_ref[...] = (g * _sigmoid(g)) * on
        if want_fin:
            @pl.when(c == nch - 1)
            def _():
                write_final(sf_scr, 0)


def _ret_call(p3, w, s0, rope_tabs, *, layer, nb, t, want_fin):
    tc = RET_CHUNK
    nch = t // tc
    rope = rope_tabs is not None
    has_s0 = s0 is not None

    def kv_chunk(p, c):
        return jnp.where(p == 0, nch - 1 - c, c)

    def q_chunk(p, c):
        return jnp.where(p == 0, 0, c)

    def seg_spec(seg, cmap):
        return pl.BlockSpec((None, tc, SEG), lambda b, p, c: (b, cmap(p, c), seg))

    args = [p3, p3, p3, p3]
    in_specs = [seg_spec(S_RQ, q_chunk), seg_spec(S_RK, kv_chunk), seg_spec(S_RV, kv_chunk),
                seg_spec(S_RG, q_chunk)]
    if rope:
        args += [rope_tabs[0], rope_tabs[1]]
        in_specs += [pl.BlockSpec((tc, 2 * RET_DIM), lambda b, p, c: (kv_chunk(p, c), 0)),
                     pl.BlockSpec((tc, 2 * RET_DIM), lambda b, p, c: (kv_chunk(p, c), 0))]
    args += list(w['ret_layer_tabs']) + list(w['ret_const_tabs']) + [w['ret_gn']]
    in_specs += ([_pick_spec(a, layer) for a in w['ret_layer_tabs']]
                 + [_const_spec(a.shape) for a in w['ret_const_tabs']] + [_pick_spec(w['ret_gn'], layer)])
    if has_s0:
        args.append(s0)
        in_specs.append(pl.BlockSpec((None, None, 2, RET_HEADS, RET_DIM, RET_DIM),
                                     lambda b, p, c: (b, layer, 0, 0, 0, 0)))
    out_specs = [pl.BlockSpec((None, tc, SEG), lambda b, p, c: (b, q_chunk(p, c), 0))]
    out_shape = [jax.ShapeDtypeStruct((nb, t, SEG), F32)]
    if want_fin:
        out_specs.append(pl.BlockSpec((None, 2, RET_HEADS, RET_DIM, RET_DIM),
                                      lambda b, p, c: (b, 0, 0, 0, 0)))
        out_shape.append(jax.ShapeDtypeStruct((nb, 2, RET_HEADS, RET_DIM, RET_DIM), F32))
    return pl.pallas_call(
        functools.partial(_ret_kernel, rope=rope, has_s0=has_s0, want_fin=want_fin, nch=nch),
        grid=(nb, 2, nch),
        in_specs=in_specs,
        out_specs=out_specs,
        out_shape=out_shape,
        scratch_shapes=[pltpu.VMEM((nch, SEG, SEG), F32),
                        pltpu.VMEM((SEG, SEG), F32),
                        pltpu.VMEM((SEG, SEG), F32)],
        compiler_params=_cparams(("arbitrary", "arbitrary", "arbitrary")),
        name="retention",
    )(*args)


def _mla_kernel(*refs, t, tq, nctx, want_cache):
    refs = list(refs)
    cq_ref, ckv_ref, krp_ref = refs[:3]
    del refs[:3]
    has_ctx = nctx > 0
    if has_ctx:
        cm_ref, sm_ref, cache_ref = refs[:3]
        del refs[:3]
    qn_ref, kvn_ref, wq_ref = refs[:3]
    del refs[:3]
    if has_ctx:
        wqr_ref, wkc_ref = refs[:2]
        del refs[:2]
    wk_ref, wv_ref = refs[:2]
    del refs[:2]
    o_ref = refs.pop(0)
    lat_ref = refs.pop(0) if want_cache else None
    k_scr, vt_scr = refs
    qi = pl.program_id(1)
    s_tot = t + nctx

    def store_vt(v, c0, n):
        vt = v.T
        for h in range(MLA_HEADS):
            vt_scr[h, 0:MLA_V, c0:c0 + n] = vt[h * MLA_V:(h + 1) * MLA_V, :].astype(BF16)

    @pl.when(qi == 0)
    def _():
        lat = _rms(ckv_ref[...].astype(F32), kvn_ref[...])
        if want_cache:
            lat_ref[...] = lat
        latb = lat.astype(BF16)
        kr = krp_ref[...].astype(F32)
        if has_ctx:
            kr = kr * cm_ref[...] + _rot_half_lanes(kr, MLA_ROPE) * sm_ref[...]
        kk = jnp.dot(latb, wk_ref[...], preferred_element_type=F32)
        for h in range(MLA_HEADS):
            k_scr[h, 0:t, :] = (kk[:, h * MLA_SLAB:(h + 1) * MLA_SLAB] + kr).astype(BF16)
        store_vt(jnp.dot(latb, wv_ref[...], preferred_element_type=F32), 0, t)
        if has_ctx:
            cache = cache_ref[...].astype(BF16)
            kkc = jnp.dot(cache, wkc_ref[...], preferred_element_type=F32)
            for h in range(MLA_HEADS):
                k_scr[h, t:t + nctx, :] = kkc[:, h * MLA_SLAB:(h + 1) * MLA_SLAB].astype(BF16)
            store_vt(jnp.dot(cache[:, 0:MLA_KV_LORA], wv_ref[...], preferred_element_type=F32), t, nctx)
        tail = lax.broadcasted_iota(jnp.int32, (MLA_VT_ROWS - MLA_V, s_tot), 0)
        for h in range(MLA_HEADS):
            vt_scr[h, MLA_V:MLA_VT_ROWS, :] = jnp.where(tail == 0, 1.0, 0.0).astype(BF16)

    a = _rms(cq_ref[...].astype(F32), qn_ref[...], n=MLA_Q_LORA).astype(BF16)
    q = jnp.dot(a, wq_ref[...], preferred_element_type=F32)
    if has_ctx:
        qrot = jnp.dot(a, wqr_ref[...], preferred_element_type=F32)
        r0 = pl.multiple_of(qi * tq, tq)
        cm = cm_ref[pl.ds(r0, tq), :]
        sm = sm_ref[pl.ds(r0, tq), :]
    scale = (MLA_NOPE + MLA_ROPE) ** -0.5
    def scores_t(h):
        qh = q[:, h * MLA_SLAB:(h + 1) * MLA_SLAB]
        if has_ctx:
            qh = qh * cm + qrot[:, h * MLA_SLAB:(h + 1) * MLA_SLAB] * sm
        qt = (qh * (scale * LOG2_E)).T.astype(BF16)
        return jnp.dot(k_scr[h], qt, preferred_element_type=F32)

    outs = []
    st_next = scores_t(0)
    for h in range(MLA_HEADS):
        st = st_next
        if h + 1 < MLA_HEADS:
            st_next = scores_t(h + 1)
        m = jnp.max(st, axis=0, keepdims=True)
        e = jnp.exp2(st - m).astype(BF16)
        ot = jnp.dot(vt_scr[h], e, preferred_element_type=F32)
        outs.append(ot[0:MLA_V, :] * (1.0 / ot[MLA_V:MLA_V + 1, :]))
    o_ref[...] = jnp.concatenate(outs, axis=0).T


def _mla_call(p3, w, rope_tabs, cache, *, layer, nb, t, tq, want_cache):
    has_ctx = cache is not None
    nctx = cache.shape[2] if has_ctx else 0
    s_tot = t + nctx
    half = MLA_KV_LORA

    args = [p3, p3, p3]
    in_specs = [
        pl.BlockSpec((None, tq, SEG), lambda b, i: (b, i, S_CQ)),
        pl.BlockSpec((None, t, half), lambda b, i: (b, 0, 2 * S_KV)),
        pl.BlockSpec((None, t, half), lambda b, i: (b, 0, 2 * S_KV + 1)),
    ]
    if has_ctx:
        args += [rope_tabs[0], rope_tabs[1], cache]
        in_specs += [
            _const_spec((t, MLA_SLAB)),
            _const_spec((t, MLA_SLAB)),
            pl.BlockSpec((None, None, nctx, MLA_KV_LORA + MLA_ROPE), lambda b, i: (b, layer, 0, 0)),
        ]
    args += [w['mla_qn'], w['mla_kvn'], w['mla_wq']]
    in_specs += [_pick_spec(w['mla_qn'], layer), _pick_spec(w['mla_kvn'], layer), _pick_spec(w['mla_wq'], layer)]
    if has_ctx:
        args += [w['mla_wqr'], w['mla_wkc']]
        in_specs += [_pick_spec(w['mla_wqr'], layer), _pick_spec(w['mla_wkc'], layer)]
    args += [w['mla_wk'], w['mla_wv']]
    in_specs += [_pick_spec(w['mla_wk'], layer), _pick_spec(w['mla_wv'], layer)]
    out_specs = [pl.BlockSpec((None, tq, SEG), lambda b, i: (b, i, 0))]
    out_shape = [jax.ShapeDtypeStruct((nb, t, SEG), F32)]
    if want_cache:
        out_specs.append(pl.BlockSpec((None, t, MLA_KV_LORA), lambda b, i: (b, 0, 0)))
        out_shape.append(jax.ShapeDtypeStruct((nb, t, MLA_KV_LORA), F32))
    return pl.pallas_call(
        functools.partial(_mla_kernel, t=t, tq=tq, nctx=nctx, want_cache=want_cache),
        grid=(nb, t // tq),
        in_specs=in_specs,
        out_specs=out_specs,
        out_shape=out_shape,
        scratch_shapes=[pltpu.VMEM((MLA_HEADS, s_tot, MLA_SLAB), BF16),
                        pltpu.VMEM((MLA_HEADS, MLA_VT_ROWS, s_tot), BF16)],
        compiler_params=_cparams(("arbitrary", "arbitrary")),
        name="mla",
    )(*args)


def _merge_kernel(x_ref, mod_ref, npre_ref, npost_ref, yf_ref, yb_ref, u_ref, ret_ref,
                  cx_ref, cb_ref, cc_ref, cxp_ref, ccp_ref, cxn_ref, ccn_ref, mla_ref,
                  s5d_ref, wglu_ref, cw_ref, cbias_ref, wbr_ref, wg_ref, bg_ref, wo_ref,
                  o_ref, h_scr, *, tm, nt):
    i = pl.program_id(1)
    x = x_ref[...]
    sh = mod_ref[3:4, :]
    sc = mod_ref[4:5, :]
    gt = mod_ref[5:6, :]
    h_scr[...] = (_rms(x, npre_ref[...]) * (1.0 + sc) + sh).astype(BF16)

    y = yf_ref[...] + yb_ref[...] + s5d_ref[...] * u_ref[...].astype(F32)
    z5 = _gelu_tanh(y)
    b_s5 = z5 * _sigmoid(_dot(z5, wglu_ref[...]))

    z = cc_ref[...].astype(F32) * cx_ref[...].astype(F32)
    zprev = ((ccp_ref[...].astype(F32) * cxp_ref[...].astype(F32))[HALO - 1:HALO, :]
             * jnp.where(i > 0, 1.0, 0.0))
    znext = ((ccn_ref[...].astype(F32) * cxn_ref[...].astype(F32))[0:1, :]
             * jnp.where(i < nt - 1, 1.0, 0.0))
    row = lax.broadcasted_iota(jnp.int32, (tm, SEG), 0)
    z_dn = jnp.where(row == 0, zprev, pltpu.roll(z, 1, axis=0))
    z_up = jnp.where(row == tm - 1, znext, pltpu.roll(z, tm - 1, axis=0))
    b_conv = cb_ref[...].astype(F32) * (cbias_ref[...] + z_dn * cw_ref[0:1, :] + z * cw_ref[1:2, :]
                                        + z_up * cw_ref[2:3, :])

    branches = (b_s5, ret_ref[...], b_conv, mla_ref[...])
    merged = jnp.zeros((tm, D_MODEL), F32)
    for n in range(N_BRANCH):
        gate = _sigmoid(jnp.dot(h_scr[...], wg_ref[:, n * D_MODEL:(n + 1) * D_MODEL],
                                preferred_element_type=F32) + bg_ref[:, n * D_MODEL:(n + 1) * D_MODEL])
        merged = merged + gate * _dot(branches[n], wbr_ref[n])
    m = _dot(merged, wo_ref[...])
    o_ref[...] = x + gt * _rms(m, npost_ref[...])


def _merge_call(x, w, y_s5, ret_o, p3, mla_o, *, layer, nb, t, tm, row_of_batch):
    nt = t // tm
    hb = tm // HALO
    n_halo = t // HALO

    def dir_spec(d):
        return pl.BlockSpec((None, None, tm, SEG), lambda b, i: (d, b, i, 0))

    def seg_spec(seg):
        return pl.BlockSpec((None, tm, SEG), lambda b, i: (b, i, seg))

    def prev_spec(seg):
        return pl.BlockSpec((None, HALO, SEG), lambda b, i: (b, jnp.maximum(i * hb - 1, 0), seg))

    def next_spec(seg):
        return pl.BlockSpec((None, HALO, SEG), lambda b, i: (b, jnp.minimum((i + 1) * hb, n_halo - 1), seg))

    return pl.pallas_call(
        functools.partial(_merge_kernel, tm=tm, nt=nt),
        grid=(nb, nt),
        in_specs=[
            pl.BlockSpec((tm, D_MODEL), lambda b, i: (b * nt + i, 0)),
            pl.BlockSpec((None, None, N_MOD, D_MODEL), lambda b, i: (layer, row_of_batch(b), 0, 0)),
            _pick_spec(w['npre'], layer * N_NORM + 1),
            _pick_spec(w['npost'], layer * N_NORM + 1),
            dir_spec(0), dir_spec(1), seg_spec(S_U),
            pl.BlockSpec((None, tm, SEG), lambda b, i: (b, i, 0)),
            seg_spec(S_CX), seg_spec(S_CB), seg_spec(S_CC),
            prev_spec(S_CX), prev_spec(S_CC), next_spec(S_CX), next_spec(S_CC),
            pl.BlockSpec((None, tm, SEG), lambda b, i: (b, i, 0)),
            _pick_spec(w['s5_d'], layer), _pick_spec(w['s5_wglu'], layer), _pick_spec(w['conv_w'], layer),
            _pick_spec(w['conv_b'], layer), _resident_spec(w['w_branch'].shape[1:], (layer,)),
            _resident_spec(w['w_gate'].shape[1:], (layer,)), _pick_spec(w['b_gate'], layer),
            _resident_spec(w['w_o'].shape[1:], (layer,)),
        ],
        out_specs=pl.BlockSpec((tm, D_MODEL), lambda b, i: (b * nt + i, 0)),
        out_shape=jax.ShapeDtypeStruct((nb * t, D_MODEL), F32),
        scratch_shapes=[pltpu.VMEM((tm, D_MODEL), BF16)],
        compiler_params=_cparams(("arbitrary", "arbitrary")),
        name="merge",
    )(x, w['mod'], w['npre'], w['npost'], y_s5, y_s5, p3, ret_o, p3, p3, p3, p3, p3, p3, p3, mla_o,
      w['s5_d'], w['s5_wglu'], w['conv_w'], w['conv_b'], w['w_branch'], w['w_gate'], w['b_gate'], w['w_o'])


def _zeros_like_cols(w, n):
    return jnp.zeros(w.shape[:-1] + (n,), w.dtype)


def _axial_angles(t, dim):
    rows = t // GRID_W
    row = np.repeat(np.arange(rows, dtype=np.float64), GRID_W)
    col = np.tile(np.arange(GRID_W, dtype=np.float64), rows)
    quarter = dim // 4
    inv = ROPE_BASE ** (-np.arange(quarter, dtype=np.float64) / quarter)
    ang = np.concatenate([row[:, None] * inv, col[:, None] * inv], axis=-1)
    return np.cos(ang).astype(np.float32), np.sin(ang).astype(np.float32)


def _rope_tables(t):
    cos, sin = _axial_angles(t, RET_DIM)
    ret_cos = np.tile(np.concatenate([cos, cos], -1), (1, 2))
    ret_sin = np.tile(np.concatenate([sin, sin], -1), (1, 2))
    cos, sin = _axial_angles(t, MLA_ROPE)
    pad = MLA_SLAB - MLA_NOPE - MLA_ROPE
    mla_cos = np.concatenate([np.ones((t, MLA_NOPE), np.float32), cos, cos, np.zeros((t, pad), np.float32)], -1)
    mla_sin = np.concatenate([np.zeros((t, MLA_NOPE), np.float32), sin, sin, np.zeros((t, pad), np.float32)], -1)
    return ((jnp.asarray(ret_cos), jnp.asarray(ret_sin)), (jnp.asarray(mla_cos), jnp.asarray(mla_sin)))


def _place_rope_key(w):
    return jnp.concatenate([_zeros_like_cols(w, MLA_NOPE), w,
                            _zeros_like_cols(w, MLA_SLAB - MLA_NOPE - MLA_ROPE)], -1)


def _in_weights(w_in):
    sizes = (BRANCH_WIDTH,) * 8 + (MLA_Q_LORA, MLA_KV_LORA, MLA_ROPE)
    offs = np.cumsum((0,) + sizes)
    u, rq, rk, rv, rg, cx, cb, cc, cq, ckv, kr = [w_in[..., offs[i]:offs[i + 1]] for i in range(11)]
    rk = rk * (RET_DIM ** -0.5)
    cols = [u, rq, rk, rv, rg, cx, cb, cc,
            jnp.concatenate([cq, _zeros_like_cols(cq, SEG - MLA_Q_LORA)], -1),
            jnp.concatenate([ckv, _place_rope_key(kr)], -1)]
    return jnp.concatenate([c.astype(BF16) for c in cols], axis=-1)


def _mla_weights(w_uq, w_ukv):
    nl = w_uq.shape[0]
    dq = MLA_NOPE + MLA_ROPE
    w4 = w_uq.reshape(nl, MLA_Q_LORA, MLA_HEADS, dq)
    nope, ropew = w4[..., :MLA_NOPE], w4[..., MLA_NOPE:]
    half = MLA_ROPE // 2
    rot = jnp.concatenate([-ropew[..., half:], ropew[..., :half]], -1)
    zpad = jnp.zeros((nl, MLA_Q_LORA, MLA_HEADS, MLA_SLAB - dq), F32)
    wq = jnp.concatenate([nope, ropew, zpad], -1).reshape(nl, MLA_Q_LORA, MLA_HEADS * MLA_SLAB)
    wqr = jnp.concatenate([jnp.zeros_like(nope), rot, zpad], -1).reshape(nl, MLA_Q_LORA, MLA_HEADS * MLA_SLAB)
    rpad = jnp.zeros((nl, SEG - MLA_Q_LORA, MLA_HEADS * MLA_SLAB), F32)
    wq = jnp.concatenate([wq, rpad], 1).astype(BF16)
    wqr = jnp.concatenate([wqr, rpad], 1).astype(BF16)
    kv4 = w_ukv.reshape(nl, MLA_KV_LORA, MLA_HEADS, MLA_NOPE + MLA_V)
    wk = jnp.concatenate([kv4[..., :MLA_NOPE], jnp.zeros((nl, MLA_KV_LORA, MLA_HEADS, MLA_SLAB - MLA_NOPE), F32)],
                         -1).reshape(nl, MLA_KV_LORA, MLA_HEADS * MLA_SLAB)
    place = np.zeros((MLA_ROPE, MLA_HEADS, MLA_SLAB), np.float32)
    place[np.arange(MLA_ROPE), :, MLA_NOPE + np.arange(MLA_ROPE)] = 1.0
    place = jnp.broadcast_to(jnp.asarray(place.reshape(MLA_ROPE, MLA_HEADS * MLA_SLAB)),
                             (nl, MLA_ROPE, MLA_HEADS * MLA_SLAB))
    wkc = jnp.concatenate([wk, place], 1).astype(BF16)
    wv = kv4[..., MLA_NOPE:].reshape(nl, MLA_KV_LORA, MLA_HEADS * MLA_V).astype(BF16)
    return wq, wqr, wk.astype(BF16), wkc, wv


def _s5_tables(lam_re, lam_im, log_dt, b_re, b_im, c_re, c_im):
    nl = lam_re.shape[0]
    dt = jnp.exp(log_dt)[..., None]
    mag = jnp.exp(lam_re * dt)
    ab_re = mag * jnp.cos(lam_im * dt)
    ab_im = mag * jnp.sin(lam_im * dt)
    den = lam_re * lam_re + lam_im * lam_im
    f_re = ((ab_re - 1.0) * lam_re + ab_im * lam_im) / den
    f_im = (ab_im * lam_re - (ab_re - 1.0) * lam_im) / den
    bb_re = f_re[..., None] * b_re - f_im[..., None] * b_im
    bb_im = f_re[..., None] * b_im + f_im[..., None] * b_re
    eye = jnp.eye(S5_GROUPS, dtype=F32)
    blk_in = lambda m: jnp.einsum('ldgph,gk->ldghkp', m, eye).reshape(nl, 2, BRANCH_WIDTH, S5_LANES)
    bblk = jnp.concatenate([blk_in(bb_re), blk_in(bb_im)], -1).astype(BF16)
    blk_out = lambda m: jnp.einsum('lghp,gk->lgpkh', m, eye).reshape(nl, S5_LANES, BRANCH_WIDTH)
    cblk = jnp.concatenate([blk_out(c_re), -blk_out(c_im)], 1).astype(BF16)
    a_tab = jnp.stack([ab_re.reshape(nl, 2, S5_LANES), ab_im.reshape(nl, 2, S5_LANES)], axis=2)
    return bblk, a_tab, cblk


def _ret_tables(decay_logit, c):
    lg = jax.nn.log_sigmoid(decay_logit.astype(F32))
    lf = lg[:, 0][:, :, None, None]
    lb = lg[:, 1][:, :, None, None]
    idx = jnp.arange(c, dtype=F32)
    diff = idx[:, None] - idx[None, :]
    dm = jnp.where(diff >= 0, jnp.exp(lf * jnp.maximum(diff, 0.0)), jnp.exp(lb * jnp.maximum(-diff, 0.0)))
    lfl = jnp.repeat(lg[:, 0], RET_DIM, axis=-1)
    lbl = jnp.repeat(lg[:, 1], RET_DIM, axis=-1)
    qdf = jnp.exp(lfl[:, None, :] * (idx[:, None] + 1.0))
    qdb = jnp.exp(lbl[:, None, :] * (c - idx[:, None]))
    kdft = jnp.exp(lfl[:, :, None] * (c - 1.0 - idx[None, :]))
    kdbt = jnp.exp(lbl[:, :, None] * idx[None, :])
    nl = lg.shape[0]
    cdf = jnp.broadcast_to(jnp.exp(lfl * c)[:, :, None], (nl, SEG, SEG))
    cdb = jnp.broadcast_to(jnp.exp(lbl * c)[:, :, None], (nl, SEG, SEG))
    head = np.arange(SEG) // RET_DIM
    hm = jnp.asarray((head[None, :] == np.arange(RET_HEADS)[:, None]).astype(np.float32))
    bd_np = (head[:, None] == head[None, :]).astype(np.float32)
    bd = jnp.asarray(bd_np)
    gm = jnp.asarray(bd_np / RET_DIM, BF16)
    return (dm, qdf, qdb, kdft, kdbt, cdf, cdb), (hm, bd, gm)


def _layer(x, w, *, layer, nb, t, tm, tm_ffn, row_of_batch, ctx, rope_tabs):
    is_ctx_pass = ctx is None
    row_of_tile = lambda i: row_of_batch((i * tm_ffn) // t)
    tm_fused = TM_FFN_IN
    x, p = _ffn_in_call(x, w, layer=layer, tm=tm_fused,
                        row_of_tile=lambda i: row_of_batch((i * tm_fused) // t))
    p3 = p.reshape(nb, t, NSEG * SEG)

    if is_ctx_pass:
        s5_0, s0_layer = jnp.zeros((1, 2, 2, nb, S5_LANES), F32), 0
        ret_0, mla_cache, ret_rope, mla_rope = None, None, None, None
    else:
        s5_0, ret_0, mla_cache = ctx
        s0_layer = layer
        ret_rope, mla_rope = rope_tabs
    s5_out = _s5_call(p3, w, s5_0, layer=layer, s0_layer=s0_layer, nb=nb, t=t, want_fin=is_ctx_pass)
    ret_out = _ret_call(p3, w, ret_0, ret_rope, layer=layer, nb=nb, t=t, want_fin=is_ctx_pass)
    mla_out = _mla_call(p3, w, mla_rope, mla_cache, layer=layer, nb=nb, t=t, tq=min(t, MLA_TQ),
                        want_cache=is_ctx_pass)

    x = _merge_call(x, w, s5_out[0], ret_out[0], p3, mla_out[0], layer=layer, nb=nb, t=t, tm=tm,
                    row_of_batch=row_of_batch)
    x = _ffn_call(x, w, layer=layer, k=2, tm=tm_ffn, row_of_tile=row_of_tile)

    state = None
    if is_ctx_pass:
        kr = p3[:, :, S_KV * SEG + MLA_KV_LORA + MLA_NOPE:S_KV * SEG + MLA_KV_LORA + MLA_NOPE + MLA_ROPE]
        cache = jnp.concatenate([mla_out[1], kr.astype(F32)], axis=-1)
        state = (s5_out[1], ret_out[1], cache)
    return x, state


def kernel(x_prompt, x_sample, state_s5, state_ret, cache_mla, c, c_ctx, w_mod, b_mod, norm_pre, norm_post, ffn_w1, ffn_w3, ffn_w2, w_in, s5_lam_re, s5_lam_im, s5_log_dt, s5_b_re, s5_b_im, s5_c_re, s5_c_im, s5_d, s5_w_glu, ret_decay, ret_gn, conv_w, conv_b, mla_q_norm, mla_w_uq, mla_kv_norm, mla_w_ukv, w_branch, w_gate, b_gate, w_o):
    bp, tp, _ = x_prompt.shape
    bs, ts, _ = x_sample.shape
    assert 1 + bs <= MOD_ROWS

    cond = jnp.concatenate([c_ctx[None, :], c, jnp.zeros((MOD_ROWS - 1 - bs, D_MODEL), F32)], axis=0)
    mod = _mod_call(cond, w_mod, b_mod).reshape(DEPTH, MOD_ROWS, N_MOD, D_MODEL)

    wq, wqr, wk, wkc, wv = _mla_weights(mla_w_uq, mla_w_ukv)
    bblk, a_tab, cblk = _s5_tables(s5_lam_re, s5_lam_im, s5_log_dt, s5_b_re, s5_b_im, s5_c_re, s5_c_im)
    ret_layer_tabs, ret_const_tabs = _ret_tables(ret_decay, RET_CHUNK)
    w = dict(
        mod=mod,
        npre=norm_pre.reshape(DEPTH * N_NORM, 1, D_MODEL),
        npost=norm_post.reshape(DEPTH * N_NORM, 1, D_MODEL),
        ffn=(ffn_w1.astype(BF16), ffn_w3.astype(BF16), ffn_w2.astype(BF16)),
        w_in=_in_weights(w_in),
        s5_bblk=bblk, s5_a=a_tab, s5_cblk=cblk,
        s5_d=s5_d[:, None, :], s5_wglu=s5_w_glu.astype(BF16),
        ret_layer_tabs=ret_layer_tabs, ret_const_tabs=ret_const_tabs, ret_gn=ret_gn[:, None, :],
        conv_w=conv_w, conv_b=conv_b[:, None, :],
        mla_qn=jnp.concatenate([mla_q_norm, jnp.zeros((DEPTH, SEG - MLA_Q_LORA), F32)], -1)[:, None, :],
        mla_kvn=mla_kv_norm[:, None, :],
        mla_wq=wq, mla_wqr=wqr, mla_wk=wk, mla_wkc=wkc, mla_wv=wv,
        w_branch=w_branch.astype(BF16), w_gate=w_gate.astype(BF16), b_gate=b_gate[:, None, :],
        w_o=w_o.astype(BF16),
    )
    rope_tabs = _rope_tables(ts)
    s5_0 = state_s5.transpose(1, 2, 5, 0, 3, 4).reshape(DEPTH, 2, 2, bs, S5_LANES)

    xp = x_prompt.reshape(bp * tp, D_MODEL)
    xs = x_sample.reshape(bs * ts, D_MODEL)
    s5_list, ret_list, mla_list = [], [], []
    for l in range(DEPTH):
        xp, (s5_s, ret_s, mla_c) = _layer(xp, w, layer=l, nb=bp, t=tp, tm=TM_CTX, tm_ffn=TM_CTX_FFN,
                                          row_of_batch=lambda b: 0, ctx=None, rope_tabs=None)
        s5_list.append(s5_s)
        ret_list.append(ret_s)
        mla_list.append(mla_c)
        xs, _ = _layer(xs, w, layer=l, nb=bs, t=ts, tm=TM_LAT, tm_ffn=TM_LAT_FFN, row_of_batch=lambda b: 1 + b,
                       ctx=(s5_0, state_ret, cache_mla), rope_tabs=rope_tabs)
    new_s5 = jnp.stack(s5_list, 0).reshape(DEPTH, 2, 2, bp, S5_GROUPS, S5_STATE).transpose(3, 0, 1, 4, 5, 2)
    return (xp.reshape(bp, tp, D_MODEL), xs.reshape(bs, ts, D_MODEL),
            new_s5, jnp.stack(ret_list, axis=1), jnp.stack(mla_list, axis=1))
```
